```python
import functools
import jax, jax.numpy as jnp
from jax import lax
import numpy as np

D_MODEL = 1024
BATCH = 8
SEQ = 2048
DEPTH = 1
DEC_BATCH = 128
DEC_SEQ = 1
PAST_LEN = 16384
PAGE_SIZE = 128

POOL_WIDTH = D_MODEL // 2
POOL_WINDOWS = (2, 4, 8, 16)
N_POOL_GROUPS = len(POOL_WINDOWS)
POOL_GROUP_DIM = POOL_WIDTH // N_POOL_GROUPS
POOL_STATE_LEN = max(POOL_WINDOWS) - 1
N_HEADS = 8
QK_NOPE_DIM = 64
QK_ROPE_DIM = 32
QK_HEAD_DIM = QK_NOPE_DIM + QK_ROPE_DIM
V_HEAD_DIM = 64
MLA_WIDTH = N_HEADS * V_HEAD_DIM
Q_LORA_RANK = 384
KV_LORA_RANK = 256
MIX_WIDTH = POOL_WIDTH + MLA_WIDTH
IN_WIDTH = POOL_WIDTH + Q_LORA_RANK + KV_LORA_RANK + QK_ROPE_DIM
D_FF = 2816
CONV_WIDTH = 3
CONV_STATE_LEN = CONV_WIDTH - 1
ROPE_BASE = 10000.0
RMS_EPS = 1e-6
Q_BLOCK = 128

kernel_name = "hymba_pool_mla_convffn_step"


def rms_norm(x, g):
    xf = x.astype(jnp.float32)
    y = xf * lax.rsqrt(jnp.mean(xf * xf, axis=-1, keepdims=True) + RMS_EPS)
    return (y * g.astype(jnp.float32)).astype(x.dtype)


def rope_angles(pos):
    inv = ROPE_BASE ** (-jnp.arange(0, QK_ROPE_DIM, 2, dtype=jnp.float32) / QK_ROPE_DIM)
    ang = pos.astype(jnp.float32)[:, None] * inv[None, :]
    return jnp.cos(ang), jnp.sin(ang)


def apply_rope(x, cos, sin):
    x1, x2 = jnp.split(x.astype(jnp.float32), 2, axis=-1)
    return jnp.concatenate([x1 * cos - x2 * sin, x2 * cos + x1 * sin], axis=-1).astype(x.dtype)


def qk_gain(g):
    g_pe = g[QK_NOPE_DIM:]
    return jnp.concatenate([g[:QK_NOPE_DIM], g_pe, g_pe])


def mla_queries(c_q, pos, g_q_a, w_uq, g_qn):
    c_q = rms_norm(c_q, g_q_a)
    q = (c_q @ w_uq).reshape(c_q.shape[:2] + (N_HEADS, QK_HEAD_DIM))
    cos, sin = rope_angles(pos)
    q_pe = apply_rope(q[..., QK_NOPE_DIM:], cos[:, None, :], sin[:, None, :])
    q = jnp.concatenate([q[..., :QK_NOPE_DIM], q_pe], axis=-1)
    return rms_norm(q, qk_gain(g_qn))


def mla_latents(c_kv_raw, k_pe_raw, pos, g_kv_a):
    ckv = rms_norm(c_kv_raw, g_kv_a)
    cos, sin = rope_angles(pos)
    return ckv, apply_rope(k_pe_raw, cos, sin)


def mla_keys(ckv, kpe, w_uk, g_kn):
    k_nope = jnp.einsum('btc,chd->bthd', ckv, w_uk)
    k_pe = jnp.broadcast_to(kpe[:, :, None, :], k_nope.shape[:3] + (QK_ROPE_DIM,)).astype(k_nope.dtype)
    return rms_norm(jnp.concatenate([k_nope, k_pe], axis=-1), qk_gain(g_kn))


def mla_attend_prompt(q, ckv, kpe, w_uk, w_uv, g_kn):
    B, S = q.shape[:2]
    k = mla_keys(ckv, kpe, w_uk, g_kn).astype(jnp.float32)
    ckv_f = ckv.astype(jnp.float32)
    w_uv_f = w_uv.astype(jnp.float32)
    n_blocks = S // Q_BLOCK
    q_blocks = q.astype(jnp.float32).reshape(B, n_blocks, Q_BLOCK, N_HEADS, QK_HEAD_DIM).transpose(1, 0, 2, 3, 4)
    starts = jnp.arange(n_blocks, dtype=jnp.int32) * Q_BLOCK
    key_pos = jnp.arange(S, dtype=jnp.int32)
    scale = QK_HEAD_DIM ** -0.5

    def block(args):
        qb, start = args
        s = jnp.einsum('bqhd,bkhd->bhqk', qb, k) * scale
        q_pos = start + jnp.arange(Q_BLOCK, dtype=jnp.int32)
        s = jnp.where(key_pos[None, :] <= q_pos[:, None], s, -jnp.inf)
        p = jax.nn.softmax(s, axis=-1)
        ctx = jnp.einsum('bhqk,bkc->bqhc', p, ckv_f)
        return jnp.einsum('bqhc,chv->bqhv', ctx, w_uv_f)

    out = lax.map(block, (q_blocks, starts))
    return out.transpose(1, 0, 2, 3, 4).reshape(B, S, MLA_WIDTH).astype(q.dtype)


def mla_attend_sample(q, ckv_new, kpe_new, cache_ckv, cache_kpe, page_table, w_uk, w_uv, g_kn):
    B, Sq = q.shape[:2]
    qf = q.astype(jnp.float32)
    scale = QK_HEAD_DIM ** -0.5

    def update(carry, ckv_blk, kpe_blk, mask):
        m, l, acc = carry
        k = mla_keys(ckv_blk, kpe_blk, w_uk, g_kn).astype(jnp.float32)
        s = jnp.einsum('bqhd,bkhd->bhqk', qf, k) * scale
        if mask is not None:
            s = jnp.where(mask, s, -jnp.inf)
        m_new = jnp.maximum(m, jnp.max(s, axis=-1))
        alpha = jnp.exp(m - m_new)
        p = jnp.exp(s - m_new[..., None])
        l = l * alpha + jnp.sum(p, axis=-1)
        acc = acc * alpha[..., None] + jnp.einsum('bhqk,bkc->bhqc', p, ckv_blk.astype(jnp.float32))
        return (m_new, l, acc)

    init = (jnp.full((B, N_HEADS, Sq), -jnp.inf, jnp.float32),
            jnp.zeros((B, N_HEADS, Sq), jnp.float32),
            jnp.zeros((B, N_HEADS, Sq, KV_LORA_RANK), jnp.float32))

    def page_step(carry, phys):
        return update(carry, cache_ckv[phys], cache_kpe[phys], None), None

    carry, _ = lax.scan(page_step, init, page_table.T)
    causal = jnp.arange(Sq)[None, :] <= jnp.arange(Sq)[:, None]
    m, l, acc = update(carry, ckv_new, kpe_new, causal[None, None])
    ctx = acc / l[..., None]
    out = jnp.einsum('bhqc,chv->bqhv', ctx, w_uv.astype(jnp.float32))
    return out.reshape(B, Sq, MLA_WIDTH).astype(q.dtype)


def pool_mix(u, prefix, pos, w_pool, s_pool):
    B, S = u.shape[:2]
    P = POOL_STATE_LEN
    buf = jnp.concatenate([prefix.astype(u.dtype), u], axis=1)
    bf = buf.astype(jnp.float32)
    cs = jnp.concatenate([jnp.zeros((B, 1, POOL_WIDTH), jnp.float32), jnp.cumsum(bf, axis=1)], axis=1)
    end = cs[:, P + 1:P + 1 + S]
    diffs = []
    for g, w in enumerate(POOL_WINDOWS):
        ch = slice(g * POOL_GROUP_DIM, (g + 1) * POOL_GROUP_DIM)
        win = end[..., ch] - cs[:, P + 1 - w:P + 1 - w + S, ch]
        cnt = jnp.minimum(w, pos + 1).astype(jnp.float32)[None, :, None]
        diffs.append(win / cnt - bf[:, P:, ch])
    d = jnp.stack(diffs, axis=2)
    y = jnp.einsum('bsgc,gcd->bsgd', d, w_pool.astype(jnp.float32)).reshape(B, S, POOL_WIDTH)
    return (y * s_pool.astype(jnp.float32)).astype(u.dtype), buf[:, -P:]


def causal_dwconv(u, prefix, w, b):
    S = u.shape[1]
    buf = jnp.concatenate([prefix.astype(u.dtype), u], axis=1)
    y = b
    for k in range(CONV_WIDTH):
        y = y + buf[:, k:k + S] * w[k]
    return y, buf[:, -CONV_STATE_LEN:]


def trunk_layer(x, pos, pool_prefix, conv_prefix, attend, lp):
    h = rms_norm(x, lp['g_mix'])
    proj = h @ lp['w_in']
    u_pool, c_q, c_kv, k_pe = jnp.split(
        proj, [POOL_WIDTH, POOL_WIDTH + Q_LORA_RANK, POOL_WIDTH + Q_LORA_RANK + KV_LORA_RANK], axis=-1)
    pool_out, pool_state = pool_mix(u_pool, pool_prefix, pos, lp['w_pool'], lp['s_pool'])
    q = mla_queries(c_q, pos, lp['g_q_a'], lp['w_uq'], lp['g_qn'])
    ckv, kpe = mla_latents(c_kv, k_pe, pos, lp['g_kv_a'])
    attn_out = attend(q, ckv, kpe)
    g_out = lp['g_out']
    mixed = jnp.concatenate([rms_norm(pool_out, g_out[:POOL_WIDTH]),
                             rms_norm(attn_out, g_out[POOL_WIDTH:])], axis=-1)
    x = x + mixed @ lp['w_o']
    h2 = rms_norm(x, lp['g_ffn'])
    gate, conv_state = causal_dwconv(h2 @ lp['w_gate'], conv_prefix, lp['conv_w'], lp['conv_b'])
    x = x + (jax.nn.silu(gate) * (h2 @ lp['w_up'])) @ lp['w_down']
    return x, ckv, kpe, pool_state, conv_state


def setup_inputs(seed: int = 0) -> dict:
    key = jax.random.key(seed)
    ks = jax.random.split(key, 32)
    n_pages = PAST_LEN // PAGE_SIZE
    n_phys = (DEC_BATCH * n_pages * 5) // 4
    f32 = jnp.float32

    def nrm(k, shape, scale):
        return jax.random.normal(k, shape, f32) * scale

    def gain(k, shape):
        return 1.0 + 0.05 * jax.random.normal(k, shape, f32)

    perm = jax.random.permutation(ks[0], n_phys)[:DEC_BATCH * n_pages]
    return {
        "x_prompt": nrm(ks[1], (BATCH, SEQ, D_MODEL), 1.0),
        "x_sample": nrm(ks[2], (DEC_BATCH, DEC_SEQ, D_MODEL), 1.0),
        "cache_ckv": nrm(ks[3], (DEPTH, n_phys, PAGE_SIZE, KV_LORA_RANK), 1.0),
        "cache_kpe": nrm(ks[4], (DEPTH, n_phys, PAGE_SIZE, QK_ROPE_DIM), 1.0),
        "state_pool": nrm(ks[5], (DEPTH, DEC_BATCH, POOL_STATE_LEN, POOL_WIDTH), 1.0),
        "state_conv": nrm(ks[6], (DEPTH, DEC_BATCH, CONV_STATE_LEN, D_FF), 1.0),
        "page_table": perm.reshape(DEC_BATCH, n_pages).astype(jnp.int32),
        "g_mix": gain(ks[7], (DEPTH, D_MODEL)),
        "w_in": nrm(ks[8], (DEPTH, D_MODEL, IN_WIDTH), D_MODEL ** -0.5),
        "g_q_a": gain(ks[9], (DEPTH, Q_LORA_RANK)),
        "w_uq": nrm(ks[10], (DEPTH, Q_LORA_RANK, N_HEADS * QK_HEAD_DIM), Q_LORA_RANK ** -0.5),
        "g_qn": gain(ks[11], (DEPTH, QK_NOPE_DIM + QK_ROPE_DIM // 2)),
        "g_kv_a": gain(ks[12], (DEPTH, KV_LORA_RANK)),
        "w_uk": nrm(ks[13], (DEPTH, KV_LORA_RANK, N_HEADS, QK_NOPE_DIM), KV_LORA_RANK ** -0.5),
        "g_kn": gain(ks[14], (DEPTH, QK_NOPE_DIM + QK_ROPE_DIM // 2)),
        "w_uv": nrm(ks[15], (DEPTH, KV_LORA_RANK, N_HEADS, V_HEAD_DIM), KV_LORA_RANK ** -0.5),
        "w_pool": nrm(ks[16], (DEPTH, N_POOL_GROUPS, POOL_GROUP_DIM, POOL_GROUP_DIM), POOL_GROUP_DIM ** -0.5),
        "s_pool": gain(ks[17], (DEPTH, POOL_WIDTH)),
        "g_out": gain(ks[18], (DEPTH, MIX_WIDTH)),
        "w_o": nrm(ks[19], (DEPTH, MIX_WIDTH, D_MODEL), MIX_WIDTH ** -0.5),
        "g_ffn": gain(ks[20], (DEPTH, D_MODEL)),
        "w_gate": nrm(ks[21], (DEPTH, D_MODEL, D_FF), D_MODEL ** -0.5),
        "w_up": nrm(ks[22], (DEPTH, D_MODEL, D_FF), D_MODEL ** -0.5),
        "conv_w": nrm(ks[23], (DEPTH, CONV_WIDTH, D_FF), CONV_WIDTH ** -0.5),
        "conv_b": nrm(ks[24], (DEPTH, D_FF), 0.02),
        "w_down": nrm(ks[25], (DEPTH, D_FF, D_MODEL), D_FF ** -0.5),
    }


def reference(x_prompt, x_sample, cache_ckv, cache_kpe, state_pool, state_conv, page_table,
              g_mix, w_in, g_q_a, w_uq, g_qn, g_kv_a, w_uk, g_kn, w_uv, w_pool, s_pool,
              g_out, w_o, g_ffn, w_gate, w_up, conv_w, conv_b, w_down):
    B, S = x_prompt.shape[:2]
    past_len = page_table.shape[1] * PAGE_SIZE
    pos_prompt = jnp.arange(S, dtype=jnp.int32)
    pos_sample = past_len + jnp.arange(x_sample.shape[1], dtype=jnp.int32)
    yp, ys = x_prompt, x_sample
    ckv_p_l, kpe_p_l, pool_p_l, conv_p_l = [], [], [], []
    ckv_s_l, kpe_s_l, pool_s_l, conv_s_l = [], [], [], []
    for l in range(DEPTH):
        lp = {
            'g_mix': g_mix[l], 'w_in': w_in[l], 'g_q_a': g_q_a[l], 'w_uq': w_uq[l], 'g_qn': g_qn[l],
            'g_kv_a': g_kv_a[l], 'w_pool': w_pool[l], 's_pool': s_pool[l], 'g_out': g_out[l],
            'w_o': w_o[l], 'g_ffn': g_ffn[l], 'w_gate': w_gate[l], 'w_up': w_up[l],
            'conv_w': conv_w[l], 'conv_b': conv_b[l], 'w_down': w_down[l],
        }
        attend_p = functools.partial(mla_attend_prompt, w_uk=w_uk[l], w_uv=w_uv[l], g_kn=g_kn[l])
        pool0 = jnp.zeros((B, POOL_STATE_LEN, POOL_WIDTH), x_prompt.dtype)
        conv0 = jnp.zeros((B, CONV_STATE_LEN, D_FF), x_prompt.dtype)
        yp, ckv_p, kpe_p, pool_p, conv_p = trunk_layer(yp, pos_prompt, pool0, conv0, attend_p, lp)
        attend_s = functools.partial(mla_attend_sample, cache_ckv=cache_ckv[l], cache_kpe=cache_kpe[l],
                                     page_table=page_table, w_uk=w_uk[l], w_uv=w_uv[l], g_kn=g_kn[l])
        ys, ckv_s, kpe_s, pool_s, conv_s = trunk_layer(ys, pos_sample, state_pool[l], state_conv[l], attend_s, lp)
        ckv_p_l.append(ckv_p); kpe_p_l.append(kpe_p); pool_p_l.append(pool_p); conv_p_l.append(conv_p)
        ckv_s_l.append(ckv_s); kpe_s_l.append(kpe_s); pool_s_l.append(pool_s); conv_s_l.append(conv_s)
    new_ckv_prompt = jnp.stack(ckv_p_l)
    new_kpe_prompt = jnp.stack(kpe_p_l)
    new_pool_prompt = jnp.stack(pool_p_l)
    new_conv_prompt = jnp.stack(conv_p_l)
    new_ckv_sample = jnp.stack(ckv_s_l)
    new_kpe_sample = jnp.stack(kpe_s_l)
    new_pool_sample = jnp.stack(pool_s_l)
    new_conv_sample = jnp.stack(conv_s_l)
    return (yp, ys, new_ckv_prompt, new_kpe_prompt, new_pool_prompt, new_conv_prompt,
            new_ckv_sample, new_kpe_sample, new_pool_sample, new_conv_sample)
```

```python
import functools

import jax
import jax.numpy as jnp
from jax import lax
from jax.experimental import pallas as pl
from jax.experimental.pallas import tpu as pltpu

N_HEADS = 8
QK_NOPE_DIM = 64
QK_ROPE_DIM = 32
QK_HEAD_DIM = QK_NOPE_DIM + QK_ROPE_DIM
V_HEAD_DIM = 64
POOL_WINDOWS = (2, 4, 8, 16)
POOL_STATE_LEN = max(POOL_WINDOWS) - 1
CONV_WIDTH = 3
ROPE_BASE = 10000.0
RMS_EPS = 1e-6

LANE = 128
BF16_SUBLANES = 16
VMEM_LIMIT = 48 * 1024 * 1024

_NT = (((1,), (1,)), ((), ()))

bf16 = jnp.bfloat16
f32 = jnp.float32


def _rms(x, width):
    return x * lax.rsqrt(jnp.sum(x * x, axis=-1, keepdims=True) * (1.0 / width) + RMS_EPS)


def _dot(a, b):
    return jnp.dot(a, b, preferred_element_type=f32)


def _params(*sem):
    return pltpu.CompilerParams(dimension_semantics=sem, vmem_limit_bytes=VMEM_LIMIT)


def _proj_body(x_ref, rope_ref, gmix_ref, win_ref, gqa_ref, wuq_ref, gq_ref, gkva_ref, wukv_ref,
               u_ref, q_ref, ckv_ref, kpe_ref, k_ref, v_ref, *, pool_w, q_rank, kv_rank):
    x = x_ref[...]
    h = (_rms(x, x.shape[-1]) * gmix_ref[...]).astype(bf16)
    proj = _dot(h, win_ref[...])
    u_ref[...] = proj[:, :pool_w]
    o = pool_w
    cq = (_rms(proj[:, o:o + q_rank], q_rank) * gqa_ref[...]).astype(bf16)
    o += q_rank
    ckv = _rms(proj[:, o:o + kv_rank], kv_rank) * gkva_ref[...]
    o += kv_rank
    ckv_ref[...] = ckv
    cos = rope_ref[:, 0:LANE]
    sin_lo = rope_ref[:, LANE:2 * LANE]
    sin_hi = rope_ref[:, 2 * LANE:3 * LANE]

    def rope(t):
        half = QK_ROPE_DIM // 2
        return t * cos + pltpu.roll(t, LANE - half, 1) * sin_lo + pltpu.roll(t, half, 1) * sin_hi

    kpe = rope(proj[:, o:o + LANE])
    kpe_ref[...] = kpe
    qraw = _dot(cq, wuq_ref[...])
    kn = _dot(ckv.astype(bf16), wukv_ref[...])
    for hd in range(N_HEADS):
        sl = slice(hd * LANE, (hd + 1) * LANE)
        qh = rope(qraw[:, sl])
        q_ref[:, sl] = (_rms(qh, QK_HEAD_DIM) * gq_ref[:, sl]).astype(bf16)
        kh = kn[:, sl] + kpe
        k_ref[:, sl] = _rms(kh, QK_HEAD_DIM).astype(bf16)
    v_ref[...] = kn[:, N_HEADS * LANE:].astype(bf16)


def _proj(x, rope, n_rope_blocks, w, tm):
    M, D = x.shape
    pool_w = w['pool_w']
    q_rank = w['g_q_a'].shape[1]
    kv_rank = w['g_kv_a'].shape[1]
    hq = N_HEADS * LANE
    hv = N_HEADS * V_HEAD_DIM
    const = lambda a: pl.BlockSpec(a.shape, lambda i: (0, 0))
    row = lambda n: pl.BlockSpec((tm, n), lambda i: (i, 0))
    out_shapes = (
        jax.ShapeDtypeStruct((M, pool_w), f32), jax.ShapeDtypeStruct((M, hq), bf16),
        jax.ShapeDtypeStruct((M, kv_rank), f32), jax.ShapeDtypeStruct((M, LANE), f32),
        jax.ShapeDtypeStruct((M, hq), bf16), jax.ShapeDtypeStruct((M, hv), bf16))
    return pl.pallas_call(
        functools.partial(_proj_body, pool_w=pool_w, q_rank=q_rank, kv_rank=kv_rank),
        grid=(M // tm,),
        in_specs=[row(D), pl.BlockSpec((tm, 3 * LANE), lambda i: (i % n_rope_blocks, 0)),
                  const(w['g_mix']), const(w['w_in']), const(w['g_q_a']), const(w['w_uq']),
                  const(w['g_q']), const(w['g_kv_a']), const(w['w_ukv'])],
        out_specs=(row(pool_w), row(hq), row(kv_rank), row(LANE), row(hq), row(hv)),
        out_shape=out_shapes,
        compiler_params=_params("parallel"),
        name="in_proj",
    )(x, rope, w['g_mix'], w['w_in'], w['g_q_a'], w['w_uq'], w['g_q'], w['g_kv_a'], w['w_ukv'])


def _pool_finish(diffs, wpool_ref, spool_ref, gout_ref, o_ref):
    ys = []
    for g, d in enumerate(diffs):
        sl = slice(g * LANE, (g + 1) * LANE)
        ys.append(_dot(d.astype(bf16), wpool_ref[g]) * spool_ref[:, sl])
    width = LANE * len(ys)
    ssq = sum(jnp.sum(y * y, axis=-1, keepdims=True) for y in ys)
    scale = lax.rsqrt(ssq * (1.0 / width) + RMS_EPS)
    for g, y in enumerate(ys):
        sl = slice(g * LANE, (g + 1) * LANE)
        o_ref[:, sl] = (y * scale * gout_ref[:, sl]).astype(o_ref.dtype)


def _pool_prompt_body(u_ref, wpool_ref, spool_ref, gout_ref, o_ref, buf, *, ts):
    j = pl.program_id(1)
    halo = POOL_STATE_LEN + 1

    @pl.when(j == 0)
    def _():
        buf[0:halo, :] = jnp.zeros((halo, buf.shape[1]), f32)

    @pl.when(j > 0)
    def _():
        buf[0:halo, :] = buf[ts:ts + halo, :]

    buf[halo:halo + ts, :] = u_ref[...]
    pos = j * ts + lax.broadcasted_iota(jnp.int32, (ts, 1), 0)
    diffs = []
    for g, wdw in enumerate(POOL_WINDOWS):
        sl = slice(g * LANE, (g + 1) * LANE)
        x = buf[halo:halo + ts, sl]
        win = x
        for k in range(1, wdw):
            win = win + buf[halo - k:halo - k + ts, sl]
        cnt = jnp.minimum(wdw, pos + 1).astype(f32)
        diffs.append(win / cnt - x)
    _pool_finish(diffs, wpool_ref, spool_ref, gout_ref, o_ref)


def _pool_prompt(u, B, S, w, ts):
    M, W = u.shape
    ns = S // ts
    const2 = lambda a: pl.BlockSpec(a.shape, lambda b, j: (0,) * a.ndim)
    return pl.pallas_call(
        functools.partial(_pool_prompt_body, ts=ts),
        grid=(B, ns),
        in_specs=[pl.BlockSpec((ts, W), lambda b, j: (b * ns + j, 0)),
                  const2(w['w_pool']), const2(w['s_pool']), const2(w['g_out_pool'])],
        out_specs=pl.BlockSpec((ts, W), lambda b, j: (b * ns + j, 0)),
        out_shape=jax.ShapeDtypeStruct((M, W), bf16),
        scratch_shapes=[pltpu.VMEM((ts + POOL_STATE_LEN + 1, W), f32)],
        compiler_params=_params("arbitrary", "arbitrary"),
        name="pool_prompt",
    )(u, w['w_pool'], w['s_pool'], w['g_out_pool'])


def _pool_sample_body(u_ref, st_ref, wpool_ref, spool_ref, gout_ref, o_ref, *, pos):
    diffs = []
    for g, wdw in enumerate(POOL_WINDOWS):
        sl = slice(g * LANE, (g + 1) * LANE)
        x = u_ref[:, sl]
        win = x
        for k in range(1, wdw):
            win = win + st_ref[POOL_STATE_LEN - k, :, sl]
        diffs.append(win / float(min(wdw, pos + 1)) - x)
    _pool_finish(diffs, wpool_ref, spool_ref, gout_ref, o_ref)


def _pool_sample(u, st_t, pos, w):
    M, W = u.shape
    return pl.pallas_call(
        functools.partial(_pool_sample_body, pos=pos),
        out_shape=jax.ShapeDtypeStruct((M, W), bf16),
        compiler_params=pltpu.CompilerParams(vmem_limit_bytes=VMEM_LIMIT),
        name="pool_sample",
    )(u, st_t, w['w_pool'], w['s_pool'], w['g_out_pool'])


def _flash_body(q_ref, k_ref, v_ref, o_ref, *, tq):
    i = pl.program_id(2)
    lane = lax.broadcasted_iota(jnp.int32, (tq, LANE), 1)
    rows = lax.broadcasted_iota(jnp.int32, (tq, tq), 0)
    cols = lax.broadcasted_iota(jnp.int32, (tq, tq), 1)
    outs = []
    for hh in range(2):
        sl = slice(hh * LANE, (hh + 1) * LANE)
        q = q_ref[:, sl]

        def update(j, carry, masked):
            m, l, acc = carry
            start = pl.multiple_of(j * tq, tq)
            k = k_ref[pl.ds(start, tq), sl]
            v = v_ref[pl.ds(start, tq), :]
            s = lax.dot_general(q, k, _NT, preferred_element_type=f32)
            if masked:
                s = jnp.where(cols <= rows, s, -1e30)
            m_new = jnp.maximum(m, jnp.max(s, axis=-1, keepdims=True))
            alpha = jnp.exp(m - m_new)
            p = jnp.exp(s - m_new)
            l = l * alpha + jnp.sum(p, axis=-1, keepdims=True)
            acc = acc * alpha + _dot(p.astype(bf16), v)
            return m_new, l, acc

        init = (jnp.full((tq, 1), -1e30, f32), jnp.zeros((tq, 1), f32), jnp.zeros((tq, LANE), f32))
        carry = lax.fori_loop(0, i, lambda j, c: update(j, c, False), init)
        m, l, acc = update(i, carry, True)
        outs.append(acc / l)
    half = lane < V_HEAD_DIM
    o_ref[...] = jnp.where(half, outs[0], outs[1])


def _flash(q, k, v, B, S, tq):
    M = q.shape[0]
    nq = S // tq
    pairs = N_HEADS // 2
    return pl.pallas_call(
        functools.partial(_flash_body, tq=tq),
        grid=(B, pairs, nq),
        in_specs=[pl.BlockSpec((tq, 2 * LANE), lambda b, g, i: (b * nq + i, g)),
                  pl.BlockSpec((S, 2 * LANE), lambda b, g, i: (b, g)),
                  pl.BlockSpec((S, 2 * V_HEAD_DIM), lambda b, g, i: (b, g))],
        out_specs=pl.BlockSpec((tq, 2 * V_HEAD_DIM), lambda b, g, i: (b * nq + i, g)),
        out_shape=jax.ShapeDtypeStruct((M, N_HEADS * V_HEAD_DIM), f32),
        compiler_params=_params("parallel", "parallel", "arbitrary"),
        name="flash_prompt",
    )(q, k, v)


def _mm_body(a_ref, b_ref, o_ref):
    o_ref[...] = _dot(a_ref[...].astype(bf16), b_ref[...])


def _mm(a, b, name):
    return pl.pallas_call(
        _mm_body,
        out_shape=jax.ShapeDtypeStruct((a.shape[0], b.shape[1]), f32),
        compiler_params=pltpu.CompilerParams(vmem_limit_bytes=VMEM_LIMIT),
        name=name,
    )(a, b)


def _decode_body(pt_ref, q_ref, knew_ref, qabs_ref, qpe_ref, cnew_ref, wukt_ref, cache_ckv, cache_kpe,
                 o_ref, ckv_buf, kpe_buf, sems, lhs, m_scr, l_scr, acc_scr, *, n_pages, cp, page, tile):
    b = pl.program_id(0)
    c = pl.program_id(1)
    nc = pl.num_programs(1)
    total = pl.num_programs(0) * nc
    flat = b * nc + c
    slot = flat % 2
    nope_rows = wukt_ref.shape[0]

    def copies(flat_idx, slot_idx):
        base = (flat_idx // nc) * n_pages + (flat_idx % nc) * cp
        out = []
        for p in range(cp):
            phys = pt_ref[base + p]
            rows = pl.ds(p * page, page)
            out.append(pltpu.make_async_copy(cache_ckv.at[phys], ckv_buf.at[slot_idx, rows], sems.at[0, slot_idx]))
            out.append(pltpu.make_async_copy(cache_kpe.at[phys], kpe_buf.at[slot_idx, rows], sems.at[1, slot_idx]))
        return out

    @pl.when(flat == 0)
    def _():
        for cpy in copies(flat, slot):
            cpy.start()

    @pl.when(flat + 1 < total)
    def _():
        for cpy in copies(flat + 1, 1 - slot):
            cpy.start()

    @pl.when(c == 0)
    def _():
        lhs[0:nope_rows, :] = wukt_ref[...]
        lhs[nope_rows:nope_rows + BF16_SUBLANES, :] = qabs_ref[0]
        s_new = jnp.sum(q_ref[0].astype(f32) * knew_ref[0].astype(f32), axis=-1, keepdims=True)
        m_scr[...] = s_new
        l_scr[...] = jnp.ones_like(l_scr)
        acc_scr[...] = jnp.broadcast_to(cnew_ref[0], acc_scr.shape)

    for cpy in copies(flat, slot):
        cpy.wait()

    qpe = qpe_ref[0]
    ones = jnp.ones(qpe.shape, bf16)

    def tile_step(t, carry):
        start = pl.multiple_of(t * tile, tile)
        ct = ckv_buf[slot, pl.ds(start, tile), :].astype(bf16)
        kt = lax.dot_general(lhs[...], ct, _NT, preferred_element_type=f32)
        kn = kt[0:nope_rows].reshape(N_HEADS, nope_rows // N_HEADS, tile)
        ssq = jnp.sum(kn * kn, axis=1)
        num = kt[nope_rows:nope_rows + N_HEADS]
        kp = kpe_buf[slot, pl.ds(start, tile), :]
        num = num + lax.dot_general(qpe, kp.astype(bf16), _NT, preferred_element_type=f32)
        ssq = ssq + lax.dot_general(ones, (kp * kp).astype(bf16), _NT, preferred_element_type=f32)
        s = num * lax.rsqrt(ssq * (1.0 / QK_HEAD_DIM) + RMS_EPS)
        m = m_scr[...]
        m_new = jnp.maximum(m, jnp.max(s, axis=-1, keepdims=True))
        alpha = jnp.exp(m - m_new)
        p = jnp.exp(s - m_new)
        l_scr[...] = l_scr[...] * alpha + jnp.sum(p, axis=-1, keepdims=True)
        acc_scr[...] = acc_scr[...] * alpha + _dot(p.astype(bf16), ct)
        m_scr[...] = m_new
        return carry

    lax.fori_loop(0, (cp * page) // tile, tile_step, 0)

    @pl.when(c == nc - 1)
    def _():
        o_ref[0] = acc_scr[...] / l_scr[...]


def _decode(page_table, q3, knew3, qabs, qpe, cnew, wukt, cache_ckv, cache_kpe, cp, tile):
    DB, n_pages = page_table.shape
    page, kv_rank = cache_ckv.shape[1:]
    rope = cache_kpe.shape[2]
    nc = n_pages // cp
    T = cp * page
    per_seq = lambda a: pl.BlockSpec((1,) + a.shape[1:], lambda b, c, pt: (b,) + (0,) * (a.ndim - 1))
    grid_spec = pltpu.PrefetchScalarGridSpec(
        num_scalar_prefetch=1,
        grid=(DB, nc),
        in_specs=[per_seq(q3), per_seq(knew3), per_seq(qabs), per_seq(qpe), per_seq(cnew),
                  pl.BlockSpec(wukt.shape, lambda b, c, pt: (0, 0)),
                  pl.BlockSpec(memory_space=pl.ANY), pl.BlockSpec(memory_space=pl.ANY)],
        out_specs=pl.BlockSpec((1, N_HEADS, kv_rank), lambda b, c, pt: (b, 0, 0)),
        scratch_shapes=[pltpu.VMEM((2, T, kv_rank), f32), pltpu.VMEM((2, T, rope), f32),
                        pltpu.SemaphoreType.DMA((2, 2)),
                        pltpu.VMEM((wukt.shape[0] + BF16_SUBLANES, kv_rank), bf16),
                        pltpu.VMEM((N_HEADS, 1), f32), pltpu.VMEM((N_HEADS, 1), f32),
                        pltpu.VMEM((N_HEADS, kv_rank), f32)])
    return pl.pallas_call(
        functools.partial(_decode_body, n_pages=n_pages, cp=cp, page=page, tile=tile),
        grid_spec=grid_spec,
        out_shape=jax.ShapeDtypeStruct((DB, N_HEADS, kv_rank), f32),
        compiler_params=_params("arbitrary", "arbitrary"),
        name="decode_attn",
    )(page_table.reshape(-1), q3, knew3, qabs, qpe, cnew, wukt, cache_ckv, cache_kpe)


def _outproj_body(x_ref, mp_ref, attn_ref, gatt_ref, wo_ref, gffn_ref, x1_ref, h2_ref):
    pw = mp_ref.shape[1]
    attn = attn_ref[...]
    na = (_rms(attn, attn.shape[-1]) * gatt_ref[...]).astype(bf16)
    x1 = x_ref[...] + _dot(mp_ref[...], wo_ref[0:pw, :]) + _dot(na, wo_ref[pw:, :])
    x1_ref[...] = x1
    h2_ref[...] = (_rms(x1, x1.shape[-1]) * gffn_ref[...]).astype(bf16)


def _outproj(x, mp, attn, w, tm):
    M, D = x.shape
    const = lambda a: pl.BlockSpec(a.shape, lambda i: (0, 0))
    row = lambda n: pl.BlockSpec((tm, n), lambda i: (i, 0))
    return pl.pallas_call(
        _outproj_body,
        grid=(M // tm,),
        in_specs=[row(D), row(mp.shape[1]), row(attn.shape[1]),
                  const(w['g_out_attn']), const(w['w_o']), const(w['g_ffn'])],
        out_specs=(row(D), row(D)),
        out_shape=(jax.ShapeDtypeStruct((M, D), f32), jax.ShapeDtypeStruct((M, D), bf16)),
        compiler_params=_params("parallel"),
        name="out_proj",
    )(x, mp, attn, w['g_out_attn'], w['w_o'], w['g_ffn'])


def _ffn_finish(f, g, g1, g2, h2, x1_ref, wu_ref, wd_ref, cw_ref, cb_ref, y_ref, acc):
    gate = cb_ref[...] + g2 * cw_ref[0:1, :] + g1 * cw_ref[1:2, :] + g * cw_ref[2:3, :]
    act = (gate * jax.nn.sigmoid(gate)) * _dot(h2, wu_ref[...])
    part = _dot(act.astype(bf16), wd_ref[...])

    @pl.when(f == 0)
    def _():
        acc[...] = x1_ref[...] + part

    @pl.when(f > 0)
    def _():
        acc[...] += part

    @pl.when(f == pl.num_programs(1) - 1)
    def _():
        y_ref[...] = acc[...]


def _ffn_prompt_body(h2_ref, halo_ref, x1_ref, wg_ref, wu_ref, wd_ref, cw_ref, cb_ref,
                     y_ref, tail_ref, acc, gbuf, *, tm, tiles_per_seq):
    i = pl.program_id(0)
    f = pl.program_id(1)
    hb = BF16_SUBLANES
    h2 = h2_ref[...]
    g = _dot(h2, wg_ref[...])
    gh = _dot(halo_ref[...], wg_ref[...])
    gbuf[0:hb, :] = jnp.where(i % tiles_per_seq == 0, 0.0, gh)
    gbuf[hb:hb + tm, :] = g
    tail_ref[0] = gbuf[tm:tm + hb, :]
    g1 = gbuf[hb - 1:hb - 1 + tm, :]
    g2 = gbuf[hb - 2:hb - 2 + tm, :]
    _ffn_finish(f, g, g1, g2, h2, x1_ref, wu_ref, wd_ref, cw_ref, cb_ref, y_ref, acc)


def _ffn_prompt(h2, x1, S, w, tm, tf):
    M, D = x1.shape
    F = w['w_gate'].shape[1]
    hb = BF16_SUBLANES
    nt = M // tm
    halo_blocks = tm // hb
    y, tail = pl.pallas_call(
        functools.partial(_ffn_prompt_body, tm=tm, tiles_per_seq=S // tm),
        grid=(nt, F // tf),
        in_specs=[pl.BlockSpec((tm, D), lambda i, f: (i, 0)),
                  pl.BlockSpec((hb, D), lambda i, f: (jnp.maximum(i * halo_blocks - 1, 0), 0)),
                  pl.BlockSpec((tm, D), lambda i, f: (i, 0)),
                  pl.BlockSpec((D, tf), lambda i, f: (0, f)),
                  pl.BlockSpec((D, tf), lambda i, f: (0, f)),
                  pl.BlockSpec((tf, D), lambda i, f: (f, 0)),
                  pl.BlockSpec((CONV_WIDTH, tf), lambda i, f: (0, f)),
                  pl.BlockSpec((1, tf), lambda i, f: (0, f))],
        out_specs=(pl.BlockSpec((tm, D), lambda i, f: (i, 0)),
                   pl.BlockSpec((1, hb, tf), lambda i, f: (i, 0, f))),
        out_shape=(jax.ShapeDtypeStruct((M, D), f32), jax.ShapeDtypeStruct((nt, hb, F), f32)),
        scratch_shapes=[pltpu.VMEM((tm, D), f32), pltpu.VMEM((tm + hb, tf), f32)],
        compiler_params=_params("parallel", "arbitrary"),
        name="ffn_prompt",
    )(h2, h2, x1, w['w_gate'], w['w_up'], w['w_down'], w['conv_w'], w['conv_b'])
    return y, tail


def _ffn_sample_body(h2_ref, st_ref, x1_ref, wg_ref, wu_ref, wd_ref, cw_ref, cb_ref, y_ref, g_ref, acc):
    f = pl.program_id(1)
    h2 = h2_ref[...]
    g = _dot(h2, wg_ref[...])
    g_ref[...] = g
    _ffn_finish(f, g, st_ref[1], st_ref[0], h2, x1_ref, wu_ref, wd_ref, cw_ref, cb_ref, y_ref, acc)


def _ffn_sample(h2, x1, st_t, w, tf):
    M, D = x1.shape
    F = w['w_gate'].shape[1]
    return pl.pallas_call(
        _ffn_sample_body,
        grid=(1, F // tf),
        in_specs=[pl.BlockSpec((M, D), lambda i, f: (0, 0)),
                  pl.BlockSpec((CONV_WIDTH - 1, M, tf), lambda i, f: (0, 0, f)),
                  pl.BlockSpec((M, D), lambda i, f: (0, 0)),
                  pl.BlockSpec((D, tf), lambda i, f: (0, f)),
                  pl.BlockSpec((D, tf), lambda i, f: (0, f)),
                  pl.BlockSpec((tf, D), lambda i, f: (f, 0)),
                  pl.BlockSpec((CONV_WIDTH, tf), lambda i, f: (0, f)),
                  pl.BlockSpec((1, tf), lambda i, f: (0, f))],
        out_specs=(pl.BlockSpec((M, D), lambda i, f: (0, 0)),
                   pl.BlockSpec((M, tf), lambda i, f: (0, f))),
        out_shape=(jax.ShapeDtypeStruct((M, D), f32), jax.ShapeDtypeStruct((M, F), f32)),
        scratch_shapes=[pltpu.VMEM((M, D), f32)],
        compiler_params=_params("parallel", "arbitrary"),
        name="ffn_sample",
    )(h2, st_t, x1, w['w_gate'], w['w_up'], w['w_down'], w['conv_w'], w['conv_b'])


def _qk_gain(g):
    g_pe = g[QK_NOPE_DIM:]
    return jnp.concatenate([g[:QK_NOPE_DIM], g_pe, g_pe])


def _rope_table(pos):
    half = QK_ROPE_DIM // 2
    inv = ROPE_BASE ** (-jnp.arange(0, QK_ROPE_DIM, 2, dtype=f32) / QK_ROPE_DIM)
    ang = pos.astype(f32)[:, None] * inv[None, :]
    cos, sin = jnp.cos(ang), jnp.sin(ang)
    P = pos.shape[0]
    z = lambda n: jnp.zeros((P, n), f32)
    tail = LANE - QK_HEAD_DIM
    c = jnp.concatenate([jnp.ones((P, QK_NOPE_DIM), f32), cos, cos, z(tail)], axis=1)
    s_lo = jnp.concatenate([z(QK_NOPE_DIM), -sin, z(half), z(tail)], axis=1)
    s_hi = jnp.concatenate([z(QK_NOPE_DIM), z(half), sin, z(tail)], axis=1)
    return jnp.concatenate([c, s_lo, s_hi], axis=1)


def _prep_weights(g_mix, w_in, g_q_a, w_uq, g_qn, g_kv_a, w_uk, g_kn, w_uv, w_pool, s_pool,
                  g_out, w_o, g_ffn, w_gate, w_up, conv_w, conv_b, w_down):
    D = w_in.shape[0]
    q_rank = g_q_a.shape[0]
    kv_rank = g_kv_a.shape[0]
    pool_w = s_pool.shape[0]
    head_pad = LANE - QK_HEAD_DIM
    main = pool_w + q_rank + kv_rank
    w_in_p = jnp.concatenate([w_in[:, :main], jnp.zeros((D, QK_NOPE_DIM), f32), w_in[:, main:],
                              jnp.zeros((D, head_pad), f32)], axis=1).astype(bf16)
    w_uq_p = jnp.pad(w_uq.reshape(q_rank, N_HEADS, QK_HEAD_DIM), ((0, 0), (0, 0), (0, head_pad)))
    w_uq_p = w_uq_p.reshape(q_rank, N_HEADS * LANE).astype(bf16)
    gq = _qk_gain(g_qn) * _qk_gain(g_kn) * (QK_HEAD_DIM ** -0.5)
    gq = jnp.tile(jnp.pad(gq, (0, head_pad)), N_HEADS)[None, :]
    w_uk_p = jnp.pad(w_uk, ((0, 0), (0, 0), (0, LANE - QK_NOPE_DIM))).reshape(kv_rank, N_HEADS * LANE)
    w_ukv = jnp.concatenate([w_uk_p, w_uv.reshape(kv_rank, N_HEADS * V_HEAD_DIM)], axis=1).astype(bf16)
    w_uk_t = w_uk.reshape(kv_rank, N_HEADS * QK_NOPE_DIM).T.astype(bf16)
    eye = jnp.eye(N_HEADS, dtype=f32)
    w_uk_bd = jnp.einsum('chd,hg->hdgc', jnp.pad(w_uk, ((0, 0), (0, 0), (0, LANE - QK_NOPE_DIM))), eye)
    w_uk_bd = w_uk_bd.reshape(N_HEADS * LANE, N_HEADS * kv_rank).astype(bf16)
    w_uv_bd = jnp.einsum('chv,hg->hcgv', w_uv, eye).reshape(N_HEADS * kv_rank, N_HEADS * V_HEAD_DIM).astype(bf16)
    return {
        'pool_w': pool_w,
        'g_mix': g_mix[None, :], 'w_in': w_in_p, 'g_q_a': g_q_a[None, :], 'w_uq': w_uq_p, 'g_q': gq,
        'g_kv_a': g_kv_a[None, :], 'w_ukv': w_ukv, 'w_uk_t': w_uk_t, 'w_uk_bd': w_uk_bd, 'w_uv_bd': w_uv_bd,
        'w_pool': w_pool.astype(bf16), 's_pool': s_pool[None, :],
        'g_out_pool': g_out[None, :pool_w], 'g_out_attn': g_out[None, pool_w:],
        'w_o': w_o.astype(bf16), 'g_ffn': g_ffn[None, :],
        'w_gate': w_gate.astype(bf16), 'w_up': w_up.astype(bf16), 'w_down': w_down.astype(bf16),
        'conv_w': conv_w, 'conv_b': conv_b[None, :],
    }


def _tile(n, pref):
    t = min(n, pref)
    assert n % t == 0, (n, pref)
    return t


def _ffn_cols(F):
    half = F // 2
    return half if half % LANE == 0 else F


def kernel(x_prompt, x_sample, cache_ckv, cache_kpe, state_pool, state_conv, page_table, g_mix, w_in, g_q_a,
           w_uq, g_qn, g_kv_a, w_uk, g_kn, w_uv, w_pool, s_pool, g_out, w_o, g_ffn, w_gate, w_up, conv_w,
           conv_b, w_down):
    depth = g_mix.shape[0]
    assert depth == 1, "single-layer trunk only"
    B, S, D = x_prompt.shape
    DB, DS = x_sample.shape[:2]
    assert DS == 1, "one new token per sequence"
    n_pages = page_table.shape[1]
    page = cache_ckv.shape[2]
    past_len = n_pages * page
    l = 0
    w = _prep_weights(g_mix[l], w_in[l], g_q_a[l], w_uq[l], g_qn[l], g_kv_a[l], w_uk[l], g_kn[l], w_uv[l],
                      w_pool[l], s_pool[l], g_out[l], w_o[l], g_ffn[l], w_gate[l], w_up[l], conv_w[l],
                      conv_b[l], w_down[l])
    kv_rank = g_kv_a.shape[1]
    F = w_gate.shape[2]
    tf = _ffn_cols(F)

    xp = x_prompt.reshape(B * S, D)
    tm = _tile(S, 256)
    rope_p = _rope_table(jnp.arange(S, dtype=jnp.int32))
    u_p, q_p, ckv_p, kpe_p, k_p, v_p = _proj(xp, rope_p, S // tm, w, tm)
    mp_p = _pool_prompt(u_p, B, S, w, _tile(S, 512))
    attn_p = _flash(q_p, k_p, v_p, B, S, _tile(S, 256))
    x1_p, h2_p = _outproj(xp, mp_p, attn_p, w, _tile(S, 512))
    tmf = _tile(S, 512)
    y_p, tail = _ffn_prompt(h2_p, x1_p, S, w, tmf, tf)
    tiles_per_seq = S // tmf
    conv_p = tail[tiles_per_seq - 1::tiles_per_seq, BF16_SUBLANES - (CONV_WIDTH - 1):, :]

    xs = x_sample.reshape(DB, D)
    pos_s = past_len + jnp.arange(DS, dtype=jnp.int32)
    rope_s = jnp.tile(_rope_table(pos_s), (DB, 1))
    u_s, q_s, ckv_s, kpe_s, k_s, _ = _proj(xs, rope_s, 1, w, DB)
    st_pool_t = jnp.transpose(state_pool[l], (1, 0, 2))
    mp_s = _pool_sample(u_s, st_pool_t, past_len, w)
    qabs = _mm(q_s, w['w_uk_bd'], "absorb_q").reshape(DB, N_HEADS, kv_rank).astype(bf16)
    qabs = jnp.pad(qabs, ((0, 0), (0, BF16_SUBLANES - N_HEADS), (0, 0)))
    q3 = q_s.reshape(DB, N_HEADS, LANE)
    qpe = q3[:, :, QK_NOPE_DIM:QK_HEAD_DIM]
    cp = _tile(n_pages, 16)
    ctx = _decode(page_table, q3, k_s.reshape(DB, N_HEADS, LANE), qabs, qpe, ckv_s.reshape(DB, 1, kv_rank),
                  w['w_uk_t'], cache_ckv[l], cache_kpe[l], cp, _tile(cp * page, 256))
    attn_s = _mm(ctx.reshape(DB, N_HEADS * kv_rank), w['w_uv_bd'], "value_up")
    x1_s, h2_s = _outproj(xs, mp_s, attn_s, w, DB)
    st_conv_t = jnp.transpose(state_conv[l], (1, 0, 2))
    y_s, g_s = _ffn_sample(h2_s, x1_s, st_conv_t, w, tf)

    kpe_sl = slice(QK_NOPE_DIM, QK_HEAD_DIM)
    P = POOL_STATE_LEN
    return (
        y_p.reshape(B, S, D),
        y_s.reshape(DB, DS, D),
        ckv_p.reshape(1, B, S, kv_rank),
        kpe_p[:, kpe_sl].reshape(1, B, S, QK_ROPE_DIM),
        u_p.reshape(B, S, -1)[None, :, S - P:, :],
        conv_p[None],
        ckv_s.reshape(1, DB, DS, kv_rank),
        kpe_s[:, kpe_sl].reshape(1, DB, DS, QK_ROPE_DIM),
        jnp.concatenate([state_pool[l], u_s[:, None, :]], axis=1)[None, :, -P:, :],
        jnp.concatenate([state_conv[l], g_s[:, None, :]], axis=1)[None, :, -(CONV_WIDTH - 1):, :],
    )
```

```python
import functools

import jax
import jax.numpy as jnp
from jax import lax
from jax.experimental import pallas as pl
from jax.experimental.pallas import tpu as pltpu

N_HEADS = 8
QK_NOPE_DIM = 64
QK_ROPE_DIM = 32
QK_HEAD_DIM = QK_NOPE_DIM + QK_ROPE_DIM
V_HEAD_DIM = 64
POOL_WINDOWS = (2, 4, 8, 16)
POOL_STATE_LEN = max(POOL_WINDOWS) - 1
CONV_WIDTH = 3
ROPE_BASE = 10000.0
RMS_EPS = 1e-6

LANE = 128
BF16_SUBLANES = 16
VMEM_LIMIT = 48 * 1024 * 1024

_NT = (((1,), (1,)), ((), ()))

bf16 = jnp.bfloat16
f32 = jnp.float32


def _rms(x, width):
    return x * lax.rsqrt(jnp.sum(x * x, axis=-1, keepdims=True) * (1.0 / width) + RMS_EPS)


def _dot(a, b):
    return jnp.dot(a, b, preferred_element_type=f32)


def _params(*sem):
    return pltpu.CompilerParams(dimension_semantics=sem, vmem_limit_bytes=VMEM_LIMIT)


def _proj_body(x_ref, rope_ref, gmix_ref, win_ref, gqa_ref, wuq_ref, gq_ref, gkva_ref, wukv_ref,
               u_ref, q_ref, ckv_ref, kpe_ref, k_ref, v_ref, *, pool_w, q_rank, kv_rank):
    x = x_ref[...]
    h = (_rms(x, x.shape[-1]) * gmix_ref[...]).astype(bf16)
    proj = _dot(h, win_ref[...])
    u_ref[...] = proj[:, :pool_w]
    o = pool_w
    cq = (_rms(proj[:, o:o + q_rank], q_rank) * gqa_ref[...]).astype(bf16)
    o += q_rank
    ckv = _rms(proj[:, o:o + kv_rank], kv_rank) * gkva_ref[...]
    o += kv_rank
    ckv_ref[...] = ckv
    cos = rope_ref[:, 0:LANE]
    sin_lo = rope_ref[:, LANE:2 * LANE]
    sin_hi = rope_ref[:, 2 * LANE:3 * LANE]

    def rope(t):
        half = QK_ROPE_DIM // 2
        return t * cos + pltpu.roll(t, LANE - half, 1) * sin_lo + pltpu.roll(t, half, 1) * sin_hi

    kpe = rope(proj[:, o:o + LANE])
    kpe_ref[...] = kpe
    qraw = _dot(cq, wuq_ref[...])
    kn = _dot(ckv.astype(bf16), wukv_ref[...])
    for hd in range(N_HEADS):
        sl = slice(hd * LANE, (hd + 1) * LANE)
        qh = rope(qraw[:, sl])
        q_ref[:, sl] = (_rms(qh, QK_HEAD_DIM) * gq_ref[:, sl]).astype(bf16)
        kh = kn[:, sl] + kpe
        k_ref[:, sl] = _rms(kh, QK_HEAD_DIM).astype(bf16)
    v_ref[...] = kn[:, N_HEADS * LANE:].astype(bf16)


def _proj(x, rope, n_rope_blocks, w, tm):
    M, D = x.shape
    pool_w = w['pool_w']
    q_rank = w['g_q_a'].shape[1]
    kv_rank = w['g_kv_a'].shape[1]
    hq = N_HEADS * LANE
    hv = N_HEADS * V_HEAD_DIM
    const = lambda a: pl.BlockSpec(a.shape, lambda i: (0, 0))
    row = lambda n: pl.BlockSpec((tm, n), lambda i: (i, 0))
    out_shapes = (
        jax.ShapeDtypeStruct((M, pool_w), f32), jax.ShapeDtypeStruct((M, hq), bf16),
        jax.ShapeDtypeStruct((M, kv_rank), f32), jax.ShapeDtypeStruct((M, LANE), f32),
        jax.ShapeDtypeStruct((M, hq), bf16), jax.ShapeDtypeStruct((M, hv), bf16))
    return pl.pallas_call(
        functools.partial(_proj_body, pool_w=pool_w, q_rank=q_rank, kv_rank=kv_rank),
        grid=(M // tm,),
        in_specs=[row(D), pl.BlockSpec((tm, 3 * LANE), lambda i: (i % n_rope_blocks, 0)),
                  const(w['g_mix']), const(w['w_in']), const(w['g_q_a']), const(w['w_uq']),
                  const(w['g_q']), const(w['g_kv_a']), const(w['w_ukv'])],
        out_specs=(row(pool_w), row(hq), row(kv_rank), row(LANE), row(hq), row(hv)),
        out_shape=out_shapes,
        compiler_params=_params("parallel"),
        name="in_proj",
    )(x, rope, w['g_mix'], w['w_in'], w['g_q_a'], w['w_uq'], w['g_q'], w['g_kv_a'], w['w_ukv'])


def _pool_finish(diffs, wpool_ref, spool_ref, gout_ref, o_ref):
    ys = []
    for g, d in enumerate(diffs):
        sl = slice(g * LANE, (g + 1) * LANE)
        ys.append(_dot(d.astype(bf16), wpool_ref[g]) * spool_ref[:, sl])
    width = LANE * len(ys)
    ssq = sum(jnp.sum(y * y, axis=-1, keepdims=True) for y in ys)
    scale = lax.rsqrt(ssq * (1.0 / width) + RMS_EPS)
    for g, y in enumerate(ys):
        sl = slice(g * LANE, (g + 1) * LANE)
        o_ref[:, sl] = (y * scale * gout_ref[:, sl]).astype(o_ref.dtype)


def _pool_prompt_body(u_ref, wpool_ref, spool_ref, gout_ref, o_ref, buf, *, ts):
    j = pl.program_id(1)
    halo = POOL_STATE_LEN + 1

    @pl.when(j == 0)
    def _():
        buf[0:halo, :] = jnp.zeros((halo, buf.shape[1]), f32)

    @pl.when(j > 0)
    def _():
        buf[0:halo, :] = buf[ts:ts + halo, :]

    buf[halo:halo + ts, :] = u_ref[...]
    pos = j * ts + lax.broadcasted_iota(jnp.int32, (ts, 1), 0)
    diffs = []
    for g, wdw in enumerate(POOL_WINDOWS):
        sl = slice(g * LANE, (g + 1) * LANE)
        x = buf[halo:halo + ts, sl]
        win = x
        for k in range(1, wdw):
            win = win + buf[halo - k:halo - k + ts, sl]
        cnt = jnp.minimum(wdw, pos + 1).astype(f32)
        diffs.append(win / cnt - x)
    _pool_finish(diffs, wpool_ref, spool_ref, gout_ref, o_ref)


def _pool_prompt(u, B, S, w, ts):
    M, W = u.shape
    ns = S // ts
    const2 = lambda a: pl.BlockSpec(a.shape, lambda b, j: (0,) * a.ndim)
    return pl.pallas_call(
        functools.partial(_pool_prompt_body, ts=ts),
        grid=(B, ns),
        in_specs=[pl.BlockSpec((ts, W), lambda b, j: (b * ns + j, 0)),
                  const2(w['w_pool']), const2(w['s_pool']), const2(w['g_out_pool'])],
        out_specs=pl.BlockSpec((ts, W), lambda b, j: (b * ns + j, 0)),
        out_shape=jax.ShapeDtypeStruct((M, W), bf16),
        scratch_shapes=[pltpu.VMEM((ts + POOL_STATE_LEN + 1, W), f32)],
        compiler_params=_params("arbitrary", "arbitrary"),
        name="pool_prompt",
    )(u, w['w_pool'], w['s_pool'], w['g_out_pool'])


def _pool_sample_body(u_ref, st_ref, wpool_ref, spool_ref, gout_ref, o_ref, *, pos):
    diffs = []
    for g, wdw in enumerate(POOL_WINDOWS):
        sl = slice(g * LANE, (g + 1) * LANE)
        x = u_ref[:, sl]
        win = x
        for k in range(1, wdw):
            win = win + st_ref[POOL_STATE_LEN - k, :, sl]
        diffs.append(win / float(min(wdw, pos + 1)) - x)
    _pool_finish(diffs, wpool_ref, spool_ref, gout_ref, o_ref)


def _pool_sample(u, st_t, pos, w):
    M, W = u.shape
    return pl.pallas_call(
        functools.partial(_pool_sample_body, pos=pos),
        out_shape=jax.ShapeDtypeStruct((M, W), bf16),
        compiler_params=pltpu.CompilerParams(vmem_limit_bytes=VMEM_LIMIT),
        name="pool_sample",
    )(u, st_t, w['w_pool'], w['s_pool'], w['g_out_pool'])


def _flash_body(q_ref, k_ref, v_ref, o_ref, m_scr, l_scr, acc_scr, *, tq):
    i = pl.program_id(2)
    rows = lax.broadcasted_iota(jnp.int32, (tq, tq), 0)
    cols = lax.broadcasted_iota(jnp.int32, (tq, tq), 1)
    m_scr[...] = jnp.full(m_scr.shape, -1e30, f32)
    l_scr[...] = jnp.zeros(l_scr.shape, f32)
    acc_scr[...] = jnp.zeros(acc_scr.shape, f32)

    def step(j, masked):
        start = pl.multiple_of(j * tq, tq)
        v = v_ref[pl.ds(start, tq), :]
        for hh in range(2):
            sl = slice(hh * LANE, (hh + 1) * LANE)
            k = k_ref[pl.ds(start, tq), sl]
            s = lax.dot_general(q_ref[:, sl], k, _NT, preferred_element_type=f32)
            if masked:
                s = jnp.where(cols <= rows, s, -1e30)
            m = m_scr[hh]
            m_new = jnp.maximum(m, jnp.max(s, axis=-1, keepdims=True))
            alpha = jnp.exp(m - m_new)
            p = jnp.exp(s - m_new)
            l_scr[hh] = l_scr[hh] * alpha + jnp.sum(p, axis=-1, keepdims=True)
            acc_scr[hh] = acc_scr[hh] * alpha + _dot(p.astype(bf16), v)
            m_scr[hh] = m_new

    def body(j, carry):
        step(j, False)
        return carry

    lax.fori_loop(0, i, body, 0)
    step(i, True)
    lane = lax.broadcasted_iota(jnp.int32, (tq, LANE), 1)
    o_ref[...] = jnp.where(lane < V_HEAD_DIM, acc_scr[0] / l_scr[0], acc_scr[1] / l_scr[1])


def _flash(q, k, v, B, S, tq):
    M = q.shape[0]
    nq = S // tq
    pairs = N_HEADS // 2
    return pl.pallas_call(
        functools.partial(_flash_body, tq=tq),
        grid=(B, pairs, nq),
        in_specs=[pl.BlockSpec((tq, 2 * LANE), lambda b, g, i: (b * nq + i, g)),
                  pl.BlockSpec((S, 2 * LANE), lambda b, g, i: (b, g)),
                  pl.BlockSpec((S, 2 * V_HEAD_DIM), lambda b, g, i: (b, g))],
        out_specs=pl.BlockSpec((tq, 2 * V_HEAD_DIM), lambda b, g, i: (b * nq + i, g)),
        out_shape=jax.ShapeDtypeStruct((M, N_HEADS * V_HEAD_DIM), f32),
        scratch_shapes=[pltpu.VMEM((2, tq, 1), f32), pltpu.VMEM((2, tq, 1), f32),
                        pltpu.VMEM((2, tq, 2 * V_HEAD_DIM), f32)],
        compiler_params=_params("parallel", "parallel", "arbitrary"),
        name="flash_prompt",
    )(q, k, v)


def _mm_body(a_ref, b_ref, o_ref):
    o_ref[...] = _dot(a_ref[...].astype(bf16), b_ref[...])


def _mm(a, b, name):
    return pl.pallas_call(
        _mm_body,
        out_shape=jax.ShapeDtypeStruct((a.shape[0], b.shape[1]), f32),
        compiler_params=pltpu.CompilerParams(vmem_limit_bytes=VMEM_LIMIT),
        name=name,
    )(a, b)


def _decode_body(pt_ref, q_ref, knew_ref, qabs_ref, qpe_ref, cnew_ref, wukt_ref, cache_ckv, cache_kpet,
                 o_ref, ckv_buf, kpe_buf, sems, lhs, m_scr, l_scr, acc_scr, *, n_pages, cp):
    b = pl.program_id(0)
    c = pl.program_id(1)
    nc = pl.num_programs(1)
    total = pl.num_programs(0) * nc
    flat = b * nc + c
    slot = flat % 2
    nope_rows = wukt_ref.shape[0]
    page = cache_ckv.shape[1]

    def copies(flat_idx, slot_idx):
        base = (flat_idx // nc) * n_pages + (flat_idx % nc) * cp
        out = []
        for p in range(cp):
            phys = pt_ref[base + p]
            rows = pl.ds(p * page, page)
            out.append(pltpu.make_async_copy(cache_ckv.at[phys], ckv_buf.at[slot_idx, rows], sems.at[0, slot_idx]))
            out.append(pltpu.make_async_copy(cache_kpet.at[phys], kpe_buf.at[slot_idx, p], sems.at[1, slot_idx]))
        return out

    @pl.when(flat == 0)
    def _():
        for cpy in copies(flat, slot):
            cpy.start()

    @pl.when(flat + 1 < total)
    def _():
        for cpy in copies(flat + 1, 1 - slot):
            cpy.start()

    @pl.when(c == 0)
    def _():
        lhs[0:nope_rows, :] = wukt_ref[...]
        lhs[nope_rows:nope_rows + BF16_SUBLANES, :] = qabs_ref[0]
        s_new = jnp.sum(q_ref[0].astype(f32) * knew_ref[0].astype(f32), axis=-1, keepdims=True)
        m_scr[...] = s_new
        l_scr[...] = jnp.ones_like(l_scr)
        acc_scr[...] = jnp.broadcast_to(cnew_ref[0], acc_scr.shape)

    for cpy in copies(flat, slot):
        cpy.wait()

    ct = ckv_buf[slot].astype(bf16)
    kt = lax.dot_general(lhs[...], ct, _NT, preferred_element_type=f32)
    tokens = ct.shape[0]
    kn = kt[0:nope_rows].reshape(N_HEADS, nope_rows // N_HEADS, tokens)
    ssq = jnp.sum(kn * kn, axis=1)
    num = kt[nope_rows:nope_rows + N_HEADS]
    qpe = qpe_ref[0]
    ones = jnp.ones(qpe.shape, bf16)
    pe_num, pe_ssq = [], []
    for p in range(cp):
        kp = kpe_buf[slot, p]
        pe_num.append(_dot(qpe, kp.astype(bf16)))
        pe_ssq.append(_dot(ones, (kp * kp).astype(bf16)))
    num = num + jnp.concatenate(pe_num, axis=1)
    ssq = ssq + jnp.concatenate(pe_ssq, axis=1)
    s = num * lax.rsqrt(ssq * (1.0 / QK_HEAD_DIM) + RMS_EPS)
    m = m_scr[...]
    m_new = jnp.maximum(m, jnp.max(s, axis=-1, keepdims=True))
    alpha = jnp.exp(m - m_new)
    p = jnp.exp(s - m_new)
    l_scr[...] = l_scr[...] * alpha + jnp.sum(p, axis=-1, keepdims=True)
    acc_scr[...] = acc_scr[...] * alpha + _dot(p.astype(bf16), ct)
    m_scr[...] = m_new

    @pl.when(c == nc - 1)
    def _():
        o_ref[0] = acc_scr[...] / l_scr[...]


def _decode(page_table, q3, knew3, qabs, qpe, cnew, wukt, cache_ckv, cache_kpet, cp):
    DB, n_pages = page_table.shape
    page, kv_rank = cache_ckv.shape[1:]
    rope = cache_kpet.shape[1]
    nc = n_pages // cp
    T = cp * page
    per_seq = lambda a: pl.BlockSpec((1,) + a.shape[1:], lambda b, c, pt: (b,) + (0,) * (a.ndim - 1))
    grid_spec = pltpu.PrefetchScalarGridSpec(
        num_scalar_prefetch=1,
        grid=(DB, nc),
        in_specs=[per_seq(q3), per_seq(knew3), per_seq(qabs), per_seq(qpe), per_seq(cnew),
                  pl.BlockSpec(wukt.shape, lambda b, c, pt: (0, 0)),
                  pl.BlockSpec(memory_space=pl.ANY), pl.BlockSpec(memory_space=pl.ANY)],
        out_specs=pl.BlockSpec((1, N_HEADS, kv_rank), lambda b, c, pt: (b, 0, 0)),
        scratch_shapes=[pltpu.VMEM((2, T, kv_rank), f32), pltpu.VMEM((2, cp, rope, page), f32),
                        pltpu.SemaphoreType.DMA((2, 2)),
                        pltpu.VMEM((wukt.shape[0] + BF16_SUBLANES, kv_rank), bf16),
                        pltpu.VMEM((N_HEADS, 1), f32), pltpu.VMEM((N_HEADS, 1), f32),
                        pltpu.VMEM((N_HEADS, kv_rank), f32)])
    return pl.pallas_call(
        functools.partial(_decode_body, n_pages=n_pages, cp=cp),
        grid_spec=grid_spec,
        out_shape=jax.ShapeDtypeStruct((DB, N_HEADS, kv_rank), f32),
        compiler_params=_params("arbitrary", "arbitrary"),
        name="decode_attn",
    )(page_table.reshape(-1), q3, knew3, qabs, qpe, cnew, wukt, cache_ckv, cache_kpet)


def _outproj_body(x_ref, mp_ref, attn_ref, gatt_ref, wo_ref, gffn_ref, x1_ref, h2_ref):
    pw = mp_ref.shape[1]
    attn = attn_ref[...]
    na = (_rms(attn, attn.shape[-1]) * gatt_ref[...]).astype(bf16)
    x1 = x_ref[...] + _dot(mp_ref[...], wo_ref[0:pw, :]) + _dot(na, wo_ref[pw:, :])
    x1_ref[...] = x1
    h2_ref[...] = (_rms(x1, x1.shape[-1]) * gffn_ref[...]).astype(bf16)


def _outproj(x, mp, attn, w, tm):
    M, D = x.shape
    const = lambda a: pl.BlockSpec(a.shape, lambda i: (0, 0))
    row = lambda n: pl.BlockSpec((tm, n), lambda i: (i, 0))
    return pl.pallas_call(
        _outproj_body,
        grid=(M // tm,),
        in_specs=[row(D), row(mp.shape[1]), row(attn.shape[1]),
                  const(w['g_out_attn']), const(w['w_o']), const(w['g_ffn'])],
        out_specs=(row(D), row(D)),
        out_shape=(jax.ShapeDtypeStruct((M, D), f32), jax.ShapeDtypeStruct((M, D), bf16)),
        compiler_params=_params("parallel"),
        name="out_proj",
    )(x, mp, attn, w['g_out_attn'], w['w_o'], w['g_ffn'])


def _ffn_finish(f, g, g1, g2, h2, x1_ref, wu_ref, wd_ref, cw_ref, cb_ref, y_ref, acc):
    gate = cb_ref[...] + g2 * cw_ref[0:1, :] + g1 * cw_ref[1:2, :] + g * cw_ref[2:3, :]
    act = (gate * jax.nn.sigmoid(gate)) * _dot(h2, wu_ref[...])
    part = _dot(act.astype(bf16), wd_ref[...])

    @pl.when(f == 0)
    def _():
        acc[...] = x1_ref[...] + part

    @pl.when(f > 0)
    def _():
        acc[...] += part

    @pl.when(f == pl.num_programs(1) - 1)
    def _():
        y_ref[...] = acc[...]


def _ffn_prompt_body(h2_ref, halo_ref, x1_ref, wg_ref, wu_ref, wd_ref, cw_ref, cb_ref,
                     y_ref, tail_ref, acc, gbuf, *, tm, tiles_per_seq):
    i = pl.program_id(0)
    f = pl.program_id(1)
    hb = BF16_SUBLANES
    h2 = h2_ref[...]
    g = _dot(h2, wg_ref[...])
    gh = _dot(halo_ref[...], wg_ref[...])
    gbuf[0:hb, :] = jnp.where(i % tiles_per_seq == 0, 0.0, gh)
    gbuf[hb:hb + tm, :] = g
    tail_ref[0] = gbuf[tm:tm + hb, :]
    g1 = gbuf[hb - 1:hb - 1 + tm, :]
    g2 = gbuf[hb - 2:hb - 2 + tm, :]
    _ffn_finish(f, g, g1, g2, h2, x1_ref, wu_ref, wd_ref, cw_ref, cb_ref, y_ref, acc)


def _ffn_prompt(h2, x1, S, w, tm, tf):
    M, D = x1.shape
    F = w['w_gate'].shape[1]
    hb = BF16_SUBLANES
    nt = M // tm
    halo_blocks = tm // hb
    y, tail = pl.pallas_call(
        functools.partial(_ffn_prompt_body, tm=tm, tiles_per_seq=S // tm),
        grid=(nt, F // tf),
        in_specs=[pl.BlockSpec((tm, D), lambda i, f: (i, 0)),
                  pl.BlockSpec((hb, D), lambda i, f: (jnp.maximum(i * halo_blocks - 1, 0), 0)),
                  pl.BlockSpec((tm, D), lambda i, f: (i, 0)),
                  pl.BlockSpec((D, tf), lambda i, f: (0, f)),
                  pl.BlockSpec((D, tf), lambda i, f: (0, f)),
                  pl.BlockSpec((tf, D), lambda i, f: (f, 0)),
                  pl.BlockSpec((CONV_WIDTH, tf), lambda i, f: (0, f)),
                  pl.BlockSpec((1, tf), lambda i, f: (0, f))],
        out_specs=(pl.BlockSpec((tm, D), lambda i, f: (i, 0)),
                   pl.BlockSpec((1, hb, tf), lambda i, f: (i, 0, f))),
        out_shape=(jax.ShapeDtypeStruct((M, D), f32), jax.ShapeDtypeStruct((nt, hb, F), f32)),
        scratch_shapes=[pltpu.VMEM((tm, D), f32), pltpu.VMEM((tm + hb, tf), f32)],
        compiler_params=_params("parallel", "arbitrary"),
        name="ffn_prompt",
    )(h2, h2, x1, w['w_gate'], w['w_up'], w['w_down'], w['conv_w'], w['conv_b'])
    return y, tail


def _ffn_sample_body(h2_ref, st_ref, x1_ref, wg_ref, wu_ref, wd_ref, cw_ref, cb_ref, y_ref, g_ref, acc):
    f = pl.program_id(1)
    h2 = h2_ref[...]
    g = _dot(h2, wg_ref[...])
    g_ref[...] = g
    _ffn_finish(f, g, st_ref[1], st_ref[0], h2, x1_ref, wu_ref, wd_ref, cw_ref, cb_ref, y_ref, acc)


def _ffn_sample(h2, x1, st_t, w, tf):
    M, D = x1.shape
    F = w['w_gate'].shape[1]
    return pl.pallas_call(
        _ffn_sample_body,
        grid=(1, F // tf),
        in_specs=[pl.BlockSpec((M, D), lambda i, f: (0, 0)),
                  pl.BlockSpec((CONV_WIDTH - 1, M, tf), lambda i, f: (0, 0, f)),
                  pl.BlockSpec((M, D), lambda i, f: (0, 0)),
                  pl.BlockSpec((D, tf), lambda i, f: (0, f)),
                  pl.BlockSpec((D, tf), lambda i, f: (0, f)),
                  pl.BlockSpec((tf, D), lambda i, f: (f, 0)),
                  pl.BlockSpec((CONV_WIDTH, tf), lambda i, f: (0, f)),
                  pl.BlockSpec((1, tf), lambda i, f: (0, f))],
        out_specs=(pl.BlockSpec((M, D), lambda i, f: (0, 0)),
                   pl.BlockSpec((M, tf), lambda i, f: (0, f))),
        out_shape=(jax.ShapeDtypeStruct((M, D), f32), jax.ShapeDtypeStruct((M, F), f32)),
        scratch_shapes=[pltpu.VMEM((M, D), f32)],
        compiler_params=_params("parallel", "arbitrary"),
        name="ffn_sample",
    )(h2, st_t, x1, w['w_gate'], w['w_up'], w['w_down'], w['conv_w'], w['conv_b'])


def _qk_gain(g):
    g_pe = g[QK_NOPE_DIM:]
    return jnp.concatenate([g[:QK_NOPE_DIM], g_pe, g_pe])


def _rope_table(pos):
    half = QK_ROPE_DIM // 2
    inv = ROPE_BASE ** (-jnp.arange(0, QK_ROPE_DIM, 2, dtype=f32) / QK_ROPE_DIM)
    ang = pos.astype(f32)[:, None] * inv[None, :]
    cos, sin = jnp.cos(ang), jnp.sin(ang)
    P = pos.shape[0]
    z = lambda n: jnp.zeros((P, n), f32)
    tail = LANE - QK_HEAD_DIM
    c = jnp.concatenate([jnp.ones((P, QK_NOPE_DIM), f32), cos, cos, z(tail)], axis=1)
    s_lo = jnp.concatenate([z(QK_NOPE_DIM), -sin, z(half), z(tail)], axis=1)
    s_hi = jnp.concatenate([z(QK_NOPE_DIM), z(half), sin, z(tail)], axis=1)
    return jnp.concatenate([c, s_lo, s_hi], axis=1)


def _prep_weights(g_mix, w_in, g_q_a, w_uq, g_qn, g_kv_a, w_uk, g_kn, w_uv, w_pool, s_pool,
                  g_out, w_o, g_ffn, w_gate, w_up, conv_w, conv_b, w_down):
    D = w_in.shape[0]
    q_rank = g_q_a.shape[0]
    kv_rank = g_kv_a.shape[0]
    pool_w = s_pool.shape[0]
    head_pad = LANE - QK_HEAD_DIM
    main = pool_w + q_rank + kv_rank
    w_in_p = jnp.concatenate([w_in[:, :main], jnp.zeros((D, QK_NOPE_DIM), f32), w_in[:, main:],
                              jnp.zeros((D, head_pad), f32)], axis=1).astype(bf16)
    w_uq_p = jnp.pad(w_uq.reshape(q_rank, N_HEADS, QK_HEAD_DIM), ((0, 0), (0, 0), (0, head_pad)))
    w_uq_p = w_uq_p.reshape(q_rank, N_HEADS * LANE).astype(bf16)
    gq = _qk_gain(g_qn) * _qk_gain(g_kn) * (QK_HEAD_DIM ** -0.5)
    gq = jnp.tile(jnp.pad(gq, (0, head_pad)), N_HEADS)[None, :]
    w_uk_p = jnp.pad(w_uk, ((0, 0), (0, 0), (0, LANE - QK_NOPE_DIM))).reshape(kv_rank, N_HEADS * LANE)
    w_ukv = jnp.concatenate([w_uk_p, w_uv.reshape(kv_rank, N_HEADS * V_HEAD_DIM)], axis=1).astype(bf16)
    w_uk_t = w_uk.reshape(kv_rank, N_HEADS * QK_NOPE_DIM).T.astype(bf16)
    eye = jnp.eye(N_HEADS, dtype=f32)
    w_uk_bd = jnp.einsum('chd,hg->hdgc', jnp.pad(w_uk, ((0, 0), (0, 0), (0, LANE - QK_NOPE_DIM))), eye)
    w_uk_bd = w_uk_bd.reshape(N_HEADS * LANE, N_HEADS * kv_rank).astype(bf16)
    w_uv_bd = jnp.einsum('chv,hg->hcgv', w_uv, eye).reshape(N_HEADS * kv_rank, N_HEADS * V_HEAD_DIM).astype(bf16)
    return {
        'pool_w': pool_w,
        'g_mix': g_mix[None, :], 'w_in': w_in_p, 'g_q_a': g_q_a[None, :], 'w_uq': w_uq_p, 'g_q': gq,
        'g_kv_a': g_kv_a[None, :], 'w_ukv': w_ukv, 'w_uk_t': w_uk_t, 'w_uk_bd': w_uk_bd, 'w_uv_bd': w_uv_bd,
        'w_pool': w_pool.astype(bf16), 's_pool': s_pool[None, :],
        'g_out_pool': g_out[None, :pool_w], 'g_out_attn': g_out[None, pool_w:],
        'w_o': w_o.astype(bf16), 'g_ffn': g_ffn[None, :],
        'w_gate': w_gate.astype(bf16), 'w_up': w_up.astype(bf16), 'w_down': w_down.astype(bf16),
        'conv_w': conv_w, 'conv_b': conv_b[None, :],
    }


def _tile(n, pref):
    t = min(n, pref)
    assert n % t == 0, (n, pref)
    return t


def _ffn_cols(F):
    half = F // 2
    return half if half % LANE == 0 else F


def kernel(x_prompt, x_sample, cache_ckv, cache_kpe, state_pool, state_conv, page_table, g_mix, w_in, g_q_a,
           w_uq, g_qn, g_kv_a, w_uk, g_kn, w_uv, w_pool, s_pool, g_out, w_o, g_ffn, w_gate, w_up, conv_w,
           conv_b, w_down):
    depth = g_mix.shape[0]
    assert depth == 1, "single-layer trunk only"
    B, S, D = x_prompt.shape
    DB, DS = x_sample.shape[:2]
    assert DS == 1, "one new token per sequence"
    n_pages = page_table.shape[1]
    page = cache_ckv.shape[2]
    past_len = n_pages * page
    l = 0
    w = _prep_weights(g_mix[l], w_in[l], g_q_a[l], w_uq[l], g_qn[l], g_kv_a[l], w_uk[l], g_kn[l], w_uv[l],
                      w_pool[l], s_pool[l], g_out[l], w_o[l], g_ffn[l], w_gate[l], w_up[l], conv_w[l],
                      conv_b[l], w_down[l])
    kv_rank = g_kv_a.shape[1]
    F = w_gate.shape[2]
    tf = _ffn_cols(F)

    xp = x_prompt.reshape(B * S, D)
    tm = _tile(S, 256)
    rope_p = _rope_table(jnp.arange(S, dtype=jnp.int32))
    u_p, q_p, ckv_p, kpe_p, k_p, v_p = _proj(xp, rope_p, S // tm, w, tm)
    mp_p = _pool_prompt(u_p, B, S, w, _tile(S, 512))
    attn_p = _flash(q_p, k_p, v_p, B, S, _tile(S, 512))
    x1_p, h2_p = _outproj(xp, mp_p, attn_p, w, _tile(S, 512))
    tmf = _tile(S, 512)
    y_p, tail = _ffn_prompt(h2_p, x1_p, S, w, tmf, tf)
    tiles_per_seq = S // tmf
    conv_p = tail[tiles_per_seq - 1::tiles_per_seq, BF16_SUBLANES - (CONV_WIDTH - 1):, :]

    xs = x_sample.reshape(DB, D)
    pos_s = past_len + jnp.arange(DS, dtype=jnp.int32)
    rope_s = jnp.tile(_rope_table(pos_s), (DB, 1))
    u_s, q_s, ckv_s, kpe_s, k_s, _ = _proj(xs, rope_s, 1, w, DB)
    st_pool_t = jnp.transpose(state_pool[l], (1, 0, 2))
    mp_s = _pool_sample(u_s, st_pool_t, past_len, w)
    qabs = _mm(q_s, w['w_uk_bd'], "absorb_q").reshape(DB, N_HEADS, kv_rank).astype(bf16)
    qabs = jnp.pad(qabs, ((0, 0), (0, BF16_SUBLANES - N_HEADS), (0, 0)))
    q3 = q_s.reshape(DB, N_HEADS, LANE)
    qpe = q3[:, :, QK_NOPE_DIM:QK_HEAD_DIM]
    cp = _tile(n_pages, 16)
    ctx = _decode(page_table, q3, k_s.reshape(DB, N_HEADS, LANE), qabs, qpe, ckv_s.reshape(DB, 1, kv_rank),
                  w['w_uk_t'], cache_ckv[l], jnp.swapaxes(cache_kpe[l], 1, 2), cp)
    attn_s = _mm(ctx.reshape(DB, N_HEADS * kv_rank), w['w_uv_bd'], "value_up")
    x1_s, h2_s = _outproj(xs, mp_s, attn_s, w, DB)
    st_conv_t = jnp.transpose(state_conv[l], (1, 0, 2))
    y_s, g_s = _ffn_sample(h2_s, x1_s, st_conv_t, w, tf)

    kpe_sl = slice(QK_NOPE_DIM, QK_HEAD_DIM)
    P = POOL_STATE_LEN
    return (
        y_p.reshape(B, S, D),
        y_s.reshape(DB, DS, D),
        ckv_p.reshape(1, B, S, kv_rank),
        kpe_p[:, kpe_sl].reshape(1, B, S, QK_ROPE_DIM),
        u_p.reshape(B, S, -1)[None, :, S - P:, :],
        conv_p[None],
        ckv_s.reshape(1, DB, DS, kv_rank),
        kpe_s[:, kpe_sl].reshape(1, DB, DS, QK_ROPE_DIM),
        jnp.concatenate([state_pool[l], u_s[:, None, :]], axis=1)[None, :, -P:, :],
        jnp.concatenate([state_conv[l], g_s[:, None, :]], axis=1)[None, :, -(CONV_WIDTH - 1):, :],
    )
```

```python
import functools

import jax
import jax.numpy as jnp
from jax import lax
from jax.experimental import pallas as pl
from jax.experimental.pallas import tpu as pltpu

N_HEADS = 8
QK_NOPE_DIM = 64
QK_ROPE_DIM = 32
QK_HEAD_DIM = QK_NOPE_DIM + QK_ROPE_DIM
V_HEAD_DIM = 64
POOL_WINDOWS = (2, 4, 8, 16)
POOL_STATE_LEN = max(POOL_WINDOWS) - 1
CONV_WIDTH = 3
ROPE_BASE = 10000.0
RMS_EPS = 1e-6
LOG2_E = 1.4426950408889634

LANE = 128
BF16_SUBLANES = 16
VMEM_LIMIT = 48 * 1024 * 1024

_NT = (((1,), (1,)), ((), ()))

bf16 = jnp.bfloat16
f32 = jnp.float32


def _rms(x, width):
    return x * lax.rsqrt(jnp.sum(x * x, axis=-1, keepdims=True) * (1.0 / width) + RMS_EPS)


def _dot(a, b):
    return jnp.dot(a, b, preferred_element_type=f32)


def _params(*sem):
    return pltpu.CompilerParams(dimension_semantics=sem, vmem_limit_bytes=VMEM_LIMIT)


def _proj_body(x_ref, rope_ref, gmix_ref, win_ref, gqa_ref, wuq_ref, gq_ref, gkva_ref, wuk_ref, wuvt_ref,
               u_ref, q_ref, ckv_ref, kpe_ref, k_ref, vt_ref, *, pool_w, q_rank, kv_rank):
    x = x_ref[...]
    h = (_rms(x, x.shape[-1]) * gmix_ref[...]).astype(bf16)
    proj = _dot(h, win_ref[...])
    u_ref[...] = proj[:, :pool_w]
    o = pool_w
    cq = (_rms(proj[:, o:o + q_rank], q_rank) * gqa_ref[...]).astype(bf16)
    o += q_rank
    ckv = _rms(proj[:, o:o + kv_rank], kv_rank) * gkva_ref[...]
    o += kv_rank
    ckv_ref[...] = ckv
    cos = rope_ref[:, 0:LANE]
    sin_lo = rope_ref[:, LANE:2 * LANE]
    sin_hi = rope_ref[:, 2 * LANE:3 * LANE]

    def rope(t):
        half = QK_ROPE_DIM // 2
        return t * cos + pltpu.roll(t, LANE - half, 1) * sin_lo + pltpu.roll(t, half, 1) * sin_hi

    kpe = rope(proj[:, o:o + LANE])
    kpe_ref[...] = kpe
    qraw = _dot(cq, wuq_ref[...])
    ckv_b = ckv.astype(bf16)
    kn = _dot(ckv_b, wuk_ref[...])
    vt_ref[0] = lax.dot_general(wuvt_ref[...], ckv_b, _NT, preferred_element_type=f32).astype(bf16)
    for hd in range(N_HEADS):
        sl = slice(hd * LANE, (hd + 1) * LANE)
        qh = rope(qraw[:, sl])
        q_ref[:, sl] = (_rms(qh, QK_HEAD_DIM) * gq_ref[:, sl]).astype(bf16)
        kh = kn[:, sl] + kpe
        k_ref[:, sl] = _rms(kh, QK_HEAD_DIM).astype(bf16)


def _proj(x, rope, n_rope_blocks, w, tm):
    M, D = x.shape
    pool_w = w['pool_w']
    q_rank = w['g_q_a'].shape[1]
    kv_rank = w['g_kv_a'].shape[1]
    hq = N_HEADS * LANE
    hv = N_HEADS * V_HEAD_DIM
    const = lambda a: pl.BlockSpec(a.shape, lambda i: (0, 0))
    row = lambda n: pl.BlockSpec((tm, n), lambda i: (i, 0))
    out_shapes = (
        jax.ShapeDtypeStruct((M, pool_w), f32), jax.ShapeDtypeStruct((M, hq), bf16),
        jax.ShapeDtypeStruct((M, kv_rank), f32), jax.ShapeDtypeStruct((M, LANE), f32),
        jax.ShapeDtypeStruct((M, hq), bf16), jax.ShapeDtypeStruct((M // tm, hv, tm), bf16))
    return pl.pallas_call(
        functools.partial(_proj_body, pool_w=pool_w, q_rank=q_rank, kv_rank=kv_rank),
        grid=(M // tm,),
        in_specs=[row(D), pl.BlockSpec((tm, 3 * LANE), lambda i: (i % n_rope_blocks, 0)),
                  const(w['g_mix']), const(w['w_in']), const(w['g_q_a']), const(w['w_uq']),
                  const(w['g_q']), const(w['g_kv_a']), const(w['w_uk']), const(w['w_uv_t'])],
        out_specs=(row(pool_w), row(hq), row(kv_rank), row(LANE), row(hq),
                   pl.BlockSpec((1, hv, tm), lambda i: (i, 0, 0))),
        out_shape=out_shapes,
        compiler_params=_params("parallel"),
        name="in_proj",
    )(x, rope, w['g_mix'], w['w_in'], w['g_q_a'], w['w_uq'], w['g_q'], w['g_kv_a'], w['w_uk'], w['w_uv_t'])


def _pool_finish(diffs, wpool_ref, spool_ref, gout_ref, o_ref):
    ys = []
    for g, d in enumerate(diffs):
        sl = slice(g * LANE, (g + 1) * LANE)
        ys.append(_dot(d.astype(bf16), wpool_ref[g]) * spool_ref[:, sl])
    width = LANE * len(ys)
    ssq = sum(jnp.sum(y * y, axis=-1, keepdims=True) for y in ys)
    scale = lax.rsqrt(ssq * (1.0 / width) + RMS_EPS)
    for g, y in enumerate(ys):
        sl = slice(g * LANE, (g + 1) * LANE)
        o_ref[:, sl] = (y * scale * gout_ref[:, sl]).astype(o_ref.dtype)


def _pool_prompt_body(u_ref, wpool_ref, spool_ref, gout_ref, o_ref, buf, *, ts):
    j = pl.program_id(1)
    halo = POOL_STATE_LEN + 1

    @pl.when(j == 0)
    def _():
        buf[0:halo, :] = jnp.zeros((halo, buf.shape[1]), f32)

    @pl.when(j > 0)
    def _():
        buf[0:halo, :] = buf[ts:ts + halo, :]

    buf[halo:halo + ts, :] = u_ref[...]
    pos = j * ts + lax.broadcasted_iota(jnp.int32, (ts, 1), 0)
    diffs = []
    for g, wdw in enumerate(POOL_WINDOWS):
        sl = slice(g * LANE, (g + 1) * LANE)
        x = buf[halo:halo + ts, sl]
        win = x
        for k in range(1, wdw):
            win = win + buf[halo - k:halo - k + ts, sl]
        cnt = jnp.minimum(wdw, pos + 1).astype(f32)
        diffs.append(win / cnt - x)
    _pool_finish(diffs, wpool_ref, spool_ref, gout_ref, o_ref)


def _pool_prompt(u, B, S, w, ts):
    M, W = u.shape
    ns = S // ts
    const2 = lambda a: pl.BlockSpec(a.shape, lambda b, j: (0,) * a.ndim)
    return pl.pallas_call(
        functools.partial(_pool_prompt_body, ts=ts),
        grid=(B, ns),
        in_specs=[pl.BlockSpec((ts, W), lambda b, j: (b * ns + j, 0)),
                  const2(w['w_pool']), const2(w['s_pool']), const2(w['g_out_pool'])],
        out_specs=pl.BlockSpec((ts, W), lambda b, j: (b * ns + j, 0)),
        out_shape=jax.ShapeDtypeStruct((M, W), bf16),
        scratch_shapes=[pltpu.VMEM((ts + POOL_STATE_LEN + 1, W), f32)],
        compiler_params=_params("arbitrary", "arbitrary"),
        name="pool_prompt",
    )(u, w['w_pool'], w['s_pool'], w['g_out_pool'])


def _pool_sample_body(u_ref, st_ref, wpool_ref, spool_ref, gout_ref, o_ref, *, pos):
    diffs = []
    for g, wdw in enumerate(POOL_WINDOWS):
        sl = slice(g * LANE, (g + 1) * LANE)
        x = u_ref[:, sl]
        win = x
        for k in range(1, wdw):
            win = win + st_ref[POOL_STATE_LEN - k, :, sl]
        diffs.append(win / float(min(wdw, pos + 1)) - x)
    _pool_finish(diffs, wpool_ref, spool_ref, gout_ref, o_ref)


def _pool_sample(u, st_t, pos, w):
    M, W = u.shape
    return pl.pallas_call(
        functools.partial(_pool_sample_body, pos=pos),
        out_shape=jax.ShapeDtypeStruct((M, W), bf16),
        compiler_params=pltpu.CompilerParams(vmem_limit_bytes=VMEM_LIMIT),
        name="pool_sample",
    )(u, st_t, w['w_pool'], w['s_pool'], w['g_out_pool'])


def _flash_body(q_ref, k_ref, vt_ref, o_ref, m_scr, l_scr, acc_scr, *, tq):
    i = pl.program_id(2)
    key_idx = lax.broadcasted_iota(jnp.int32, (tq, tq), 0)
    qry_idx = lax.broadcasted_iota(jnp.int32, (tq, tq), 1)
    m_scr[...] = jnp.full(m_scr.shape, -1e30, f32)
    l_scr[...] = jnp.zeros(l_scr.shape, f32)
    acc_scr[...] = jnp.zeros(acc_scr.shape, f32)

    def step(j, masked):
        start = pl.multiple_of(j * tq, tq)
        for hh in range(2):
            sl = slice(hh * LANE, (hh + 1) * LANE)
            k = k_ref[pl.ds(start, tq), sl]
            st = lax.dot_general(k, q_ref[:, sl], _NT, preferred_element_type=f32)
            if masked:
                st = jnp.where(key_idx <= qry_idx, st, -1e30)
            m = m_scr[hh]
            m_new = jnp.maximum(m, jnp.max(st, axis=0, keepdims=True))
            alpha = jnp.exp2(m - m_new)
            pt = jnp.exp2(st - m_new)
            l_scr[hh] = l_scr[hh] * alpha + jnp.sum(pt, axis=0, keepdims=True)
            vt = vt_ref[j, hh * V_HEAD_DIM:(hh + 1) * V_HEAD_DIM, :]
            acc_scr[hh] = acc_scr[hh] * alpha + _dot(vt, pt.astype(bf16))
            m_scr[hh] = m_new

    def body(j, carry):
        step(j, False)
        return carry

    lax.fori_loop(0, i, body, 0)
    step(i, True)
    out_t = jnp.concatenate([acc_scr[0] / l_scr[0], acc_scr[1] / l_scr[1]], axis=0)
    o_ref[...] = out_t.T


def _flash(q, k, vt, B, S, tq):
    M = q.shape[0]
    nq = S // tq
    pairs = N_HEADS // 2
    return pl.pallas_call(
        functools.partial(_flash_body, tq=tq),
        grid=(B, pairs, nq),
        in_specs=[pl.BlockSpec((tq, 2 * LANE), lambda b, g, i: (b * nq + i, g)),
                  pl.BlockSpec((S, 2 * LANE), lambda b, g, i: (b, g)),
                  pl.BlockSpec((nq, 2 * V_HEAD_DIM, tq), lambda b, g, i: (b, g, 0))],
        out_specs=pl.BlockSpec((tq, 2 * V_HEAD_DIM), lambda b, g, i: (b * nq + i, g)),
        out_shape=jax.ShapeDtypeStruct((M, N_HEADS * V_HEAD_DIM), f32),
        scratch_shapes=[pltpu.VMEM((2, 1, tq), f32), pltpu.VMEM((2, 1, tq), f32),
                        pltpu.VMEM((2, V_HEAD_DIM, tq), f32)],
        compiler_params=_params("parallel", "parallel", "arbitrary"),
        name="flash_prompt",
    )(q, k, vt)


def _mm_body(a_ref, b_ref, o_ref):
    o_ref[...] = _dot(a_ref[...].astype(bf16), b_ref[...])


def _mm(a, b, name):
    return pl.pallas_call(
        _mm_body,
        out_shape=jax.ShapeDtypeStruct((a.shape[0], b.shape[1]), f32),
        compiler_params=pltpu.CompilerParams(vmem_limit_bytes=VMEM_LIMIT),
        name=name,
    )(a, b)


def _decode_body(pt_ref, q_ref, knew_ref, qabs_ref, qpe_ref, cnew_ref, wukt_ref, cache_ckv, cache_kpet,
                 o_ref, ckv_buf, kpe_buf, sems, lhs, m_scr, l_scr, acc_scr, *, n_pages, cp):
    b = pl.program_id(0)
    c = pl.program_id(1)
    nc = pl.num_programs(1)
    total = pl.num_programs(0) * nc
    flat = b * nc + c
    slot = flat % 2
    nope_rows = wukt_ref.shape[0]
    page = cache_ckv.shape[1]

    def copies(flat_idx, slot_idx):
        base = (flat_idx // nc) * n_pages + (flat_idx % nc) * cp
        out = []
        for p in range(cp):
            phys = pt_ref[base + p]
            rows = pl.ds(p * page, page)
            out.append(pltpu.make_async_copy(cache_ckv.at[phys], ckv_buf.at[slot_idx, rows], sems.at[0, slot_idx]))
            out.append(pltpu.make_async_copy(cache_kpet.at[phys], kpe_buf.at[slot_idx, p], sems.at[1, slot_idx]))
        return out

    @pl.when(flat == 0)
    def _():
        for cpy in copies(flat, slot):
            cpy.start()

    @pl.when(flat + 1 < total)
    def _():
        for cpy in copies(flat + 1, 1 - slot):
            cpy.start()

    @pl.when(c == 0)
    def _():
        lhs[0:nope_rows, :] = wukt_ref[...]
        lhs[nope_rows:nope_rows + BF16_SUBLANES, :] = qabs_ref[0]
        s_new = jnp.sum(q_ref[0].astype(f32) * knew_ref[0].astype(f32), axis=-1, keepdims=True)
        m_scr[...] = s_new
        l_scr[...] = jnp.ones_like(l_scr)
        acc_scr[...] = jnp.broadcast_to(cnew_ref[0], acc_scr.shape)

    for cpy in copies(flat, slot):
        cpy.wait()

    ct = ckv_buf[slot].astype(bf16)
    kt = lax.dot_general(lhs[...], ct, _NT, preferred_element_type=f32)
    tokens = ct.shape[0]
    kn = kt[0:nope_rows].reshape(N_HEADS, nope_rows // N_HEADS, tokens)
    ssq = jnp.sum(kn * kn, axis=1)
    num = kt[nope_rows:nope_rows + N_HEADS]
    qpe = qpe_ref[0]
    ones = jnp.ones(qpe.shape, bf16)
    pe_num, pe_ssq = [], []
    for p in range(cp):
        kp = kpe_buf[slot, p]
        pe_num.append(_dot(qpe, kp.astype(bf16)))
        pe_ssq.append(_dot(ones, (kp * kp).astype(bf16)))
    num = num + jnp.concatenate(pe_num, axis=1)
    ssq = ssq + jnp.concatenate(pe_ssq, axis=1)
    s = num * lax.rsqrt(ssq * (1.0 / QK_HEAD_DIM) + RMS_EPS)
    m = m_scr[...]
    m_new = jnp.maximum(m, jnp.max(s, axis=-1, keepdims=True))
    alpha = jnp.exp2(m - m_new)
    p = jnp.exp2(s - m_new)
    l_scr[...] = l_scr[...] * alpha + jnp.sum(p, axis=-1, keepdims=True)
    acc_scr[...] = acc_scr[...] * alpha + _dot(p.astype(bf16), ct)
    m_scr[...] = m_new

    @pl.when(c == nc - 1)
    def _():
        o_ref[0] = acc_scr[...] / l_scr[...]


def _decode(page_table, q3, knew3, qabs, qpe, cnew, wukt, cache_ckv, cache_kpet, cp):
    DB, n_pages = page_table.shape
    page, kv_rank = cache_ckv.shape[1:]
    rope = cache_kpet.shape[1]
    nc = n_pages // cp
    T = cp * page
    per_seq = lambda a: pl.BlockSpec((1,) + a.shape[1:], lambda b, c, pt: (b,) + (0,) * (a.ndim - 1))
    grid_spec = pltpu.PrefetchScalarGridSpec(
        num_scalar_prefetch=1,
        grid=(DB, nc),
        in_specs=[per_seq(q3), per_seq(knew3), per_seq(qabs), per_seq(qpe), per_seq(cnew),
                  pl.BlockSpec(wukt.shape, lambda b, c, pt: (0, 0)),
                  pl.BlockSpec(memory_space=pl.ANY), pl.BlockSpec(memory_space=pl.ANY)],
        out_specs=pl.BlockSpec((1, N_HEADS, kv_rank), lambda b, c, pt: (b, 0, 0)),
        scratch_shapes=[pltpu.VMEM((2, T, kv_rank), f32), pltpu.VMEM((2, cp, rope, page), f32),
                        pltpu.SemaphoreType.DMA((2, 2)),
                        pltpu.VMEM((wukt.shape[0] + BF16_SUBLANES, kv_rank), bf16),
                        pltpu.VMEM((N_HEADS, 1), f32), pltpu.VMEM((N_HEADS, 1), f32),
                        pltpu.VMEM((N_HEADS, kv_rank), f32)])
    return pl.pallas_call(
        functools.partial(_decode_body, n_pages=n_pages, cp=cp),
        grid_spec=grid_spec,
        out_shape=jax.ShapeDtypeStruct((DB, N_HEADS, kv_rank), f32),
        compiler_params=_params("arbitrary", "arbitrary"),
        name="decode_attn",
    )(page_table.reshape(-1), q3, knew3, qabs, qpe, cnew, wukt, cache_ckv, cache_kpet)


def _outproj_body(x_ref, mp_ref, attn_ref, gatt_ref, wo_ref, gffn_ref, x1_ref, h2_ref):
    pw = mp_ref.shape[1]
    attn = attn_ref[...]
    na = (_rms(attn, attn.shape[-1]) * gatt_ref[...]).astype(bf16)
    x1 = x_ref[...] + _dot(mp_ref[...], wo_ref[0:pw, :]) + _dot(na, wo_ref[pw:, :])
    x1_ref[...] = x1
    h2_ref[...] = (_rms(x1, x1.shape[-1]) * gffn_ref[...]).astype(bf16)


def _outproj(x, mp, attn, w, tm):
    M, D = x.shape
    const = lambda a: pl.BlockSpec(a.shape, lambda i: (0, 0))
    row = lambda n: pl.BlockSpec((tm, n), lambda i: (i, 0))
    return pl.pallas_call(
        _outproj_body,
        grid=(M // tm,),
        in_specs=[row(D), row(mp.shape[1]), row(attn.shape[1]),
                  const(w['g_out_attn']), const(w['w_o']), const(w['g_ffn'])],
        out_specs=(row(D), row(D)),
        out_shape=(jax.ShapeDtypeStruct((M, D), f32), jax.ShapeDtypeStruct((M, D), bf16)),
        compiler_params=_params("parallel"),
        name="out_proj",
    )(x, mp, attn, w['g_out_attn'], w['w_o'], w['g_ffn'])


def _ffn_finish(f, g, g1, g2, h2, x1_ref, wu_ref, wd_ref, cw_ref, cb_ref, y_ref, acc):
    gate = cb_ref[...] + g2 * cw_ref[0:1, :] + g1 * cw_ref[1:2, :] + g * cw_ref[2:3, :]
    act = (gate * jax.nn.sigmoid(gate)) * _dot(h2, wu_ref[...])
    part = _dot(act.astype(bf16), wd_ref[...])

    @pl.when(f == 0)
    def _():
        acc[...] = x1_ref[...] + part

    @pl.when(f > 0)
    def _():
        acc[...] += part

    @pl.when(f == pl.num_programs(1) - 1)
    def _():
        y_ref[...] = acc[...]


def _ffn_prompt_body(h2_ref, halo_ref, x1_ref, wg_ref, wu_ref, wd_ref, cw_ref, cb_ref,
                     y_ref, tail_ref, acc, gbuf, *, tm, tiles_per_seq):
    i = pl.program_id(0)
    f = pl.program_id(1)
    hb = BF16_SUBLANES
    h2 = h2_ref[...]
    g = _dot(h2, wg_ref[...])
    gh = _dot(halo_ref[...], wg_ref[...])
    gbuf[0:hb, :] = jnp.where(i % tiles_per_seq == 0, 0.0, gh)
    gbuf[hb:hb + tm, :] = g
    tail_ref[0] = gbuf[tm:tm + hb, :]
    g1 = gbuf[hb - 1:hb - 1 + tm, :]
    g2 = gbuf[hb - 2:hb - 2 + tm, :]
    _ffn_finish(f, g, g1, g2, h2, x1_ref, wu_ref, wd_ref, cw_ref, cb_ref, y_ref, acc)


def _ffn_prompt(h2, x1, S, w, tm, tf):
    M, D = x1.shape
    F = w['w_gate'].shape[1]
    hb = BF16_SUBLANES
    nt = M // tm
    halo_blocks = tm // hb
    y, tail = pl.pallas_call(
        functools.partial(_ffn_prompt_body, tm=tm, tiles_per_seq=S // tm),
        grid=(nt, F // tf),
        in_specs=[pl.BlockSpec((tm, D), lambda i, f: (i, 0)),
                  pl.BlockSpec((hb, D), lambda i, f: (jnp.maximum(i * halo_blocks - 1, 0), 0)),
                  pl.BlockSpec((tm, D), lambda i, f: (i, 0)),
                  pl.BlockSpec((D, tf), lambda i, f: (0, f)),
                  pl.BlockSpec((D, tf), lambda i, f: (0, f)),
                  pl.BlockSpec((tf, D), lambda i, f: (f, 0)),
                  pl.BlockSpec((CONV_WIDTH, tf), lambda i, f: (0, f)),
                  pl.BlockSpec((1, tf), lambda i, f: (0, f))],
        out_specs=(pl.BlockSpec((tm, D), lambda i, f: (i, 0)),
                   pl.BlockSpec((1, hb, tf), lambda i, f: (i, 0, f))),
        out_shape=(jax.ShapeDtypeStruct((M, D), f32), jax.ShapeDtypeStruct((nt, hb, F), f32)),
        scratch_shapes=[pltpu.VMEM((tm, D), f32), pltpu.VMEM((tm + hb, tf), f32)],
        compiler_params=_params("parallel", "arbitrary"),
        name="ffn_prompt",
    )(h2, h2, x1, w['w_gate'], w['w_up'], w['w_down'], w['conv_w'], w['conv_b'])
    return y, tail


def _ffn_sample_body(h2_ref, st_ref, x1_ref, wg_ref, wu_ref, wd_ref, cw_ref, cb_ref, y_ref, g_ref, acc):
    f = pl.program_id(1)
    h2 = h2_ref[...]
    g = _dot(h2, wg_ref[...])
    g_ref[...] = g
    _ffn_finish(f, g, st_ref[1], st_ref[0], h2, x1_ref, wu_ref, wd_ref, cw_ref, cb_ref, y_ref, acc)


def _ffn_sample(h2, x1, st_t, w, tf):
    M, D = x1.shape
    F = w['w_gate'].shape[1]
    return pl.pallas_call(
        _ffn_sample_body,
        grid=(1, F // tf),
        in_specs=[pl.BlockSpec((M, D), lambda i, f: (0, 0)),
                  pl.BlockSpec((CONV_WIDTH - 1, M, tf), lambda i, f: (0, 0, f)),
                  pl.BlockSpec((M, D), lambda i, f: (0, 0)),
                  pl.BlockSpec((D, tf), lambda i, f: (0, f)),
                  pl.BlockSpec((D, tf), lambda i, f: (0, f)),
                  pl.BlockSpec((tf, D), lambda i, f: (f, 0)),
                  pl.BlockSpec((CONV_WIDTH, tf), lambda i, f: (0, f)),
                  pl.BlockSpec((1, tf), lambda i, f: (0, f))],
        out_specs=(pl.BlockSpec((M, D), lambda i, f: (0, 0)),
                   pl.BlockSpec((M, tf), lambda i, f: (0, f))),
        out_shape=(jax.ShapeDtypeStruct((M, D), f32), jax.ShapeDtypeStruct((M, F), f32)),
        scratch_shapes=[pltpu.VMEM((M, D), f32)],
        compiler_params=_params("parallel", "arbitrary"),
        name="ffn_sample",
    )(h2, st_t, x1, w['w_gate'], w['w_up'], w['w_down'], w['conv_w'], w['conv_b'])


def _qk_gain(g):
    g_pe = g[QK_NOPE_DIM:]
    return jnp.concatenate([g[:QK_NOPE_DIM], g_pe, g_pe])


def _rope_table(pos):
    half = QK_ROPE_DIM // 2
    inv = ROPE_BASE ** (-jnp.arange(0, QK_ROPE_DIM, 2, dtype=f32) / QK_ROPE_DIM)
    ang = pos.astype(f32)[:, None] * inv[None, :]
    cos, sin = jnp.cos(ang), jnp.sin(ang)
    P = pos.shape[0]
    z = lambda n: jnp.zeros((P, n), f32)
    tail = LANE - QK_HEAD_DIM
    c = jnp.concatenate([jnp.ones((P, QK_NOPE_DIM), f32), cos, cos, z(tail)], axis=1)
    s_lo = jnp.concatenate([z(QK_NOPE_DIM), -sin, z(half), z(tail)], axis=1)
    s_hi = jnp.concatenate([z(QK_NOPE_DIM), z(half), sin, z(tail)], axis=1)
    return jnp.concatenate([c, s_lo, s_hi], axis=1)


def _prep_weights(g_mix, w_in, g_q_a, w_uq, g_qn, g_kv_a, w_uk, g_kn, w_uv, w_pool, s_pool,
                  g_out, w_o, g_ffn, w_gate, w_up, conv_w, conv_b, w_down):
    D = w_in.shape[0]
    q_rank = g_q_a.shape[0]
    kv_rank = g_kv_a.shape[0]
    pool_w = s_pool.shape[0]
    head_pad = LANE - QK_HEAD_DIM
    main = pool_w + q_rank + kv_rank
    w_in_p = jnp.concatenate([w_in[:, :main], jnp.zeros((D, QK_NOPE_DIM), f32), w_in[:, main:],
                              jnp.zeros((D, head_pad), f32)], axis=1).astype(bf16)
    w_uq_p = jnp.pad(w_uq.reshape(q_rank, N_HEADS, QK_HEAD_DIM), ((0, 0), (0, 0), (0, head_pad)))
    w_uq_p = w_uq_p.reshape(q_rank, N_HEADS * LANE).astype(bf16)
    gq = _qk_gain(g_qn) * _qk_gain(g_kn) * (QK_HEAD_DIM ** -0.5 * LOG2_E)
    gq = jnp.tile(jnp.pad(gq, (0, head_pad)), N_HEADS)[None, :]
    w_uk_p = jnp.pad(w_uk, ((0, 0), (0, 0), (0, LANE - QK_NOPE_DIM))).reshape(kv_rank, N_HEADS * LANE)
    w_uk_p = w_uk_p.astype(bf16)
    w_uv_t = w_uv.reshape(kv_rank, N_HEADS * V_HEAD_DIM).T.astype(bf16)
    w_uk_t = w_uk.reshape(kv_rank, N_HEADS * QK_NOPE_DIM).T.astype(bf16)
    eye = jnp.eye(N_HEADS, dtype=f32)
    w_uk_bd = jnp.einsum('chd,hg->hdgc', jnp.pad(w_uk, ((0, 0), (0, 0), (0, LANE - QK_NOPE_DIM))), eye)
    w_uk_bd = w_uk_bd.reshape(N_HEADS * LANE, N_HEADS * kv_rank).astype(bf16)
    w_uv_bd = jnp.einsum('chv,hg->hcgv', w_uv, eye).reshape(N_HEADS * kv_rank, N_HEADS * V_HEAD_DIM).astype(bf16)
    return {
        'pool_w': pool_w,
        'g_mix': g_mix[None, :], 'w_in': w_in_p, 'g_q_a': g_q_a[None, :], 'w_uq': w_uq_p, 'g_q': gq,
        'g_kv_a': g_kv_a[None, :], 'w_uk': w_uk_p, 'w_uv_t': w_uv_t, 'w_uk_t': w_uk_t, 'w_uk_bd': w_uk_bd, 'w_uv_bd': w_uv_bd,
        'w_pool': w_pool.astype(bf16), 's_pool': s_pool[None, :],
        'g_out_pool': g_out[None, :pool_w], 'g_out_attn': g_out[None, pool_w:],
        'w_o': w_o.astype(bf16), 'g_ffn': g_ffn[None, :],
        'w_gate': w_gate.astype(bf16), 'w_up': w_up.astype(bf16), 'w_down': w_down.astype(bf16),
        'conv_w': conv_w, 'conv_b': conv_b[None, :],
    }


def _tile(n, pref):
    t = min(n, pref)
    assert n % t == 0, (n, pref)
    return t


def _ffn_cols(F):
    half = F // 2
    return half if half % LANE == 0 else F


def kernel(x_prompt, x_sample, cache_ckv, cache_kpe, state_pool, state_conv, page_table, g_mix, w_in, g_q_a,
           w_uq, g_qn, g_kv_a, w_uk, g_kn, w_uv, w_pool, s_pool, g_out, w_o, g_ffn, w_gate, w_up, conv_w,
           conv_b, w_down):
    depth = g_mix.shape[0]
    assert depth == 1, "single-layer trunk only"
    B, S, D = x_prompt.shape
    DB, DS = x_sample.shape[:2]
    assert DS == 1, "one new token per sequence"
    n_pages = page_table.shape[1]
    page = cache_ckv.shape[2]
    past_len = n_pages * page
    l = 0
    w = _prep_weights(g_mix[l], w_in[l], g_q_a[l], w_uq[l], g_qn[l], g_kv_a[l], w_uk[l], g_kn[l], w_uv[l],
                      w_pool[l], s_pool[l], g_out[l], w_o[l], g_ffn[l], w_gate[l], w_up[l], conv_w[l],
                      conv_b[l], w_down[l])
    kv_rank = g_kv_a.shape[1]
    F = w_gate.shape[2]
    tf = _ffn_cols(F)

    xp = x_prompt.reshape(B * S, D)
    tm = _tile(S, 512)
    rope_p = _rope_table(jnp.arange(S, dtype=jnp.int32))
    u_p, q_p, ckv_p, kpe_p, k_p, vt_p = _proj(xp, rope_p, S // tm, w, tm)
    mp_p = _pool_prompt(u_p, B, S, w, _tile(S, 512))
    attn_p = _flash(q_p, k_p, vt_p, B, S, tm)
    x1_p, h2_p = _outproj(xp, mp_p, attn_p, w, _tile(S, 512))
    tmf = _tile(S, 512)
    y_p, tail = _ffn_prompt(h2_p, x1_p, S, w, tmf, tf)
    tiles_per_seq = S // tmf
    conv_p = tail[tiles_per_seq - 1::tiles_per_seq, BF16_SUBLANES - (CONV_WIDTH - 1):, :]

    xs = x_sample.reshape(DB, D)
    pos_s = past_len + jnp.arange(DS, dtype=jnp.int32)
    rope_s = jnp.tile(_rope_table(pos_s), (DB, 1))
    u_s, q_s, ckv_s, kpe_s, k_s, _ = _proj(xs, rope_s, 1, w, DB)
    st_pool_t = jnp.transpose(state_pool[l], (1, 0, 2))
    mp_s = _pool_sample(u_s, st_pool_t, past_len, w)
    qabs = _mm(q_s, w['w_uk_bd'], "absorb_q").reshape(DB, N_HEADS, kv_rank).astype(bf16)
    qabs = jnp.pad(qabs, ((0, 0), (0, BF16_SUBLANES - N_HEADS), (0, 0)))
    q3 = q_s.reshape(DB, N_HEADS, LANE)
    qpe = q3[:, :, QK_NOPE_DIM:QK_HEAD_DIM]
    cp = _tile(n_pages, 32)
    ctx = _decode(page_table, q3, k_s.reshape(DB, N_HEADS, LANE), qabs, qpe, ckv_s.reshape(DB, 1, kv_rank),
                  w['w_uk_t'], cache_ckv[l], jnp.swapaxes(cache_kpe[l], 1, 2), cp)
    attn_s = _mm(ctx.reshape(DB, N_HEADS * kv_rank), w['w_uv_bd'], "value_up")
    x1_s, h2_s = _outproj(xs, mp_s, attn_s, w, DB)
    st_conv_t = jnp.transpose(state_conv[l], (1, 0, 2))
    y_s, g_s = _ffn_sample(h2_s, x1_s, st_conv_t, w, tf)

    kpe_sl = slice(QK_NOPE_DIM, QK_HEAD_DIM)
    P = POOL_STATE_LEN
    return (
        y_p.reshape(B, S, D),
        y_s.reshape(DB, DS, D),
        ckv_p.reshape(1, B, S, kv_rank),
        kpe_p[:, kpe_sl].reshape(1, B, S, QK_ROPE_DIM),
        u_p.reshape(B, S, -1)[None, :, S - P:, :],
        conv_p[None],
        ckv_s.reshape(1, DB, DS, kv_rank),
        kpe_s[:, kpe_sl].reshape(1, DB, DS, QK_ROPE_DIM),
        jnp.concatenate([state_pool[l], u_s[:, None, :]], axis=1)[None, :, -P:, :],
        jnp.concatenate([state_conv[l], g_s[:, None, :]], axis=1)[None, :, -(CONV_WIDTH - 1):, :],
    )
```

```python
import functools

import jax
import jax.numpy as jnp
from jax import lax
from jax.experimental import pallas as pl
from jax.experimental.pallas import tpu as pltpu

N_HEADS = 8
QK_NOPE_DIM = 64
QK_ROPE_DIM = 32
QK_HEAD_DIM = QK_NOPE_DIM + QK_ROPE_DIM
V_HEAD_DIM = 64
POOL_WINDOWS = (2, 4, 8, 16)
POOL_STATE_LEN = max(POOL_WINDOWS) - 1
CONV_WIDTH = 3
ROPE_BASE = 10000.0
RMS_EPS = 1e-6
LOG2_E = 1.4426950408889634

LANE = 128
BF16_SUBLANES = 16
VMEM_LIMIT = 48 * 1024 * 1024

_NT = (((1,), (1,)), ((), ()))

bf16 = jnp.bfloat16
f32 = jnp.float32


def _rms(x, width):
    return x * lax.rsqrt(jnp.sum(x * x, axis=-1, keepdims=True) * (1.0 / width) + RMS_EPS)


def _dot(a, b):
    return jnp.dot(a, b, preferred_element_type=f32)


def _params(*sem):
    return pltpu.CompilerParams(dimension_semantics=sem, vmem_limit_bytes=VMEM_LIMIT)


def _proj_body(x_ref, rope_ref, gmix_ref, win_ref, gqa_ref, wuq_ref, gq_ref, gkva_ref, wuk_ref, wuvt_ref,
               u_ref, q_ref, ckv_ref, kpe_ref, k_ref, vt_ref, *, pool_w, q_rank, kv_rank):
    x = x_ref[...]
    h = (_rms(x, x.shape[-1]) * gmix_ref[...]).astype(bf16)
    proj = _dot(h, win_ref[...])
    u_ref[...] = proj[:, :pool_w]
    o = pool_w
    cq = (_rms(proj[:, o:o + q_rank], q_rank) * gqa_ref[...]).astype(bf16)
    o += q_rank
    ckv = _rms(proj[:, o:o + kv_rank], kv_rank) * gkva_ref[...]
    o += kv_rank
    ckv_ref[...] = ckv
    cos = rope_ref[:, 0:LANE]
    sin_lo = rope_ref[:, LANE:2 * LANE]
    sin_hi = rope_ref[:, 2 * LANE:3 * LANE]

    def rope(t):
        half = QK_ROPE_DIM // 2
        return t * cos + pltpu.roll(t, LANE - half, 1) * sin_lo + pltpu.roll(t, half, 1) * sin_hi

    kpe = rope(proj[:, o:o + LANE])
    kpe_ref[...] = kpe
    qraw = _dot(cq, wuq_ref[...])
    ckv_b = ckv.astype(bf16)
    kn = _dot(ckv_b, wuk_ref[...])
    vt_ref[0] = lax.dot_general(wuvt_ref[...], ckv_b, _NT, preferred_element_type=f32).astype(bf16)
    for hd in range(N_HEADS):
        sl = slice(hd * LANE, (hd + 1) * LANE)
        qh = rope(qraw[:, sl])
        q_ref[:, sl] = (_rms(qh, QK_HEAD_DIM) * gq_ref[:, sl]).astype(bf16)
        kh = kn[:, sl] + kpe
        k_ref[:, sl] = _rms(kh, QK_HEAD_DIM).astype(bf16)


def _proj(x, rope, n_rope_blocks, w, tm):
    M, D = x.shape
    pool_w = w['pool_w']
    q_rank = w['g_q_a'].shape[1]
    kv_rank = w['g_kv_a'].shape[1]
    hq = N_HEADS * LANE
    hv = N_HEADS * V_HEAD_DIM
    const = lambda a: pl.BlockSpec(a.shape, lambda i: (0, 0))
    row = lambda n: pl.BlockSpec((tm, n), lambda i: (i, 0))
    out_shapes = (
        jax.ShapeDtypeStruct((M, pool_w), f32), jax.ShapeDtypeStruct((M, hq), bf16),
        jax.ShapeDtypeStruct((M, kv_rank), f32), jax.ShapeDtypeStruct((M, LANE), f32),
        jax.ShapeDtypeStruct((M, hq), bf16), jax.ShapeDtypeStruct((M // tm, hv, tm), bf16))
    return pl.pallas_call(
        functools.partial(_proj_body, pool_w=pool_w, q_rank=q_rank, kv_rank=kv_rank),
        grid=(M // tm,),
        in_specs=[row(D), pl.BlockSpec((tm, 3 * LANE), lambda i: (i % n_rope_blocks, 0)),
                  const(w['g_mix']), const(w['w_in']), const(w['g_q_a']), const(w['w_uq']),
                  const(w['g_q']), const(w['g_kv_a']), const(w['w_uk']), const(w['w_uv_t'])],
        out_specs=(row(pool_w), row(hq), row(kv_rank), row(LANE), row(hq),
                   pl.BlockSpec((1, hv, tm), lambda i: (i, 0, 0))),
        out_shape=out_shapes,
        compiler_params=_params("parallel"),
        name="in_proj",
    )(x, rope, w['g_mix'], w['w_in'], w['g_q_a'], w['w_uq'], w['g_q'], w['g_kv_a'], w['w_uk'], w['w_uv_t'])


def _pool_finish(diffs, wpool_ref, spool_ref, gout_ref, o_ref):
    ys = []
    for g, d in enumerate(diffs):
        sl = slice(g * LANE, (g + 1) * LANE)
        ys.append(_dot(d.astype(bf16), wpool_ref[g]) * spool_ref[:, sl])
    width = LANE * len(ys)
    ssq = sum(jnp.sum(y * y, axis=-1, keepdims=True) for y in ys)
    scale = lax.rsqrt(ssq * (1.0 / width) + RMS_EPS)
    for g, y in enumerate(ys):
        sl = slice(g * LANE, (g + 1) * LANE)
        o_ref[:, sl] = (y * scale * gout_ref[:, sl]).astype(o_ref.dtype)


def _pool_prompt_body(u_ref, wpool_ref, spool_ref, gout_ref, o_ref, buf, *, ts):
    j = pl.program_id(1)
    halo = POOL_STATE_LEN + 1

    @pl.when(j == 0)
    def _():
        buf[0:halo, :] = jnp.zeros((halo, buf.shape[1]), f32)

    @pl.when(j > 0)
    def _():
        buf[0:halo, :] = buf[ts:ts + halo, :]

    buf[halo:halo + ts, :] = u_ref[...]
    pos = j * ts + lax.broadcasted_iota(jnp.int32, (ts, 1), 0)
    diffs = []
    for g, wdw in enumerate(POOL_WINDOWS):
        sl = slice(g * LANE, (g + 1) * LANE)
        x = buf[halo:halo + ts, sl]
        win = x
        for k in range(1, wdw):
            win = win + buf[halo - k:halo - k + ts, sl]
        cnt = jnp.minimum(wdw, pos + 1).astype(f32)
        diffs.append(win / cnt - x)
    _pool_finish(diffs, wpool_ref, spool_ref, gout_ref, o_ref)


def _pool_prompt(u, B, S, w, ts):
    M, W = u.shape
    ns = S // ts
    const2 = lambda a: pl.BlockSpec(a.shape, lambda b, j: (0,) * a.ndim)
    return pl.pallas_call(
        functools.partial(_pool_prompt_body, ts=ts),
        grid=(B, ns),
        in_specs=[pl.BlockSpec((ts, W), lambda b, j: (b * ns + j, 0)),
                  const2(w['w_pool']), const2(w['s_pool']), const2(w['g_out_pool'])],
        out_specs=pl.BlockSpec((ts, W), lambda b, j: (b * ns + j, 0)),
        out_shape=jax.ShapeDtypeStruct((M, W), bf16),
        scratch_shapes=[pltpu.VMEM((ts + POOL_STATE_LEN + 1, W), f32)],
        compiler_params=_params("arbitrary", "arbitrary"),
        name="pool_prompt",
    )(u, w['w_pool'], w['s_pool'], w['g_out_pool'])


def _pool_sample_body(u_ref, st_ref, wpool_ref, spool_ref, gout_ref, o_ref, *, pos):
    diffs = []
    for g, wdw in enumerate(POOL_WINDOWS):
        sl = slice(g * LANE, (g + 1) * LANE)
        x = u_ref[:, sl]
        win = x
        for k in range(1, wdw):
            win = win + st_ref[POOL_STATE_LEN - k, :, sl]
        diffs.append(win / float(min(wdw, pos + 1)) - x)
    _pool_finish(diffs, wpool_ref, spool_ref, gout_ref, o_ref)


def _pool_sample(u, st_t, pos, w):
    M, W = u.shape
    return pl.pallas_call(
        functools.partial(_pool_sample_body, pos=pos),
        out_shape=jax.ShapeDtypeStruct((M, W), bf16),
        compiler_params=pltpu.CompilerParams(vmem_limit_bytes=VMEM_LIMIT),
        name="pool_sample",
    )(u, st_t, w['w_pool'], w['s_pool'], w['g_out_pool'])


def _flash_body(q_ref, k_ref, vt_ref, o_ref, m_scr, l_scr, acc_scr, *, tq):
    i = pl.program_id(2)
    key_idx = lax.broadcasted_iota(jnp.int32, (tq, tq), 0)
    qry_idx = lax.broadcasted_iota(jnp.int32, (tq, tq), 1)
    m_scr[...] = jnp.full(m_scr.shape, -1e30, f32)
    l_scr[...] = jnp.zeros(l_scr.shape, f32)
    acc_scr[...] = jnp.zeros(acc_scr.shape, f32)

    def step(j, masked):
        start = pl.multiple_of(j * tq, tq)
        for hh in range(2):
            sl = slice(hh * LANE, (hh + 1) * LANE)
            k = k_ref[pl.ds(start, tq), sl]
            st = lax.dot_general(k, q_ref[:, sl], _NT, preferred_element_type=f32)
            if masked:
                st = jnp.where(key_idx <= qry_idx, st, -1e30)
            m = m_scr[hh]
            m_new = jnp.maximum(m, jnp.max(st, axis=0, keepdims=True))
            alpha = jnp.exp2(m - m_new)
            pt = jnp.exp2(st - m_new)
            l_scr[hh] = l_scr[hh] * alpha + jnp.sum(pt, axis=0, keepdims=True)
            vt = vt_ref[j, hh * V_HEAD_DIM:(hh + 1) * V_HEAD_DIM, :]
            acc_scr[hh] = acc_scr[hh] * alpha + _dot(vt, pt.astype(bf16))
            m_scr[hh] = m_new

    def body(j, carry):
        step(j, False)
        return carry

    lax.fori_loop(0, i, body, 0)
    step(i, True)
    out_t = jnp.concatenate([acc_scr[0] / l_scr[0], acc_scr[1] / l_scr[1]], axis=0)
    o_ref[...] = out_t.T


def _flash(q, k, vt, B, S, tq):
    M = q.shape[0]
    nq = S // tq
    pairs = N_HEADS // 2
    return pl.pallas_call(
        functools.partial(_flash_body, tq=tq),
        grid=(B, pairs, nq),
        in_specs=[pl.BlockSpec((tq, 2 * LANE), lambda b, g, i: (b * nq + i, g)),
                  pl.BlockSpec((S, 2 * LANE), lambda b, g, i: (b, g)),
                  pl.BlockSpec((nq, 2 * V_HEAD_DIM, tq), lambda b, g, i: (b, g, 0))],
        out_specs=pl.BlockSpec((tq, 2 * V_HEAD_DIM), lambda b, g, i: (b * nq + i, g)),
        out_shape=jax.ShapeDtypeStruct((M, N_HEADS * V_HEAD_DIM), f32),
        scratch_shapes=[pltpu.VMEM((2, 1, tq), f32), pltpu.VMEM((2, 1, tq), f32),
                        pltpu.VMEM((2, V_HEAD_DIM, tq), f32)],
        compiler_params=_params("parallel", "parallel", "arbitrary"),
        name="flash_prompt",
    )(q, k, vt)


def _mm_body(a_ref, b_ref, o_ref):
    o_ref[...] = _dot(a_ref[...].astype(bf16), b_ref[...])


def _mm(a, b, name):
    return pl.pallas_call(
        _mm_body,
        out_shape=jax.ShapeDtypeStruct((a.shape[0], b.shape[1]), f32),
        compiler_params=pltpu.CompilerParams(vmem_limit_bytes=VMEM_LIMIT),
        name=name,
    )(a, b)


def _decode_body(pt_ref, q_ref, knew_ref, qabs_ref, qpe_ref, cnew_ref, wukt_ref, cache_ckv, cache_kpet,
                 o_ref, ckv_buf, kpe_buf, sems, lhs, ct_scr, s_scr, m_scr, l_scr, acc_scr, *, n_pages, cp, nc, tile):
    f = pl.program_id(0)
    total = pl.num_programs(0) - 1
    nope_rows = wukt_ref.shape[0]
    page = cache_ckv.shape[1]
    tiles = (cp * page) // tile
    pages_per_tile = tile // page

    def copies(chunk, slot_idx):
        base = (chunk // nc) * n_pages + (chunk % nc) * cp
        out = []
        for p in range(cp):
            phys = pt_ref[base + p]
            rows = pl.ds(p * page, page)
            out.append(pltpu.make_async_copy(cache_ckv.at[phys], ckv_buf.at[slot_idx, rows], sems.at[0, slot_idx]))
            out.append(pltpu.make_async_copy(cache_kpet.at[phys], kpe_buf.at[slot_idx, p], sems.at[1, slot_idx]))
        return out

    @pl.when(f == 0)
    def _():
        for cpy in copies(0, 0):
            cpy.start()
        ct_scr[1] = jnp.zeros(ct_scr.shape[1:], bf16)
        s_scr[1] = jnp.zeros(s_scr.shape[1:], f32)
        m_scr[...] = jnp.zeros_like(m_scr)
        l_scr[...] = jnp.ones_like(l_scr)
        acc_scr[...] = jnp.zeros_like(acc_scr)
        lhs[0:nope_rows, :] = wukt_ref[...]

    @pl.when(f % nc == 0)
    def _():
        lhs[nope_rows:nope_rows + BF16_SUBLANES, :] = qabs_ref[0]

    @pl.when((f >= 1) & ((f - 1) % nc == 0))
    def _():
        s_new = jnp.sum(q_ref[0].astype(f32) * knew_ref[0].astype(f32), axis=-1, keepdims=True)
        m_scr[...] = s_new
        l_scr[...] = jnp.ones_like(l_scr)
        acc_scr[...] = jnp.broadcast_to(cnew_ref[0], acc_scr.shape)

    def main(slot):
        prev = 1 - slot
        nxt = copies(jnp.minimum(f + 1, total - 1), prev)
        for cpy in nxt:
            cpy.start()
        for cpy in copies(jnp.minimum(f, total - 1), slot):
            cpy.wait()
        s_prev = s_scr[prev]
        m = m_scr[...]
        m_new = jnp.maximum(m, jnp.max(s_prev, axis=-1, keepdims=True))
        alpha = jnp.exp2(m - m_new)
        p = jnp.exp2(s_prev - m_new)
        l_scr[...] = l_scr[...] * alpha + jnp.sum(p, axis=-1, keepdims=True)
        m_scr[...] = m_new
        p = p.astype(bf16)
        acc = acc_scr[...] * alpha
        qpe = qpe_ref[0]
        ones = jnp.ones(qpe.shape, bf16)
        for t in range(tiles):
            rows = slice(t * tile, (t + 1) * tile)
            ct = ckv_buf[slot, rows, :].astype(bf16)
            ct_scr[slot, rows, :] = ct
            kt = lax.dot_general(lhs[...], ct, _NT, preferred_element_type=f32)
            kn = kt[0:nope_rows].reshape(N_HEADS, nope_rows // N_HEADS, tile)
            ssq = jnp.sum(kn * kn, axis=1)
            num = kt[nope_rows:nope_rows + N_HEADS]
            pe_num, pe_ssq = [], []
            for pg in range(t * pages_per_tile, (t + 1) * pages_per_tile):
                kp = kpe_buf[slot, pg]
                pe_num.append(_dot(qpe, kp.astype(bf16)))
                pe_ssq.append(_dot(ones, (kp * kp).astype(bf16)))
            num = num + jnp.concatenate(pe_num, axis=1)
            ssq = ssq + jnp.concatenate(pe_ssq, axis=1)
            s_scr[slot, :, rows] = num * lax.rsqrt(ssq * (1.0 / QK_HEAD_DIM) + RMS_EPS)
            acc = acc + _dot(p[:, rows], ct_scr[prev, rows, :])
        acc_scr[...] = acc

        @pl.when(f == total)
        def _():
            for cpy in nxt:
                cpy.wait()

    @pl.when(f % 2 == 0)
    def _():
        main(0)

    @pl.when(f % 2 == 1)
    def _():
        main(1)

    @pl.when((f >= 1) & (f % nc == 0))
    def _():
        o_ref[0] = acc_scr[...] / l_scr[...]


def _decode(page_table, q3, knew3, qabs, qpe, cnew, wukt, cache_ckv, cache_kpet, cp, tile):
    DB, n_pages = page_table.shape
    page, kv_rank = cache_ckv.shape[1:]
    rope = cache_kpet.shape[1]
    nc = n_pages // cp
    T = cp * page
    total = DB * nc
    head_seq = lambda f, pt: jnp.minimum(f, total - 1) // nc
    tail_seq = lambda f, pt: jnp.maximum(f - 1, 0) // nc
    per_seq = lambda a, seq: pl.BlockSpec((1,) + a.shape[1:], lambda f, pt: (seq(f, pt),) + (0,) * (a.ndim - 1))
    grid_spec = pltpu.PrefetchScalarGridSpec(
        num_scalar_prefetch=1,
        grid=(total + 1,),
        in_specs=[per_seq(q3, tail_seq), per_seq(knew3, tail_seq), per_seq(qabs, head_seq), per_seq(qpe, head_seq),
                  per_seq(cnew, tail_seq), pl.BlockSpec(wukt.shape, lambda f, pt: (0, 0)),
                  pl.BlockSpec(memory_space=pl.ANY), pl.BlockSpec(memory_space=pl.ANY)],
        out_specs=pl.BlockSpec((1, N_HEADS, kv_rank), lambda f, pt: (tail_seq(f, pt), 0, 0)),
        scratch_shapes=[pltpu.VMEM((2, T, kv_rank), f32), pltpu.VMEM((2, cp, rope, page), f32),
                        pltpu.SemaphoreType.DMA((2, 2)),
                        pltpu.VMEM((wukt.shape[0] + BF16_SUBLANES, kv_rank), bf16),
                        pltpu.VMEM((2, T, kv_rank), bf16), pltpu.VMEM((2, N_HEADS, T), f32),
                        pltpu.VMEM((N_HEADS, 1), f32), pltpu.VMEM((N_HEADS, 1), f32),
                        pltpu.VMEM((N_HEADS, kv_rank), f32)])
    return pl.pallas_call(
        functools.partial(_decode_body, n_pages=n_pages, cp=cp, nc=nc, tile=tile),
        grid_spec=grid_spec,
        out_shape=jax.ShapeDtypeStruct((DB, N_HEADS, kv_rank), f32),
        compiler_params=_params("arbitrary"),
        name="decode_attn",
    )(page_table.reshape(-1), q3, knew3, qabs, qpe, cnew, wukt, cache_ckv, cache_kpet)


def _outproj_body(x_ref, mp_ref, attn_ref, gatt_ref, wo_ref, gffn_ref, x1_ref, h2_ref):
    pw = mp_ref.shape[1]
    attn = attn_ref[...]
    na = (_rms(attn, attn.shape[-1]) * gatt_ref[...]).astype(bf16)
    x1 = x_ref[...] + _dot(mp_ref[...], wo_ref[0:pw, :]) + _dot(na, wo_ref[pw:, :])
    x1_ref[...] = x1
    h2_ref[...] = (_rms(x1, x1.shape[-1]) * gffn_ref[...]).astype(bf16)


def _outproj(x, mp, attn, w, tm):
    M, D = x.shape
    const = lambda a: pl.BlockSpec(a.shape, lambda i: (0, 0))
    row = lambda n: pl.BlockSpec((tm, n), lambda i: (i, 0))
    return pl.pallas_call(
        _outproj_body,
        grid=(M // tm,),
        in_specs=[row(D), row(mp.shape[1]), row(attn.shape[1]),
                  const(w['g_out_attn']), const(w['w_o']), const(w['g_ffn'])],
        out_specs=(row(D), row(D)),
        out_shape=(jax.ShapeDtypeStruct((M, D), f32), jax.ShapeDtypeStruct((M, D), bf16)),
        compiler_params=_params("parallel"),
        name="out_proj",
    )(x, mp, attn, w['g_out_attn'], w['w_o'], w['g_ffn'])


def _ffn_finish(f, g, g1, g2, h2, x1_ref, wu_ref, wd_ref, cw_ref, cb_ref, y_ref, acc):
    gate = cb_ref[...] + g2 * cw_ref[0:1, :] + g1 * cw_ref[1:2, :] + g * cw_ref[2:3, :]
    act = (gate * jax.nn.sigmoid(gate)) * _dot(h2, wu_ref[...])
    part = _dot(act.astype(bf16), wd_ref[...])

    @pl.when(f == 0)
    def _():
        acc[...] = x1_ref[...] + part

    @pl.when(f > 0)
    def _():
        acc[...] += part

    @pl.when(f == pl.num_programs(1) - 1)
    def _():
        y_ref[...] = acc[...]


def _ffn_prompt_body(h2_ref, halo_ref, x1_ref, wg_ref, wu_ref, wd_ref, cw_ref, cb_ref,
                     y_ref, tail_ref, acc, gbuf, *, tm, tiles_per_seq):
    i = pl.program_id(0)
    f = pl.program_id(1)
    hb = BF16_SUBLANES
    h2 = h2_ref[...]
    g = _dot(h2, wg_ref[...])
    gh = _dot(halo_ref[...], wg_ref[...])
    gbuf[0:hb, :] = jnp.where(i % tiles_per_seq == 0, 0.0, gh)
    gbuf[hb:hb + tm, :] = g
    tail_ref[0] = gbuf[tm:tm + hb, :]
    g1 = gbuf[hb - 1:hb - 1 + tm, :]
    g2 = gbuf[hb - 2:hb - 2 + tm, :]
    _ffn_finish(f, g, g1, g2, h2, x1_ref, wu_ref, wd_ref, cw_ref, cb_ref, y_ref, acc)


def _ffn_prompt(h2, x1, S, w, tm, tf):
    M, D = x1.shape
    F = w['w_gate'].shape[1]
    hb = BF16_SUBLANES
    nt = M // tm
    halo_blocks = tm // hb
    y, tail = pl.pallas_call(
        functools.partial(_ffn_prompt_body, tm=tm, tiles_per_seq=S // tm),
        grid=(nt, F // tf),
        in_specs=[pl.BlockSpec((tm, D), lambda i, f: (i, 0)),
                  pl.BlockSpec((hb, D), lambda i, f: (jnp.maximum(i * halo_blocks - 1, 0), 0)),
                  pl.BlockSpec((tm, D), lambda i, f: (i, 0)),
                  pl.BlockSpec((D, tf), lambda i, f: (0, f)),
                  pl.BlockSpec((D, tf), lambda i, f: (0, f)),
                  pl.BlockSpec((tf, D), lambda i, f: (f, 0)),
                  pl.BlockSpec((CONV_WIDTH, tf), lambda i, f: (0, f)),
                  pl.BlockSpec((1, tf), lambda i, f: (0, f))],
        out_specs=(pl.BlockSpec((tm, D), lambda i, f: (i, 0)),
                   pl.BlockSpec((1, hb, tf), lambda i, f: (i, 0, f))),
        out_shape=(jax.ShapeDtypeStruct((M, D), f32), jax.ShapeDtypeStruct((nt, hb, F), f32)),
        scratch_shapes=[pltpu.VMEM((tm, D), f32), pltpu.VMEM((tm + hb, tf), f32)],
        compiler_params=_params("parallel", "arbitrary"),
        name="ffn_prompt",
    )(h2, h2, x1, w['w_gate'], w['w_up'], w['w_down'], w['conv_w'], w['conv_b'])
    return y, tail


def _ffn_sample_body(h2_ref, st_ref, x1_ref, wg_ref, wu_ref, wd_ref, cw_ref, cb_ref, y_ref, g_ref, acc):
    f = pl.program_id(1)
    h2 = h2_ref[...]
    g = _dot(h2, wg_ref[...])
    g_ref[...] = g
    _ffn_finish(f, g, st_ref[1], st_ref[0], h2, x1_ref, wu_ref, wd_ref, cw_ref, cb_ref, y_ref, acc)


def _ffn_sample(h2, x1, st_t, w, tf):
    M, D = x1.shape
    F = w['w_gate'].shape[1]
    return pl.pallas_call(
        _ffn_sample_body,
        grid=(1, F // tf),
        in_specs=[pl.BlockSpec((M, D), lambda i, f: (0, 0)),
                  pl.BlockSpec((CONV_WIDTH - 1, M, tf), lambda i, f: (0, 0, f)),
                  pl.BlockSpec((M, D), lambda i, f: (0, 0)),
                  pl.BlockSpec((D, tf), lambda i, f: (0, f)),
                  pl.BlockSpec((D, tf), lambda i, f: (0, f)),
                  pl.BlockSpec((tf, D), lambda i, f: (f, 0)),
                  pl.BlockSpec((CONV_WIDTH, tf), lambda i, f: (0, f)),
                  pl.BlockSpec((1, tf), lambda i, f: (0, f))],
        out_specs=(pl.BlockSpec((M, D), lambda i, f: (0, 0)),
                   pl.BlockSpec((M, tf), lambda i, f: (0, f))),
        out_shape=(jax.ShapeDtypeStruct((M, D), f32), jax.ShapeDtypeStruct((M, F), f32)),
        scratch_shapes=[pltpu.VMEM((M, D), f32)],
        compiler_params=_params("parallel", "arbitrary"),
        name="ffn_sample",
    )(h2, st_t, x1, w['w_gate'], w['w_up'], w['w_down'], w['conv_w'], w['conv_b'])


def _qk_gain(g):
    g_pe = g[QK_NOPE_DIM:]
    return jnp.concatenate([g[:QK_NOPE_DIM], g_pe, g_pe])


def _rope_table(pos):
    half = QK_ROPE_DIM // 2
    inv = ROPE_BASE ** (-jnp.arange(0, QK_ROPE_DIM, 2, dtype=f32) / QK_ROPE_DIM)
    ang = pos.astype(f32)[:, None] * inv[None, :]
    cos, sin = jnp.cos(ang), jnp.sin(ang)
    P = pos.shape[0]
    z = lambda n: jnp.zeros((P, n), f32)
    tail = LANE - QK_HEAD_DIM
    c = jnp.concatenate([jnp.ones((P, QK_NOPE_DIM), f32), cos, cos, z(tail)], axis=1)
    s_lo = jnp.concatenate([z(QK_NOPE_DIM), -sin, z(half), z(tail)], axis=1)
    s_hi = jnp.concatenate([z(QK_NOPE_DIM), z(half), sin, z(tail)], axis=1)
    return jnp.concatenate([c, s_lo, s_hi], axis=1)


def _prep_weights(g_mix, w_in, g_q_a, w_uq, g_qn, g_kv_a, w_uk, g_kn, w_uv, w_pool, s_pool,
                  g_out, w_o, g_ffn, w_gate, w_up, conv_w, conv_b, w_down):
    D = w_in.shape[0]
    q_rank = g_q_a.shape[0]
    kv_rank = g_kv_a.shape[0]
    pool_w = s_pool.shape[0]
    head_pad = LANE - QK_HEAD_DIM
    main = pool_w + q_rank + kv_rank
    w_in_p = jnp.concatenate([w_in[:, :main], jnp.zeros((D, QK_NOPE_DIM), f32), w_in[:, main:],
                              jnp.zeros((D, head_pad), f32)], axis=1).astype(bf16)
    w_uq_p = jnp.pad(w_uq.reshape(q_rank, N_HEADS, QK_HEAD_DIM), ((0, 0), (0, 0), (0, head_pad)))
    w_uq_p = w_uq_p.reshape(q_rank, N_HEADS * LANE).astype(bf16)
    gq = _qk_gain(g_qn) * _qk_gain(g_kn) * (QK_HEAD_DIM ** -0.5 * LOG2_E)
    gq = jnp.tile(jnp.pad(gq, (0, head_pad)), N_HEADS)[None, :]
    w_uk_p = jnp.pad(w_uk, ((0, 0), (0, 0), (0, LANE - QK_NOPE_DIM))).reshape(kv_rank, N_HEADS * LANE)
    w_uk_p = w_uk_p.astype(bf16)
    w_uv_t = w_uv.reshape(kv_rank, N_HEADS * V_HEAD_DIM).T.astype(bf16)
    w_uk_t = w_uk.reshape(kv_rank, N_HEADS * QK_NOPE_DIM).T.astype(bf16)
    eye = jnp.eye(N_HEADS, dtype=f32)
    w_uk_bd = jnp.einsum('chd,hg->hdgc', jnp.pad(w_uk, ((0, 0), (0, 0), (0, LANE - QK_NOPE_DIM))), eye)
    w_uk_bd = w_uk_bd.reshape(N_HEADS * LANE, N_HEADS * kv_rank).astype(bf16)
    w_uv_bd = jnp.einsum('chv,hg->hcgv', w_uv, eye).reshape(N_HEADS * kv_rank, N_HEADS * V_HEAD_DIM).astype(bf16)
    return {
        'pool_w': pool_w,
        'g_mix': g_mix[None, :], 'w_in': w_in_p, 'g_q_a': g_q_a[None, :], 'w_uq': w_uq_p, 'g_q': gq,
        'g_kv_a': g_kv_a[None, :], 'w_uk': w_uk_p, 'w_uv_t': w_uv_t, 'w_uk_t': w_uk_t, 'w_uk_bd': w_uk_bd, 'w_uv_bd': w_uv_bd,
        'w_pool': w_pool.astype(bf16), 's_pool': s_pool[None, :],
        'g_out_pool': g_out[None, :pool_w], 'g_out_attn': g_out[None, pool_w:],
        'w_o': w_o.astype(bf16), 'g_ffn': g_ffn[None, :],
        'w_gate': w_gate.astype(bf16), 'w_up': w_up.astype(bf16), 'w_down': w_down.astype(bf16),
        'conv_w': conv_w, 'conv_b': conv_b[None, :],
    }


def _tile(n, pref):
    t = min(n, pref)
    assert n % t == 0, (n, pref)
    return t


def _ffn_cols(F):
    half = F // 2
    return half if half % LANE == 0 else F


def kernel(x_prompt, x_sample, cache_ckv, cache_kpe, state_pool, state_conv, page_table, g_mix, w_in, g_q_a,
           w_uq, g_qn, g_kv_a, w_uk, g_kn, w_uv, w_pool, s_pool, g_out, w_o, g_ffn, w_gate, w_up, conv_w,
           conv_b, w_down):
    depth = g_mix.shape[0]
    assert depth == 1, "single-layer trunk only"
    B, S, D = x_prompt.shape
    DB, DS = x_sample.shape[:2]
    assert DS == 1, "one new token per sequence"
    n_pages = page_table.shape[1]
    page = cache_ckv.shape[2]
    past_len = n_pages * page
    l = 0
    w = _prep_weights(g_mix[l], w_in[l], g_q_a[l], w_uq[l], g_qn[l], g_kv_a[l], w_uk[l], g_kn[l], w_uv[l],
                      w_pool[l], s_pool[l], g_out[l], w_o[l], g_ffn[l], w_gate[l], w_up[l], conv_w[l],
                      conv_b[l], w_down[l])
    kv_rank = g_kv_a.shape[1]
    F = w_gate.shape[2]
    tf = _ffn_cols(F)

    xp = x_prompt.reshape(B * S, D)
    tm = _tile(S, 512)
    rope_p = _rope_table(jnp.arange(S, dtype=jnp.int32))
    u_p, q_p, ckv_p, kpe_p, k_p, vt_p = _proj(xp, rope_p, S // tm, w, tm)
    mp_p = _pool_prompt(u_p, B, S, w, _tile(S, 512))
    attn_p = _flash(q_p, k_p, vt_p, B, S, tm)
    x1_p, h2_p = _outproj(xp, mp_p, attn_p, w, _tile(S, 512))
    tmf = _tile(S, 512)
    y_p, tail = _ffn_prompt(h2_p, x1_p, S, w, tmf, tf)
    tiles_per_seq = S // tmf
    conv_p = tail[tiles_per_seq - 1::tiles_per_seq, BF16_SUBLANES - (CONV_WIDTH - 1):, :]

    xs = x_sample.reshape(DB, D)
    pos_s = past_len + jnp.arange(DS, dtype=jnp.int32)
    rope_s = jnp.tile(_rope_table(pos_s), (DB, 1))
    u_s, q_s, ckv_s, kpe_s, k_s, _ = _proj(xs, rope_s, 1, w, DB)
    st_pool_t = jnp.transpose(state_pool[l], (1, 0, 2))
    mp_s = _pool_sample(u_s, st_pool_t, past_len, w)
    qabs = _mm(q_s, w['w_uk_bd'], "absorb_q").reshape(DB, N_HEADS, kv_rank).astype(bf16)
    qabs = jnp.pad(qabs, ((0, 0), (0, BF16_SUBLANES - N_HEADS), (0, 0)))
    q3 = q_s.reshape(DB, N_HEADS, LANE)
    qpe = q3[:, :, QK_NOPE_DIM:QK_HEAD_DIM]
    cp = _tile(n_pages, 32)
    ctx = _decode(page_table, q3, k_s.reshape(DB, N_HEADS, LANE), qabs, qpe, ckv_s.reshape(DB, 1, kv_rank),
                  w['w_uk_t'], cache_ckv[l], jnp.swapaxes(cache_kpe[l], 1, 2), cp, _tile(cp * page, 256))
    attn_s = _mm(ctx.reshape(DB, N_HEADS * kv_rank), w['w_uv_bd'], "value_up")
    x1_s, h2_s = _outproj(xs, mp_s, attn_s, w, DB)
    st_conv_t = jnp.transpose(state_conv[l], (1, 0, 2))
    y_s, g_s = _ffn_sample(h2_s, x1_s, st_conv_t, w, tf)

    kpe_sl = slice(QK_NOPE_DIM, QK_HEAD_DIM)
    P = POOL_STATE_LEN
    return (
        y_p.reshape(B, S, D),
        y_s.reshape(DB, DS, D),
        ckv_p.reshape(1, B, S, kv_rank),
        kpe_p[:, kpe_sl].reshape(1, B, S, QK_ROPE_DIM),
        u_p.reshape(B, S, -1)[None, :, S - P:, :],
        conv_p[None],
        ckv_s.reshape(1, DB, DS, kv_rank),
        kpe_s[:, kpe_sl].reshape(1, DB, DS, QK_ROPE_DIM),
        jnp.concatenate([state_pool[l], u_s[:, None, :]], axis=1)[None, :, -P:, :],
        jnp.concatenate([state_conv[l], g_s[:, None, :]], axis=1)[None, :, -(CONV_WIDTH - 1):, :],
    )
```

```python
import functools

import jax
import jax.numpy as jnp
from jax import lax
from jax.experimental import pallas as pl
from jax.experimental.pallas import tpu as pltpu

N_HEADS = 8
QK_NOPE_DIM = 64
QK_ROPE_DIM = 32
QK_HEAD_DIM = QK_NOPE_DIM + QK_ROPE_DIM
V_HEAD_DIM = 64
POOL_WINDOWS = (2, 4, 8, 16)
POOL_STATE_LEN = max(POOL_WINDOWS) - 1
CONV_WIDTH = 3
ROPE_BASE = 10000.0
RMS_EPS = 1e-6
LOG2_E = 1.4426950408889634

LANE = 128
BF16_SUBLANES = 16
VMEM_LIMIT = 48 * 1024 * 1024

_NT = (((1,), (1,)), ((), ()))

bf16 = jnp.bfloat16
f32 = jnp.float32


def _rms(x, width):
    return x * lax.rsqrt(jnp.sum(x * x, axis=-1, keepdims=True) * (1.0 / width) + RMS_EPS)


def _dot(a, b):
    return jnp.dot(a, b, preferred_element_type=f32)


def _params(*sem):
    return pltpu.CompilerParams(dimension_semantics=sem, vmem_limit_bytes=VMEM_LIMIT)


def _proj_body(x_ref, rope_ref, gmix_ref, win_ref, gqa_ref, wuq_ref, gq_ref, gkva_ref, wuk_ref, wuvt_ref,
               u_ref, q_ref, ckv_ref, kpe_ref, k_ref, vt_ref, *, pool_w, q_rank, kv_rank):
    x = x_ref[...]
    h = (_rms(x, x.shape[-1]) * gmix_ref[...]).astype(bf16)
    proj = _dot(h, win_ref[...])
    u_ref[...] = proj[:, :pool_w]
    o = pool_w
    cq = (_rms(proj[:, o:o + q_rank], q_rank) * gqa_ref[...]).astype(bf16)
    o += q_rank
    ckv = _rms(proj[:, o:o + kv_rank], kv_rank) * gkva_ref[...]
    o += kv_rank
    ckv_ref[...] = ckv
    cos = rope_ref[:, 0:LANE]
    sin_lo = rope_ref[:, LANE:2 * LANE]
    sin_hi = rope_ref[:, 2 * LANE:3 * LANE]
    sin_abs = sin_hi - sin_lo
    half = QK_ROPE_DIM // 2
    raw_kpe = proj[:, o:o + LANE]
    kpe = raw_kpe * cos + pltpu.roll(raw_kpe, LANE - half, 1) * sin_lo + pltpu.roll(raw_kpe, half, 1) * sin_hi
    kpe_ref[...] = kpe
    qraw = _dot(cq, wuq_ref[...])
    hq = N_HEADS * LANE
    ckv_b = ckv.astype(bf16)
    kn = _dot(ckv_b, wuk_ref[...])
    vt_ref[0] = lax.dot_general(wuvt_ref[...], ckv_b, _NT, preferred_element_type=f32).astype(bf16)
    for hd in range(N_HEADS):
        sl = slice(hd * LANE, (hd + 1) * LANE)
        qh = qraw[:, sl] * cos + qraw[:, hq + hd * LANE:hq + (hd + 1) * LANE] * sin_abs
        q_ref[:, sl] = (_rms(qh, QK_HEAD_DIM) * gq_ref[:, sl]).astype(bf16)
        kh = kn[:, sl] + kpe
        k_ref[:, sl] = _rms(kh, QK_HEAD_DIM).astype(bf16)


def _proj(x, rope, n_rope_blocks, w, tm, slab):
    M, D = x.shape
    pool_w = w['pool_w']
    q_rank = w['g_q_a'].shape[1]
    kv_rank = w['g_kv_a'].shape[1]
    hq = N_HEADS * LANE
    hv = N_HEADS * V_HEAD_DIM
    per_slab = slab // tm
    const = lambda a: pl.BlockSpec(a.shape, lambda i: (0, 0))
    row = lambda n: pl.BlockSpec((tm, n), lambda i: (i, 0))
    out_shapes = (
        jax.ShapeDtypeStruct((M, pool_w), f32), jax.ShapeDtypeStruct((M, hq), bf16),
        jax.ShapeDtypeStruct((M, kv_rank), f32), jax.ShapeDtypeStruct((M, LANE), f32),
        jax.ShapeDtypeStruct((M, hq), bf16), jax.ShapeDtypeStruct((M // slab, hv, slab), bf16))
    return pl.pallas_call(
        functools.partial(_proj_body, pool_w=pool_w, q_rank=q_rank, kv_rank=kv_rank),
        grid=(M // tm,),
        in_specs=[row(D), pl.BlockSpec((tm, 3 * LANE), lambda i: (i % n_rope_blocks, 0)),
                  const(w['g_mix']), const(w['w_in']), const(w['g_q_a']), const(w['w_uq']),
                  const(w['g_q']), const(w['g_kv_a']), const(w['w_uk']), const(w['w_uv_t'])],
        out_specs=(row(pool_w), row(hq), row(kv_rank), row(LANE), row(hq),
                   pl.BlockSpec((1, hv, tm), lambda i: (i // per_slab, 0, i % per_slab))),
        out_shape=out_shapes,
        compiler_params=_params("parallel"),
        name="in_proj",
    )(x, rope, w['g_mix'], w['w_in'], w['g_q_a'], w['w_uq'], w['g_q'], w['g_kv_a'], w['w_uk'], w['w_uv_t'])


def _pool_finish(diffs, wpool_ref, spool_ref, gout_ref, o_ref):
    ys = []
    for g, d in enumerate(diffs):
        sl = slice(g * LANE, (g + 1) * LANE)
        ys.append(_dot(d.astype(bf16), wpool_ref[g]) * spool_ref[:, sl])
    width = LANE * len(ys)
    ssq = sum(jnp.sum(y * y, axis=-1, keepdims=True) for y in ys)
    scale = lax.rsqrt(ssq * (1.0 / width) + RMS_EPS)
    for g, y in enumerate(ys):
        sl = slice(g * LANE, (g + 1) * LANE)
        o_ref[:, sl] = (y * scale * gout_ref[:, sl]).astype(o_ref.dtype)


def _pool_prompt_body(u_ref, wpool_ref, spool_ref, gout_ref, o_ref, buf, *, ts):
    j = pl.program_id(1)
    halo = POOL_STATE_LEN + 1

    @pl.when(j == 0)
    def _():
        buf[0:halo, :] = jnp.zeros((halo, buf.shape[1]), f32)

    @pl.when(j > 0)
    def _():
        buf[0:halo, :] = buf[ts:ts + halo, :]

    buf[halo:halo + ts, :] = u_ref[...]
    pos = j * ts + lax.broadcasted_iota(jnp.int32, (ts, 1), 0)
    diffs = []
    for g, wdw in enumerate(POOL_WINDOWS):
        sl = slice(g * LANE, (g + 1) * LANE)
        x = buf[halo:halo + ts, sl]
        win = x
        for k in range(1, wdw):
            win = win + buf[halo - k:halo - k + ts, sl]
        cnt = jnp.minimum(wdw, pos + 1).astype(f32)
        diffs.append(win / cnt - x)
    _pool_finish(diffs, wpool_ref, spool_ref, gout_ref, o_ref)


def _pool_prompt(u, B, S, w, ts):
    M, W = u.shape
    ns = S // ts
    const2 = lambda a: pl.BlockSpec(a.shape, lambda b, j: (0,) * a.ndim)
    return pl.pallas_call(
        functools.partial(_pool_prompt_body, ts=ts),
        grid=(B, ns),
        in_specs=[pl.BlockSpec((ts, W), lambda b, j: (b * ns + j, 0)),
                  const2(w['w_pool']), const2(w['s_pool']), const2(w['g_out_pool'])],
        out_specs=pl.BlockSpec((ts, W), lambda b, j: (b * ns + j, 0)),
        out_shape=jax.ShapeDtypeStruct((M, W), bf16),
        scratch_shapes=[pltpu.VMEM((ts + POOL_STATE_LEN + 1, W), f32)],
        compiler_params=_params("arbitrary", "arbitrary"),
        name="pool_prompt",
    )(u, w['w_pool'], w['s_pool'], w['g_out_pool'])


def _pool_sample_body(u_ref, st_ref, wpool_ref, spool_ref, gout_ref, o_ref, *, pos):
    diffs = []
    for g, wdw in enumerate(POOL_WINDOWS):
        sl = slice(g * LANE, (g + 1) * LANE)
        x = u_ref[:, sl]
        win = x
        for k in range(1, wdw):
            win = win + st_ref[POOL_STATE_LEN - k, :, sl]
        diffs.append(win / float(min(wdw, pos + 1)) - x)
    _pool_finish(diffs, wpool_ref, spool_ref, gout_ref, o_ref)


def _pool_sample(u, st_t, pos, w):
    M, W = u.shape
    return pl.pallas_call(
        functools.partial(_pool_sample_body, pos=pos),
        out_shape=jax.ShapeDtypeStruct((M, W), bf16),
        compiler_params=pltpu.CompilerParams(vmem_limit_bytes=VMEM_LIMIT),
        name="pool_sample",
    )(u, st_t, w['w_pool'], w['s_pool'], w['g_out_pool'])


def _flash_body(q_ref, k_ref, vt_ref, o_ref, m_scr, l_scr, acc_scr, *, tq):
    i = pl.program_id(2)
    key_idx = lax.broadcasted_iota(jnp.int32, (tq, tq), 0)
    qry_idx = lax.broadcasted_iota(jnp.int32, (tq, tq), 1)
    m_scr[...] = jnp.full(m_scr.shape, -1e30, f32)
    l_scr[...] = jnp.zeros(l_scr.shape, f32)
    acc_scr[...] = jnp.zeros(acc_scr.shape, f32)

    def step(j, masked):
        start = pl.multiple_of(j * tq, tq)
        for hh in range(2):
            sl = slice(hh * LANE, (hh + 1) * LANE)
            k = k_ref[pl.ds(start, tq), sl]
            st = lax.dot_general(k, q_ref[:, sl], _NT, preferred_element_type=f32)
            if masked:
                st = jnp.where(key_idx <= qry_idx, st, -1e30)
            m = m_scr[hh]
            m_new = jnp.maximum(m, jnp.max(st, axis=0, keepdims=True))
            alpha = jnp.exp2(m - m_new)
            pt = jnp.exp2(st - m_new)
            l_scr[hh] = l_scr[hh] * alpha + jnp.sum(pt, axis=0, keepdims=True)
            vt = vt_ref[j, hh * V_HEAD_DIM:(hh + 1) * V_HEAD_DIM, :]
            acc_scr[hh] = acc_scr[hh] * alpha + _dot(vt, pt.astype(bf16))
            m_scr[hh] = m_new

    def body(j, carry):
        step(j, False)
        return carry

    lax.fori_loop(0, i, body, 0)
    step(i, True)
    out_t = jnp.concatenate([acc_scr[0] / l_scr[0], acc_scr[1] / l_scr[1]], axis=0)
    o_ref[...] = out_t.T


def _flash(q, k, vt, B, S, tq):
    M = q.shape[0]
    nq = S // tq
    pairs = N_HEADS // 2
    return pl.pallas_call(
        functools.partial(_flash_body, tq=tq),
        grid=(B, pairs, nq),
        in_specs=[pl.BlockSpec((tq, 2 * LANE), lambda b, g, i: (b * nq + i, g)),
                  pl.BlockSpec((S, 2 * LANE), lambda b, g, i: (b, g)),
                  pl.BlockSpec((nq, 2 * V_HEAD_DIM, tq), lambda b, g, i: (b, g, 0))],
        out_specs=pl.BlockSpec((tq, 2 * V_HEAD_DIM), lambda b, g, i: (b * nq + i, g)),
        out_shape=jax.ShapeDtypeStruct((M, N_HEADS * V_HEAD_DIM), f32),
        scratch_shapes=[pltpu.VMEM((2, 1, tq), f32), pltpu.VMEM((2, 1, tq), f32),
                        pltpu.VMEM((2, V_HEAD_DIM, tq), f32)],
        compiler_params=_params("parallel", "parallel", "arbitrary"),
        name="flash_prompt",
    )(q, k, vt)


def _mm_body(a_ref, b_ref, o_ref):
    o_ref[...] = _dot(a_ref[...].astype(bf16), b_ref[...])


def _mm(a, b, name):
    return pl.pallas_call(
        _mm_body,
        out_shape=jax.ShapeDtypeStruct((a.shape[0], b.shape[1]), f32),
        compiler_params=pltpu.CompilerParams(vmem_limit_bytes=VMEM_LIMIT),
        name=name,
    )(a, b)


def _decode_body(pt_ref, q_ref, knew_ref, qabs_ref, qpe_ref, cnew_ref, wukt_ref, cache_ckv, cache_kpet,
                 o_ref, ckv_buf, kpe_buf, sems, lhs, m_scr, l_scr, acc_scr, *, n_pages, cp):
    b = pl.program_id(0)
    c = pl.program_id(1)
    nc = pl.num_programs(1)
    total = pl.num_programs(0) * nc
    flat = b * nc + c
    slot = flat % 2
    nope_rows = wukt_ref.shape[0]
    page = cache_ckv.shape[1]

    def copies(flat_idx, slot_idx):
        base = (flat_idx // nc) * n_pages + (flat_idx % nc) * cp
        out = []
        for p in range(cp):
            phys = pt_ref[base + p]
            rows = pl.ds(p * page, page)
            out.append(pltpu.make_async_copy(cache_ckv.at[phys], ckv_buf.at[slot_idx, rows], sems.at[0, slot_idx]))
            out.append(pltpu.make_async_copy(cache_kpet.at[phys], kpe_buf.at[slot_idx, p], sems.at[1, slot_idx]))
        return out

    @pl.when(flat == 0)
    def _():
        for cpy in copies(flat, slot):
            cpy.start()

    @pl.when(flat + 1 < total)
    def _():
        for cpy in copies(flat + 1, 1 - slot):
            cpy.start()

    @pl.when(c == 0)
    def _():
        lhs[0:nope_rows, :] = wukt_ref[...]
        lhs[nope_rows:nope_rows + BF16_SUBLANES, :] = qabs_ref[0]
        s_new = jnp.sum(q_ref[0].astype(f32) * knew_ref[0].astype(f32), axis=-1, keepdims=True)
        m_scr[...] = s_new
        l_scr[...] = jnp.ones_like(l_scr)
        acc_scr[...] = jnp.broadcast_to(cnew_ref[0], acc_scr.shape)

    for cpy in copies(flat, slot):
        cpy.wait()

    ct = ckv_buf[slot].astype(bf16)
    kt = lax.dot_general(lhs[...], ct, _NT, preferred_element_type=f32)
    tokens = ct.shape[0]
    kn = kt[0:nope_rows].reshape(N_HEADS, nope_rows // N_HEADS, tokens)
    ssq = jnp.sum(kn * kn, axis=1)
    num = kt[nope_rows:nope_rows + N_HEADS]
    qpe = qpe_ref[0]
    ones = jnp.ones(qpe.shape, bf16)
    pe_num, pe_ssq = [], []
    for p in range(cp):
        kp = kpe_buf[slot, p]
        pe_num.append(_dot(qpe, kp.astype(bf16)))
        pe_ssq.append(_dot(ones, (kp * kp).astype(bf16)))
    num = num + jnp.concatenate(pe_num, axis=1)
    ssq = ssq + jnp.concatenate(pe_ssq, axis=1)
    s = num * lax.rsqrt(ssq * (1.0 / QK_HEAD_DIM) + RMS_EPS)
    m = m_scr[...]
    m_new = jnp.maximum(m, jnp.max(s, axis=-1, keepdims=True))
    alpha = jnp.exp2(m - m_new)
    p = jnp.exp2(s - m_new)
    l_scr[...] = l_scr[...] * alpha + jnp.sum(p, axis=-1, keepdims=True)
    acc_scr[...] = acc_scr[...] * alpha + _dot(p.astype(bf16), ct)
    m_scr[...] = m_new

    @pl.when(c == nc - 1)
    def _():
        o_ref[0] = acc_scr[...] / l_scr[...]


def _decode(page_table, q3, knew3, qabs, qpe, cnew, wukt, cache_ckv, cache_kpet, cp):
    DB, n_pages = page_table.shape
    page, kv_rank = cache_ckv.shape[1:]
    rope = cache_kpet.shape[1]
    nc = n_pages // cp
    T = cp * page
    per_seq = lambda a: pl.BlockSpec((1,) + a.shape[1:], lambda b, c, pt: (b,) + (0,) * (a.ndim - 1))
    grid_spec = pltpu.PrefetchScalarGridSpec(
        num_scalar_prefetch=1,
        grid=(DB, nc),
        in_specs=[per_seq(q3), per_seq(knew3), per_seq(qabs), per_seq(qpe), per_seq(cnew),
                  pl.BlockSpec(wukt.shape, lambda b, c, pt: (0, 0)),
                  pl.BlockSpec(memory_space=pl.ANY), pl.BlockSpec(memory_space=pl.ANY)],
        out_specs=pl.BlockSpec((1, N_HEADS, kv_rank), lambda b, c, pt: (b, 0, 0)),
        scratch_shapes=[pltpu.VMEM((2, T, kv_rank), f32), pltpu.VMEM((2, cp, rope, page), f32),
                        pltpu.SemaphoreType.DMA((2, 2)),
                        pltpu.VMEM((wukt.shape[0] + BF16_SUBLANES, kv_rank), bf16),
                        pltpu.VMEM((N_HEADS, 1), f32), pltpu.VMEM((N_HEADS, 1), f32),
                        pltpu.VMEM((N_HEADS, kv_rank), f32)])
    return pl.pallas_call(
        functools.partial(_decode_body, n_pages=n_pages, cp=cp),
        grid_spec=grid_spec,
        out_shape=jax.ShapeDtypeStruct((DB, N_HEADS, kv_rank), f32),
        compiler_params=_params("arbitrary", "arbitrary"),
        name="decode_attn",
    )(page_table.reshape(-1), q3, knew3, qabs, qpe, cnew, wukt, cache_ckv, cache_kpet)


def _outproj_body(x_ref, mp_ref, attn_ref, gatt_ref, wo_ref, gffn_ref, x1_ref, h2_ref):
    attn = attn_ref[...]
    na = (_rms(attn, attn.shape[-1]) * gatt_ref[...]).astype(bf16)
    mixed = jnp.concatenate([mp_ref[...], na], axis=1)
    x1 = x_ref[...] + _dot(mixed, wo_ref[...])
    x1_ref[...] = x1
    h2_ref[...] = (_rms(x1, x1.shape[-1]) * gffn_ref[...]).astype(bf16)


def _outproj(x, mp, attn, w, tm):
    M, D = x.shape
    const = lambda a: pl.BlockSpec(a.shape, lambda i: (0, 0))
    row = lambda n: pl.BlockSpec((tm, n), lambda i: (i, 0))
    return pl.pallas_call(
        _outproj_body,
        grid=(M // tm,),
        in_specs=[row(D), row(mp.shape[1]), row(attn.shape[1]),
                  const(w['g_out_attn']), const(w['w_o']), const(w['g_ffn'])],
        out_specs=(row(D), row(D)),
        out_shape=(jax.ShapeDtypeStruct((M, D), f32), jax.ShapeDtypeStruct((M, D), bf16)),
        compiler_params=_params("parallel"),
        name="out_proj",
    )(x, mp, attn, w['g_out_attn'], w['w_o'], w['g_ffn'])


def _ffn_finish(f, g, g1, g2, h2, x1_ref, wu_ref, wd_ref, cw_ref, cb_ref, y_ref, acc):
    gate = cb_ref[...] + g2 * cw_ref[0:1, :] + g1 * cw_ref[1:2, :] + g * cw_ref[2:3, :]
    act = (gate * jax.nn.sigmoid(gate)) * _dot(h2, wu_ref[...])
    part = _dot(act.astype(bf16), wd_ref[...])

    @pl.when(f == 0)
    def _():
        acc[...] = x1_ref[...] + part

    @pl.when(f > 0)
    def _():
        acc[...] += part

    @pl.when(f == pl.num_programs(1) - 1)
    def _():
        y_ref[...] = acc[...]


def _ffn_prompt_body(h2_ref, halo_ref, x1_ref, wg_ref, wu_ref, wd_ref, cw_ref, cb_ref,
                     y_ref, tail_ref, acc, gbuf, *, tm, tiles_per_seq):
    i = pl.program_id(0)
    f = pl.program_id(1)
    hb = BF16_SUBLANES
    h2 = h2_ref[...]
    g = _dot(h2, wg_ref[...])
    gh = _dot(halo_ref[...], wg_ref[...])
    gbuf[0:hb, :] = jnp.where(i % tiles_per_seq == 0, 0.0, gh)
    gbuf[hb:hb + tm, :] = g
    tail_ref[0] = gbuf[tm:tm + hb, :]
    g1 = gbuf[hb - 1:hb - 1 + tm, :]
    g2 = gbuf[hb - 2:hb - 2 + tm, :]
    _ffn_finish(f, g, g1, g2, h2, x1_ref, wu_ref, wd_ref, cw_ref, cb_ref, y_ref, acc)


def _ffn_prompt(h2, x1, S, w, tm, tf):
    M, D = x1.shape
    F = w['w_gate'].shape[1]
    hb = BF16_SUBLANES
    nt = M // tm
    halo_blocks = tm // hb
    y, tail = pl.pallas_call(
        functools.partial(_ffn_prompt_body, tm=tm, tiles_per_seq=S // tm),
        grid=(nt, F // tf),
        in_specs=[pl.BlockSpec((tm, D), lambda i, f: (i, 0)),
                  pl.BlockSpec((hb, D), lambda i, f: (jnp.maximum(i * halo_blocks - 1, 0), 0)),
                  pl.BlockSpec((tm, D), lambda i, f: (i, 0)),
                  pl.BlockSpec((D, tf), lambda i, f: (0, f)),
                  pl.BlockSpec((D, tf), lambda i, f: (0, f)),
                  pl.BlockSpec((tf, D), lambda i, f: (f, 0)),
                  pl.BlockSpec((CONV_WIDTH, tf), lambda i, f: (0, f)),
                  pl.BlockSpec((1, tf), lambda i, f: (0, f))],
        out_specs=(pl.BlockSpec((tm, D), lambda i, f: (i, 0)),
                   pl.BlockSpec((1, hb, tf), lambda i, f: (i, 0, f))),
        out_shape=(jax.ShapeDtypeStruct((M, D), f32), jax.ShapeDtypeStruct((nt, hb, F), f32)),
        scratch_shapes=[pltpu.VMEM((tm, D), f32), pltpu.VMEM((tm + hb, tf), f32)],
        compiler_params=_params("parallel", "arbitrary"),
        name="ffn_prompt",
    )(h2, h2, x1, w['w_gate'], w['w_up'], w['w_down'], w['conv_w'], w['conv_b'])
    return y, tail


def _ffn_sample_body(h2_ref, st_ref, x1_ref, wg_ref, wu_ref, wd_ref, cw_ref, cb_ref, y_ref, g_ref, acc):
    f = pl.program_id(1)
    h2 = h2_ref[...]
    g = _dot(h2, wg_ref[...])
    g_ref[...] = g
    _ffn_finish(f, g, st_ref[1], st_ref[0], h2, x1_ref, wu_ref, wd_ref, cw_ref, cb_ref, y_ref, acc)


def _ffn_sample(h2, x1, st_t, w, tf):
    M, D = x1.shape
    F = w['w_gate'].shape[1]
    return pl.pallas_call(
        _ffn_sample_body,
        grid=(1, F // tf),
        in_specs=[pl.BlockSpec((M, D), lambda i, f: (0, 0)),
                  pl.BlockSpec((CONV_WIDTH - 1, M, tf), lambda i, f: (0, 0, f)),
                  pl.BlockSpec((M, D), lambda i, f: (0, 0)),
                  pl.BlockSpec((D, tf), lambda i, f: (0, f)),
                  pl.BlockSpec((D, tf), lambda i, f: (0, f)),
                  pl.BlockSpec((tf, D), lambda i, f: (f, 0)),
                  pl.BlockSpec((CONV_WIDTH, tf), lambda i, f: (0, f)),
                  pl.BlockSpec((1, tf), lambda i, f: (0, f))],
        out_specs=(pl.BlockSpec((M, D), lambda i, f: (0, 0)),
                   pl.BlockSpec((M, tf), lambda i, f: (0, f))),
        out_shape=(jax.ShapeDtypeStruct((M, D), f32), jax.ShapeDtypeStruct((M, F), f32)),
        scratch_shapes=[pltpu.VMEM((M, D), f32)],
        compiler_params=_params("parallel", "arbitrary"),
        name="ffn_sample",
    )(h2, st_t, x1, w['w_gate'], w['w_up'], w['w_down'], w['conv_w'], w['conv_b'])


def _qk_gain(g):
    g_pe = g[QK_NOPE_DIM:]
    return jnp.concatenate([g[:QK_NOPE_DIM], g_pe, g_pe])


def _rope_table(pos):
    half = QK_ROPE_DIM // 2
    inv = ROPE_BASE ** (-jnp.arange(0, QK_ROPE_DIM, 2, dtype=f32) / QK_ROPE_DIM)
    ang = pos.astype(f32)[:, None] * inv[None, :]
    cos, sin = jnp.cos(ang), jnp.sin(ang)
    P = pos.shape[0]
    z = lambda n: jnp.zeros((P, n), f32)
    tail = LANE - QK_HEAD_DIM
    c = jnp.concatenate([jnp.ones((P, QK_NOPE_DIM), f32), cos, cos, z(tail)], axis=1)
    s_lo = jnp.concatenate([z(QK_NOPE_DIM), -sin, z(half), z(tail)], axis=1)
    s_hi = jnp.concatenate([z(QK_NOPE_DIM), z(half), sin, z(tail)], axis=1)
    return jnp.concatenate([c, s_lo, s_hi], axis=1)


def _prep_weights(g_mix, w_in, g_q_a, w_uq, g_qn, g_kv_a, w_uk, g_kn, w_uv, w_pool, s_pool,
                  g_out, w_o, g_ffn, w_gate, w_up, conv_w, conv_b, w_down):
    D = w_in.shape[0]
    q_rank = g_q_a.shape[0]
    kv_rank = g_kv_a.shape[0]
    pool_w = s_pool.shape[0]
    head_pad = LANE - QK_HEAD_DIM
    half = QK_ROPE_DIM // 2
    main = pool_w + q_rank + kv_rank
    w_in_p = jnp.concatenate([w_in[:, :main], jnp.zeros((D, QK_NOPE_DIM), f32), w_in[:, main:],
                              jnp.zeros((D, head_pad), f32)], axis=1).astype(bf16)
    w_uq_h = w_uq.reshape(q_rank, N_HEADS, QK_HEAD_DIM)
    w_uq_p = jnp.pad(w_uq_h, ((0, 0), (0, 0), (0, head_pad)))
    lo = w_uq_h[:, :, QK_NOPE_DIM:QK_NOPE_DIM + half]
    hi = w_uq_h[:, :, QK_NOPE_DIM + half:]
    w_uq_rot = jnp.concatenate([jnp.zeros((q_rank, N_HEADS, QK_NOPE_DIM), f32), -hi, lo,
                                jnp.zeros((q_rank, N_HEADS, head_pad), f32)], axis=2)
    w_uq_p = jnp.concatenate([w_uq_p.reshape(q_rank, N_HEADS * LANE),
                              w_uq_rot.reshape(q_rank, N_HEADS * LANE)], axis=1).astype(bf16)
    gq = _qk_gain(g_qn) * _qk_gain(g_kn) * (QK_HEAD_DIM ** -0.5 * LOG2_E)
    gq = jnp.tile(jnp.pad(gq, (0, head_pad)), N_HEADS)[None, :]
    w_uk_p = jnp.pad(w_uk, ((0, 0), (0, 0), (0, LANE - QK_NOPE_DIM))).reshape(kv_rank, N_HEADS * LANE)
    w_uk_p = w_uk_p.astype(bf16)
    w_uv_t = w_uv.reshape(kv_rank, N_HEADS * V_HEAD_DIM).T.astype(bf16)
    w_uk_t = w_uk.reshape(kv_rank, N_HEADS * QK_NOPE_DIM).T.astype(bf16)
    eye = jnp.eye(N_HEADS, dtype=f32)
    w_uk_bd = jnp.einsum('chd,hg->hdgc', jnp.pad(w_uk, ((0, 0), (0, 0), (0, LANE - QK_NOPE_DIM))), eye)
    w_uk_bd = w_uk_bd.reshape(N_HEADS * LANE, N_HEADS * kv_rank).astype(bf16)
    w_uv_bd = jnp.einsum('chv,hg->hcgv', w_uv, eye).reshape(N_HEADS * kv_rank, N_HEADS * V_HEAD_DIM).astype(bf16)
    return {
        'pool_w': pool_w,
        'g_mix': g_mix[None, :], 'w_in': w_in_p, 'g_q_a': g_q_a[None, :], 'w_uq': w_uq_p, 'g_q': gq,
        'g_kv_a': g_kv_a[None, :], 'w_uk': w_uk_p, 'w_uv_t': w_uv_t, 'w_uk_t': w_uk_t, 'w_uk_bd': w_uk_bd, 'w_uv_bd': w_uv_bd,
        'w_pool': w_pool.astype(bf16), 's_pool': s_pool[None, :],
        'g_out_pool': g_out[None, :pool_w], 'g_out_attn': g_out[None, pool_w:],
        'w_o': w_o.astype(bf16), 'g_ffn': g_ffn[None, :],
        'w_gate': w_gate.astype(bf16), 'w_up': w_up.astype(bf16), 'w_down': w_down.astype(bf16),
        'conv_w': conv_w, 'conv_b': conv_b[None, :],
    }


def _tile(n, pref):
    t = min(n, pref)
    assert n % t == 0, (n, pref)
    return t


def _ffn_cols(F):
    half = F // 2
    return half if half % LANE == 0 else F


def kernel(x_prompt, x_sample, cache_ckv, cache_kpe, state_pool, state_conv, page_table, g_mix, w_in, g_q_a,
           w_uq, g_qn, g_kv_a, w_uk, g_kn, w_uv, w_pool, s_pool, g_out, w_o, g_ffn, w_gate, w_up, conv_w,
           conv_b, w_down):
    depth = g_mix.shape[0]
    assert depth == 1, "single-layer trunk only"
    B, S, D = x_prompt.shape
    DB, DS = x_sample.shape[:2]
    assert DS == 1, "one new token per sequence"
    n_pages = page_table.shape[1]
    page = cache_ckv.shape[2]
    past_len = n_pages * page
    l = 0
    w = _prep_weights(g_mix[l], w_in[l], g_q_a[l], w_uq[l], g_qn[l], g_kv_a[l], w_uk[l], g_kn[l], w_uv[l],
                      w_pool[l], s_pool[l], g_out[l], w_o[l], g_ffn[l], w_gate[l], w_up[l], conv_w[l],
                      conv_b[l], w_down[l])
    kv_rank = g_kv_a.shape[1]
    F = w_gate.shape[2]
    tf = _ffn_cols(F)

    xp = x_prompt.reshape(B * S, D)
    tq = _tile(S, 512)
    tm = _tile(tq, 256)
    rope_p = _rope_table(jnp.arange(S, dtype=jnp.int32))
    u_p, q_p, ckv_p, kpe_p, k_p, vt_p = _proj(xp, rope_p, S // tm, w, tm, tq)
    mp_p = _pool_prompt(u_p, B, S, w, _tile(S, 512))
    attn_p = _flash(q_p, k_p, vt_p, B, S, tq)
    x1_p, h2_p = _outproj(xp, mp_p, attn_p, w, _tile(S, 512))
    tmf = _tile(S, 512)
    y_p, tail = _ffn_prompt(h2_p, x1_p, S, w, tmf, tf)
    tiles_per_seq = S // tmf
    conv_p = tail[tiles_per_seq - 1::tiles_per_seq, BF16_SUBLANES - (CONV_WIDTH - 1):, :]

    xs = x_sample.reshape(DB, D)
    pos_s = past_len + jnp.arange(DS, dtype=jnp.int32)
    rope_s = jnp.tile(_rope_table(pos_s), (DB, 1))
    u_s, q_s, ckv_s, kpe_s, k_s, _ = _proj(xs, rope_s, 1, w, DB, DB)
    st_pool_t = jnp.transpose(state_pool[l], (1, 0, 2))
    mp_s = _pool_sample(u_s, st_pool_t, past_len, w)
    qabs = _mm(q_s, w['w_uk_bd'], "absorb_q").reshape(DB, N_HEADS, kv_rank).astype(bf16)
    qabs = jnp.pad(qabs, ((0, 0), (0, BF16_SUBLANES - N_HEADS), (0, 0)))
    q3 = q_s.reshape(DB, N_HEADS, LANE)
    qpe = q3[:, :, QK_NOPE_DIM:QK_HEAD_DIM]
    cp = _tile(n_pages, 32)
    ctx = _decode(page_table, q3, k_s.reshape(DB, N_HEADS, LANE), qabs, qpe, ckv_s.reshape(DB, 1, kv_rank),
                  w['w_uk_t'], cache_ckv[l], jnp.swapaxes(cache_kpe[l], 1, 2), cp)
    attn_s = _mm(ctx.reshape(DB, N_HEADS * kv_rank), w['w_uv_bd'], "value_up")
    x1_s, h2_s = _outproj(xs, mp_s, attn_s, w, DB)
    st_conv_t = jnp.transpose(state_conv[l], (1, 0, 2))
    y_s, g_s = _ffn_sample(h2_s, x1_s, st_conv_t, w, tf)

    kpe_sl = slice(QK_NOPE_DIM, QK_HEAD_DIM)
    P = POOL_STATE_LEN
    return (
        y_p.reshape(B, S, D),
        y_s.reshape(DB, DS, D),
        ckv_p.reshape(1, B, S, kv_rank),
        kpe_p[:, kpe_sl].reshape(1, B, S, QK_ROPE_DIM),
        u_p.reshape(B, S, -1)[None, :, S - P:, :],
        conv_p[None],
        ckv_s.reshape(1, DB, DS, kv_rank),
        kpe_s[:, kpe_sl].reshape(1, DB, DS, QK_ROPE_DIM),
        jnp.concatenate([state_pool[l], u_s[:, None, :]], axis=1)[None, :, -P:, :],
        jnp.concatenate([state_conv[l], g_s[:, None, :]], axis=1)[None, :, -(CONV_WIDTH - 1):, :],
    )
```

```python
import functools

import jax
import jax.numpy as jnp
from jax import lax
from jax.experimental import pallas as pl
from jax.experimental.pallas import tpu as pltpu

N_HEADS = 8
QK_NOPE_DIM = 64
QK_ROPE_DIM = 32
QK_HEAD_DIM = QK_NOPE_DIM + QK_ROPE_DIM
V_HEAD_DIM = 64
POOL_WINDOWS = (2, 4, 8, 16)
POOL_STATE_LEN = max(POOL_WINDOWS) - 1
CONV_WIDTH = 3
ROPE_BASE = 10000.0
RMS_EPS = 1e-6
LOG2_E = 1.4426950408889634

LANE = 128
BF16_SUBLANES = 16
VMEM_LIMIT = 48 * 1024 * 1024
DECODE_SLOTS = 3

_NT = (((1,), (1,)), ((), ()))

bf16 = jnp.bfloat16
f32 = jnp.float32


def _rms(x, width):
    return x * lax.rsqrt(jnp.sum(x * x, axis=-1, keepdims=True) * (1.0 / width) + RMS_EPS)


def _dot(a, b):
    return jnp.dot(a, b, preferred_element_type=f32)


def _params(*sem):
    return pltpu.CompilerParams(dimension_semantics=sem, vmem_limit_bytes=VMEM_LIMIT)


def _proj_body(x_ref, rope_ref, gmix_ref, win_ref, gqa_ref, wuq_ref, gq_ref, gkva_ref, wuk_ref, wuvt_ref,
               u_ref, q_ref, ckv_ref, kpe_ref, k_ref, vt_ref, *, pool_w, q_rank, kv_rank):
    x = x_ref[...]
    h = (_rms(x, x.shape[-1]) * gmix_ref[...]).astype(bf16)
    proj = _dot(h, win_ref[...])
    u_ref[...] = proj[:, :pool_w]
    o = pool_w
    cq = (_rms(proj[:, o:o + q_rank], q_rank) * gqa_ref[...]).astype(bf16)
    o += q_rank
    ckv = _rms(proj[:, o:o + kv_rank], kv_rank) * gkva_ref[...]
    o += kv_rank
    ckv_ref[...] = ckv
    cos = rope_ref[:, 0:LANE]
    sin_lo = rope_ref[:, LANE:2 * LANE]
    sin_hi = rope_ref[:, 2 * LANE:3 * LANE]
    sin_abs = sin_hi - sin_lo
    half = QK_ROPE_DIM // 2
    raw_kpe = proj[:, o:o + LANE]
    kpe = raw_kpe * cos + pltpu.roll(raw_kpe, LANE - half, 1) * sin_lo + pltpu.roll(raw_kpe, half, 1) * sin_hi
    kpe_ref[...] = kpe
    qraw = _dot(cq, wuq_ref[...])
    hq = N_HEADS * LANE
    ckv_b = ckv.astype(bf16)
    kn = _dot(ckv_b, wuk_ref[...])
    vt_ref[0] = lax.dot_general(wuvt_ref[...], ckv_b, _NT, preferred_element_type=f32).astype(bf16)
    for hd in range(N_HEADS):
        sl = slice(hd * LANE, (hd + 1) * LANE)
        qh = qraw[:, sl] * cos + qraw[:, hq + hd * LANE:hq + (hd + 1) * LANE] * sin_abs
        q_ref[:, sl] = (_rms(qh, QK_HEAD_DIM) * gq_ref[:, sl]).astype(bf16)
        kh = kn[:, sl] + kpe
        k_ref[:, sl] = _rms(kh, QK_HEAD_DIM).astype(bf16)


def _proj(x, rope, n_rope_blocks, w, tm, slab):
    M, D = x.shape
    pool_w = w['pool_w']
    q_rank = w['g_q_a'].shape[1]
    kv_rank = w['g_kv_a'].shape[1]
    hq = N_HEADS * LANE
    hv = N_HEADS * V_HEAD_DIM
    per_slab = slab // tm
    const = lambda a: pl.BlockSpec(a.shape, lambda i: (0, 0))
    row = lambda n: pl.BlockSpec((tm, n), lambda i: (i, 0))
    out_shapes = (
        jax.ShapeDtypeStruct((M, pool_w), f32), jax.ShapeDtypeStruct((M, hq), bf16),
        jax.ShapeDtypeStruct((M, kv_rank), f32), jax.ShapeDtypeStruct((M, LANE), f32),
        jax.ShapeDtypeStruct((M, hq), bf16), jax.ShapeDtypeStruct((M // slab, hv, slab), bf16))
    return pl.pallas_call(
        functools.partial(_proj_body, pool_w=pool_w, q_rank=q_rank, kv_rank=kv_rank),
        grid=(M // tm,),
        in_specs=[row(D), pl.BlockSpec((tm, 3 * LANE), lambda i: (i % n_rope_blocks, 0)),
                  const(w['g_mix']), const(w['w_in']), const(w['g_q_a']), const(w['w_uq']),
                  const(w['g_q']), const(w['g_kv_a']), const(w['w_uk']), const(w['w_uv_t'])],
        out_specs=(row(pool_w), row(hq), row(kv_rank), row(LANE), row(hq),
                   pl.BlockSpec((1, hv, tm), lambda i: (i // per_slab, 0, i % per_slab))),
        out_shape=out_shapes,
        compiler_params=_params("parallel"),
        name="in_proj",
    )(x, rope, w['g_mix'], w['w_in'], w['g_q_a'], w['w_uq'], w['g_q'], w['g_kv_a'], w['w_uk'], w['w_uv_t'])


def _pool_finish(diffs, wpool_ref, spool_ref, gout_ref, o_ref):
    ys = []
    for g, d in enumerate(diffs):
        sl = slice(g * LANE, (g + 1) * LANE)
        ys.append(_dot(d.astype(bf16), wpool_ref[g]) * spool_ref[:, sl])
    width = LANE * len(ys)
    ssq = sum(jnp.sum(y * y, axis=-1, keepdims=True) for y in ys)
    scale = lax.rsqrt(ssq * (1.0 / width) + RMS_EPS)
    for g, y in enumerate(ys):
        sl = slice(g * LANE, (g + 1) * LANE)
        o_ref[:, sl] = (y * scale * gout_ref[:, sl]).astype(o_ref.dtype)


def _pool_prompt_body(u_ref, wpool_ref, spool_ref, gout_ref, o_ref, buf, *, ts):
    j = pl.program_id(1)
    halo = POOL_STATE_LEN + 1

    @pl.when(j == 0)
    def _():
        buf[0:halo, :] = jnp.zeros((halo, buf.shape[1]), f32)

    @pl.when(j > 0)
    def _():
        buf[0:halo, :] = buf[ts:ts + halo, :]

    buf[halo:halo + ts, :] = u_ref[...]
    pos = j * ts + lax.broadcasted_iota(jnp.int32, (ts, 1), 0)
    diffs = []
    for g, wdw in enumerate(POOL_WINDOWS):
        sl = slice(g * LANE, (g + 1) * LANE)
        x = buf[halo:halo + ts, sl]
        win = x
        for k in range(1, wdw):
            win = win + buf[halo - k:halo - k + ts, sl]
        cnt = jnp.minimum(wdw, pos + 1).astype(f32)
        diffs.append(win / cnt - x)
    _pool_finish(diffs, wpool_ref, spool_ref, gout_ref, o_ref)


def _pool_prompt(u, B, S, w, ts):
    M, W = u.shape
    ns = S // ts
    const2 = lambda a: pl.BlockSpec(a.shape, lambda b, j: (0,) * a.ndim)
    return pl.pallas_call(
        functools.partial(_pool_prompt_body, ts=ts),
        grid=(B, ns),
        in_specs=[pl.BlockSpec((ts, W), lambda b, j: (b * ns + j, 0)),
                  const2(w['w_pool']), const2(w['s_pool']), const2(w['g_out_pool'])],
        out_specs=pl.BlockSpec((ts, W), lambda b, j: (b * ns + j, 0)),
        out_shape=jax.ShapeDtypeStruct((M, W), bf16),
        scratch_shapes=[pltpu.VMEM((ts + POOL_STATE_LEN + 1, W), f32)],
        compiler_params=_params("arbitrary", "arbitrary"),
        name="pool_prompt",
    )(u, w['w_pool'], w['s_pool'], w['g_out_pool'])


def _pool_sample_body(u_ref, st_ref, wpool_ref, spool_ref, gout_ref, o_ref, *, pos):
    diffs = []
    for g, wdw in enumerate(POOL_WINDOWS):
        sl = slice(g * LANE, (g + 1) * LANE)
        x = u_ref[:, sl]
        win = x
        for k in range(1, wdw):
            win = win + st_ref[POOL_STATE_LEN - k, :, sl]
        diffs.append(win / float(min(wdw, pos + 1)) - x)
    _pool_finish(diffs, wpool_ref, spool_ref, gout_ref, o_ref)


def _pool_sample(u, st_t, pos, w):
    M, W = u.shape
    return pl.pallas_call(
        functools.partial(_pool_sample_body, pos=pos),
        out_shape=jax.ShapeDtypeStruct((M, W), bf16),
        compiler_params=pltpu.CompilerParams(vmem_limit_bytes=VMEM_LIMIT),
        name="pool_sample",
    )(u, st_t, w['w_pool'], w['s_pool'], w['g_out_pool'])


def _flash_body(q_ref, k_ref, vt_ref, o_ref, m_scr, l_scr, acc_scr, *, tq):
    i = pl.program_id(2)
    heads = m_scr.shape[0]
    key_idx = lax.broadcasted_iota(jnp.int32, (tq, tq), 0)
    qry_idx = lax.broadcasted_iota(jnp.int32, (tq, tq), 1)
    m_scr[...] = jnp.full(m_scr.shape, -1e30, f32)
    l_scr[...] = jnp.zeros(l_scr.shape, f32)
    acc_scr[...] = jnp.zeros(acc_scr.shape, f32)

    def step(j, masked):
        start = pl.multiple_of(j * tq, tq)
        for hh in range(heads):
            sl = slice(hh * LANE, (hh + 1) * LANE)
            k = k_ref[pl.ds(start, tq), sl]
            st = lax.dot_general(k, q_ref[:, sl], _NT, preferred_element_type=f32)
            if masked:
                st = jnp.where(key_idx <= qry_idx, st, -1e30)
            m = m_scr[hh]
            m_new = jnp.maximum(m, jnp.max(st, axis=0, keepdims=True))
            alpha = jnp.exp2(m - m_new)
            pt = jnp.exp2(st - m_new)
            l_scr[hh] = l_scr[hh] * alpha + jnp.sum(pt, axis=0, keepdims=True)
            vt = vt_ref[j, hh * V_HEAD_DIM:(hh + 1) * V_HEAD_DIM, :]
            acc_scr[hh] = acc_scr[hh] * alpha + _dot(vt, pt.astype(bf16))
            m_scr[hh] = m_new

    def body(j, carry):
        step(j, False)
        return carry

    lax.fori_loop(0, i, body, 0)
    step(i, True)
    out_t = jnp.concatenate([acc_scr[hh] / l_scr[hh] for hh in range(heads)], axis=0)
    o_ref[...] = out_t.T


def _flash(q, k, vt, B, S, tq, heads):
    M = q.shape[0]
    nq = S // tq
    groups = N_HEADS // heads
    return pl.pallas_call(
        functools.partial(_flash_body, tq=tq),
        grid=(B, groups, nq),
        in_specs=[pl.BlockSpec((tq, heads * LANE), lambda b, g, i: (b * nq + i, g)),
                  pl.BlockSpec((S, heads * LANE), lambda b, g, i: (b, g)),
                  pl.BlockSpec((nq, heads * V_HEAD_DIM, tq), lambda b, g, i: (b, g, 0))],
        out_specs=pl.BlockSpec((tq, heads * V_HEAD_DIM), lambda b, g, i: (b * nq + i, g)),
        out_shape=jax.ShapeDtypeStruct((M, N_HEADS * V_HEAD_DIM), f32),
        scratch_shapes=[pltpu.VMEM((heads, 1, tq), f32), pltpu.VMEM((heads, 1, tq), f32),
                        pltpu.VMEM((heads, V_HEAD_DIM, tq), f32)],
        compiler_params=_params("parallel", "parallel", "arbitrary"),
        name="flash_prompt",
    )(q, k, vt)


def _mm_body(a_ref, b_ref, o_ref):
    o_ref[...] = _dot(a_ref[...].astype(bf16), b_ref[...])


def _mm(a, b, name):
    return pl.pallas_call(
        _mm_body,
        out_shape=jax.ShapeDtypeStruct((a.shape[0], b.shape[1]), f32),
        compiler_params=pltpu.CompilerParams(vmem_limit_bytes=VMEM_LIMIT),
        name=name,
    )(a, b)


def _decode_body(pt_ref, q_ref, knew_ref, qabs_ref, qpe_ref, cnew_ref, wukt_ref, cache_ckv, cache_kpet,
                 o_ref, ckv_buf, kpe_buf, sems, lhs, m_scr, l_scr, acc_scr, *, n_pages, cp):
    b = pl.program_id(0)
    c = pl.program_id(1)
    nc = pl.num_programs(1)
    total = pl.num_programs(0) * nc
    flat = b * nc + c
    n_slots = ckv_buf.shape[0]
    ahead = n_slots - 1
    slot = flat % n_slots
    nope_rows = wukt_ref.shape[0]
    page = cache_ckv.shape[1]

    def copies(flat_idx, slot_idx):
        base = (flat_idx // nc) * n_pages + (flat_idx % nc) * cp
        out = []
        for p in range(cp):
            phys = pt_ref[base + p]
            rows = pl.ds(p * page, page)
            out.append(pltpu.make_async_copy(cache_ckv.at[phys], ckv_buf.at[slot_idx, rows], sems.at[0, slot_idx]))
            out.append(pltpu.make_async_copy(cache_kpet.at[phys], kpe_buf.at[slot_idx, p], sems.at[1, slot_idx]))
        return out

    @pl.when(flat == 0)
    def _():
        for k in range(ahead):
            for cpy in copies(jnp.minimum(k, total - 1), k):
                cpy.start()

    @pl.when(c == 0)
    def _():
        lhs[0:nope_rows, :] = wukt_ref[...]
        lhs[nope_rows:nope_rows + BF16_SUBLANES, :] = qabs_ref[0]
        s_new = jnp.sum(q_ref[0].astype(f32) * knew_ref[0].astype(f32), axis=-1, keepdims=True)
        m_scr[...] = s_new
        l_scr[...] = jnp.ones_like(l_scr)
        acc_scr[...] = jnp.broadcast_to(cnew_ref[0], acc_scr.shape)

    for cpy in copies(flat, slot):
        cpy.wait()

    ct = ckv_buf[slot].astype(bf16)
    for cpy in copies(jnp.minimum(flat + ahead, total - 1), (flat + ahead) % n_slots):
        cpy.start()
    kt = lax.dot_general(lhs[...], ct, _NT, preferred_element_type=f32)
    tokens = ct.shape[0]
    kn = kt[0:nope_rows].reshape(N_HEADS, nope_rows // N_HEADS, tokens)
    ssq = jnp.sum(kn * kn, axis=1)
    num = kt[nope_rows:nope_rows + N_HEADS]
    qpe = qpe_ref[0]
    ones = jnp.ones(qpe.shape, bf16)
    pe_num, pe_ssq = [], []
    for p in range(cp):
        kp = kpe_buf[slot, p]
        pe_num.append(_dot(qpe, kp.astype(bf16)))
        pe_ssq.append(_dot(ones, (kp * kp).astype(bf16)))
    num = num + jnp.concatenate(pe_num, axis=1)
    ssq = ssq + jnp.concatenate(pe_ssq, axis=1)
    s = num * lax.rsqrt(ssq * (1.0 / QK_HEAD_DIM) + RMS_EPS)
    m = m_scr[...]
    m_new = jnp.maximum(m, jnp.max(s, axis=-1, keepdims=True))
    alpha = jnp.exp2(m - m_new)
    p = jnp.exp2(s - m_new)
    l_scr[...] = l_scr[...] * alpha + jnp.sum(p, axis=-1, keepdims=True)
    acc_scr[...] = acc_scr[...] * alpha + _dot(p.astype(bf16), ct)
    m_scr[...] = m_new

    @pl.when(c == nc - 1)
    def _():
        o_ref[0] = acc_scr[...] / l_scr[...]

    @pl.when(flat == total - 1)
    def _():
        for k in range(1, ahead + 1):
            for cpy in copies(total - 1, (flat + k) % n_slots):
                cpy.wait()


def _decode(page_table, q3, knew3, qabs, qpe, cnew, wukt, cache_ckv, cache_kpet, cp):
    DB, n_pages = page_table.shape
    page, kv_rank = cache_ckv.shape[1:]
    rope = cache_kpet.shape[1]
    nc = n_pages // cp
    T = cp * page
    per_seq = lambda a: pl.BlockSpec((1,) + a.shape[1:], lambda b, c, pt: (b,) + (0,) * (a.ndim - 1))
    grid_spec = pltpu.PrefetchScalarGridSpec(
        num_scalar_prefetch=1,
        grid=(DB, nc),
        in_specs=[per_seq(q3), per_seq(knew3), per_seq(qabs), per_seq(qpe), per_seq(cnew),
                  pl.BlockSpec(wukt.shape, lambda b, c, pt: (0, 0)),
                  pl.BlockSpec(memory_space=pl.ANY), pl.BlockSpec(memory_space=pl.ANY)],
        out_specs=pl.BlockSpec((1, N_HEADS, kv_rank), lambda b, c, pt: (b, 0, 0)),
        scratch_shapes=[pltpu.VMEM((DECODE_SLOTS, T, kv_rank), f32), pltpu.VMEM((DECODE_SLOTS, cp, rope, page), f32),
                        pltpu.SemaphoreType.DMA((2, DECODE_SLOTS)),
                        pltpu.VMEM((wukt.shape[0] + BF16_SUBLANES, kv_rank), bf16),
                        pltpu.VMEM((N_HEADS, 1), f32), pltpu.VMEM((N_HEADS, 1), f32),
                        pltpu.VMEM((N_HEADS, kv_rank), f32)])
    return pl.pallas_call(
        functools.partial(_decode_body, n_pages=n_pages, cp=cp),
        grid_spec=grid_spec,
        out_shape=jax.ShapeDtypeStruct((DB, N_HEADS, kv_rank), f32),
        compiler_params=_params("arbitrary", "arbitrary"),
        name="decode_attn",
    )(page_table.reshape(-1), q3, knew3, qabs, qpe, cnew, wukt, cache_ckv, cache_kpet)


def _outproj_body(x_ref, mp_ref, attn_ref, gatt_ref, wo_ref, gffn_ref, x1_ref, h2_ref):
    attn = attn_ref[...]
    na = (_rms(attn, attn.shape[-1]) * gatt_ref[...]).astype(bf16)
    mixed = jnp.concatenate([mp_ref[...], na], axis=1)
    x1 = x_ref[...] + _dot(mixed, wo_ref[...])
    x1_ref[...] = x1
    h2_ref[...] = (_rms(x1, x1.shape[-1]) * gffn_ref[...]).astype(bf16)


def _outproj(x, mp, attn, w, tm):
    M, D = x.shape
    const = lambda a: pl.BlockSpec(a.shape, lambda i: (0, 0))
    row = lambda n: pl.BlockSpec((tm, n), lambda i: (i, 0))
    return pl.pallas_call(
        _outproj_body,
        grid=(M // tm,),
        in_specs=[row(D), row(mp.shape[1]), row(attn.shape[1]),
                  const(w['g_out_attn']), const(w['w_o']), const(w['g_ffn'])],
        out_specs=(row(D), row(D)),
        out_shape=(jax.ShapeDtypeStruct((M, D), f32), jax.ShapeDtypeStruct((M, D), bf16)),
        compiler_params=_params("parallel"),
        name="out_proj",
    )(x, mp, attn, w['g_out_attn'], w['w_o'], w['g_ffn'])


def _ffn_finish(f, g, g1, g2, h2, x1_ref, wu_ref, wd_ref, cw_ref, cb_ref, y_ref, acc):
    gate = cb_ref[...] + g2 * cw_ref[0:1, :] + g1 * cw_ref[1:2, :] + g * cw_ref[2:3, :]
    act = (gate * jax.nn.sigmoid(gate)) * _dot(h2, wu_ref[...])
    part = _dot(act.astype(bf16), wd_ref[...])

    @pl.when(f == 0)
    def _():
        acc[...] = x1_ref[...] + part

    @pl.when(f > 0)
    def _():
        acc[...] += part

    @pl.when(f == pl.num_programs(1) - 1)
    def _():
        y_ref[...] = acc[...]


def _ffn_prompt_body(h2_ref, halo_ref, x1_ref, wg_ref, wu_ref, wd_ref, cw_ref, cb_ref,
                     y_ref, tail_ref, acc, gbuf, *, tm, tiles_per_seq):
    i = pl.program_id(0)
    f = pl.program_id(1)
    hb = BF16_SUBLANES
    h2 = h2_ref[...]
    g = _dot(h2, wg_ref[...])
    gh = _dot(halo_ref[...], wg_ref[...])
    gbuf[0:hb, :] = jnp.where(i % tiles_per_seq == 0, 0.0, gh)
    gbuf[hb:hb + tm, :] = g
    tail_ref[0] = gbuf[tm:tm + hb, :]
    g1 = gbuf[hb - 1:hb - 1 + tm, :]
    g2 = gbuf[hb - 2:hb - 2 + tm, :]
    _ffn_finish(f, g, g1, g2, h2, x1_ref, wu_ref, wd_ref, cw_ref, cb_ref, y_ref, acc)


def _ffn_prompt(h2, x1, S, w, tm, tf):
    M, D = x1.shape
    F = w['w_gate'].shape[1]
    hb = BF16_SUBLANES
    nt = M // tm
    halo_blocks = tm // hb
    resident = dict(pipeline_mode=pl.Buffered(1)) if tf == F else {}
    y, tail = pl.pallas_call(
        functools.partial(_ffn_prompt_body, tm=tm, tiles_per_seq=S // tm),
        grid=(nt, F // tf),
        in_specs=[pl.BlockSpec((tm, D), lambda i, f: (i, 0)),
                  pl.BlockSpec((hb, D), lambda i, f: (jnp.maximum(i * halo_blocks - 1, 0), 0)),
                  pl.BlockSpec((tm, D), lambda i, f: (i, 0)),
                  pl.BlockSpec((D, tf), lambda i, f: (0, f), **resident),
                  pl.BlockSpec((D, tf), lambda i, f: (0, f), **resident),
                  pl.BlockSpec((tf, D), lambda i, f: (f, 0), **resident),
                  pl.BlockSpec((CONV_WIDTH, tf), lambda i, f: (0, f)),
                  pl.BlockSpec((1, tf), lambda i, f: (0, f))],
        out_specs=(pl.BlockSpec((tm, D), lambda i, f: (i, 0)),
                   pl.BlockSpec((1, hb, tf), lambda i, f: (i, 0, f))),
        out_shape=(jax.ShapeDtypeStruct((M, D), f32), jax.ShapeDtypeStruct((nt, hb, F), f32)),
        scratch_shapes=[pltpu.VMEM((tm, D), f32), pltpu.VMEM((tm + hb, tf), f32)],
        compiler_params=_params("parallel", "arbitrary"),
        name="ffn_prompt",
    )(h2, h2, x1, w['w_gate'], w['w_up'], w['w_down'], w['conv_w'], w['conv_b'])
    return y, tail


def _ffn_sample_body(h2_ref, st_ref, x1_ref, wg_ref, wu_ref, wd_ref, cw_ref, cb_ref, y_ref, g_ref, acc):
    f = pl.program_id(1)
    h2 = h2_ref[...]
    g = _dot(h2, wg_ref[...])
    g_ref[...] = g
    _ffn_finish(f, g, st_ref[1], st_ref[0], h2, x1_ref, wu_ref, wd_ref, cw_ref, cb_ref, y_ref, acc)


def _ffn_sample(h2, x1, st_t, w, tf):
    M, D = x1.shape
    F = w['w_gate'].shape[1]
    return pl.pallas_call(
        _ffn_sample_body,
        grid=(1, F // tf),
        in_specs=[pl.BlockSpec((M, D), lambda i, f: (0, 0)),
                  pl.BlockSpec((CONV_WIDTH - 1, M, tf), lambda i, f: (0, 0, f)),
                  pl.BlockSpec((M, D), lambda i, f: (0, 0)),
                  pl.BlockSpec((D, tf), lambda i, f: (0, f)),
                  pl.BlockSpec((D, tf), lambda i, f: (0, f)),
                  pl.BlockSpec((tf, D), lambda i, f: (f, 0)),
                  pl.BlockSpec((CONV_WIDTH, tf), lambda i, f: (0, f)),
                  pl.BlockSpec((1, tf), lambda i, f: (0, f))],
        out_specs=(pl.BlockSpec((M, D), lambda i, f: (0, 0)),
                   pl.BlockSpec((M, tf), lambda i, f: (0, f))),
        out_shape=(jax.ShapeDtypeStruct((M, D), f32), jax.ShapeDtypeStruct((M, F), f32)),
        scratch_shapes=[pltpu.VMEM((M, D), f32)],
        compiler_params=_params("parallel", "arbitrary"),
        name="ffn_sample",
    )(h2, st_t, x1, w['w_gate'], w['w_up'], w['w_down'], w['conv_w'], w['conv_b'])


def _qk_gain(g):
    g_pe = g[QK_NOPE_DIM:]
    return jnp.concatenate([g[:QK_NOPE_DIM], g_pe, g_pe])


def _rope_table(pos):
    half = QK_ROPE_DIM // 2
    inv = ROPE_BASE ** (-jnp.arange(0, QK_ROPE_DIM, 2, dtype=f32) / QK_ROPE_DIM)
    ang = pos.astype(f32)[:, None] * inv[None, :]
    cos, sin = jnp.cos(ang), jnp.sin(ang)
    P = pos.shape[0]
    z = lambda n: jnp.zeros((P, n), f32)
    tail = LANE - QK_HEAD_DIM
    c = jnp.concatenate([jnp.ones((P, QK_NOPE_DIM), f32), cos, cos, z(tail)], axis=1)
    s_lo = jnp.concatenate([z(QK_NOPE_DIM), -sin, z(half), z(tail)], axis=1)
    s_hi = jnp.concatenate([z(QK_NOPE_DIM), z(half), sin, z(tail)], axis=1)
    return jnp.concatenate([c, s_lo, s_hi], axis=1)


def _prep_weights(g_mix, w_in, g_q_a, w_uq, g_qn, g_kv_a, w_uk, g_kn, w_uv, w_pool, s_pool,
                  g_out, w_o, g_ffn, w_gate, w_up, conv_w, conv_b, w_down):
    D = w_in.shape[0]
    q_rank = g_q_a.shape[0]
    kv_rank = g_kv_a.shape[0]
    pool_w = s_pool.shape[0]
    head_pad = LANE - QK_HEAD_DIM
    half = QK_ROPE_DIM // 2
    main = pool_w + q_rank + kv_rank
    w_in_p = jnp.concatenate([w_in[:, :main], jnp.zeros((D, QK_NOPE_DIM), f32), w_in[:, main:],
                              jnp.zeros((D, head_pad), f32)], axis=1).astype(bf16)
    w_uq_h = w_uq.reshape(q_rank, N_HEADS, QK_HEAD_DIM)
    w_uq_p = jnp.pad(w_uq_h, ((0, 0), (0, 0), (0, head_pad)))
    lo = w_uq_h[:, :, QK_NOPE_DIM:QK_NOPE_DIM + half]
    hi = w_uq_h[:, :, QK_NOPE_DIM + half:]
    w_uq_rot = jnp.concatenate([jnp.zeros((q_rank, N_HEADS, QK_NOPE_DIM), f32), -hi, lo,
                                jnp.zeros((q_rank, N_HEADS, head_pad), f32)], axis=2)
    w_uq_p = jnp.concatenate([w_uq_p.reshape(q_rank, N_HEADS * LANE),
                              w_uq_rot.reshape(q_rank, N_HEADS * LANE)], axis=1).astype(bf16)
    gq = _qk_gain(g_qn) * _qk_gain(g_kn) * (QK_HEAD_DIM ** -0.5 * LOG2_E)
    gq = jnp.tile(jnp.pad(gq, (0, head_pad)), N_HEADS)[None, :]
    w_uk_p = jnp.pad(w_uk, ((0, 0), (0, 0), (0, LANE - QK_NOPE_DIM))).reshape(kv_rank, N_HEADS * LANE)
    w_uk_p = w_uk_p.astype(bf16)
    w_uv_t = w_uv.reshape(kv_rank, N_HEADS * V_HEAD_DIM).T.astype(bf16)
    w_uk_t = w_uk.reshape(kv_rank, N_HEADS * QK_NOPE_DIM).T.astype(bf16)
    eye = jnp.eye(N_HEADS, dtype=f32)
    w_uk_bd = jnp.einsum('chd,hg->hdgc', jnp.pad(w_uk, ((0, 0), (0, 0), (0, LANE - QK_NOPE_DIM))), eye)
    w_uk_bd = w_uk_bd.reshape(N_HEADS * LANE, N_HEADS * kv_rank).astype(bf16)
    w_uv_bd = jnp.einsum('chv,hg->hcgv', w_uv, eye).reshape(N_HEADS * kv_rank, N_HEADS * V_HEAD_DIM).astype(bf16)
    return {
        'pool_w': pool_w,
        'g_mix': g_mix[None, :], 'w_in': w_in_p, 'g_q_a': g_q_a[None, :], 'w_uq': w_uq_p, 'g_q': gq,
        'g_kv_a': g_kv_a[None, :], 'w_uk': w_uk_p, 'w_uv_t': w_uv_t, 'w_uk_t': w_uk_t, 'w_uk_bd': w_uk_bd, 'w_uv_bd': w_uv_bd,
        'w_pool': w_pool.astype(bf16), 's_pool': s_pool[None, :],
        'g_out_pool': g_out[None, :pool_w], 'g_out_attn': g_out[None, pool_w:],
        'w_o': w_o.astype(bf16), 'g_ffn': g_ffn[None, :],
        'w_gate': w_gate.astype(bf16), 'w_up': w_up.astype(bf16), 'w_down': w_down.astype(bf16),
        'conv_w': conv_w, 'conv_b': conv_b[None, :],
    }


def _tile(n, pref):
    t = min(n, pref)
    assert n % t == 0, (n, pref)
    return t


def _ffn_cols(F):
    half = F // 2
    return half if half % LANE == 0 else F


def kernel(x_prompt, x_sample, cache_ckv, cache_kpe, state_pool, state_conv, page_table, g_mix, w_in, g_q_a,
           w_uq, g_qn, g_kv_a, w_uk, g_kn, w_uv, w_pool, s_pool, g_out, w_o, g_ffn, w_gate, w_up, conv_w,
           conv_b, w_down):
    depth = g_mix.shape[0]
    assert depth == 1, "single-layer trunk only"
    B, S, D = x_prompt.shape
    DB, DS = x_sample.shape[:2]
    assert DS == 1, "one new token per sequence"
    n_pages = page_table.shape[1]
    page = cache_ckv.shape[2]
    past_len = n_pages * page
    l = 0
    w = _prep_weights(g_mix[l], w_in[l], g_q_a[l], w_uq[l], g_qn[l], g_kv_a[l], w_uk[l], g_kn[l], w_uv[l],
                      w_pool[l], s_pool[l], g_out[l], w_o[l], g_ffn[l], w_gate[l], w_up[l], conv_w[l],
                      conv_b[l], w_down[l])
    kv_rank = g_kv_a.shape[1]
    F = w_gate.shape[2]
    tf = _ffn_cols(F)

    xp = x_prompt.reshape(B * S, D)
    tq = _tile(S, 512)
    tm = _tile(tq, 256)
    rope_p = _rope_table(jnp.arange(S, dtype=jnp.int32))
    u_p, q_p, ckv_p, kpe_p, k_p, vt_p = _proj(xp, rope_p, S // tm, w, tm, tq)
    mp_p = _pool_prompt(u_p, B, S, w, _tile(S, 512))
    attn_p = _flash(q_p, k_p, vt_p, B, S, tq, 8)
    x1_p, h2_p = _outproj(xp, mp_p, attn_p, w, _tile(S, 512))
    tmf = _tile(S, 256)
    y_p, tail = _ffn_prompt(h2_p, x1_p, S, w, tmf, F)
    tiles_per_seq = S // tmf
    conv_p = tail[tiles_per_seq - 1::tiles_per_seq, BF16_SUBLANES - (CONV_WIDTH - 1):, :]

    xs = x_sample.reshape(DB, D)
    pos_s = past_len + jnp.arange(DS, dtype=jnp.int32)
    rope_s = jnp.tile(_rope_table(pos_s), (DB, 1))
    u_s, q_s, ckv_s, kpe_s, k_s, _ = _proj(xs, rope_s, 1, w, DB, DB)
    st_pool_t = jnp.transpose(state_pool[l], (1, 0, 2))
    mp_s = _pool_sample(u_s, st_pool_t, past_len, w)
    qabs = _mm(q_s, w['w_uk_bd'], "absorb_q").reshape(DB, N_HEADS, kv_rank).astype(bf16)
    qabs = jnp.pad(qabs, ((0, 0), (0, BF16_SUBLANES - N_HEADS), (0, 0)))
    q3 = q_s.reshape(DB, N_HEADS, LANE)
    qpe = q3[:, :, QK_NOPE_DIM:QK_HEAD_DIM]
    cp = _tile(n_pages, 32)
    ctx = _decode(page_table, q3, k_s.reshape(DB, N_HEADS, LANE), qabs, qpe, ckv_s.reshape(DB, 1, kv_rank),
                  w['w_uk_t'], cache_ckv[l], jnp.swapaxes(cache_kpe[l], 1, 2), cp)
    attn_s = _mm(ctx.reshape(DB, N_HEADS * kv_rank), w['w_uv_bd'], "value_up")
    x1_s, h2_s = _outproj(xs, mp_s, attn_s, w, DB)
    st_conv_t = jnp.transpose(state_conv[l], (1, 0, 2))
    y_s, g_s = _ffn_sample(h2_s, x1_s, st_conv_t, w, tf)

    kpe_sl = slice(QK_NOPE_DIM, QK_HEAD_DIM)
    P = POOL_STATE_LEN
    return (
        y_p.reshape(B, S, D),
        y_s.reshape(DB, DS, D),
        ckv_p.reshape(1, B, S, kv_rank),
        kpe_p[:, kpe_sl].reshape(1, B, S, QK_ROPE_DIM),
        u_p.reshape(B, S, -1)[None, :, S - P:, :],
        conv_p[None],
        ckv_s.reshape(1, DB, DS, kv_rank),
        kpe_s[:, kpe_sl].reshape(1, DB, DS, QK_ROPE_DIM),
        jnp.concatenate([state_pool[l], u_s[:, None, :]], axis=1)[None, :, -P:, :],
        jnp.concatenate([state_conv[l], g_s[:, None, :]], axis=1)[None, :, -(CONV_WIDTH - 1):, :],
    )
```

```python
import functools

import jax
import jax.numpy as jnp
from jax import lax
from jax.experimental import pallas as pl
from jax.experimental.pallas import tpu as pltpu

N_HEADS = 8
QK_NOPE_DIM = 64
QK_ROPE_DIM = 32
QK_HEAD_DIM = QK_NOPE_DIM + QK_ROPE_DIM
V_HEAD_DIM = 64
POOL_WINDOWS = (2, 4, 8, 16)
POOL_STATE_LEN = max(POOL_WINDOWS) - 1
CONV_WIDTH = 3
ROPE_BASE = 10000.0
RMS_EPS = 1e-6
LOG2_E = 1.4426950408889634

LANE = 128
BF16_SUBLANES = 16
VMEM_LIMIT = 48 * 1024 * 1024
DECODE_SLOTS = 3
LOOKAHEAD = 1

_NT = (((1,), (1,)), ((), ()))

bf16 = jnp.bfloat16
f32 = jnp.float32


def _rms(x, width):
    return x * lax.rsqrt(jnp.sum(x * x, axis=-1, keepdims=True) * (1.0 / width) + RMS_EPS)


def _dot(a, b):
    return jnp.dot(a, b, preferred_element_type=f32)


def _params(*sem):
    return pltpu.CompilerParams(dimension_semantics=sem, vmem_limit_bytes=VMEM_LIMIT)


def _proj_body(x_ref, rope_ref, gmix_ref, win_ref, gqa_ref, wuq_ref, gq_ref, gkva_ref, wuk_ref, wuvt_ref,
               u_ref, q_ref, ckv_ref, kpe_ref, k_ref, vt_ref, *, pool_w, q_rank, kv_rank):
    x = x_ref[...]
    h = (_rms(x, x.shape[-1]) * gmix_ref[...]).astype(bf16)
    proj = _dot(h, win_ref[...])
    u_ref[...] = proj[:, :pool_w]
    o = pool_w
    cq = (_rms(proj[:, o:o + q_rank], q_rank) * gqa_ref[...]).astype(bf16)
    o += q_rank
    ckv = _rms(proj[:, o:o + kv_rank], kv_rank) * gkva_ref[...]
    o += kv_rank
    ckv_ref[...] = ckv
    cos = rope_ref[:, 0:LANE]
    sin_lo = rope_ref[:, LANE:2 * LANE]
    sin_hi = rope_ref[:, 2 * LANE:3 * LANE]
    sin_abs = sin_hi - sin_lo
    half = QK_ROPE_DIM // 2
    raw_kpe = proj[:, o:o + LANE]
    kpe = raw_kpe * cos + pltpu.roll(raw_kpe, LANE - half, 1) * sin_lo + pltpu.roll(raw_kpe, half, 1) * sin_hi
    kpe_ref[...] = kpe
    qraw = _dot(cq, wuq_ref[...])
    hq = N_HEADS * LANE
    ckv_b = ckv.astype(bf16)
    kn = _dot(ckv_b, wuk_ref[...])
    vt_ref[0] = lax.dot_general(wuvt_ref[...], ckv_b, _NT, preferred_element_type=f32).astype(bf16)
    for hd in range(N_HEADS):
        sl = slice(hd * LANE, (hd + 1) * LANE)
        qh = qraw[:, sl] * cos + qraw[:, hq + hd * LANE:hq + (hd + 1) * LANE] * sin_abs
        q_ref[:, sl] = (_rms(qh, QK_HEAD_DIM) * gq_ref[:, sl]).astype(bf16)
        kh = kn[:, sl] + kpe
        k_ref[:, sl] = _rms(kh, QK_HEAD_DIM).astype(bf16)


def _proj(x, rope, n_rope_blocks, w, tm, slab):
    M, D = x.shape
    pool_w = w['pool_w']
    q_rank = w['g_q_a'].shape[1]
    kv_rank = w['g_kv_a'].shape[1]
    hq = N_HEADS * LANE
    hv = N_HEADS * V_HEAD_DIM
    per_slab = slab // tm
    const = lambda a: pl.BlockSpec(a.shape, lambda i: (0, 0))
    row = lambda n: pl.BlockSpec((tm, n), lambda i: (i, 0))
    out_shapes = (
        jax.ShapeDtypeStruct((M, pool_w), f32), jax.ShapeDtypeStruct((M, hq), bf16),
        jax.ShapeDtypeStruct((M, kv_rank), f32), jax.ShapeDtypeStruct((M, LANE), f32),
        jax.ShapeDtypeStruct((M, hq), bf16), jax.ShapeDtypeStruct((M // slab, hv, slab), bf16))
    return pl.pallas_call(
        functools.partial(_proj_body, pool_w=pool_w, q_rank=q_rank, kv_rank=kv_rank),
        grid=(M // tm,),
        in_specs=[row(D), pl.BlockSpec((tm, 3 * LANE), lambda i: (i % n_rope_blocks, 0)),
                  const(w['g_mix']), const(w['w_in']), const(w['g_q_a']), const(w['w_uq']),
                  const(w['g_q']), const(w['g_kv_a']), const(w['w_uk']), const(w['w_uv_t'])],
        out_specs=(row(pool_w), row(hq), row(kv_rank), row(LANE), row(hq),
                   pl.BlockSpec((1, hv, tm), lambda i: (i // per_slab, 0, i % per_slab))),
        out_shape=out_shapes,
        compiler_params=_params("parallel"),
        name="in_proj",
    )(x, rope, w['g_mix'], w['w_in'], w['g_q_a'], w['w_uq'], w['g_q'], w['g_kv_a'], w['w_uk'], w['w_uv_t'])


def _pool_finish(diffs, wpool_ref, spool_ref, gout_ref, o_ref):
    ys = []
    for g, d in enumerate(diffs):
        sl = slice(g * LANE, (g + 1) * LANE)
        ys.append(_dot(d.astype(bf16), wpool_ref[g]) * spool_ref[:, sl])
    width = LANE * len(ys)
    ssq = sum(jnp.sum(y * y, axis=-1, keepdims=True) for y in ys)
    scale = lax.rsqrt(ssq * (1.0 / width) + RMS_EPS)
    for g, y in enumerate(ys):
        sl = slice(g * LANE, (g + 1) * LANE)
        o_ref[:, sl] = (y * scale * gout_ref[:, sl]).astype(o_ref.dtype)


def _pool_prompt_body(u_ref, wpool_ref, spool_ref, gout_ref, o_ref, buf, *, ts):
    j = pl.program_id(1)
    halo = POOL_STATE_LEN + 1

    @pl.when(j == 0)
    def _():
        buf[0:halo, :] = jnp.zeros((halo, buf.shape[1]), f32)

    @pl.when(j > 0)
    def _():
        buf[0:halo, :] = buf[ts:ts + halo, :]

    buf[halo:halo + ts, :] = u_ref[...]
    pos = j * ts + lax.broadcasted_iota(jnp.int32, (ts, 1), 0)
    diffs = []
    for g, wdw in enumerate(POOL_WINDOWS):
        sl = slice(g * LANE, (g + 1) * LANE)
        x = buf[halo:halo + ts, sl]
        win = x
        for k in range(1, wdw):
            win = win + buf[halo - k:halo - k + ts, sl]
        cnt = jnp.minimum(wdw, pos + 1).astype(f32)
        diffs.append(win / cnt - x)
    _pool_finish(diffs, wpool_ref, spool_ref, gout_ref, o_ref)


def _pool_prompt(u, B, S, w, ts):
    M, W = u.shape
    ns = S // ts
    const2 = lambda a: pl.BlockSpec(a.shape, lambda b, j: (0,) * a.ndim)
    return pl.pallas_call(
        functools.partial(_pool_prompt_body, ts=ts),
        grid=(B, ns),
        in_specs=[pl.BlockSpec((ts, W), lambda b, j: (b * ns + j, 0)),
                  const2(w['w_pool']), const2(w['s_pool']), const2(w['g_out_pool'])],
        out_specs=pl.BlockSpec((ts, W), lambda b, j: (b * ns + j, 0)),
        out_shape=jax.ShapeDtypeStruct((M, W), bf16),
        scratch_shapes=[pltpu.VMEM((ts + POOL_STATE_LEN + 1, W), f32)],
        compiler_params=_params("arbitrary", "arbitrary"),
        name="pool_prompt",
    )(u, w['w_pool'], w['s_pool'], w['g_out_pool'])


def _pool_sample_body(u_ref, st_ref, wpool_ref, spool_ref, gout_ref, o_ref, *, pos):
    diffs = []
    for g, wdw in enumerate(POOL_WINDOWS):
        sl = slice(g * LANE, (g + 1) * LANE)
        x = u_ref[:, sl]
        win = x
        for k in range(1, wdw):
            win = win + st_ref[POOL_STATE_LEN - k, :, sl]
        diffs.append(win / float(min(wdw, pos + 1)) - x)
    _pool_finish(diffs, wpool_ref, spool_ref, gout_ref, o_ref)


def _pool_sample(u, st_t, pos, w):
    M, W = u.shape
    return pl.pallas_call(
        functools.partial(_pool_sample_body, pos=pos),
        out_shape=jax.ShapeDtypeStruct((M, W), bf16),
        compiler_params=pltpu.CompilerParams(vmem_limit_bytes=VMEM_LIMIT),
        name="pool_sample",
    )(u, st_t, w['w_pool'], w['s_pool'], w['g_out_pool'])


def _flash_body(q_ref, k_ref, vt_ref, o_ref, m_scr, acc_scr, *, tq):
    i = pl.program_id(2)
    heads = m_scr.shape[0]
    key_idx = lax.broadcasted_iota(jnp.int32, (tq, tq), 0)
    qry_idx = lax.broadcasted_iota(jnp.int32, (tq, tq), 1)
    m_scr[...] = jnp.full(m_scr.shape, -1e30, f32)
    acc_scr[...] = jnp.zeros(acc_scr.shape, f32)
    ones_rows = jnp.ones((BF16_SUBLANES, tq), bf16)

    def step(j, masked):
        start = pl.multiple_of(j * tq, tq)

        def scores(hh):
            sl = slice(hh * LANE, (hh + 1) * LANE)
            return lax.dot_general(k_ref[pl.ds(start, tq), sl], q_ref[:, sl], _NT, preferred_element_type=f32)

        ready = [scores(hh) for hh in range(min(LOOKAHEAD, heads))]
        for hh in range(heads):
            st = ready.pop(0)
            if hh + LOOKAHEAD < heads:
                ready.append(scores(hh + LOOKAHEAD))
            if masked:
                st = jnp.where(key_idx <= qry_idx, st, -1e30)
            m = m_scr[hh]
            m_new = jnp.maximum(m, jnp.max(st, axis=0, keepdims=True))
            alpha = jnp.exp2(m - m_new)
            pt = jnp.exp2((st - m_new).astype(bf16))
            vt = jnp.concatenate([vt_ref[j, hh * V_HEAD_DIM:(hh + 1) * V_HEAD_DIM, :], ones_rows], axis=0)
            acc_scr[hh] = acc_scr[hh] * alpha + _dot(vt, pt)
            m_scr[hh] = m_new

    def body(j, carry):
        step(j, False)
        return carry

    lax.fori_loop(0, i, body, 0)
    step(i, True)
    dv = V_HEAD_DIM
    out_t = jnp.concatenate([acc_scr[hh, 0:dv] / acc_scr[hh, dv:dv + 1] for hh in range(heads)], axis=0)
    o_ref[...] = out_t.T


def _flash(q, k, vt, B, S, tq, heads):
    M = q.shape[0]
    nq = S // tq
    groups = N_HEADS // heads
    return pl.pallas_call(
        functools.partial(_flash_body, tq=tq),
        grid=(B, groups, nq),
        in_specs=[pl.BlockSpec((tq, heads * LANE), lambda b, g, i: (b * nq + i, g)),
                  pl.BlockSpec((S, heads * LANE), lambda b, g, i: (b, g)),
                  pl.BlockSpec((nq, heads * V_HEAD_DIM, tq), lambda b, g, i: (b, g, 0))],
        out_specs=pl.BlockSpec((tq, heads * V_HEAD_DIM), lambda b, g, i: (b * nq + i, g)),
        out_shape=jax.ShapeDtypeStruct((M, N_HEADS * V_HEAD_DIM), f32),
        scratch_shapes=[pltpu.VMEM((heads, 1, tq), f32),
                        pltpu.VMEM((heads, V_HEAD_DIM + BF16_SUBLANES, tq), f32)],
        compiler_params=_params("parallel", "parallel", "arbitrary"),
        name="flash_prompt",
    )(q, k, vt)


def _mm_body(a_ref, b_ref, o_ref):
    o_ref[...] = _dot(a_ref[...].astype(bf16), b_ref[...])


def _mm(a, b, name):
    return pl.pallas_call(
        _mm_body,
        out_shape=jax.ShapeDtypeStruct((a.shape[0], b.shape[1]), f32),
        compiler_params=pltpu.CompilerParams(vmem_limit_bytes=VMEM_LIMIT),
        name=name,
    )(a, b)


def _decode_body(pt_ref, q_ref, knew_ref, qabs_ref, qpe_ref, cnew_ref, wukt_ref, cache_ckv, cache_kpet,
                 o_ref, ckv_buf, kpe_buf, sems, lhs, m_scr, l_scr, acc_scr, *, n_pages, cp, parts):
    b = pl.program_id(0)
    c = pl.program_id(1)
    nc = pl.num_programs(1)
    total = pl.num_programs(0) * nc
    flat = b * nc + c
    n_slots = ckv_buf.shape[0]
    ahead = n_slots - 1
    slot = flat % n_slots
    nope_rows = wukt_ref.shape[0]
    page = cache_ckv.shape[1]

    def copies(flat_idx, slot_idx):
        base = (flat_idx // nc) * n_pages + (flat_idx % nc) * cp
        out = []
        for p in range(cp):
            phys = pt_ref[base + p]
            rows = pl.ds(p * page, page)
            out.append(pltpu.make_async_copy(cache_ckv.at[phys], ckv_buf.at[slot_idx, rows], sems.at[0, slot_idx]))
            out.append(pltpu.make_async_copy(cache_kpet.at[phys], kpe_buf.at[slot_idx, p], sems.at[1, slot_idx]))
        return out

    @pl.when(flat == 0)
    def _():
        for k in range(ahead):
            for cpy in copies(jnp.minimum(k, total - 1), k):
                cpy.start()

    @pl.when(c == 0)
    def _():
        lhs[0:nope_rows, :] = wukt_ref[...]
        lhs[nope_rows:nope_rows + BF16_SUBLANES, :] = qabs_ref[0]
        s_new = jnp.sum(q_ref[0].astype(f32) * knew_ref[0].astype(f32), axis=-1, keepdims=True)
        m_scr[...] = s_new
        l_scr[...] = jnp.ones_like(l_scr)
        acc_scr[...] = jnp.broadcast_to(cnew_ref[0], acc_scr.shape)

    for cpy in copies(flat, slot):
        cpy.wait()

    qpe = qpe_ref[0]
    ones = jnp.ones(qpe.shape, bf16)
    fold = (lax.broadcasted_iota(jnp.int32, (N_HEADS, 8 * N_HEADS), 1) // 8
            == lax.broadcasted_iota(jnp.int32, (N_HEADS, 8 * N_HEADS), 0)).astype(bf16)
    pages_per_part = cp // parts
    tokens = pages_per_part * page
    cts, scores = [], []
    for part in range(parts):
        ct = ckv_buf[slot, part * tokens:(part + 1) * tokens, :].astype(bf16)
        if part == 0:
            for cpy in copies(jnp.minimum(flat + ahead, total - 1), (flat + ahead) % n_slots):
                cpy.start()
        kt = lax.dot_general(lhs[...], ct, _NT, preferred_element_type=f32)
        kn = kt[0:nope_rows].reshape(N_HEADS, nope_rows // (8 * N_HEADS), 8, tokens)
        partial = jnp.sum(kn * kn, axis=1).reshape(8 * N_HEADS, tokens)
        ssq = _dot(fold, partial.astype(bf16))
        num = kt[nope_rows:nope_rows + N_HEADS]
        pe_num, pe_ssq = [], []
        for p in range(part * pages_per_part, (part + 1) * pages_per_part):
            kp = kpe_buf[slot, p]
            pe_num.append(_dot(qpe, kp.astype(bf16)))
            pe_ssq.append(_dot(ones, (kp * kp).astype(bf16)))
        num = num + jnp.concatenate(pe_num, axis=1)
        ssq = ssq + jnp.concatenate(pe_ssq, axis=1)
        scores.append(num * lax.rsqrt(ssq * (1.0 / QK_HEAD_DIM) + RMS_EPS))
        cts.append(ct)
    s = jnp.concatenate(scores, axis=1)
    m = m_scr[...]
    m_new = jnp.maximum(m, jnp.max(s, axis=-1, keepdims=True))
    alpha = jnp.exp2(m - m_new)
    p = jnp.exp2(s - m_new)
    l_scr[...] = l_scr[...] * alpha + jnp.sum(p, axis=-1, keepdims=True)
    acc = acc_scr[...] * alpha
    p = p.astype(bf16)
    for part, ct in enumerate(cts):
        acc = acc + _dot(p[:, part * tokens:(part + 1) * tokens], ct)
    acc_scr[...] = acc
    m_scr[...] = m_new

    @pl.when(c == nc - 1)
    def _():
        o_ref[0] = acc_scr[...] / l_scr[...]

    @pl.when(flat == total - 1)
    def _():
        for k in range(1, ahead + 1):
            for cpy in copies(total - 1, (flat + k) % n_slots):
                cpy.wait()


def _decode(page_table, q3, knew3, qabs, qpe, cnew, wukt, cache_ckv, cache_kpet, cp, parts):
    DB, n_pages = page_table.shape
    page, kv_rank = cache_ckv.shape[1:]
    rope = cache_kpet.shape[1]
    nc = n_pages // cp
    T = cp * page
    per_seq = lambda a: pl.BlockSpec((1,) + a.shape[1:], lambda b, c, pt: (b,) + (0,) * (a.ndim - 1))
    grid_spec = pltpu.PrefetchScalarGridSpec(
        num_scalar_prefetch=1,
        grid=(DB, nc),
        in_specs=[per_seq(q3), per_seq(knew3), per_seq(qabs), per_seq(qpe), per_seq(cnew),
                  pl.BlockSpec(wukt.shape, lambda b, c, pt: (0, 0)),
                  pl.BlockSpec(memory_space=pl.ANY), pl.BlockSpec(memory_space=pl.ANY)],
        out_specs=pl.BlockSpec((1, N_HEADS, kv_rank), lambda b, c, pt: (b, 0, 0)),
        scratch_shapes=[pltpu.VMEM((DECODE_SLOTS, T, kv_rank), f32), pltpu.VMEM((DECODE_SLOTS, cp, rope, page), f32),
                        pltpu.SemaphoreType.DMA((2, DECODE_SLOTS)),
                        pltpu.VMEM((wukt.shape[0] + BF16_SUBLANES, kv_rank), bf16),
                        pltpu.VMEM((N_HEADS, 1), f32), pltpu.VMEM((N_HEADS, 1), f32),
                        pltpu.VMEM((N_HEADS, kv_rank), f32)])
    return pl.pallas_call(
        functools.partial(_decode_body, n_pages=n_pages, cp=cp, parts=parts),
        grid_spec=grid_spec,
        out_shape=jax.ShapeDtypeStruct((DB, N_HEADS, kv_rank), f32),
        compiler_params=_params("arbitrary", "arbitrary"),
        name="decode_attn",
    )(page_table.reshape(-1), q3, knew3, qabs, qpe, cnew, wukt, cache_ckv, cache_kpet)


def _outproj_body(x_ref, mp_ref, attn_ref, gatt_ref, wo_ref, gffn_ref, x1_ref, h2_ref):
    attn = attn_ref[...]
    na = (_rms(attn, attn.shape[-1]) * gatt_ref[...]).astype(bf16)
    mixed = jnp.concatenate([mp_ref[...], na], axis=1)
    x1 = x_ref[...] + _dot(mixed, wo_ref[...])
    x1_ref[...] = x1
    h2_ref[...] = (_rms(x1, x1.shape[-1]) * gffn_ref[...]).astype(bf16)


def _outproj(x, mp, attn, w, tm):
    M, D = x.shape
    const = lambda a: pl.BlockSpec(a.shape, lambda i: (0, 0))
    row = lambda n: pl.BlockSpec((tm, n), lambda i: (i, 0))
    return pl.pallas_call(
        _outproj_body,
        grid=(M // tm,),
        in_specs=[row(D), row(mp.shape[1]), row(attn.shape[1]),
                  const(w['g_out_attn']), const(w['w_o']), const(w['g_ffn'])],
        out_specs=(row(D), row(D)),
        out_shape=(jax.ShapeDtypeStruct((M, D), f32), jax.ShapeDtypeStruct((M, D), bf16)),
        compiler_params=_params("parallel"),
        name="out_proj",
    )(x, mp, attn, w['g_out_attn'], w['w_o'], w['g_ffn'])


def _ffn_finish(f, g, g1, g2, h2, x1_ref, wu_ref, wd_ref, cw_ref, cb_ref, y_ref, acc):
    gate = cb_ref[...] + g2 * cw_ref[0:1, :] + g1 * cw_ref[1:2, :] + g * cw_ref[2:3, :]
    act = (gate * jax.nn.sigmoid(gate)) * _dot(h2, wu_ref[...])
    part = _dot(act.astype(bf16), wd_ref[...])

    @pl.when(f == 0)
    def _():
        acc[...] = x1_ref[...] + part

    @pl.when(f > 0)
    def _():
        acc[...] += part

    @pl.when(f == pl.num_programs(1) - 1)
    def _():
        y_ref[...] = acc[...]


def _ffn_prompt_body(h2_ref, halo_ref, x1_ref, wg_ref, wu_ref, wd_ref, cw_ref, cb_ref,
                     y_ref, tail_ref, acc, gbuf, *, tm, tiles_per_seq):
    i = pl.program_id(0)
    f = pl.program_id(1)
    hb = BF16_SUBLANES
    h2 = h2_ref[...]
    g = _dot(h2, wg_ref[...])
    gh = _dot(halo_ref[...], wg_ref[...])
    gbuf[0:hb, :] = jnp.where(i % tiles_per_seq == 0, 0.0, gh)
    gbuf[hb:hb + tm, :] = g
    tail_ref[0] = gbuf[tm:tm + hb, :]
    g1 = gbuf[hb - 1:hb - 1 + tm, :]
    g2 = gbuf[hb - 2:hb - 2 + tm, :]
    _ffn_finish(f, g, g1, g2, h2, x1_ref, wu_ref, wd_ref, cw_ref, cb_ref, y_ref, acc)


def _ffn_prompt(h2, x1, S, w, tm, tf):
    M, D = x1.shape
    F = w['w_gate'].shape[1]
    hb = BF16_SUBLANES
    nt = M // tm
    halo_blocks = tm // hb
    resident = dict(pipeline_mode=pl.Buffered(1)) if tf == F else {}
    y, tail = pl.pallas_call(
        functools.partial(_ffn_prompt_body, tm=tm, tiles_per_seq=S // tm),
        grid=(nt, F // tf),
        in_specs=[pl.BlockSpec((tm, D), lambda i, f: (i, 0)),
                  pl.BlockSpec((hb, D), lambda i, f: (jnp.maximum(i * halo_blocks - 1, 0), 0)),
                  pl.BlockSpec((tm, D), lambda i, f: (i, 0)),
                  pl.BlockSpec((D, tf), lambda i, f: (0, f), **resident),
                  pl.BlockSpec((D, tf), lambda i, f: (0, f), **resident),
                  pl.BlockSpec((tf, D), lambda i, f: (f, 0), **resident),
                  pl.BlockSpec((CONV_WIDTH, tf), lambda i, f: (0, f)),
                  pl.BlockSpec((1, tf), lambda i, f: (0, f))],
        out_specs=(pl.BlockSpec((tm, D), lambda i, f: (i, 0)),
                   pl.BlockSpec((1, hb, tf), lambda i, f: (i, 0, f))),
        out_shape=(jax.ShapeDtypeStruct((M, D), f32), jax.ShapeDtypeStruct((nt, hb, F), f32)),
        scratch_shapes=[pltpu.VMEM((tm, D), f32), pltpu.VMEM((tm + hb, tf), f32)],
        compiler_params=_params("parallel", "arbitrary"),
        name="ffn_prompt",
    )(h2, h2, x1, w['w_gate'], w['w_up'], w['w_down'], w['conv_w'], w['conv_b'])
    return y, tail


def _ffn_sample_body(h2_ref, st_ref, x1_ref, wg_ref, wu_ref, wd_ref, cw_ref, cb_ref, y_ref, g_ref, acc):
    f = pl.program_id(1)
    h2 = h2_ref[...]
    g = _dot(h2, wg_ref[...])
    g_ref[...] = g
    _ffn_finish(f, g, st_ref[1], st_ref[0], h2, x1_ref, wu_ref, wd_ref, cw_ref, cb_ref, y_ref, acc)


def _ffn_sample(h2, x1, st_t, w, tf):
    M, D = x1.shape
    F = w['w_gate'].shape[1]
    return pl.pallas_call(
        _ffn_sample_body,
        grid=(1, F // tf),
        in_specs=[pl.BlockSpec((M, D), lambda i, f: (0, 0)),
                  pl.BlockSpec((CONV_WIDTH - 1, M, tf), lambda i, f: (0, 0, f)),
                  pl.BlockSpec((M, D), lambda i, f: (0, 0)),
                  pl.BlockSpec((D, tf), lambda i, f: (0, f)),
                  pl.BlockSpec((D, tf), lambda i, f: (0, f)),
                  pl.BlockSpec((tf, D), lambda i, f: (f, 0)),
                  pl.BlockSpec((CONV_WIDTH, tf), lambda i, f: (0, f)),
                  pl.BlockSpec((1, tf), lambda i, f: (0, f))],
        out_specs=(pl.BlockSpec((M, D), lambda i, f: (0, 0)),
                   pl.BlockSpec((M, tf), lambda i, f: (0, f))),
        out_shape=(jax.ShapeDtypeStruct((M, D), f32), jax.ShapeDtypeStruct((M, F), f32)),
        scratch_shapes=[pltpu.VMEM((M, D), f32)],
        compiler_params=_params("parallel", "arbitrary"),
        name="ffn_sample",
    )(h2, st_t, x1, w['w_gate'], w['w_up'], w['w_down'], w['conv_w'], w['conv_b'])


def _qk_gain(g):
    g_pe = g[QK_NOPE_DIM:]
    return jnp.concatenate([g[:QK_NOPE_DIM], g_pe, g_pe])


def _rope_table(pos):
    half = QK_ROPE_DIM // 2
    inv = ROPE_BASE ** (-jnp.arange(0, QK_ROPE_DIM, 2, dtype=f32) / QK_ROPE_DIM)
    ang = pos.astype(f32)[:, None] * inv[None, :]
    cos, sin = jnp.cos(ang), jnp.sin(ang)
    P = pos.shape[0]
    z = lambda n: jnp.zeros((P, n), f32)
    tail = LANE - QK_HEAD_DIM
    c = jnp.concatenate([jnp.ones((P, QK_NOPE_DIM), f32), cos, cos, z(tail)], axis=1)
    s_lo = jnp.concatenate([z(QK_NOPE_DIM), -sin, z(half), z(tail)], axis=1)
    s_hi = jnp.concatenate([z(QK_NOPE_DIM), z(half), sin, z(tail)], axis=1)
    return jnp.concatenate([c, s_lo, s_hi], axis=1)


def _prep_weights(g_mix, w_in, g_q_a, w_uq, g_qn, g_kv_a, w_uk, g_kn, w_uv, w_pool, s_pool,
                  g_out, w_o, g_ffn, w_gate, w_up, conv_w, conv_b, w_down):
    D = w_in.shape[0]
    q_rank = g_q_a.shape[0]
    kv_rank = g_kv_a.shape[0]
    pool_w = s_pool.shape[0]
    head_pad = LANE - QK_HEAD_DIM
    half = QK_ROPE_DIM // 2
    main = pool_w + q_rank + kv_rank
    w_in_p = jnp.concatenate([w_in[:, :main], jnp.zeros((D, QK_NOPE_DIM), f32), w_in[:, main:],
                              jnp.zeros((D, head_pad), f32)], axis=1).astype(bf16)
    w_uq_h = w_uq.reshape(q_rank, N_HEADS, QK_HEAD_DIM)
    w_uq_p = jnp.pad(w_uq_h, ((0, 0), (0, 0), (0, head_pad)))
    lo = w_uq_h[:, :, QK_NOPE_DIM:QK_NOPE_DIM + half]
    hi = w_uq_h[:, :, QK_NOPE_DIM + half:]
    w_uq_rot = jnp.concatenate([jnp.zeros((q_rank, N_HEADS, QK_NOPE_DIM), f32), -hi, lo,
                                jnp.zeros((q_rank, N_HEADS, head_pad), f32)], axis=2)
    w_uq_p = jnp.concatenate([w_uq_p.reshape(q_rank, N_HEADS * LANE),
                              w_uq_rot.reshape(q_rank, N_HEADS * LANE)], axis=1).astype(bf16)
    gq = _qk_gain(g_qn) * _qk_gain(g_kn) * (QK_HEAD_DIM ** -0.5 * LOG2_E)
    gq = jnp.tile(jnp.pad(gq, (0, head_pad)), N_HEADS)[None, :]
    w_uk_p = jnp.pad(w_uk, ((0, 0), (0, 0), (0, LANE - QK_NOPE_DIM))).reshape(kv_rank, N_HEADS * LANE)
    w_uk_p = w_uk_p.astype(bf16)
    w_uv_t = w_uv.reshape(kv_rank, N_HEADS * V_HEAD_DIM).T.astype(bf16)
    w_uk_t = w_uk.reshape(kv_rank, N_HEADS * QK_NOPE_DIM).T.astype(bf16)
    eye = jnp.eye(N_HEADS, dtype=f32)
    w_uk_bd = jnp.einsum('chd,hg->hdgc', jnp.pad(w_uk, ((0, 0), (0, 0), (0, LANE - QK_NOPE_DIM))), eye)
    w_uk_bd = w_uk_bd.reshape(N_HEADS * LANE, N_HEADS * kv_rank).astype(bf16)
    w_uv_bd = jnp.einsum('chv,hg->hcgv', w_uv, eye).reshape(N_HEADS * kv_rank, N_HEADS * V_HEAD_DIM).astype(bf16)
    return {
        'pool_w': pool_w,
        'g_mix': g_mix[None, :], 'w_in': w_in_p, 'g_q_a': g_q_a[None, :], 'w_uq': w_uq_p, 'g_q': gq,
        'g_kv_a': g_kv_a[None, :], 'w_uk': w_uk_p, 'w_uv_t': w_uv_t, 'w_uk_t': w_uk_t, 'w_uk_bd': w_uk_bd, 'w_uv_bd': w_uv_bd,
        'w_pool': w_pool.astype(bf16), 's_pool': s_pool[None, :],
        'g_out_pool': g_out[None, :pool_w], 'g_out_attn': g_out[None, pool_w:],
        'w_o': w_o.astype(bf16), 'g_ffn': g_ffn[None, :],
        'w_gate': w_gate.astype(bf16), 'w_up': w_up.astype(bf16), 'w_down': w_down.astype(bf16),
        'conv_w': conv_w, 'conv_b': conv_b[None, :],
    }


def _tile(n, pref):
    t = min(n, pref)
    assert n % t == 0, (n, pref)
    return t


def _ffn_cols(F):
    half = F // 2
    return half if half % LANE == 0 else F


def kernel(x_prompt, x_sample, cache_ckv, cache_kpe, state_pool, state_conv, page_table, g_mix, w_in, g_q_a,
           w_uq, g_qn, g_kv_a, w_uk, g_kn, w_uv, w_pool, s_pool, g_out, w_o, g_ffn, w_gate, w_up, conv_w,
           conv_b, w_down):
    depth = g_mix.shape[0]
    assert depth == 1, "single-layer trunk only"
    B, S, D = x_prompt.shape
    DB, DS = x_sample.shape[:2]
    assert DS == 1, "one new token per sequence"
    n_pages = page_table.shape[1]
    page = cache_ckv.shape[2]
    past_len = n_pages * page
    l = 0
    w = _prep_weights(g_mix[l], w_in[l], g_q_a[l], w_uq[l], g_qn[l], g_kv_a[l], w_uk[l], g_kn[l], w_uv[l],
                      w_pool[l], s_pool[l], g_out[l], w_o[l], g_ffn[l], w_gate[l], w_up[l], conv_w[l],
                      conv_b[l], w_down[l])
    kv_rank = g_kv_a.shape[1]
    F = w_gate.shape[2]
    tf = _ffn_cols(F)

    xp = x_prompt.reshape(B * S, D)
    tq = _tile(S, 512)
    tm = _tile(tq, 256)
    rope_p = _rope_table(jnp.arange(S, dtype=jnp.int32))
    u_p, q_p, ckv_p, kpe_p, k_p, vt_p = _proj(xp, rope_p, S // tm, w, tm, tq)
    mp_p = _pool_prompt(u_p, B, S, w, _tile(S, 512))
    attn_p = _flash(q_p, k_p, vt_p, B, S, tq, 8)
    x1_p, h2_p = _outproj(xp, mp_p, attn_p, w, _tile(S, 512))
    tmf = _tile(S, 256)
    y_p, tail = _ffn_prompt(h2_p, x1_p, S, w, tmf, F)
    tiles_per_seq = S // tmf
    conv_p = tail[tiles_per_seq - 1::tiles_per_seq, BF16_SUBLANES - (CONV_WIDTH - 1):, :]

    xs = x_sample.reshape(DB, D)
    pos_s = past_len + jnp.arange(DS, dtype=jnp.int32)
    rope_s = jnp.tile(_rope_table(pos_s), (DB, 1))
    u_s, q_s, ckv_s, kpe_s, k_s, _ = _proj(xs, rope_s, 1, w, DB, DB)
    st_pool_t = jnp.transpose(state_pool[l], (1, 0, 2))
    mp_s = _pool_sample(u_s, st_pool_t, past_len, w)
    qabs = _mm(q_s, w['w_uk_bd'], "absorb_q").reshape(DB, N_HEADS, kv_rank).astype(bf16)
    qabs = jnp.pad(qabs, ((0, 0), (0, BF16_SUBLANES - N_HEADS), (0, 0)))
    q3 = q_s.reshape(DB, N_HEADS, LANE)
    qpe = q3[:, :, QK_NOPE_DIM:QK_HEAD_DIM]
    cp = _tile(n_pages, 32)
    ctx = _decode(page_table, q3, k_s.reshape(DB, N_HEADS, LANE), qabs, qpe, ckv_s.reshape(DB, 1, kv_rank),
                  w['w_uk_t'], cache_ckv[l], jnp.swapaxes(cache_kpe[l], 1, 2), cp, 1)
    attn_s = _mm(ctx.reshape(DB, N_HEADS * kv_rank), w['w_uv_bd'], "value_up")
    x1_s, h2_s = _outproj(xs, mp_s, attn_s, w, DB)
    st_conv_t = jnp.transpose(state_conv[l], (1, 0, 2))
    y_s, g_s = _ffn_sample(h2_s, x1_s, st_conv_t, w, tf)

    kpe_sl = slice(QK_NOPE_DIM, QK_HEAD_DIM)
    P = POOL_STATE_LEN
    return (
        y_p.reshape(B, S, D),
        y_s.reshape(DB, DS, D),
        ckv_p.reshape(1, B, S, kv_rank),
        kpe_p[:, kpe_sl].reshape(1, B, S, QK_ROPE_DIM),
        u_p.reshape(B, S, -1)[None, :, S - P:, :],
        conv_p[None],
        ckv_s.reshape(1, DB, DS, kv_rank),
        kpe_s[:, kpe_sl].reshape(1, DB, DS, QK_ROPE_DIM),
        jnp.concatenate([state_pool[l], u_s[:, None, :]], axis=1)[None, :, -P:, :],
        jnp.concatenate([state_conv[l], g_s[:, None, :]], axis=1)[None, :, -(CONV_WIDTH - 1):, :],
    )
```

```python
import functools

import jax
import jax.numpy as jnp
from jax import lax
from jax.experimental import pallas as pl
from jax.experimental.pallas import tpu as pltpu

N_HEADS = 8
QK_NOPE_DIM = 64
QK_ROPE_DIM = 32
QK_HEAD_DIM = QK_NOPE_DIM + QK_ROPE_DIM
V_HEAD_DIM = 64
POOL_WINDOWS = (2, 4, 8, 16)
POOL_STATE_LEN = max(POOL_WINDOWS) - 1
CONV_WIDTH = 3
ROPE_BASE = 10000.0
RMS_EPS = 1e-6
LOG2_E = 1.4426950408889634

LANE = 128
BF16_SUBLANES = 16
VMEM_LIMIT = 48 * 1024 * 1024
DECODE_SLOTS = 3
LOOKAHEAD = 1

_NT = (((1,), (1,)), ((), ()))

bf16 = jnp.bfloat16
f32 = jnp.float32


def _rms(x, width):
    return x * lax.rsqrt(jnp.sum(x * x, axis=-1, keepdims=True) * (1.0 / width) + RMS_EPS)


def _dot(a, b):
    return jnp.dot(a, b, preferred_element_type=f32)


def _params(*sem):
    return pltpu.CompilerParams(dimension_semantics=sem, vmem_limit_bytes=VMEM_LIMIT)


def _proj_body(x_ref, rope_ref, gmix_ref, win_ref, gqa_ref, wuq_ref, gq_ref, gkva_ref, wuk_ref, wuvt_ref,
               u_ref, q_ref, ckv_ref, kpe_ref, k_ref, vt_ref, *, pool_w, q_rank, kv_rank):
    x = x_ref[...]
    h = (_rms(x, x.shape[-1]) * gmix_ref[...]).astype(bf16)
    proj = _dot(h, win_ref[...])
    u_ref[...] = proj[:, :pool_w]
    o = pool_w
    cq = (_rms(proj[:, o:o + q_rank], q_rank) * gqa_ref[...]).astype(bf16)
    o += q_rank
    ckv = _rms(proj[:, o:o + kv_rank], kv_rank) * gkva_ref[...]
    o += kv_rank
    ckv_ref[...] = ckv
    cos = rope_ref[:, 0:LANE]
    sin_lo = rope_ref[:, LANE:2 * LANE]
    sin_hi = rope_ref[:, 2 * LANE:3 * LANE]
    sin_abs = sin_hi - sin_lo
    half = QK_ROPE_DIM // 2
    raw_kpe = proj[:, o:o + LANE]
    kpe = raw_kpe * cos + pltpu.roll(raw_kpe, LANE - half, 1) * sin_lo + pltpu.roll(raw_kpe, half, 1) * sin_hi
    kpe_ref[...] = kpe
    qraw = _dot(cq, wuq_ref[...])
    hq = N_HEADS * LANE
    ckv_b = ckv.astype(bf16)
    kn = _dot(ckv_b, wuk_ref[...])
    vt_ref[0] = lax.dot_general(wuvt_ref[...], ckv_b, _NT, preferred_element_type=f32).astype(bf16)
    for hd in range(N_HEADS):
        sl = slice(hd * LANE, (hd + 1) * LANE)
        qh = qraw[:, sl] * cos + qraw[:, hq + hd * LANE:hq + (hd + 1) * LANE] * sin_abs
        q_ref[:, sl] = (_rms(qh, QK_HEAD_DIM) * gq_ref[:, sl]).astype(bf16)
        kh = kn[:, sl] + kpe
        k_ref[:, sl] = _rms(kh, QK_HEAD_DIM).astype(bf16)


def _proj(x, rope, n_rope_blocks, w, tm, slab):
    M, D = x.shape
    pool_w = w['pool_w']
    q_rank = w['g_q_a'].shape[1]
    kv_rank = w['g_kv_a'].shape[1]
    hq = N_HEADS * LANE
    hv = N_HEADS * V_HEAD_DIM
    per_slab = slab // tm
    const = lambda a: pl.BlockSpec(a.shape, lambda i: (0, 0))
    row = lambda n: pl.BlockSpec((tm, n), lambda i: (i, 0))
    out_shapes = (
        jax.ShapeDtypeStruct((M, pool_w), f32), jax.ShapeDtypeStruct((M, hq), bf16),
        jax.ShapeDtypeStruct((M, kv_rank), f32), jax.ShapeDtypeStruct((M, LANE), f32),
        jax.ShapeDtypeStruct((M, hq), bf16), jax.ShapeDtypeStruct((M // slab, hv, slab), bf16))
    return pl.pallas_call(
        functools.partial(_proj_body, pool_w=pool_w, q_rank=q_rank, kv_rank=kv_rank),
        grid=(M // tm,),
        in_specs=[row(D), pl.BlockSpec((tm, 3 * LANE), lambda i: (i % n_rope_blocks, 0)),
                  const(w['g_mix']), const(w['w_in']), const(w['g_q_a']), const(w['w_uq']),
                  const(w['g_q']), const(w['g_kv_a']), const(w['w_uk']), const(w['w_uv_t'])],
        out_specs=(row(pool_w), row(hq), row(kv_rank), row(LANE), row(hq),
                   pl.BlockSpec((1, hv, tm), lambda i: (i // per_slab, 0, i % per_slab))),
        out_shape=out_shapes,
        compiler_params=_params("parallel"),
        name="in_proj",
    )(x, rope, w['g_mix'], w['w_in'], w['g_q_a'], w['w_uq'], w['g_q'], w['g_kv_a'], w['w_uk'], w['w_uv_t'])


def _pool_finish(diffs, wpool_ref, spool_ref, gout_ref, o_ref):
    ys = []
    for g, d in enumerate(diffs):
        sl = slice(g * LANE, (g + 1) * LANE)
        ys.append(_dot(d.astype(bf16), wpool_ref[g]) * spool_ref[:, sl])
    width = LANE * len(ys)
    ssq = sum(jnp.sum(y * y, axis=-1, keepdims=True) for y in ys)
    scale = lax.rsqrt(ssq * (1.0 / width) + RMS_EPS)
    for g, y in enumerate(ys):
        sl = slice(g * LANE, (g + 1) * LANE)
        o_ref[:, sl] = (y * scale * gout_ref[:, sl]).astype(o_ref.dtype)


def _pool_prompt_body(u_ref, wpool_ref, spool_ref, gout_ref, o_ref, buf, *, ts):
    j = pl.program_id(1)
    halo = POOL_STATE_LEN + 1

    @pl.when(j == 0)
    def _():
        buf[0:halo, :] = jnp.zeros((halo, buf.shape[1]), f32)

    @pl.when(j > 0)
    def _():
        buf[0:halo, :] = buf[ts:ts + halo, :]

    buf[halo:halo + ts, :] = u_ref[...]
    pos = j * ts + lax.broadcasted_iota(jnp.int32, (ts, 1), 0)
    diffs = []
    for g, wdw in enumerate(POOL_WINDOWS):
        sl = slice(g * LANE, (g + 1) * LANE)
        x = buf[halo:halo + ts, sl]
        win = x
        for k in range(1, wdw):
            win = win + buf[halo - k:halo - k + ts, sl]
        cnt = jnp.minimum(wdw, pos + 1).astype(f32)
        diffs.append(win / cnt - x)
    _pool_finish(diffs, wpool_ref, spool_ref, gout_ref, o_ref)


def _pool_prompt(u, B, S, w, ts):
    M, W = u.shape
    ns = S // ts
    const2 = lambda a: pl.BlockSpec(a.shape, lambda b, j: (0,) * a.ndim)
    return pl.pallas_call(
        functools.partial(_pool_prompt_body, ts=ts),
        grid=(B, ns),
        in_specs=[pl.BlockSpec((ts, W), lambda b, j: (b * ns + j, 0)),
                  const2(w['w_pool']), const2(w['s_pool']), const2(w['g_out_pool'])],
        out_specs=pl.BlockSpec((ts, W), lambda b, j: (b * ns + j, 0)),
        out_shape=jax.ShapeDtypeStruct((M, W), bf16),
        scratch_shapes=[pltpu.VMEM((ts + POOL_STATE_LEN + 1, W), f32)],
        compiler_params=_params("arbitrary", "arbitrary"),
        name="pool_prompt",
    )(u, w['w_pool'], w['s_pool'], w['g_out_pool'])


def _pool_sample_body(u_ref, st_ref, wpool_ref, spool_ref, gout_ref, o_ref, *, pos):
    diffs = []
    for g, wdw in enumerate(POOL_WINDOWS):
        sl = slice(g * LANE, (g + 1) * LANE)
        x = u_ref[:, sl]
        win = x
        for k in range(1, wdw):
            win = win + st_ref[POOL_STATE_LEN - k, :, sl]
        diffs.append(win / float(min(wdw, pos + 1)) - x)
    _pool_finish(diffs, wpool_ref, spool_ref, gout_ref, o_ref)


def _pool_sample(u, st_t, pos, w):
    M, W = u.shape
    return pl.pallas_call(
        functools.partial(_pool_sample_body, pos=pos),
        out_shape=jax.ShapeDtypeStruct((M, W), bf16),
        compiler_params=pltpu.CompilerParams(vmem_limit_bytes=VMEM_LIMIT),
        name="pool_sample",
    )(u, st_t, w['w_pool'], w['s_pool'], w['g_out_pool'])


def _mix_out(x, mp, attn, gatt_ref, wo_ref, gffn_ref, x1_ref, h2_ref):
    na = (_rms(attn, attn.shape[-1]) * gatt_ref[...]).astype(bf16)
    x1 = x + _dot(jnp.concatenate([mp, na], axis=1), wo_ref[...])
    x1_ref[...] = x1
    h2_ref[...] = (_rms(x1, x1.shape[-1]) * gffn_ref[...]).astype(bf16)


def _flash_body(q_ref, k_ref, vt_ref, x_ref, mp_ref, gatt_ref, wo_ref, gffn_ref, x1_ref, h2_ref,
                m_scr, acc_scr, *, tq):
    i = pl.program_id(1)
    heads = m_scr.shape[0]
    key_idx = lax.broadcasted_iota(jnp.int32, (tq, tq), 0)
    qry_idx = lax.broadcasted_iota(jnp.int32, (tq, tq), 1)
    m_scr[...] = jnp.full(m_scr.shape, -1e30, f32)
    acc_scr[...] = jnp.zeros(acc_scr.shape, f32)
    ones_rows = jnp.ones((BF16_SUBLANES, tq), bf16)

    def step(j, masked):
        start = pl.multiple_of(j * tq, tq)

        def scores(hh):
            sl = slice(hh * LANE, (hh + 1) * LANE)
            return lax.dot_general(k_ref[pl.ds(start, tq), sl], q_ref[:, sl], _NT, preferred_element_type=f32)

        ready = [scores(hh) for hh in range(min(LOOKAHEAD, heads))]
        for hh in range(heads):
            st = ready.pop(0)
            if hh + LOOKAHEAD < heads:
                ready.append(scores(hh + LOOKAHEAD))
            if masked:
                st = jnp.where(key_idx <= qry_idx, st, -1e30)
            m = m_scr[hh]
            m_new = jnp.maximum(m, jnp.max(st, axis=0, keepdims=True))
            alpha = jnp.exp2(m - m_new)
            pt = jnp.exp2((st - m_new).astype(bf16))
            vt = jnp.concatenate([vt_ref[j, hh * V_HEAD_DIM:(hh + 1) * V_HEAD_DIM, :], ones_rows], axis=0)
            acc_scr[hh] = acc_scr[hh] * alpha + _dot(vt, pt)
            m_scr[hh] = m_new

    def body(j, carry):
        step(j, False)
        return carry

    lax.fori_loop(0, i, body, 0)
    step(i, True)
    dv = V_HEAD_DIM
    out_t = jnp.concatenate([acc_scr[hh, 0:dv] / acc_scr[hh, dv:dv + 1] for hh in range(heads)], axis=0)
    _mix_out(x_ref[...], mp_ref[...], out_t.T, gatt_ref, wo_ref, gffn_ref, x1_ref, h2_ref)


def _flash(q, k, vt, x, mp, w, B, S, tq):
    M, D = x.shape
    nq = S // tq
    const = lambda a: pl.BlockSpec(a.shape, lambda b, i: (0, 0))
    tile = lambda n: pl.BlockSpec((tq, n), lambda b, i: (b * nq + i, 0))
    return pl.pallas_call(
        functools.partial(_flash_body, tq=tq),
        grid=(B, nq),
        in_specs=[tile(N_HEADS * LANE),
                  pl.BlockSpec((S, N_HEADS * LANE), lambda b, i: (b, 0)),
                  pl.BlockSpec((nq, N_HEADS * V_HEAD_DIM, tq), lambda b, i: (b, 0, 0)),
                  tile(D), tile(mp.shape[1]),
                  const(w['g_out_attn']), const(w['w_o']), const(w['g_ffn'])],
        out_specs=(tile(D), tile(D)),
        out_shape=(jax.ShapeDtypeStruct((M, D), f32), jax.ShapeDtypeStruct((M, D), bf16)),
        scratch_shapes=[pltpu.VMEM((N_HEADS, 1, tq), f32),
                        pltpu.VMEM((N_HEADS, V_HEAD_DIM + BF16_SUBLANES, tq), f32)],
        compiler_params=_params("parallel", "arbitrary"),
        name="flash_prompt",
    )(q, k, vt, x, mp, w['g_out_attn'], w['w_o'], w['g_ffn'])


def _mm_body(a_ref, b_ref, o_ref):
    o_ref[...] = _dot(a_ref[...].astype(bf16), b_ref[...])


def _mm(a, b, name):
    return pl.pallas_call(
        _mm_body,
        out_shape=jax.ShapeDtypeStruct((a.shape[0], b.shape[1]), f32),
        compiler_params=pltpu.CompilerParams(vmem_limit_bytes=VMEM_LIMIT),
        name=name,
    )(a, b)


def _decode_body(pt_ref, q_ref, knew_ref, qabs_ref, qpe_ref, cnew_ref, wukt_ref, cache_ckv, cache_kpet,
                 o_ref, ckv_buf, kpe_buf, sems, lhs, m_scr, l_scr, acc_scr, *, n_pages, cp, parts):
    b = pl.program_id(0)
    c = pl.program_id(1)
    nc = pl.num_programs(1)
    total = pl.num_programs(0) * nc
    flat = b * nc + c
    n_slots = ckv_buf.shape[0]
    ahead = n_slots - 1
    slot = flat % n_slots
    nope_rows = wukt_ref.shape[0]
    page = cache_ckv.shape[1]

    def copies(flat_idx, slot_idx):
        base = (flat_idx // nc) * n_pages + (flat_idx % nc) * cp
        out = []
        for p in range(cp):
            phys = pt_ref[base + p]
            rows = pl.ds(p * page, page)
            out.append(pltpu.make_async_copy(cache_ckv.at[phys], ckv_buf.at[slot_idx, rows], sems.at[0, slot_idx]))
            out.append(pltpu.make_async_copy(cache_kpet.at[phys], kpe_buf.at[slot_idx, p], sems.at[1, slot_idx]))
        return out

    @pl.when(flat == 0)
    def _():
        for k in range(ahead):
            for cpy in copies(jnp.minimum(k, total - 1), k):
                cpy.start()

    @pl.when(c == 0)
    def _():
        lhs[0:nope_rows, :] = wukt_ref[...]
        lhs[nope_rows:nope_rows + BF16_SUBLANES, :] = qabs_ref[0]
        s_new = jnp.sum(q_ref[0].astype(f32) * knew_ref[0].astype(f32), axis=-1, keepdims=True)
        m_scr[...] = s_new
        l_scr[...] = jnp.ones_like(l_scr)
        acc_scr[...] = jnp.broadcast_to(cnew_ref[0], acc_scr.shape)

    for cpy in copies(flat, slot):
        cpy.wait()

    qpe = qpe_ref[0]
    ones = jnp.ones(qpe.shape, bf16)
    fold = (lax.broadcasted_iota(jnp.int32, (N_HEADS, 8 * N_HEADS), 1) // 8
            == lax.broadcasted_iota(jnp.int32, (N_HEADS, 8 * N_HEADS), 0)).astype(bf16)
    pages_per_part = cp // parts
    tokens = pages_per_part * page
    cts, scores = [], []
    for part in range(parts):
        ct = ckv_buf[slot, part * tokens:(part + 1) * tokens, :].astype(bf16)
        if part == 0:
            for cpy in copies(jnp.minimum(flat + ahead, total - 1), (flat + ahead) % n_slots):
                cpy.start()
        kt = lax.dot_general(lhs[...], ct, _NT, preferred_element_type=f32)
        kn = kt[0:nope_rows].reshape(N_HEADS, nope_rows // (8 * N_HEADS), 8, tokens)
        partial = jnp.sum(kn * kn, axis=1).reshape(8 * N_HEADS, tokens)
        ssq = _dot(fold, partial.astype(bf16))
        num = kt[nope_rows:nope_rows + N_HEADS]
        pe_num, pe_ssq = [], []
        for p in range(part * pages_per_part, (part + 1) * pages_per_part):
            kp = kpe_buf[slot, p]
            pe_num.append(_dot(qpe, kp.astype(bf16)))
            pe_ssq.append(_dot(ones, (kp * kp).astype(bf16)))
        num = num + jnp.concatenate(pe_num, axis=1)
        ssq = ssq + jnp.concatenate(pe_ssq, axis=1)
        scores.append(num * lax.rsqrt(ssq * (1.0 / QK_HEAD_DIM) + RMS_EPS))
        cts.append(ct)
    s = jnp.concatenate(scores, axis=1)
    m = m_scr[...]
    m_new = jnp.maximum(m, jnp.max(s, axis=-1, keepdims=True))
    alpha = jnp.exp2(m - m_new)
    p = jnp.exp2(s - m_new)
    l_scr[...] = l_scr[...] * alpha + jnp.sum(p, axis=-1, keepdims=True)
    acc = acc_scr[...] * alpha
    p = p.astype(bf16)
    for part, ct in enumerate(cts):
        acc = acc + _dot(p[:, part * tokens:(part + 1) * tokens], ct)
    acc_scr[...] = acc
    m_scr[...] = m_new

    @pl.when(c == nc - 1)
    def _():
        o_ref[0] = acc_scr[...] / l_scr[...]

    @pl.when(flat == total - 1)
    def _():
        for k in range(1, ahead + 1):
            for cpy in copies(total - 1, (flat + k) % n_slots):
                cpy.wait()


def _decode(page_table, q3, knew3, qabs, qpe, cnew, wukt, cache_ckv, cache_kpet, cp, parts):
    DB, n_pages = page_table.shape
    page, kv_rank = cache_ckv.shape[1:]
    rope = cache_kpet.shape[1]
    nc = n_pages // cp
    T = cp * page
    per_seq = lambda a: pl.BlockSpec((1,) + a.shape[1:], lambda b, c, pt: (b,) + (0,) * (a.ndim - 1))
    grid_spec = pltpu.PrefetchScalarGridSpec(
        num_scalar_prefetch=1,
        grid=(DB, nc),
        in_specs=[per_seq(q3), per_seq(knew3), per_seq(qabs), per_seq(qpe), per_seq(cnew),
                  pl.BlockSpec(wukt.shape, lambda b, c, pt: (0, 0)),
                  pl.BlockSpec(memory_space=pl.ANY), pl.BlockSpec(memory_space=pl.ANY)],
        out_specs=pl.BlockSpec((1, N_HEADS, kv_rank), lambda b, c, pt: (b, 0, 0)),
        scratch_shapes=[pltpu.VMEM((DECODE_SLOTS, T, kv_rank), f32), pltpu.VMEM((DECODE_SLOTS, cp, rope, page), f32),
                        pltpu.SemaphoreType.DMA((2, DECODE_SLOTS)),
                        pltpu.VMEM((wukt.shape[0] + BF16_SUBLANES, kv_rank), bf16),
                        pltpu.VMEM((N_HEADS, 1), f32), pltpu.VMEM((N_HEADS, 1), f32),
                        pltpu.VMEM((N_HEADS, kv_rank), f32)])
    return pl.pallas_call(
        functools.partial(_decode_body, n_pages=n_pages, cp=cp, parts=parts),
        grid_spec=grid_spec,
        out_shape=jax.ShapeDtypeStruct((DB, N_HEADS, kv_rank), f32),
        compiler_params=_params("arbitrary", "arbitrary"),
        name="decode_attn",
    )(page_table.reshape(-1), q3, knew3, qabs, qpe, cnew, wukt, cache_ckv, cache_kpet)


def _outproj_body(x_ref, mp_ref, attn_ref, gatt_ref, wo_ref, gffn_ref, x1_ref, h2_ref):
    _mix_out(x_ref[...], mp_ref[...], attn_ref[...], gatt_ref, wo_ref, gffn_ref, x1_ref, h2_ref)


def _outproj(x, mp, attn, w, tm):
    M, D = x.shape
    const = lambda a: pl.BlockSpec(a.shape, lambda i: (0, 0))
    row = lambda n: pl.BlockSpec((tm, n), lambda i: (i, 0))
    return pl.pallas_call(
        _outproj_body,
        grid=(M // tm,),
        in_specs=[row(D), row(mp.shape[1]), row(attn.shape[1]),
                  const(w['g_out_attn']), const(w['w_o']), const(w['g_ffn'])],
        out_specs=(row(D), row(D)),
        out_shape=(jax.ShapeDtypeStruct((M, D), f32), jax.ShapeDtypeStruct((M, D), bf16)),
        compiler_params=_params("parallel"),
        name="out_proj",
    )(x, mp, attn, w['g_out_attn'], w['w_o'], w['g_ffn'])


def _ffn_finish(f, g, g1, g2, h2, x1_ref, wu_ref, wd_ref, cw_ref, cb_ref, y_ref, acc):
    gate = cb_ref[...] + g2 * cw_ref[0:1, :] + g1 * cw_ref[1:2, :] + g * cw_ref[2:3, :]
    act = (gate * jax.nn.sigmoid(gate)) * _dot(h2, wu_ref[...])
    part = _dot(act.astype(bf16), wd_ref[...])

    @pl.when(f == 0)
    def _():
        acc[...] = x1_ref[...] + part

    @pl.when(f > 0)
    def _():
        acc[...] += part

    @pl.when(f == pl.num_programs(1) - 1)
    def _():
        y_ref[...] = acc[...]


def _ffn_prompt_body(h2_ref, halo_ref, x1_ref, wg_ref, wu_ref, wd_ref, cw_ref, cb_ref,
                     y_ref, tail_ref, acc, gbuf, *, tm, tiles_per_seq):
    i = pl.program_id(0)
    f = pl.program_id(1)
    hb = BF16_SUBLANES
    h2 = h2_ref[...]
    g = _dot(h2, wg_ref[...])
    gh = _dot(halo_ref[...], wg_ref[...])
    gbuf[0:hb, :] = jnp.where(i % tiles_per_seq == 0, 0.0, gh)
    gbuf[hb:hb + tm, :] = g
    tail_ref[0] = gbuf[tm:tm + hb, :]
    g1 = gbuf[hb - 1:hb - 1 + tm, :]
    g2 = gbuf[hb - 2:hb - 2 + tm, :]
    _ffn_finish(f, g, g1, g2, h2, x1_ref, wu_ref, wd_ref, cw_ref, cb_ref, y_ref, acc)


def _ffn_prompt(h2, x1, S, w, tm, tf):
    M, D = x1.shape
    F = w['w_gate'].shape[1]
    hb = BF16_SUBLANES
    nt = M // tm
    halo_blocks = tm // hb
    resident = dict(pipeline_mode=pl.Buffered(1)) if tf == F else {}
    y, tail = pl.pallas_call(
        functools.partial(_ffn_prompt_body, tm=tm, tiles_per_seq=S // tm),
        grid=(nt, F // tf),
        in_specs=[pl.BlockSpec((tm, D), lambda i, f: (i, 0)),
                  pl.BlockSpec((hb, D), lambda i, f: (jnp.maximum(i * halo_blocks - 1, 0), 0)),
                  pl.BlockSpec((tm, D), lambda i, f: (i, 0)),
                  pl.BlockSpec((D, tf), lambda i, f: (0, f), **resident),
                  pl.BlockSpec((D, tf), lambda i, f: (0, f), **resident),
                  pl.BlockSpec((tf, D), lambda i, f: (f, 0), **resident),
                  pl.BlockSpec((CONV_WIDTH, tf), lambda i, f: (0, f)),
                  pl.BlockSpec((1, tf), lambda i, f: (0, f))],
        out_specs=(pl.BlockSpec((tm, D), lambda i, f: (i, 0)),
                   pl.BlockSpec((1, hb, tf), lambda i, f: (i, 0, f))),
        out_shape=(jax.ShapeDtypeStruct((M, D), f32), jax.ShapeDtypeStruct((nt, hb, F), f32)),
        scratch_shapes=[pltpu.VMEM((tm, D), f32), pltpu.VMEM((tm + hb, tf), f32)],
        compiler_params=_params("parallel", "arbitrary"),
        name="ffn_prompt",
    )(h2, h2, x1, w['w_gate'], w['w_up'], w['w_down'], w['conv_w'], w['conv_b'])
    return y, tail


def _ffn_sample_body(h2_ref, st_ref, x1_ref, wg_ref, wu_ref, wd_ref, cw_ref, cb_ref, y_ref, g_ref, acc):
    f = pl.program_id(1)
    h2 = h2_ref[...]
    g = _dot(h2, wg_ref[...])
    g_ref[...] = g
    _ffn_finish(f, g, st_ref[1], st_ref[0], h2, x1_ref, wu_ref, wd_ref, cw_ref, cb_ref, y_ref, acc)


def _ffn_sample(h2, x1, st_t, w, tf):
    M, D = x1.shape
    F = w['w_gate'].shape[1]
    return pl.pallas_call(
        _ffn_sample_body,
        grid=(1, F // tf),
        in_specs=[pl.BlockSpec((M, D), lambda i, f: (0, 0)),
                  pl.BlockSpec((CONV_WIDTH - 1, M, tf), lambda i, f: (0, 0, f)),
                  pl.BlockSpec((M, D), lambda i, f: (0, 0)),
                  pl.BlockSpec((D, tf), lambda i, f: (0, f)),
                  pl.BlockSpec((D, tf), lambda i, f: (0, f)),
                  pl.BlockSpec((tf, D), lambda i, f: (f, 0)),
                  pl.BlockSpec((CONV_WIDTH, tf), lambda i, f: (0, f)),
                  pl.BlockSpec((1, tf), lambda i, f: (0, f))],
        out_specs=(pl.BlockSpec((M, D), lambda i, f: (0, 0)),
                   pl.BlockSpec((M, tf), lambda i, f: (0, f))),
        out_shape=(jax.ShapeDtypeStruct((M, D), f32), jax.ShapeDtypeStruct((M, F), f32)),
        scratch_shapes=[pltpu.VMEM((M, D), f32)],
        compiler_params=_params("parallel", "arbitrary"),
        name="ffn_sample",
    )(h2, st_t, x1, w['w_gate'], w['w_up'], w['w_down'], w['conv_w'], w['conv_b'])


def _qk_gain(g):
    g_pe = g[QK_NOPE_DIM:]
    return jnp.concatenate([g[:QK_NOPE_DIM], g_pe, g_pe])


def _rope_table(pos):
    half = QK_ROPE_DIM // 2
    inv = ROPE_BASE ** (-jnp.arange(0, QK_ROPE_DIM, 2, dtype=f32) / QK_ROPE_DIM)
    ang = pos.astype(f32)[:, None] * inv[None, :]
    cos, sin = jnp.cos(ang), jnp.sin(ang)
    P = pos.shape[0]
    z = lambda n: jnp.zeros((P, n), f32)
    tail = LANE - QK_HEAD_DIM
    c = jnp.concatenate([jnp.ones((P, QK_NOPE_DIM), f32), cos, cos, z(tail)], axis=1)
    s_lo = jnp.concatenate([z(QK_NOPE_DIM), -sin, z(half), z(tail)], axis=1)
    s_hi = jnp.concatenate([z(QK_NOPE_DIM), z(half), sin, z(tail)], axis=1)
    return jnp.concatenate([c, s_lo, s_hi], axis=1)


def _prep_weights(g_mix, w_in, g_q_a, w_uq, g_qn, g_kv_a, w_uk, g_kn, w_uv, w_pool, s_pool,
                  g_out, w_o, g_ffn, w_gate, w_up, conv_w, conv_b, w_down):
    D = w_in.shape[0]
    q_rank = g_q_a.shape[0]
    kv_rank = g_kv_a.shape[0]
    pool_w = s_pool.shape[0]
    head_pad = LANE - QK_HEAD_DIM
    half = QK_ROPE_DIM // 2
    main = pool_w + q_rank + kv_rank
    w_in_p = jnp.concatenate([w_in[:, :main], jnp.zeros((D, QK_NOPE_DIM), f32), w_in[:, main:],
                              jnp.zeros((D, head_pad), f32)], axis=1).astype(bf16)
    w_uq_h = w_uq.reshape(q_rank, N_HEADS, QK_HEAD_DIM)
    w_uq_p = jnp.pad(w_uq_h, ((0, 0), (0, 0), (0, head_pad)))
    lo = w_uq_h[:, :, QK_NOPE_DIM:QK_NOPE_DIM + half]
    hi = w_uq_h[:, :, QK_NOPE_DIM + half:]
    w_uq_rot = jnp.concatenate([jnp.zeros((q_rank, N_HEADS, QK_NOPE_DIM), f32), -hi, lo,
                                jnp.zeros((q_rank, N_HEADS, head_pad), f32)], axis=2)
    w_uq_p = jnp.concatenate([w_uq_p.reshape(q_rank, N_HEADS * LANE),
                              w_uq_rot.reshape(q_rank, N_HEADS * LANE)], axis=1).astype(bf16)
    gq = _qk_gain(g_qn) * _qk_gain(g_kn) * (QK_HEAD_DIM ** -0.5 * LOG2_E)
    gq = jnp.tile(jnp.pad(gq, (0, head_pad)), N_HEADS)[None, :]
    w_uk_p = jnp.pad(w_uk, ((0, 0), (0, 0), (0, LANE - QK_NOPE_DIM))).reshape(kv_rank, N_HEADS * LANE)
    w_uk_p = w_uk_p.astype(bf16)
    w_uv_t = w_uv.reshape(kv_rank, N_HEADS * V_HEAD_DIM).T.astype(bf16)
    w_uk_t = w_uk.reshape(kv_rank, N_HEADS * QK_NOPE_DIM).T.astype(bf16)
    eye = jnp.eye(N_HEADS, dtype=f32)
    w_uk_bd = jnp.einsum('chd,hg->hdgc', jnp.pad(w_uk, ((0, 0), (0, 0), (0, LANE - QK_NOPE_DIM))), eye)
    w_uk_bd = w_uk_bd.reshape(N_HEADS * LANE, N_HEADS * kv_rank).astype(bf16)
    w_uv_bd = jnp.einsum('chv,hg->hcgv', w_uv, eye).reshape(N_HEADS * kv_rank, N_HEADS * V_HEAD_DIM).astype(bf16)
    return {
        'pool_w': pool_w,
        'g_mix': g_mix[None, :], 'w_in': w_in_p, 'g_q_a': g_q_a[None, :], 'w_uq': w_uq_p, 'g_q': gq,
        'g_kv_a': g_kv_a[None, :], 'w_uk': w_uk_p, 'w_uv_t': w_uv_t, 'w_uk_t': w_uk_t, 'w_uk_bd': w_uk_bd, 'w_uv_bd': w_uv_bd,
        'w_pool': w_pool.astype(bf16), 's_pool': s_pool[None, :],
        'g_out_pool': g_out[None, :pool_w], 'g_out_attn': g_out[None, pool_w:],
        'w_o': w_o.astype(bf16), 'g_ffn': g_ffn[None, :],
        'w_gate': w_gate.astype(bf16), 'w_up': w_up.astype(bf16), 'w_down': w_down.astype(bf16),
        'conv_w': conv_w, 'conv_b': conv_b[None, :],
    }


def _tile(n, pref):
    t = min(n, pref)
    assert n % t == 0, (n, pref)
    return t


def _ffn_cols(F):
    half = F // 2
    return half if half % LANE == 0 else F


def kernel(x_prompt, x_sample, cache_ckv, cache_kpe, state_pool, state_conv, page_table, g_mix, w_in, g_q_a,
           w_uq, g_qn, g_kv_a, w_uk, g_kn, w_uv, w_pool, s_pool, g_out, w_o, g_ffn, w_gate, w_up, conv_w,
           conv_b, w_down):
    depth = g_mix.shape[0]
    assert depth == 1, "single-layer trunk only"
    B, S, D = x_prompt.shape
    DB, DS = x_sample.shape[:2]
    assert DS == 1, "one new token per sequence"
    n_pages = page_table.shape[1]
    page = cache_ckv.shape[2]
    past_len = n_pages * page
    l = 0
    w = _prep_weights(g_mix[l], w_in[l], g_q_a[l], w_uq[l], g_qn[l], g_kv_a[l], w_uk[l], g_kn[l], w_uv[l],
                      w_pool[l], s_pool[l], g_out[l], w_o[l], g_ffn[l], w_gate[l], w_up[l], conv_w[l],
                      conv_b[l], w_down[l])
    kv_rank = g_kv_a.shape[1]
    F = w_gate.shape[2]
    tf = _ffn_cols(F)

    xp = x_prompt.reshape(B * S, D)
    tq = _tile(S, 512)
    tm = _tile(tq, 256)
    rope_p = _rope_table(jnp.arange(S, dtype=jnp.int32))
    u_p, q_p, ckv_p, kpe_p, k_p, vt_p = _proj(xp, rope_p, S // tm, w, tm, tq)
    mp_p = _pool_prompt(u_p, B, S, w, _tile(S, 512))
    x1_p, h2_p = _flash(q_p, k_p, vt_p, xp, mp_p, w, B, S, tq)
    tmf = _tile(S, 256)
    y_p, tail = _ffn_prompt(h2_p, x1_p, S, w, tmf, F)
    tiles_per_seq = S // tmf
    conv_p = tail[tiles_per_seq - 1::tiles_per_seq, BF16_SUBLANES - (CONV_WIDTH - 1):, :]

    xs = x_sample.reshape(DB, D)
    pos_s = past_len + jnp.arange(DS, dtype=jnp.int32)
    rope_s = jnp.tile(_rope_table(pos_s), (DB, 1))
    u_s, q_s, ckv_s, kpe_s, k_s, _ = _proj(xs, rope_s, 1, w, DB, DB)
    st_pool_t = jnp.transpose(state_pool[l], (1, 0, 2))
    mp_s = _pool_sample(u_s, st_pool_t, past_len, w)
    qabs = _mm(q_s, w['w_uk_bd'], "absorb_q").reshape(DB, N_HEADS, kv_rank).astype(bf16)
    qabs = jnp.pad(qabs, ((0, 0), (0, BF16_SUBLANES - N_HEADS), (0, 0)))
    q3 = q_s.reshape(DB, N_HEADS, LANE)
    qpe = q3[:, :, QK_NOPE_DIM:QK_HEAD_DIM]
    cp = _tile(n_pages, 32)
    ctx = _decode(page_table, q3, k_s.reshape(DB, N_HEADS, LANE), qabs, qpe, ckv_s.reshape(DB, 1, kv_rank),
                  w['w_uk_t'], cache_ckv[l], jnp.swapaxes(cache_kpe[l], 1, 2), cp, 1)
    attn_s = _mm(ctx.reshape(DB, N_HEADS * kv_rank), w['w_uv_bd'], "value_up")
    x1_s, h2_s = _outproj(xs, mp_s, attn_s, w, DB)
    st_conv_t = jnp.transpose(state_conv[l], (1, 0, 2))
    y_s, g_s = _ffn_sample(h2_s, x1_s, st_conv_t, w, tf)

    kpe_sl = slice(QK_NOPE_DIM, QK_HEAD_DIM)
    P = POOL_STATE_LEN
    return (
        y_p.reshape(B, S, D),
        y_s.reshape(DB, DS, D),
        ckv_p.reshape(1, B, S, kv_rank),
        kpe_p[:, kpe_sl].reshape(1, B, S, QK_ROPE_DIM),
        u_p.reshape(B, S, -1)[None, :, S - P:, :],
        conv_p[None],
        ckv_s.reshape(1, DB, DS, kv_rank),
        kpe_s[:, kpe_sl].reshape(1, DB, DS, QK_ROPE_DIM),
        jnp.concatenate([state_pool[l], u_s[:, None, :]], axis=1)[None, :, -P:, :],
        jnp.concatenate([state_conv[l], g_s[:, None, :]], axis=1)[None, :, -(CONV_WIDTH - 1):, :],
    )
```

```python
import functools

import jax
import jax.numpy as jnp
from jax import lax
from jax.experimental import pallas as pl
from jax.experimental.pallas import tpu as pltpu

N_HEADS = 8
QK_NOPE_DIM = 64
QK_ROPE_DIM = 32
QK_HEAD_DIM = QK_NOPE_DIM + QK_ROPE_DIM
V_HEAD_DIM = 64
POOL_WINDOWS = (2, 4, 8, 16)
POOL_STATE_LEN = max(POOL_WINDOWS) - 1
CONV_WIDTH = 3
ROPE_BASE = 10000.0
RMS_EPS = 1e-6
LOG2_E = 1.4426950408889634

LANE = 128
BF16_SUBLANES = 16
VMEM_LIMIT = 48 * 1024 * 1024
DECODE_SLOTS = 3
LOOKAHEAD = 1

_NT = (((1,), (1,)), ((), ()))

bf16 = jnp.bfloat16
f32 = jnp.float32


def _rms(x, width):
    return x * lax.rsqrt(jnp.sum(x * x, axis=-1, keepdims=True) * (1.0 / width) + RMS_EPS)


def _dot(a, b):
    return jnp.dot(a, b, preferred_element_type=f32)


def _params(*sem):
    return pltpu.CompilerParams(dimension_semantics=sem, vmem_limit_bytes=VMEM_LIMIT)


def _proj_body(x_ref, rope_ref, gmix_ref, win_ref, gqa_ref, wuq_ref, gq_ref, gkva_ref, wuk_ref, wuvt_ref,
               u_ref, q_ref, ckv_ref, kpe_ref, k_ref, vt_ref, *, pool_w, q_rank, kv_rank):
    x = x_ref[...]
    h = (_rms(x, x.shape[-1]) * gmix_ref[...]).astype(bf16)
    proj = _dot(h, win_ref[...])
    u_ref[...] = proj[:, :pool_w]
    o = pool_w
    cq = (_rms(proj[:, o:o + q_rank], q_rank) * gqa_ref[...]).astype(bf16)
    o += q_rank
    ckv = _rms(proj[:, o:o + kv_rank], kv_rank) * gkva_ref[...]
    o += kv_rank
    ckv_ref[...] = ckv
    cos = rope_ref[:, 0:LANE]
    sin_lo = rope_ref[:, LANE:2 * LANE]
    sin_hi = rope_ref[:, 2 * LANE:3 * LANE]
    sin_abs = sin_hi - sin_lo
    half = QK_ROPE_DIM // 2
    raw_kpe = proj[:, o:o + LANE]
    kpe = raw_kpe * cos + pltpu.roll(raw_kpe, LANE - half, 1) * sin_lo + pltpu.roll(raw_kpe, half, 1) * sin_hi
    kpe_ref[...] = kpe
    qraw = _dot(cq, wuq_ref[...])
    hq = N_HEADS * LANE
    ckv_b = ckv.astype(bf16)
    kn = _dot(ckv_b, wuk_ref[...])
    vt_ref[0] = lax.dot_general(wuvt_ref[...], ckv_b, _NT, preferred_element_type=f32).astype(bf16)
    for hd in range(N_HEADS):
        sl = slice(hd * LANE, (hd + 1) * LANE)
        qh = qraw[:, sl] * cos + qraw[:, hq + hd * LANE:hq + (hd + 1) * LANE] * sin_abs
        q_ref[:, sl] = (_rms(qh, QK_HEAD_DIM) * gq_ref[:, sl]).astype(bf16)
        kh = kn[:, sl] + kpe
        k_ref[:, sl] = _rms(kh, QK_HEAD_DIM).astype(bf16)


def _proj(x, rope, n_rope_blocks, w, tm, slab):
    M, D = x.shape
    pool_w = w['pool_w']
    q_rank = w['g_q_a'].shape[1]
    kv_rank = w['g_kv_a'].shape[1]
    hq = N_HEADS * LANE
    hv = N_HEADS * V_HEAD_DIM
    per_slab = slab // tm
    const = lambda a: pl.BlockSpec(a.shape, lambda i: (0, 0))
    row = lambda n: pl.BlockSpec((tm, n), lambda i: (i, 0))
    out_shapes = (
        jax.ShapeDtypeStruct((M, pool_w), f32), jax.ShapeDtypeStruct((M, hq), bf16),
        jax.ShapeDtypeStruct((M, kv_rank), f32), jax.ShapeDtypeStruct((M, LANE), f32),
        jax.ShapeDtypeStruct((M, hq), bf16), jax.ShapeDtypeStruct((M // slab, hv, slab), bf16))
    return pl.pallas_call(
        functools.partial(_proj_body, pool_w=pool_w, q_rank=q_rank, kv_rank=kv_rank),
        grid=(M // tm,),
        in_specs=[row(D), pl.BlockSpec((tm, 3 * LANE), lambda i: (i % n_rope_blocks, 0)),
                  const(w['g_mix']), const(w['w_in']), const(w['g_q_a']), const(w['w_uq']),
                  const(w['g_q']), const(w['g_kv_a']), const(w['w_uk']), const(w['w_uv_t'])],
        out_specs=(row(pool_w), row(hq), row(kv_rank), row(LANE), row(hq),
                   pl.BlockSpec((1, hv, tm), lambda i: (i // per_slab, 0, i % per_slab))),
        out_shape=out_shapes,
        compiler_params=_params("parallel"),
        name="in_proj",
    )(x, rope, w['g_mix'], w['w_in'], w['g_q_a'], w['w_uq'], w['g_q'], w['g_kv_a'], w['w_uk'], w['w_uv_t'])


def _pool_finish(diffs, wpool_ref, spool_ref, gout_ref, o_ref):
    ys = []
    for g, d in enumerate(diffs):
        sl = slice(g * LANE, (g + 1) * LANE)
        ys.append(_dot(d.astype(bf16), wpool_ref[g]) * spool_ref[:, sl])
    width = LANE * len(ys)
    ssq = sum(jnp.sum(y * y, axis=-1, keepdims=True) for y in ys)
    scale = lax.rsqrt(ssq * (1.0 / width) + RMS_EPS)
    for g, y in enumerate(ys):
        sl = slice(g * LANE, (g + 1) * LANE)
        o_ref[:, sl] = (y * scale * gout_ref[:, sl]).astype(o_ref.dtype)


def _pool_prompt_body(u_ref, wpool_ref, spool_ref, gout_ref, o_ref, buf, *, ts):
    j = pl.program_id(1)
    halo = POOL_STATE_LEN + 1

    @pl.when(j == 0)
    def _():
        buf[0:halo, :] = jnp.zeros((halo, buf.shape[1]), f32)

    @pl.when(j > 0)
    def _():
        buf[0:halo, :] = buf[ts:ts + halo, :]

    buf[halo:halo + ts, :] = u_ref[...]
    pos = j * ts + lax.broadcasted_iota(jnp.int32, (ts, 1), 0)
    diffs = []
    for g, wdw in enumerate(POOL_WINDOWS):
        sl = slice(g * LANE, (g + 1) * LANE)
        x = buf[halo:halo + ts, sl]
        win = x
        for k in range(1, wdw):
            win = win + buf[halo - k:halo - k + ts, sl]
        cnt = jnp.minimum(wdw, pos + 1).astype(f32)
        diffs.append(win / cnt - x)
    _pool_finish(diffs, wpool_ref, spool_ref, gout_ref, o_ref)


def _pool_prompt(u, B, S, w, ts):
    M, W = u.shape
    ns = S // ts
    const2 = lambda a: pl.BlockSpec(a.shape, lambda b, j: (0,) * a.ndim)
    return pl.pallas_call(
        functools.partial(_pool_prompt_body, ts=ts),
        grid=(B, ns),
        in_specs=[pl.BlockSpec((ts, W), lambda b, j: (b * ns + j, 0)),
                  const2(w['w_pool']), const2(w['s_pool']), const2(w['g_out_pool'])],
        out_specs=pl.BlockSpec((ts, W), lambda b, j: (b * ns + j, 0)),
        out_shape=jax.ShapeDtypeStruct((M, W), bf16),
        scratch_shapes=[pltpu.VMEM((ts + POOL_STATE_LEN + 1, W), f32)],
        compiler_params=_params("arbitrary", "arbitrary"),
        name="pool_prompt",
    )(u, w['w_pool'], w['s_pool'], w['g_out_pool'])


def _pool_sample_body(u_ref, st_ref, wpool_ref, spool_ref, gout_ref, o_ref, *, pos):
    diffs = []
    for g, wdw in enumerate(POOL_WINDOWS):
        sl = slice(g * LANE, (g + 1) * LANE)
        x = u_ref[:, sl]
        win = x
        for k in range(1, wdw):
            win = win + st_ref[POOL_STATE_LEN - k, :, sl]
        diffs.append(win / float(min(wdw, pos + 1)) - x)
    _pool_finish(diffs, wpool_ref, spool_ref, gout_ref, o_ref)


def _pool_sample(u, st_t, pos, w):
    M, W = u.shape
    return pl.pallas_call(
        functools.partial(_pool_sample_body, pos=pos),
        out_shape=jax.ShapeDtypeStruct((M, W), bf16),
        compiler_params=pltpu.CompilerParams(vmem_limit_bytes=VMEM_LIMIT),
        name="pool_sample",
    )(u, st_t, w['w_pool'], w['s_pool'], w['g_out_pool'])


def _mix_out(x, mp, attn, gatt_ref, wo_ref, gffn_ref, x1_ref, h2_ref):
    na = (_rms(attn, attn.shape[-1]) * gatt_ref[...]).astype(bf16)
    x1 = x + _dot(jnp.concatenate([mp, na], axis=1), wo_ref[...])
    x1_ref[...] = x1
    h2_ref[...] = (_rms(x1, x1.shape[-1]) * gffn_ref[...]).astype(bf16)


def _flash_body(q_ref, k_ref, vt_ref, x_ref, mp_ref, gatt_ref, wo_ref, gffn_ref, x1_ref, h2_ref,
                m_scr, acc_scr, *, tq):
    i = pl.program_id(1)
    heads = m_scr.shape[0]
    key_idx = lax.broadcasted_iota(jnp.int32, (tq, tq), 0)
    qry_idx = lax.broadcasted_iota(jnp.int32, (tq, tq), 1)
    m_scr[...] = jnp.full(m_scr.shape, -1e30, f32)
    acc_scr[...] = jnp.zeros(acc_scr.shape, f32)
    ones_rows = jnp.ones((BF16_SUBLANES, tq), bf16)

    def step(j, masked):
        start = pl.multiple_of(j * tq, tq)

        def scores(hh):
            sl = slice(hh * LANE, (hh + 1) * LANE)
            return lax.dot_general(k_ref[pl.ds(start, tq), sl], q_ref[:, sl], _NT, preferred_element_type=f32)

        ready = [scores(hh) for hh in range(min(LOOKAHEAD, heads))]
        for hh in range(heads):
            st = ready.pop(0)
            if hh + LOOKAHEAD < heads:
                ready.append(scores(hh + LOOKAHEAD))
            if masked:
                st = jnp.where(key_idx <= qry_idx, st, -1e30)
            m = m_scr[hh]
            m_new = jnp.maximum(m, jnp.max(st, axis=0, keepdims=True))
            alpha = jnp.exp2(m - m_new)
            pt = jnp.exp2((st - m_new).astype(bf16))
            vt = jnp.concatenate([vt_ref[j, hh * V_HEAD_DIM:(hh + 1) * V_HEAD_DIM, :], ones_rows], axis=0)
            acc_scr[hh] = acc_scr[hh] * alpha + _dot(vt, pt)
            m_scr[hh] = m_new

    def body(j, carry):
        step(j, False)
        return carry

    lax.fori_loop(0, i, body, 0)
    step(i, True)
    dv = V_HEAD_DIM
    out_t = jnp.concatenate([acc_scr[hh, 0:dv] / acc_scr[hh, dv:dv + 1] for hh in range(heads)], axis=0)
    _mix_out(x_ref[...], mp_ref[...], out_t.T, gatt_ref, wo_ref, gffn_ref, x1_ref, h2_ref)


def _flash(q, k, vt, x, mp, w, B, S, tq):
    M, D = x.shape
    nq = S // tq
    const = lambda a: pl.BlockSpec(a.shape, lambda b, i: (0, 0))
    tile = lambda n: pl.BlockSpec((tq, n), lambda b, i: (b * nq + i, 0))
    return pl.pallas_call(
        functools.partial(_flash_body, tq=tq),
        grid=(B, nq),
        in_specs=[tile(N_HEADS * LANE),
                  pl.BlockSpec((S, N_HEADS * LANE), lambda b, i: (b, 0)),
                  pl.BlockSpec((nq, N_HEADS * V_HEAD_DIM, tq), lambda b, i: (b, 0, 0)),
                  tile(D), tile(mp.shape[1]),
                  const(w['g_out_attn']), const(w['w_o']), const(w['g_ffn'])],
        out_specs=(tile(D), tile(D)),
        out_shape=(jax.ShapeDtypeStruct((M, D), f32), jax.ShapeDtypeStruct((M, D), bf16)),
        scratch_shapes=[pltpu.VMEM((N_HEADS, 1, tq), f32),
                        pltpu.VMEM((N_HEADS, V_HEAD_DIM + BF16_SUBLANES, tq), f32)],
        compiler_params=_params("parallel", "arbitrary"),
        name="flash_prompt",
    )(q, k, vt, x, mp, w['g_out_attn'], w['w_o'], w['g_ffn'])


def _mm_body(a_ref, b_ref, o_ref):
    o_ref[...] = _dot(a_ref[...].astype(bf16), b_ref[...])


def _mm(a, b, name):
    return pl.pallas_call(
        _mm_body,
        out_shape=jax.ShapeDtypeStruct((a.shape[0], b.shape[1]), f32),
        compiler_params=pltpu.CompilerParams(vmem_limit_bytes=VMEM_LIMIT),
        name=name,
    )(a, b)


def _decode_body(pt_ref, q_ref, knew_ref, qabs_ref, qpe_ref, cnew_ref, wukt_ref, cache_ckv, cache_kpet,
                 o_ref, ckv_buf, kpe_buf, sems, lhs, m_scr, l_scr, acc_scr, *, n_pages, cp):
    b = pl.program_id(0)
    c = pl.program_id(1)
    nc = pl.num_programs(1)
    total = pl.num_programs(0) * nc
    flat = b * nc + c
    n_slots = ckv_buf.shape[0]
    ahead = n_slots - 1
    slot = flat % n_slots
    nope_rows = wukt_ref.shape[0]
    page = cache_ckv.shape[1]

    def copies(flat_idx, slot_idx):
        base = (flat_idx // nc) * n_pages + (flat_idx % nc) * cp
        out = []
        for p in range(cp):
            phys = pt_ref[base + p]
            rows = pl.ds(p * page, page)
            out.append(pltpu.make_async_copy(cache_ckv.at[phys], ckv_buf.at[slot_idx, rows], sems.at[0, slot_idx]))
            out.append(pltpu.make_async_copy(cache_kpet.at[phys], kpe_buf.at[slot_idx, :, rows],
                                             sems.at[1, slot_idx]))
        return out

    @pl.when(flat == 0)
    def _():
        for k in range(ahead):
            for cpy in copies(jnp.minimum(k, total - 1), k):
                cpy.start()

    @pl.when(c == 0)
    def _():
        lhs[0:nope_rows, :] = wukt_ref[...]
        lhs[nope_rows:nope_rows + BF16_SUBLANES, :] = qabs_ref[0]
        s_new = jnp.sum(q_ref[0].astype(f32) * knew_ref[0].astype(f32), axis=-1, keepdims=True)
        m_scr[...] = s_new
        l_scr[...] = jnp.ones_like(l_scr)
        acc_scr[...] = jnp.broadcast_to(cnew_ref[0], acc_scr.shape)

    for cpy in copies(flat, slot):
        cpy.wait()

    qpe = qpe_ref[0]
    ones = jnp.ones(qpe.shape, bf16)
    fold = (lax.broadcasted_iota(jnp.int32, (N_HEADS, 8 * N_HEADS), 1) // 8
            == lax.broadcasted_iota(jnp.int32, (N_HEADS, 8 * N_HEADS), 0)).astype(bf16)
    tokens = cp * page
    ct = ckv_buf[slot].astype(bf16)
    for cpy in copies(jnp.minimum(flat + ahead, total - 1), (flat + ahead) % n_slots):
        cpy.start()
    kt = lax.dot_general(lhs[...], ct, _NT, preferred_element_type=f32)
    kn = kt[0:nope_rows].reshape(N_HEADS, nope_rows // (8 * N_HEADS), 8, tokens)
    partial = jnp.sum(kn * kn, axis=1).reshape(8 * N_HEADS, tokens).astype(bf16)
    kp = kpe_buf[slot]
    num = kt[nope_rows:nope_rows + N_HEADS] + _dot(qpe, kp.astype(bf16))
    ssq = _dot(fold, partial) + _dot(ones, (kp * kp).astype(bf16))
    s = num * lax.rsqrt(ssq * (1.0 / QK_HEAD_DIM) + RMS_EPS)
    m = m_scr[...]
    m_new = jnp.maximum(m, jnp.max(s, axis=-1, keepdims=True))
    alpha = jnp.exp2(m - m_new)
    p = jnp.exp2(s - m_new)
    l_scr[...] = l_scr[...] * alpha + jnp.sum(p, axis=-1, keepdims=True)
    acc_scr[...] = acc_scr[...] * alpha + _dot(p.astype(bf16), ct)
    m_scr[...] = m_new

    @pl.when(c == nc - 1)
    def _():
        o_ref[0] = acc_scr[...] / l_scr[...]

    @pl.when(flat == total - 1)
    def _():
        for k in range(1, ahead + 1):
            for cpy in copies(total - 1, (flat + k) % n_slots):
                cpy.wait()


def _decode(page_table, q3, knew3, qabs, qpe, cnew, wukt, cache_ckv, cache_kpet, cp):
    DB, n_pages = page_table.shape
    page, kv_rank = cache_ckv.shape[1:]
    rope = cache_kpet.shape[1]
    nc = n_pages // cp
    T = cp * page
    per_seq = lambda a: pl.BlockSpec((1,) + a.shape[1:], lambda b, c, pt: (b,) + (0,) * (a.ndim - 1))
    grid_spec = pltpu.PrefetchScalarGridSpec(
        num_scalar_prefetch=1,
        grid=(DB, nc),
        in_specs=[per_seq(q3), per_seq(knew3), per_seq(qabs), per_seq(qpe), per_seq(cnew),
                  pl.BlockSpec(wukt.shape, lambda b, c, pt: (0, 0)),
                  pl.BlockSpec(memory_space=pl.ANY), pl.BlockSpec(memory_space=pl.ANY)],
        out_specs=pl.BlockSpec((1, N_HEADS, kv_rank), lambda b, c, pt: (b, 0, 0)),
        scratch_shapes=[pltpu.VMEM((DECODE_SLOTS, T, kv_rank), f32), pltpu.VMEM((DECODE_SLOTS, rope, T), f32),
                        pltpu.SemaphoreType.DMA((2, DECODE_SLOTS)),
                        pltpu.VMEM((wukt.shape[0] + BF16_SUBLANES, kv_rank), bf16),
                        pltpu.VMEM((N_HEADS, 1), f32), pltpu.VMEM((N_HEADS, 1), f32),
                        pltpu.VMEM((N_HEADS, kv_rank), f32)])
    return pl.pallas_call(
        functools.partial(_decode_body, n_pages=n_pages, cp=cp),
        grid_spec=grid_spec,
        out_shape=jax.ShapeDtypeStruct((DB, N_HEADS, kv_rank), f32),
        compiler_params=_params("arbitrary", "arbitrary"),
        name="decode_attn",
    )(page_table.reshape(-1), q3, knew3, qabs, qpe, cnew, wukt, cache_ckv, cache_kpet)


def _outproj_body(x_ref, mp_ref, attn_ref, gatt_ref, wo_ref, gffn_ref, x1_ref, h2_ref):
    _mix_out(x_ref[...], mp_ref[...], attn_ref[...], gatt_ref, wo_ref, gffn_ref, x1_ref, h2_ref)


def _outproj(x, mp, attn, w, tm):
    M, D = x.shape
    const = lambda a: pl.BlockSpec(a.shape, lambda i: (0, 0))
    row = lambda n: pl.BlockSpec((tm, n), lambda i: (i, 0))
    return pl.pallas_call(
        _outproj_body,
        grid=(M // tm,),
        in_specs=[row(D), row(mp.shape[1]), row(attn.shape[1]),
                  const(w['g_out_attn']), const(w['w_o']), const(w['g_ffn'])],
        out_specs=(row(D), row(D)),
        out_shape=(jax.ShapeDtypeStruct((M, D), f32), jax.ShapeDtypeStruct((M, D), bf16)),
        compiler_params=_params("parallel"),
        name="out_proj",
    )(x, mp, attn, w['g_out_attn'], w['w_o'], w['g_ffn'])


def _ffn_finish(f, g, g1, g2, h2, x1_ref, wu_ref, wd_ref, cw_ref, cb_ref, y_ref, acc):
    gate = cb_ref[...] + g2 * cw_ref[0:1, :] + g1 * cw_ref[1:2, :] + g * cw_ref[2:3, :]
    act = (gate * jax.nn.sigmoid(gate)) * _dot(h2, wu_ref[...])
    part = _dot(act.astype(bf16), wd_ref[...])

    @pl.when(f == 0)
    def _():
        acc[...] = x1_ref[...] + part

    @pl.when(f > 0)
    def _():
        acc[...] += part

    @pl.when(f == pl.num_programs(1) - 1)
    def _():
        y_ref[...] = acc[...]


def _ffn_prompt_body(h2_ref, halo_ref, x1_ref, wg_ref, wu_ref, wd_ref, cw_ref, cb_ref,
                     y_ref, tail_ref, acc, gbuf, *, tm, tiles_per_seq):
    i = pl.program_id(0)
    f = pl.program_id(1)
    hb = BF16_SUBLANES
    h2 = h2_ref[...]
    rows = lax.broadcasted_iota(jnp.int32, (tm + hb, 1), 0)
    keep = jnp.logical_or(rows >= hb, i % tiles_per_seq != 0)
    gbuf[...] = jnp.where(keep, _dot(jnp.concatenate([halo_ref[...], h2], axis=0), wg_ref[...]), 0.0)
    g = gbuf[hb:hb + tm, :]
    tail_ref[0] = gbuf[tm:tm + hb, :]
    g1 = gbuf[hb - 1:hb - 1 + tm, :]
    g2 = gbuf[hb - 2:hb - 2 + tm, :]
    _ffn_finish(f, g, g1, g2, h2, x1_ref, wu_ref, wd_ref, cw_ref, cb_ref, y_ref, acc)


def _ffn_prompt(h2, x1, S, w, tm, tf):
    M, D = x1.shape
    F = w['w_gate'].shape[1]
    hb = BF16_SUBLANES
    nt = M // tm
    halo_blocks = tm // hb
    resident = dict(pipeline_mode=pl.Buffered(1)) if tf == F else {}
    y, tail = pl.pallas_call(
        functools.partial(_ffn_prompt_body, tm=tm, tiles_per_seq=S // tm),
        grid=(nt, F // tf),
        in_specs=[pl.BlockSpec((tm, D), lambda i, f: (i, 0)),
                  pl.BlockSpec((hb, D), lambda i, f: (jnp.maximum(i * halo_blocks - 1, 0), 0)),
                  pl.BlockSpec((tm, D), lambda i, f: (i, 0)),
                  pl.BlockSpec((D, tf), lambda i, f: (0, f), **resident),
                  pl.BlockSpec((D, tf), lambda i, f: (0, f), **resident),
                  pl.BlockSpec((tf, D), lambda i, f: (f, 0), **resident),
                  pl.BlockSpec((CONV_WIDTH, tf), lambda i, f: (0, f)),
                  pl.BlockSpec((1, tf), lambda i, f: (0, f))],
        out_specs=(pl.BlockSpec((tm, D), lambda i, f: (i, 0)),
                   pl.BlockSpec((1, hb, tf), lambda i, f: (i, 0, f))),
        out_shape=(jax.ShapeDtypeStruct((M, D), f32), jax.ShapeDtypeStruct((nt, hb, F), f32)),
        scratch_shapes=[pltpu.VMEM((tm, D), f32), pltpu.VMEM((tm + hb, tf), f32)],
        compiler_params=_params("parallel", "arbitrary"),
        name="ffn_prompt",
    )(h2, h2, x1, w['w_gate'], w['w_up'], w['w_down'], w['conv_w'], w['conv_b'])
    return y, tail


def _ffn_sample_body(h2_ref, st_ref, x1_ref, wg_ref, wu_ref, wd_ref, cw_ref, cb_ref, y_ref, g_ref, acc):
    f = pl.program_id(1)
    h2 = h2_ref[...]
    g = _dot(h2, wg_ref[...])
    g_ref[...] = g
    _ffn_finish(f, g, st_ref[1], st_ref[0], h2, x1_ref, wu_ref, wd_ref, cw_ref, cb_ref, y_ref, acc)


def _ffn_sample(h2, x1, st_t, w, tf):
    M, D = x1.shape
    F = w['w_gate'].shape[1]
    return pl.pallas_call(
        _ffn_sample_body,
        grid=(1, F // tf),
        in_specs=[pl.BlockSpec((M, D), lambda i, f: (0, 0)),
                  pl.BlockSpec((CONV_WIDTH - 1, M, tf), lambda i, f: (0, 0, f)),
                  pl.BlockSpec((M, D), lambda i, f: (0, 0)),
                  pl.BlockSpec((D, tf), lambda i, f: (0, f)),
                  pl.BlockSpec((D, tf), lambda i, f: (0, f)),
                  pl.BlockSpec((tf, D), lambda i, f: (f, 0)),
                  pl.BlockSpec((CONV_WIDTH, tf), lambda i, f: (0, f)),
                  pl.BlockSpec((1, tf), lambda i, f: (0, f))],
        out_specs=(pl.BlockSpec((M, D), lambda i, f: (0, 0)),
                   pl.BlockSpec((M, tf), lambda i, f: (0, f))),
        out_shape=(jax.ShapeDtypeStruct((M, D), f32), jax.ShapeDtypeStruct((M, F), f32)),
        scratch_shapes=[pltpu.VMEM((M, D), f32)],
        compiler_params=_params("parallel", "arbitrary"),
        name="ffn_sample",
    )(h2, st_t, x1, w['w_gate'], w['w_up'], w['w_down'], w['conv_w'], w['conv_b'])


def _qk_gain(g):
    g_pe = g[QK_NOPE_DIM:]
    return jnp.concatenate([g[:QK_NOPE_DIM], g_pe, g_pe])


def _rope_table(pos):
    half = QK_ROPE_DIM // 2
    inv = ROPE_BASE ** (-jnp.arange(0, QK_ROPE_DIM, 2, dtype=f32) / QK_ROPE_DIM)
    ang = pos.astype(f32)[:, None] * inv[None, :]
    cos, sin = jnp.cos(ang), jnp.sin(ang)
    P = pos.shape[0]
    z = lambda n: jnp.zeros((P, n), f32)
    tail = LANE - QK_HEAD_DIM
    c = jnp.concatenate([jnp.ones((P, QK_NOPE_DIM), f32), cos, cos, z(tail)], axis=1)
    s_lo = jnp.concatenate([z(QK_NOPE_DIM), -sin, z(half), z(tail)], axis=1)
    s_hi = jnp.concatenate([z(QK_NOPE_DIM), z(half), sin, z(tail)], axis=1)
    return jnp.concatenate([c, s_lo, s_hi], axis=1)


def _prep_weights(g_mix, w_in, g_q_a, w_uq, g_qn, g_kv_a, w_uk, g_kn, w_uv, w_pool, s_pool,
                  g_out, w_o, g_ffn, w_gate, w_up, conv_w, conv_b, w_down):
    D = w_in.shape[0]
    q_rank = g_q_a.shape[0]
    kv_rank = g_kv_a.shape[0]
    pool_w = s_pool.shape[0]
    head_pad = LANE - QK_HEAD_DIM
    half = QK_ROPE_DIM // 2
    main = pool_w + q_rank + kv_rank
    w_in_p = jnp.concatenate([w_in[:, :main], jnp.zeros((D, QK_NOPE_DIM), f32), w_in[:, main:],
                              jnp.zeros((D, head_pad), f32)], axis=1).astype(bf16)
    w_uq_h = w_uq.reshape(q_rank, N_HEADS, QK_HEAD_DIM)
    w_uq_p = jnp.pad(w_uq_h, ((0, 0), (0, 0), (0, head_pad)))
    lo = w_uq_h[:, :, QK_NOPE_DIM:QK_NOPE_DIM + half]
    hi = w_uq_h[:, :, QK_NOPE_DIM + half:]
    w_uq_rot = jnp.concatenate([jnp.zeros((q_rank, N_HEADS, QK_NOPE_DIM), f32), -hi, lo,
                                jnp.zeros((q_rank, N_HEADS, head_pad), f32)], axis=2)
    w_uq_p = jnp.concatenate([w_uq_p.reshape(q_rank, N_HEADS * LANE),
                              w_uq_rot.reshape(q_rank, N_HEADS * LANE)], axis=1).astype(bf16)
    gq = _qk_gain(g_qn) * _qk_gain(g_kn) * (QK_HEAD_DIM ** -0.5 * LOG2_E)
    gq = jnp.tile(jnp.pad(gq, (0, head_pad)), N_HEADS)[None, :]
    w_uk_p = jnp.pad(w_uk, ((0, 0), (0, 0), (0, LANE - QK_NOPE_DIM))).reshape(kv_rank, N_HEADS * LANE)
    w_uk_p = w_uk_p.astype(bf16)
    w_uv_t = w_uv.reshape(kv_rank, N_HEADS * V_HEAD_DIM).T.astype(bf16)
    w_uk_t = w_uk.reshape(kv_rank, N_HEADS * QK_NOPE_DIM).T.astype(bf16)
    eye = jnp.eye(N_HEADS, dtype=f32)
    w_uk_bd = jnp.einsum('chd,hg->hdgc', jnp.pad(w_uk, ((0, 0), (0, 0), (0, LANE - QK_NOPE_DIM))), eye)
    w_uk_bd = w_uk_bd.reshape(N_HEADS * LANE, N_HEADS * kv_rank).astype(bf16)
    w_uv_bd = jnp.einsum('chv,hg->hcgv', w_uv, eye).reshape(N_HEADS * kv_rank, N_HEADS * V_HEAD_DIM).astype(bf16)
    return {
        'pool_w': pool_w,
        'g_mix': g_mix[None, :], 'w_in': w_in_p, 'g_q_a': g_q_a[None, :], 'w_uq': w_uq_p, 'g_q': gq,
        'g_kv_a': g_kv_a[None, :], 'w_uk': w_uk_p, 'w_uv_t': w_uv_t, 'w_uk_t': w_uk_t, 'w_uk_bd': w_uk_bd, 'w_uv_bd': w_uv_bd,
        'w_pool': w_pool.astype(bf16), 's_pool': s_pool[None, :],
        'g_out_pool': g_out[None, :pool_w], 'g_out_attn': g_out[None, pool_w:],
        'w_o': w_o.astype(bf16), 'g_ffn': g_ffn[None, :],
        'w_gate': w_gate.astype(bf16), 'w_up': w_up.astype(bf16), 'w_down': w_down.astype(bf16),
        'conv_w': conv_w, 'conv_b': conv_b[None, :],
    }


def _tile(n, pref):
    t = min(n, pref)
    assert n % t == 0, (n, pref)
    return t


def _ffn_cols(F):
    half = F // 2
    return half if half % LANE == 0 else F


def kernel(x_prompt, x_sample, cache_ckv, cache_kpe, state_pool, state_conv, page_table, g_mix, w_in, g_q_a,
           w_uq, g_qn, g_kv_a, w_uk, g_kn, w_uv, w_pool, s_pool, g_out, w_o, g_ffn, w_gate, w_up, conv_w,
           conv_b, w_down):
    depth = g_mix.shape[0]
    assert depth == 1, "single-layer trunk only"
    B, S, D = x_prompt.shape
    DB, DS = x_sample.shape[:2]
    assert DS == 1, "one new token per sequence"
    n_pages = page_table.shape[1]
    page = cache_ckv.shape[2]
    past_len = n_pages * page
    l = 0
    w = _prep_weights(g_mix[l], w_in[l], g_q_a[l], w_uq[l], g_qn[l], g_kv_a[l], w_uk[l], g_kn[l], w_uv[l],
                      w_pool[l], s_pool[l], g_out[l], w_o[l], g_ffn[l], w_gate[l], w_up[l], conv_w[l],
                      conv_b[l], w_down[l])
    kv_rank = g_kv_a.shape[1]
    F = w_gate.shape[2]
    tf = _ffn_cols(F)

    xp = x_prompt.reshape(B * S, D)
    tq = _tile(S, 512)
    tm = _tile(tq, 256)
    rope_p = _rope_table(jnp.arange(S, dtype=jnp.int32))
    u_p, q_p, ckv_p, kpe_p, k_p, vt_p = _proj(xp, rope_p, S // tm, w, tm, tq)
    mp_p = _pool_prompt(u_p, B, S, w, _tile(S, 512))
    x1_p, h2_p = _flash(q_p, k_p, vt_p, xp, mp_p, w, B, S, tq)
    tmf = _tile(S, 256)
    y_p, tail = _ffn_prompt(h2_p, x1_p, S, w, tmf, F)
    tiles_per_seq = S // tmf
    conv_p = tail[tiles_per_seq - 1::tiles_per_seq, BF16_SUBLANES - (CONV_WIDTH - 1):, :]

    xs = x_sample.reshape(DB, D)
    pos_s = past_len + jnp.arange(DS, dtype=jnp.int32)
    rope_s = jnp.tile(_rope_table(pos_s), (DB, 1))
    u_s, q_s, ckv_s, kpe_s, k_s, _ = _proj(xs, rope_s, 1, w, DB, DB)
    st_pool_t = jnp.transpose(state_pool[l], (1, 0, 2))
    mp_s = _pool_sample(u_s, st_pool_t, past_len, w)
    qabs = _mm(q_s, w['w_uk_bd'], "absorb_q").reshape(DB, N_HEADS, kv_rank).astype(bf16)
    qabs = jnp.pad(qabs, ((0, 0), (0, BF16_SUBLANES - N_HEADS), (0, 0)))
    q3 = q_s.reshape(DB, N_HEADS, LANE)
    qpe = q3[:, :, QK_NOPE_DIM:QK_HEAD_DIM]
    cp = _tile(n_pages, 32)
    ctx = _decode(page_table, q3, k_s.reshape(DB, N_HEADS, LANE), qabs, qpe, ckv_s.reshape(DB, 1, kv_rank),
                  w['w_uk_t'], cache_ckv[l], jnp.swapaxes(cache_kpe[l], 1, 2), cp)
    attn_s = _mm(ctx.reshape(DB, N_HEADS * kv_rank), w['w_uv_bd'], "value_up")
    x1_s, h2_s = _outproj(xs, mp_s, attn_s, w, DB)
    st_conv_t = jnp.transpose(state_conv[l], (1, 0, 2))
    y_s, g_s = _ffn_sample(h2_s, x1_s, st_conv_t, w, tf)

    kpe_sl = slice(QK_NOPE_DIM, QK_HEAD_DIM)
    P = POOL_STATE_LEN
    return (
        y_p.reshape(B, S, D),
        y_s.reshape(DB, DS, D),
        ckv_p.reshape(1, B, S, kv_rank),
        kpe_p[:, kpe_sl].reshape(1, B, S, QK_ROPE_DIM),
        u_p.reshape(B, S, -1)[None, :, S - P:, :],
        conv_p[None],
        ckv_s.reshape(1, DB, DS, kv_rank),
        kpe_s[:, kpe_sl].reshape(1, DB, DS, QK_ROPE_DIM),
        jnp.concatenate([state_pool[l], u_s[:, None, :]], axis=1)[None, :, -P:, :],
        jnp.concatenate([state_conv[l], g_s[:, None, :]], axis=1)[None, :, -(CONV_WIDTH - 1):, :],
    )
```

```python
import functools

import jax
import jax.numpy as jnp
from jax import lax
from jax.experimental import pallas as pl
from jax.experimental.pallas import tpu as pltpu

N_HEADS = 8
QK_NOPE_DIM = 64
QK_ROPE_DIM = 32
QK_HEAD_DIM = QK_NOPE_DIM + QK_ROPE_DIM
V_HEAD_DIM = 64
POOL_WINDOWS = (2, 4, 8, 16)
POOL_STATE_LEN = max(POOL_WINDOWS) - 1
CONV_WIDTH = 3
ROPE_BASE = 10000.0
RMS_EPS = 1e-6
LOG2_E = 1.4426950408889634

LANE = 128
BF16_SUBLANES = 16
VMEM_LIMIT = 48 * 1024 * 1024
DECODE_SLOTS = 3
LOOKAHEAD = 1

_NT = (((1,), (1,)), ((), ()))

bf16 = jnp.bfloat16
f32 = jnp.float32


def _rms(x, width):
    return x * lax.rsqrt(jnp.sum(x * x, axis=-1, keepdims=True) * (1.0 / width) + RMS_EPS)


def _dot(a, b):
    return jnp.dot(a, b, preferred_element_type=f32)


def _params(*sem):
    return pltpu.CompilerParams(dimension_semantics=sem, vmem_limit_bytes=VMEM_LIMIT)


def _proj_body(x_ref, rope_ref, gmix_ref, win_ref, gqa_ref, wuq_ref, gq_ref, gkva_ref, wuk_ref, wuvt_ref,
               u_ref, q_ref, ckv_ref, kpe_ref, k_ref, vt_ref, *, pool_w, q_rank, kv_rank):
    tm = x_ref.shape[0]
    group = min(tm, 2 * LANE)
    groups = [slice(r, r + group) for r in range(0, tm, group)]
    half = QK_ROPE_DIM // 2
    hq = N_HEADS * LANE

    def in_projection(rows):
        x = x_ref[rows, :]
        h = (_rms(x, x.shape[-1]) * gmix_ref[...]).astype(bf16)
        return _dot(h, win_ref[...])

    def up_projections(rows, proj):
        u_ref[rows, :] = proj[:, :pool_w]
        o = pool_w
        cq = (_rms(proj[:, o:o + q_rank], q_rank) * gqa_ref[...]).astype(bf16)
        o += q_rank
        ckv = _rms(proj[:, o:o + kv_rank], kv_rank) * gkva_ref[...]
        o += kv_rank
        ckv_ref[rows, :] = ckv
        ckv_b = ckv.astype(bf16)
        qraw = _dot(cq, wuq_ref[...])
        kn = _dot(ckv_b, wuk_ref[...])
        vt_ref[0, :, rows] = lax.dot_general(wuvt_ref[...], ckv_b, _NT, preferred_element_type=f32).astype(bf16)
        return proj[:, o:o + LANE], qraw, kn

    def head_norms(rows, raw_kpe, qraw, kn):
        cos = rope_ref[rows, 0:LANE]
        sin_lo = rope_ref[rows, LANE:2 * LANE]
        sin_hi = rope_ref[rows, 2 * LANE:3 * LANE]
        sin_abs = sin_hi - sin_lo
        kpe = raw_kpe * cos + pltpu.roll(raw_kpe, LANE - half, 1) * sin_lo + pltpu.roll(raw_kpe, half, 1) * sin_hi
        kpe_ref[rows, :] = kpe
        for hd in range(N_HEADS):
            sl = slice(hd * LANE, (hd + 1) * LANE)
            qh = qraw[:, sl] * cos + qraw[:, hq + hd * LANE:hq + (hd + 1) * LANE] * sin_abs
            q_ref[rows, sl] = (_rms(qh, QK_HEAD_DIM) * gq_ref[:, sl]).astype(bf16)
            kh = kn[:, sl] + kpe
            k_ref[rows, sl] = _rms(kh, QK_HEAD_DIM).astype(bf16)

    projs = [in_projection(rows) for rows in groups]
    ups = [up_projections(rows, proj) for rows, proj in zip(groups, projs)]
    for rows, up in zip(groups, ups):
        head_norms(rows, *up)


def _proj(x, rope, n_rope_blocks, w, tm, slab):
    M, D = x.shape
    pool_w = w['pool_w']
    q_rank = w['g_q_a'].shape[1]
    kv_rank = w['g_kv_a'].shape[1]
    hq = N_HEADS * LANE
    hv = N_HEADS * V_HEAD_DIM
    per_slab = slab // tm
    const = lambda a: pl.BlockSpec(a.shape, lambda i: (0, 0))
    row = lambda n: pl.BlockSpec((tm, n), lambda i: (i, 0))
    out_shapes = (
        jax.ShapeDtypeStruct((M, pool_w), f32), jax.ShapeDtypeStruct((M, hq), bf16),
        jax.ShapeDtypeStruct((M, kv_rank), f32), jax.ShapeDtypeStruct((M, LANE), f32),
        jax.ShapeDtypeStruct((M, hq), bf16), jax.ShapeDtypeStruct((M // slab, hv, slab), bf16))
    return pl.pallas_call(
        functools.partial(_proj_body, pool_w=pool_w, q_rank=q_rank, kv_rank=kv_rank),
        grid=(M // tm,),
        in_specs=[row(D), pl.BlockSpec((tm, 3 * LANE), lambda i: (i % n_rope_blocks, 0)),
                  const(w['g_mix']), const(w['w_in']), const(w['g_q_a']), const(w['w_uq']),
                  const(w['g_q']), const(w['g_kv_a']), const(w['w_uk']), const(w['w_uv_t'])],
        out_specs=(row(pool_w), row(hq), row(kv_rank), row(LANE), row(hq),
                   pl.BlockSpec((1, hv, tm), lambda i: (i // per_slab, 0, i % per_slab))),
        out_shape=out_shapes,
        compiler_params=_params("parallel"),
        name="in_proj",
    )(x, rope, w['g_mix'], w['w_in'], w['g_q_a'], w['w_uq'], w['g_q'], w['g_kv_a'], w['w_uk'], w['w_uv_t'])


def _pool_finish(diffs, wpool_ref, spool_ref, gout_ref, o_ref):
    ys = []
    for g, d in enumerate(diffs):
        sl = slice(g * LANE, (g + 1) * LANE)
        ys.append(_dot(d.astype(bf16), wpool_ref[g]) * spool_ref[:, sl])
    width = LANE * len(ys)
    ssq = sum(jnp.sum(y * y, axis=-1, keepdims=True) for y in ys)
    scale = lax.rsqrt(ssq * (1.0 / width) + RMS_EPS)
    for g, y in enumerate(ys):
        sl = slice(g * LANE, (g + 1) * LANE)
        o_ref[:, sl] = (y * scale * gout_ref[:, sl]).astype(o_ref.dtype)


def _pool_prompt_body(u_ref, wpool_ref, spool_ref, gout_ref, o_ref, buf, *, ts):
    j = pl.program_id(1)
    halo = POOL_STATE_LEN + 1

    @pl.when(j == 0)
    def _():
        buf[0:halo, :] = jnp.zeros((halo, buf.shape[1]), f32)

    @pl.when(j > 0)
    def _():
        buf[0:halo, :] = buf[ts:ts + halo, :]

    buf[halo:halo + ts, :] = u_ref[...]
    pos = j * ts + lax.broadcasted_iota(jnp.int32, (ts, 1), 0)
    diffs = []
    for g, wdw in enumerate(POOL_WINDOWS):
        sl = slice(g * LANE, (g + 1) * LANE)
        x = buf[halo:halo + ts, sl]
        win = x
        for k in range(1, wdw):
            win = win + buf[halo - k:halo - k + ts, sl]
        cnt = jnp.minimum(wdw, pos + 1).astype(f32)
        diffs.append(win / cnt - x)
    _pool_finish(diffs, wpool_ref, spool_ref, gout_ref, o_ref)


def _pool_prompt(u, B, S, w, ts):
    M, W = u.shape
    ns = S // ts
    const2 = lambda a: pl.BlockSpec(a.shape, lambda b, j: (0,) * a.ndim)
    return pl.pallas_call(
        functools.partial(_pool_prompt_body, ts=ts),
        grid=(B, ns),
        in_specs=[pl.BlockSpec((ts, W), lambda b, j: (b * ns + j, 0)),
                  const2(w['w_pool']), const2(w['s_pool']), const2(w['g_out_pool'])],
        out_specs=pl.BlockSpec((ts, W), lambda b, j: (b * ns + j, 0)),
        out_shape=jax.ShapeDtypeStruct((M, W), bf16),
        scratch_shapes=[pltpu.VMEM((ts + POOL_STATE_LEN + 1, W), f32)],
        compiler_params=_params("arbitrary", "arbitrary"),
        name="pool_prompt",
    )(u, w['w_pool'], w['s_pool'], w['g_out_pool'])


def _pool_sample_body(u_ref, st_ref, wpool_ref, spool_ref, gout_ref, o_ref, *, pos):
    diffs = []
    for g, wdw in enumerate(POOL_WINDOWS):
        sl = slice(g * LANE, (g + 1) * LANE)
        x = u_ref[:, sl]
        win = x
        for k in range(1, wdw):
            win = win + st_ref[POOL_STATE_LEN - k, :, sl]
        diffs.append(win / float(min(wdw, pos + 1)) - x)
    _pool_finish(diffs, wpool_ref, spool_ref, gout_ref, o_ref)


def _pool_sample(u, st_t, pos, w):
    M, W = u.shape
    return pl.pallas_call(
        functools.partial(_pool_sample_body, pos=pos),
        out_shape=jax.ShapeDtypeStruct((M, W), bf16),
        compiler_params=pltpu.CompilerParams(vmem_limit_bytes=VMEM_LIMIT),
        name="pool_sample",
    )(u, st_t, w['w_pool'], w['s_pool'], w['g_out_pool'])


def _mix_out(x, mp, attn, gatt_ref, wo_ref, gffn_ref, x1_ref, h2_ref):
    na = (_rms(attn, attn.shape[-1]) * gatt_ref[...]).astype(bf16)
    x1 = x + _dot(jnp.concatenate([mp, na], axis=1), wo_ref[...])
    x1_ref[...] = x1
    h2_ref[...] = (_rms(x1, x1.shape[-1]) * gffn_ref[...]).astype(bf16)


def _flash_body(q_ref, k_ref, vt_ref, x_ref, mp_ref, gatt_ref, wo_ref, gffn_ref, x1_ref, h2_ref,
                m_scr, acc_scr, *, tq):
    i = pl.program_id(1)
    heads = m_scr.shape[0]
    key_idx = lax.broadcasted_iota(jnp.int32, (tq, tq), 0)
    qry_idx = lax.broadcasted_iota(jnp.int32, (tq, tq), 1)
    m_scr[...] = jnp.full(m_scr.shape, -1e30, f32)
    acc_scr[...] = jnp.zeros(acc_scr.shape, f32)
    ones_rows = jnp.ones((BF16_SUBLANES, tq), bf16)

    def step(j, masked):
        start = pl.multiple_of(j * tq, tq)

        def scores(hh):
            sl = slice(hh * LANE, (hh + 1) * LANE)
            return lax.dot_general(k_ref[pl.ds(start, tq), sl], q_ref[:, sl], _NT, preferred_element_type=f32)

        ready = [scores(hh) for hh in range(min(LOOKAHEAD, heads))]
        for hh in range(heads):
            st = ready.pop(0)
            if hh + LOOKAHEAD < heads:
                ready.append(scores(hh + LOOKAHEAD))
            if masked:
                st = jnp.where(key_idx <= qry_idx, st, -1e30)
            m = m_scr[hh]
            m_new = jnp.maximum(m, jnp.max(st, axis=0, keepdims=True))
            alpha = jnp.exp2(m - m_new)
            pt = jnp.exp2((st - m_new).astype(bf16))
            vt = jnp.concatenate([vt_ref[j, hh * V_HEAD_DIM:(hh + 1) * V_HEAD_DIM, :], ones_rows], axis=0)
            acc_scr[hh] = acc_scr[hh] * alpha + _dot(vt, pt)
            m_scr[hh] = m_new

    def body(j, carry):
        step(j, False)
        return carry

    lax.fori_loop(0, i, body, 0)
    step(i, True)
    dv = V_HEAD_DIM
    out_t = jnp.concatenate([acc_scr[hh, 0:dv] / acc_scr[hh, dv:dv + 1] for hh in range(heads)], axis=0)
    _mix_out(x_ref[...], mp_ref[...], out_t.T, gatt_ref, wo_ref, gffn_ref, x1_ref, h2_ref)


def _flash(q, k, vt, x, mp, w, B, S, tq):
    M, D = x.shape
    nq = S // tq
    const = lambda a: pl.BlockSpec(a.shape, lambda b, i: (0, 0))
    tile = lambda n: pl.BlockSpec((tq, n), lambda b, i: (b * nq + i, 0))
    return pl.pallas_call(
        functools.partial(_flash_body, tq=tq),
        grid=(B, nq),
        in_specs=[tile(N_HEADS * LANE),
                  pl.BlockSpec((S, N_HEADS * LANE), lambda b, i: (b, 0)),
                  pl.BlockSpec((nq, N_HEADS * V_HEAD_DIM, tq), lambda b, i: (b, 0, 0)),
                  tile(D), tile(mp.shape[1]),
                  const(w['g_out_attn']), const(w['w_o']), const(w['g_ffn'])],
        out_specs=(tile(D), tile(D)),
        out_shape=(jax.ShapeDtypeStruct((M, D), f32), jax.ShapeDtypeStruct((M, D), bf16)),
        scratch_shapes=[pltpu.VMEM((N_HEADS, 1, tq), f32),
                        pltpu.VMEM((N_HEADS, V_HEAD_DIM + BF16_SUBLANES, tq), f32)],
        compiler_params=_params("parallel", "arbitrary"),
        name="flash_prompt",
    )(q, k, vt, x, mp, w['g_out_attn'], w['w_o'], w['g_ffn'])


def _mm_body(a_ref, b_ref, o_ref):
    o_ref[...] = _dot(a_ref[...].astype(bf16), b_ref[...])


def _mm(a, b, name):
    return pl.pallas_call(
        _mm_body,
        out_shape=jax.ShapeDtypeStruct((a.shape[0], b.shape[1]), f32),
        compiler_params=pltpu.CompilerParams(vmem_limit_bytes=VMEM_LIMIT),
        name=name,
    )(a, b)


def _decode_body(pt_ref, q_ref, knew_ref, qabs_ref, qpe_ref, cnew_ref, wukt_ref, cache_ckv, cache_kpet,
                 o_ref, ckv_buf, kpe_buf, sems, lhs, m_scr, l_scr, acc_scr, *, n_pages, cp):
    b = pl.program_id(0)
    c = pl.program_id(1)
    nc = pl.num_programs(1)
    total = pl.num_programs(0) * nc
    flat = b * nc + c
    n_slots = ckv_buf.shape[0]
    ahead = n_slots - 1
    slot = flat % n_slots
    nope_rows = wukt_ref.shape[0]
    page = cache_ckv.shape[1]

    def copies(flat_idx, slot_idx):
        base = (flat_idx // nc) * n_pages + (flat_idx % nc) * cp
        out = []
        for p in range(cp):
            phys = pt_ref[base + p]
            rows = pl.ds(p * page, page)
            out.append(pltpu.make_async_copy(cache_ckv.at[phys], ckv_buf.at[slot_idx, rows], sems.at[0, slot_idx]))
            out.append(pltpu.make_async_copy(cache_kpet.at[phys], kpe_buf.at[slot_idx, :, rows],
                                             sems.at[1, slot_idx]))
        return out

    @pl.when(flat == 0)
    def _():
        for k in range(ahead):
            for cpy in copies(jnp.minimum(k, total - 1), k):
                cpy.start()

    @pl.when(c == 0)
    def _():
        lhs[0:nope_rows, :] = wukt_ref[...]
        lhs[nope_rows:nope_rows + BF16_SUBLANES, :] = qabs_ref[0]
        s_new = jnp.sum(q_ref[0].astype(f32) * knew_ref[0].astype(f32), axis=-1, keepdims=True)
        m_scr[...] = s_new
        l_scr[...] = jnp.ones_like(l_scr)
        acc_scr[...] = jnp.broadcast_to(cnew_ref[0], acc_scr.shape)

    for cpy in copies(flat, slot):
        cpy.wait()

    qpe = qpe_ref[0]
    ones = jnp.ones(qpe.shape, bf16)
    fold = (lax.broadcasted_iota(jnp.int32, (N_HEADS, 8 * N_HEADS), 1) // 8
            == lax.broadcasted_iota(jnp.int32, (N_HEADS, 8 * N_HEADS), 0)).astype(bf16)
    tokens = cp * page
    ct = ckv_buf[slot].astype(bf16)
    for cpy in copies(jnp.minimum(flat + ahead, total - 1), (flat + ahead) % n_slots):
        cpy.start()
    kt = lax.dot_general(lhs[...], ct, _NT, preferred_element_type=f32)
    kn = kt[0:nope_rows].reshape(N_HEADS, nope_rows // (8 * N_HEADS), 8, tokens)
    partial = jnp.sum(kn * kn, axis=1).reshape(8 * N_HEADS, tokens).astype(bf16)
    kp = kpe_buf[slot]
    num = kt[nope_rows:nope_rows + N_HEADS] + _dot(qpe, kp.astype(bf16))
    ssq = _dot(fold, partial) + _dot(ones, (kp * kp).astype(bf16))
    s = num * lax.rsqrt(ssq * (1.0 / QK_HEAD_DIM) + RMS_EPS)
    m = m_scr[...]
    m_new = jnp.maximum(m, jnp.max(s, axis=-1, keepdims=True))
    alpha = jnp.exp2(m - m_new)
    p = jnp.exp2(s - m_new)
    l_scr[...] = l_scr[...] * alpha + jnp.sum(p, axis=-1, keepdims=True)
    acc_scr[...] = acc_scr[...] * alpha + _dot(p.astype(bf16), ct)
    m_scr[...] = m_new

    @pl.when(c == nc - 1)
    def _():
        o_ref[0] = acc_scr[...] / l_scr[...]

    @pl.when(flat == total - 1)
    def _():
        for k in range(1, ahead + 1):
            for cpy in copies(total - 1, (flat + k) % n_slots):
                cpy.wait()


def _decode(page_table, q3, knew3, qabs, qpe, cnew, wukt, cache_ckv, cache_kpet, cp):
    DB, n_pages = page_table.shape
    page, kv_rank = cache_ckv.shape[1:]
    rope = cache_kpet.shape[1]
    nc = n_pages // cp
    T = cp * page
    per_seq = lambda a: pl.BlockSpec((1,) + a.shape[1:], lambda b, c, pt: (b,) + (0,) * (a.ndim - 1))
    grid_spec = pltpu.PrefetchScalarGridSpec(
        num_scalar_prefetch=1,
        grid=(DB, nc),
        in_specs=[per_seq(q3), per_seq(knew3), per_seq(qabs), per_seq(qpe), per_seq(cnew),
                  pl.BlockSpec(wukt.shape, lambda b, c, pt: (0, 0)),
                  pl.BlockSpec(memory_space=pl.ANY), pl.BlockSpec(memory_space=pl.ANY)],
        out_specs=pl.BlockSpec((1, N_HEADS, kv_rank), lambda b, c, pt: (b, 0, 0)),
        scratch_shapes=[pltpu.VMEM((DECODE_SLOTS, T, kv_rank), f32), pltpu.VMEM((DECODE_SLOTS, rope, T), f32),
                        pltpu.SemaphoreType.DMA((2, DECODE_SLOTS)),
                        pltpu.VMEM((wukt.shape[0] + BF16_SUBLANES, kv_rank), bf16),
                        pltpu.VMEM((N_HEADS, 1), f32), pltpu.VMEM((N_HEADS, 1), f32),
                        pltpu.VMEM((N_HEADS, kv_rank), f32)])
    return pl.pallas_call(
        functools.partial(_decode_body, n_pages=n_pages, cp=cp),
        grid_spec=grid_spec,
        out_shape=jax.ShapeDtypeStruct((DB, N_HEADS, kv_rank), f32),
        compiler_params=_params("arbitrary", "arbitrary"),
        name="decode_attn",
    )(page_table.reshape(-1), q3, knew3, qabs, qpe, cnew, wukt, cache_ckv, cache_kpet)


def _outproj_body(x_ref, mp_ref, attn_ref, gatt_ref, wo_ref, gffn_ref, x1_ref, h2_ref):
    _mix_out(x_ref[...], mp_ref[...], attn_ref[...], gatt_ref, wo_ref, gffn_ref, x1_ref, h2_ref)


def _outproj(x, mp, attn, w, tm):
    M, D = x.shape
    const = lambda a: pl.BlockSpec(a.shape, lambda i: (0, 0))
    row = lambda n: pl.BlockSpec((tm, n), lambda i: (i, 0))
    return pl.pallas_call(
        _outproj_body,
        grid=(M // tm,),
        in_specs=[row(D), row(mp.shape[1]), row(attn.shape[1]),
                  const(w['g_out_attn']), const(w['w_o']), const(w['g_ffn'])],
        out_specs=(row(D), row(D)),
        out_shape=(jax.ShapeDtypeStruct((M, D), f32), jax.ShapeDtypeStruct((M, D), bf16)),
        compiler_params=_params("parallel"),
        name="out_proj",
    )(x, mp, attn, w['g_out_attn'], w['w_o'], w['g_ffn'])


def _ffn_finish(f, g, g1, g2, h2, x1_ref, wu_ref, wd_ref, cw_ref, cb_ref, y_ref, acc):
    gate = cb_ref[...] + g2 * cw_ref[0:1, :] + g1 * cw_ref[1:2, :] + g * cw_ref[2:3, :]
    act = (gate * jax.nn.sigmoid(gate)) * _dot(h2, wu_ref[...])
    part = _dot(act.astype(bf16), wd_ref[...])

    @pl.when(f == 0)
    def _():
        acc[...] = x1_ref[...] + part

    @pl.when(f > 0)
    def _():
        acc[...] += part

    @pl.when(f == pl.num_programs(1) - 1)
    def _():
        y_ref[...] = acc[...]


def _ffn_prompt_body(h2_ref, halo_ref, x1_ref, wg_ref, wu_ref, wd_ref, cw_ref, cb_ref,
                     y_ref, tail_ref, acc, gbuf, *, tm, tiles_per_seq):
    i = pl.program_id(0)
    f = pl.program_id(1)
    hb = BF16_SUBLANES
    h2 = h2_ref[...]
    rows = lax.broadcasted_iota(jnp.int32, (tm + hb, 1), 0)
    keep = jnp.logical_or(rows >= hb, i % tiles_per_seq != 0)
    gbuf[...] = jnp.where(keep, _dot(jnp.concatenate([halo_ref[...], h2], axis=0), wg_ref[...]), 0.0)
    g = gbuf[hb:hb + tm, :]
    tail_ref[0] = gbuf[tm:tm + hb, :]
    g1 = gbuf[hb - 1:hb - 1 + tm, :]
    g2 = gbuf[hb - 2:hb - 2 + tm, :]
    _ffn_finish(f, g, g1, g2, h2, x1_ref, wu_ref, wd_ref, cw_ref, cb_ref, y_ref, acc)


def _ffn_prompt(h2, x1, S, w, tm, tf):
    M, D = x1.shape
    F = w['w_gate'].shape[1]
    hb = BF16_SUBLANES
    nt = M // tm
    halo_blocks = tm // hb
    resident = dict(pipeline_mode=pl.Buffered(1)) if tf == F else {}
    y, tail = pl.pallas_call(
        functools.partial(_ffn_prompt_body, tm=tm, tiles_per_seq=S // tm),
        grid=(nt, F // tf),
        in_specs=[pl.BlockSpec((tm, D), lambda i, f: (i, 0)),
                  pl.BlockSpec((hb, D), lambda i, f: (jnp.maximum(i * halo_blocks - 1, 0), 0)),
                  pl.BlockSpec((tm, D), lambda i, f: (i, 0)),
                  pl.BlockSpec((D, tf), lambda i, f: (0, f), **resident),
                  pl.BlockSpec((D, tf), lambda i, f: (0, f), **resident),
                  pl.BlockSpec((tf, D), lambda i, f: (f, 0), **resident),
                  pl.BlockSpec((CONV_WIDTH, tf), lambda i, f: (0, f)),
                  pl.BlockSpec((1, tf), lambda i, f: (0, f))],
        out_specs=(pl.BlockSpec((tm, D), lambda i, f: (i, 0)),
                   pl.BlockSpec((1, hb, tf), lambda i, f: (i, 0, f))),
        out_shape=(jax.ShapeDtypeStruct((M, D), f32), jax.ShapeDtypeStruct((nt, hb, F), f32)),
        scratch_shapes=[pltpu.VMEM((tm, D), f32), pltpu.VMEM((tm + hb, tf), f32)],
        compiler_params=_params("parallel", "arbitrary"),
        name="ffn_prompt",
    )(h2, h2, x1, w['w_gate'], w['w_up'], w['w_down'], w['conv_w'], w['conv_b'])
    return y, tail


def _ffn_sample_body(h2_ref, st_ref, x1_ref, wg_ref, wu_ref, wd_ref, cw_ref, cb_ref, y_ref, g_ref, acc):
    f = pl.program_id(1)
    h2 = h2_ref[...]
    g = _dot(h2, wg_ref[...])
    g_ref[...] = g
    _ffn_finish(f, g, st_ref[1], st_ref[0], h2, x1_ref, wu_ref, wd_ref, cw_ref, cb_ref, y_ref, acc)


def _ffn_sample(h2, x1, st_t, w, tf):
    M, D = x1.shape
    F = w['w_gate'].shape[1]
    return pl.pallas_call(
        _ffn_sample_body,
        grid=(1, F // tf),
        in_specs=[pl.BlockSpec((M, D), lambda i, f: (0, 0)),
                  pl.BlockSpec((CONV_WIDTH - 1, M, tf), lambda i, f: (0, 0, f)),
                  pl.BlockSpec((M, D), lambda i, f: (0, 0)),
                  pl.BlockSpec((D, tf), lambda i, f: (0, f)),
                  pl.BlockSpec((D, tf), lambda i, f: (0, f)),
                  pl.BlockSpec((tf, D), lambda i, f: (f, 0)),
                  pl.BlockSpec((CONV_WIDTH, tf), lambda i, f: (0, f)),
                  pl.BlockSpec((1, tf), lambda i, f: (0, f))],
        out_specs=(pl.BlockSpec((M, D), lambda i, f: (0, 0)),
                   pl.BlockSpec((M, tf), lambda i, f: (0, f))),
        out_shape=(jax.ShapeDtypeStruct((M, D), f32), jax.ShapeDtypeStruct((M, F), f32)),
        scratch_shapes=[pltpu.VMEM((M, D), f32)],
        compiler_params=_params("parallel", "arbitrary"),
        name="ffn_sample",
    )(h2, st_t, x1, w['w_gate'], w['w_up'], w['w_down'], w['conv_w'], w['conv_b'])


def _qk_gain(g):
    g_pe = g[QK_NOPE_DIM:]
    return jnp.concatenate([g[:QK_NOPE_DIM], g_pe, g_pe])


def _rope_table(pos):
    half = QK_ROPE_DIM // 2
    inv = ROPE_BASE ** (-jnp.arange(0, QK_ROPE_DIM, 2, dtype=f32) / QK_ROPE_DIM)
    inv_lane = jnp.concatenate([jnp.zeros((QK_NOPE_DIM,), f32), inv, inv, jnp.zeros((LANE - QK_HEAD_DIM,), f32)])
    ang = pos.astype(f32)[:, None] * inv_lane[None, :]
    cos, sin = jnp.cos(ang), jnp.sin(ang)
    first_half = jnp.arange(LANE)[None, :] < QK_NOPE_DIM + half
    return jnp.concatenate([cos, jnp.where(first_half, -sin, 0.0), jnp.where(first_half, 0.0, sin)], axis=1)


def _prep_weights(g_mix, w_in, g_q_a, w_uq, g_qn, g_kv_a, w_uk, g_kn, w_uv, w_pool, s_pool,
                  g_out, w_o, g_ffn, w_gate, w_up, conv_w, conv_b, w_down):
    D = w_in.shape[0]
    q_rank = g_q_a.shape[0]
    kv_rank = g_kv_a.shape[0]
    pool_w = s_pool.shape[0]
    head_pad = LANE - QK_HEAD_DIM
    half = QK_ROPE_DIM // 2
    main = pool_w + q_rank + kv_rank
    w_in_p = jnp.concatenate([w_in[:, :main], jnp.zeros((D, QK_NOPE_DIM), f32), w_in[:, main:],
                              jnp.zeros((D, head_pad), f32)], axis=1).astype(bf16)
    w_uq_h = w_uq.reshape(q_rank, N_HEADS, QK_HEAD_DIM)
    w_uq_p = jnp.pad(w_uq_h, ((0, 0), (0, 0), (0, head_pad)))
    lo = w_uq_h[:, :, QK_NOPE_DIM:QK_NOPE_DIM + half]
    hi = w_uq_h[:, :, QK_NOPE_DIM + half:]
    w_uq_rot = jnp.concatenate([jnp.zeros((q_rank, N_HEADS, QK_NOPE_DIM), f32), -hi, lo,
                                jnp.zeros((q_rank, N_HEADS, head_pad), f32)], axis=2)
    w_uq_p = jnp.concatenate([w_uq_p.reshape(q_rank, N_HEADS * LANE),
                              w_uq_rot.reshape(q_rank, N_HEADS * LANE)], axis=1).astype(bf16)
    gq = _qk_gain(g_qn) * _qk_gain(g_kn) * (QK_HEAD_DIM ** -0.5 * LOG2_E)
    gq = jnp.tile(jnp.pad(gq, (0, head_pad)), N_HEADS)[None, :]
    w_uk_p = jnp.pad(w_uk, ((0, 0), (0, 0), (0, LANE - QK_NOPE_DIM))).reshape(kv_rank, N_HEADS * LANE)
    w_uk_p = w_uk_p.astype(bf16)
    w_uv_t = w_uv.reshape(kv_rank, N_HEADS * V_HEAD_DIM).T.astype(bf16)
    w_uk_t = w_uk.reshape(kv_rank, N_HEADS * QK_NOPE_DIM).T.astype(bf16)
    eye = jnp.eye(N_HEADS, dtype=f32)
    w_uk_bd = jnp.einsum('chd,hg->hdgc', jnp.pad(w_uk, ((0, 0), (0, 0), (0, LANE - QK_NOPE_DIM))), eye)
    w_uk_bd = w_uk_bd.reshape(N_HEADS * LANE, N_HEADS * kv_rank).astype(bf16)
    w_uv_bd = jnp.einsum('chv,hg->hcgv', w_uv, eye).reshape(N_HEADS * kv_rank, N_HEADS * V_HEAD_DIM).astype(bf16)
    return {
        'pool_w': pool_w,
        'g_mix': g_mix[None, :], 'w_in': w_in_p, 'g_q_a': g_q_a[None, :], 'w_uq': w_uq_p, 'g_q': gq,
        'g_kv_a': g_kv_a[None, :], 'w_uk': w_uk_p, 'w_uv_t': w_uv_t, 'w_uk_t': w_uk_t, 'w_uk_bd': w_uk_bd, 'w_uv_bd': w_uv_bd,
        'w_pool': w_pool.astype(bf16), 's_pool': s_pool[None, :],
        'g_out_pool': g_out[None, :pool_w], 'g_out_attn': g_out[None, pool_w:],
        'w_o': w_o.astype(bf16), 'g_ffn': g_ffn[None, :],
        'w_gate': w_gate.astype(bf16), 'w_up': w_up.astype(bf16), 'w_down': w_down.astype(bf16),
        'conv_w': conv_w, 'conv_b': conv_b[None, :],
    }


def _tile(n, pref):
    t = min(n, pref)
    assert n % t == 0, (n, pref)
    return t


def _ffn_cols(F):
    half = F // 2
    return half if half % LANE == 0 else F


def kernel(x_prompt, x_sample, cache_ckv, cache_kpe, state_pool, state_conv, page_table, g_mix, w_in, g_q_a,
           w_uq, g_qn, g_kv_a, w_uk, g_kn, w_uv, w_pool, s_pool, g_out, w_o, g_ffn, w_gate, w_up, conv_w,
           conv_b, w_down):
    depth = g_mix.shape[0]
    assert depth == 1, "single-layer trunk only"
    B, S, D = x_prompt.shape
    DB, DS = x_sample.shape[:2]
    assert DS == 1, "one new token per sequence"
    n_pages = page_table.shape[1]
    page = cache_ckv.shape[2]
    past_len = n_pages * page
    l = 0
    w = _prep_weights(g_mix[l], w_in[l], g_q_a[l], w_uq[l], g_qn[l], g_kv_a[l], w_uk[l], g_kn[l], w_uv[l],
                      w_pool[l], s_pool[l], g_out[l], w_o[l], g_ffn[l], w_gate[l], w_up[l], conv_w[l],
                      conv_b[l], w_down[l])
    kv_rank = g_kv_a.shape[1]
    F = w_gate.shape[2]
    tf = _ffn_cols(F)

    xp = x_prompt.reshape(B * S, D)
    tq = _tile(S, 512)
    tm = _tile(tq, 512)
    rope_p = _rope_table(jnp.arange(S, dtype=jnp.int32))
    u_p, q_p, ckv_p, kpe_p, k_p, vt_p = _proj(xp, rope_p, S // tm, w, tm, tq)
    mp_p = _pool_prompt(u_p, B, S, w, _tile(S, 512))
    x1_p, h2_p = _flash(q_p, k_p, vt_p, xp, mp_p, w, B, S, tq)
    tmf = _tile(S, 512)
    y_p, tail = _ffn_prompt(h2_p, x1_p, S, w, tmf, F)
    tiles_per_seq = S // tmf
    conv_p = tail[tiles_per_seq - 1::tiles_per_seq, BF16_SUBLANES - (CONV_WIDTH - 1):, :]

    xs = x_sample.reshape(DB, D)
    pos_s = past_len + jnp.arange(DS, dtype=jnp.int32)
    rope_s = jnp.tile(_rope_table(pos_s), (DB, 1))
    u_s, q_s, ckv_s, kpe_s, k_s, _ = _proj(xs, rope_s, 1, w, DB, DB)
    st_pool_t = jnp.transpose(state_pool[l], (1, 0, 2))
    mp_s = _pool_sample(u_s, st_pool_t, past_len, w)
    qabs = _mm(q_s, w['w_uk_bd'], "absorb_q").reshape(DB, N_HEADS, kv_rank).astype(bf16)
    qabs = jnp.pad(qabs, ((0, 0), (0, BF16_SUBLANES - N_HEADS), (0, 0)))
    q3 = q_s.reshape(DB, N_HEADS, LANE)
    qpe = q3[:, :, QK_NOPE_DIM:QK_HEAD_DIM]
    cp = _tile(n_pages, 32)
    ctx = _decode(page_table, q3, k_s.reshape(DB, N_HEADS, LANE), qabs, qpe, ckv_s.reshape(DB, 1, kv_rank),
                  w['w_uk_t'], cache_ckv[l], jnp.swapaxes(cache_kpe[l], 1, 2), cp)
    attn_s = _mm(ctx.reshape(DB, N_HEADS * kv_rank), w['w_uv_bd'], "value_up")
    x1_s, h2_s = _outproj(xs, mp_s, attn_s, w, DB)
    st_conv_t = jnp.transpose(state_conv[l], (1, 0, 2))
    y_s, g_s = _ffn_sample(h2_s, x1_s, st_conv_t, w, tf)

    kpe_sl = slice(QK_NOPE_DIM, QK_HEAD_DIM)
    P = POOL_STATE_LEN
    return (
        y_p.reshape(B, S, D),
        y_s.reshape(DB, DS, D),
        ckv_p.reshape(1, B, S, kv_rank),
        kpe_p[:, kpe_sl].reshape(1, B, S, QK_ROPE_DIM),
        u_p.reshape(B, S, -1)[None, :, S - P:, :],
        conv_p[None],
        ckv_s.reshape(1, DB, DS, kv_rank),
        kpe_s[:, kpe_sl].reshape(1, DB, DS, QK_ROPE_DIM),
        jnp.concatenate([state_pool[l], u_s[:, None, :]], axis=1)[None, :, -P:, :],
        jnp.concatenate([state_conv[l], g_s[:, None, :]], axis=1)[None, :, -(CONV_WIDTH - 1):, :],
    )
```

```python
import functools

import jax
import jax.numpy as jnp
from jax import lax
from jax.experimental import pallas as pl
from jax.experimental.pallas import tpu as pltpu

N_HEADS = 8
QK_NOPE_DIM = 64
QK_ROPE_DIM = 32
QK_HEAD_DIM = QK_NOPE_DIM + QK_ROPE_DIM
V_HEAD_DIM = 64
POOL_WINDOWS = (2, 4, 8, 16)
POOL_STATE_LEN = max(POOL_WINDOWS) - 1
CONV_WIDTH = 3
ROPE_BASE = 10000.0
RMS_EPS = 1e-6
LOG2_E = 1.4426950408889634

LANE = 128
BF16_SUBLANES = 16
VMEM_LIMIT = 48 * 1024 * 1024
DECODE_SLOTS = 3
LOOKAHEAD = 1

_NT = (((1,), (1,)), ((), ()))

bf16 = jnp.bfloat16
f32 = jnp.float32


def _rms(x, width):
    return x * lax.rsqrt(jnp.sum(x * x, axis=-1, keepdims=True) * (1.0 / width) + RMS_EPS)


def _dot(a, b):
    return jnp.dot(a, b, preferred_element_type=f32)


def _params(*sem):
    return pltpu.CompilerParams(dimension_semantics=sem, vmem_limit_bytes=VMEM_LIMIT)


def _proj_body(x_ref, rope_ref, gmix_ref, win_ref, gqa_ref, wuq_ref, gq_ref, gkva_ref, wuk_ref, wuvt_ref,
               u_ref, q_ref, ckv_ref, kpe_ref, k_ref, vt_ref, *, pool_w, q_rank, kv_rank):
    tm = x_ref.shape[0]
    group = min(tm, 2 * LANE)
    groups = [slice(r, r + group) for r in range(0, tm, group)]
    half = QK_ROPE_DIM // 2
    hq = N_HEADS * LANE

    def in_projection(rows):
        x = x_ref[rows, :]
        h = (_rms(x, x.shape[-1]) * gmix_ref[...]).astype(bf16)
        return _dot(h, win_ref[...])

    def up_projections(rows, proj):
        u_ref[rows, :] = proj[:, :pool_w]
        o = pool_w
        cq = (_rms(proj[:, o:o + q_rank], q_rank) * gqa_ref[...]).astype(bf16)
        o += q_rank
        ckv = _rms(proj[:, o:o + kv_rank], kv_rank) * gkva_ref[...]
        o += kv_rank
        ckv_ref[rows, :] = ckv
        ckv_b = ckv.astype(bf16)
        qraw = _dot(cq, wuq_ref[...])
        kn = _dot(ckv_b, wuk_ref[...])
        vt_ref[0, :, rows] = lax.dot_general(wuvt_ref[...], ckv_b, _NT, preferred_element_type=f32).astype(bf16)
        return proj[:, o:o + LANE], qraw, kn

    def head_norms(rows, raw_kpe, qraw, kn):
        cos = rope_ref[rows, 0:LANE]
        sin_lo = rope_ref[rows, LANE:2 * LANE]
        sin_hi = rope_ref[rows, 2 * LANE:3 * LANE]
        sin_abs = sin_hi - sin_lo
        kpe = raw_kpe * cos + pltpu.roll(raw_kpe, LANE - half, 1) * sin_lo + pltpu.roll(raw_kpe, half, 1) * sin_hi
        kpe_ref[rows, :] = kpe
        for hd in range(N_HEADS):
            sl = slice(hd * LANE, (hd + 1) * LANE)
            qh = qraw[:, sl] * cos + qraw[:, hq + hd * LANE:hq + (hd + 1) * LANE] * sin_abs
            q_ref[rows, sl] = (_rms(qh, QK_HEAD_DIM) * gq_ref[:, sl]).astype(bf16)
            kh = kn[:, sl] + kpe
            k_ref[rows, sl] = _rms(kh, QK_HEAD_DIM).astype(bf16)

    projs = [in_projection(rows) for rows in groups]
    ups = [up_projections(rows, proj) for rows, proj in zip(groups, projs)]
    for rows, up in zip(groups, ups):
        head_norms(rows, *up)


def _proj(x, rope, n_rope_blocks, w, tm, slab):
    M, D = x.shape
    pool_w = w['pool_w']
    q_rank = w['g_q_a'].shape[1]
    kv_rank = w['g_kv_a'].shape[1]
    hq = N_HEADS * LANE
    hv = N_HEADS * V_HEAD_DIM
    per_slab = slab // tm
    const = lambda a: pl.BlockSpec(a.shape, lambda i: (0, 0))
    row = lambda n: pl.BlockSpec((tm, n), lambda i: (i, 0))
    out_shapes = (
        jax.ShapeDtypeStruct((M, pool_w), f32), jax.ShapeDtypeStruct((M, hq), bf16),
        jax.ShapeDtypeStruct((M, kv_rank), f32), jax.ShapeDtypeStruct((M, LANE), f32),
        jax.ShapeDtypeStruct((M, hq), bf16), jax.ShapeDtypeStruct((M // slab, hv, slab), bf16))
    return pl.pallas_call(
        functools.partial(_proj_body, pool_w=pool_w, q_rank=q_rank, kv_rank=kv_rank),
        grid=(M // tm,),
        in_specs=[row(D), pl.BlockSpec((tm, 3 * LANE), lambda i: (i % n_rope_blocks, 0)),
                  const(w['g_mix']), const(w['w_in']), const(w['g_q_a']), const(w['w_uq']),
                  const(w['g_q']), const(w['g_kv_a']), const(w['w_uk']), const(w['w_uv_t'])],
        out_specs=(row(pool_w), row(hq), row(kv_rank), row(LANE), row(hq),
                   pl.BlockSpec((1, hv, tm), lambda i: (i // per_slab, 0, i % per_slab))),
        out_shape=out_shapes,
        compiler_params=_params("parallel"),
        name="in_proj",
    )(x, rope, w['g_mix'], w['w_in'], w['g_q_a'], w['w_uq'], w['g_q'], w['g_kv_a'], w['w_uk'], w['w_uv_t'])


def _pool_finish(diffs, wpool_ref, spool_ref, gout_ref, o_ref):
    ys = []
    for g, d in enumerate(diffs):
        sl = slice(g * LANE, (g + 1) * LANE)
        ys.append(_dot(d.astype(bf16), wpool_ref[g]) * spool_ref[:, sl])
    width = LANE * len(ys)
    ssq = sum(jnp.sum(y * y, axis=-1, keepdims=True) for y in ys)
    scale = lax.rsqrt(ssq * (1.0 / width) + RMS_EPS)
    for g, y in enumerate(ys):
        sl = slice(g * LANE, (g + 1) * LANE)
        o_ref[:, sl] = (y * scale * gout_ref[:, sl]).astype(o_ref.dtype)


def _pool_prompt_body(u_ref, wpool_ref, spool_ref, gout_ref, o_ref, buf, *, ts):
    j = pl.program_id(1)
    halo = POOL_STATE_LEN + 1

    @pl.when(j == 0)
    def _():
        buf[0:halo, :] = jnp.zeros((halo, buf.shape[1]), f32)

    @pl.when(j > 0)
    def _():
        buf[0:halo, :] = buf[ts:ts + halo, :]

    buf[halo:halo + ts, :] = u_ref[...]
    pos = j * ts + lax.broadcasted_iota(jnp.int32, (ts, 1), 0)
    diffs = []
    for g, wdw in enumerate(POOL_WINDOWS):
        sl = slice(g * LANE, (g + 1) * LANE)
        x = buf[halo:halo + ts, sl]
        win = x
        for k in range(1, wdw):
            win = win + buf[halo - k:halo - k + ts, sl]
        cnt = jnp.minimum(wdw, pos + 1).astype(f32)
        diffs.append(win / cnt - x)
    _pool_finish(diffs, wpool_ref, spool_ref, gout_ref, o_ref)


def _pool_prompt(u, B, S, w, ts):
    M, W = u.shape
    ns = S // ts
    const2 = lambda a: pl.BlockSpec(a.shape, lambda b, j: (0,) * a.ndim)
    return pl.pallas_call(
        functools.partial(_pool_prompt_body, ts=ts),
        grid=(B, ns),
        in_specs=[pl.BlockSpec((ts, W), lambda b, j: (b * ns + j, 0)),
                  const2(w['w_pool']), const2(w['s_pool']), const2(w['g_out_pool'])],
        out_specs=pl.BlockSpec((ts, W), lambda b, j: (b * ns + j, 0)),
        out_shape=jax.ShapeDtypeStruct((M, W), bf16),
        scratch_shapes=[pltpu.VMEM((ts + POOL_STATE_LEN + 1, W), f32)],
        compiler_params=_params("arbitrary", "arbitrary"),
        name="pool_prompt",
    )(u, w['w_pool'], w['s_pool'], w['g_out_pool'])


def _pool_sample_body(u_ref, st_ref, wpool_ref, spool_ref, gout_ref, o_ref, *, pos):
    diffs = []
    for g, wdw in enumerate(POOL_WINDOWS):
        sl = slice(g * LANE, (g + 1) * LANE)
        x = u_ref[:, sl]
        win = x
        for k in range(1, wdw):
            win = win + st_ref[POOL_STATE_LEN - k, :, sl]
        diffs.append(win / float(min(wdw, pos + 1)) - x)
    _pool_finish(diffs, wpool_ref, spool_ref, gout_ref, o_ref)


def _pool_sample(u, st_t, pos, w):
    M, W = u.shape
    return pl.pallas_call(
        functools.partial(_pool_sample_body, pos=pos),
        out_shape=jax.ShapeDtypeStruct((M, W), bf16),
        compiler_params=pltpu.CompilerParams(vmem_limit_bytes=VMEM_LIMIT),
        name="pool_sample",
    )(u, st_t, w['w_pool'], w['s_pool'], w['g_out_pool'])


def _mix_out(x, mp, attn, gatt_ref, wo_ref, gffn_ref, x1_ref, h2_ref):
    na = (_rms(attn, attn.shape[-1]) * gatt_ref[...]).astype(bf16)
    x1 = x + _dot(jnp.concatenate([mp, na], axis=1), wo_ref[...])
    x1_ref[...] = x1
    h2_ref[...] = (_rms(x1, x1.shape[-1]) * gffn_ref[...]).astype(bf16)


def _flash_body(q_ref, k_ref, vt_ref, x_ref, mp_ref, gatt_ref, wo_ref, gffn_ref, x1_ref, h2_ref,
                m_scr, acc_scr, *, tq):
    i = pl.program_id(1)
    heads = m_scr.shape[0]
    key_idx = lax.broadcasted_iota(jnp.int32, (tq, tq), 0)
    qry_idx = lax.broadcasted_iota(jnp.int32, (tq, tq), 1)
    m_scr[...] = jnp.full(m_scr.shape, -1e30, f32)
    acc_scr[...] = jnp.zeros(acc_scr.shape, f32)
    ones_rows = jnp.ones((BF16_SUBLANES, tq), bf16)

    def step(j, masked):
        start = pl.multiple_of(j * tq, tq)

        def scores(hh):
            sl = slice(hh * LANE, (hh + 1) * LANE)
            return lax.dot_general(k_ref[pl.ds(start, tq), sl], q_ref[:, sl], _NT, preferred_element_type=f32)

        ready = [scores(hh) for hh in range(min(LOOKAHEAD, heads))]
        for hh in range(heads):
            st = ready.pop(0)
            if hh + LOOKAHEAD < heads:
                ready.append(scores(hh + LOOKAHEAD))
            if masked:
                st = jnp.where(key_idx <= qry_idx, st, -1e30)
            m = m_scr[hh]
            m_new = jnp.maximum(m, jnp.max(st, axis=0, keepdims=True))
            alpha = jnp.exp2(m - m_new)
            pt = jnp.exp2((st - m_new).astype(bf16))
            vt = jnp.concatenate([vt_ref[j, hh * V_HEAD_DIM:(hh + 1) * V_HEAD_DIM, :], ones_rows], axis=0)
            acc_scr[hh] = acc_scr[hh] * alpha + _dot(vt, pt)
            m_scr[hh] = m_new

    def body(j, carry):
        step(j, False)
        return carry

    lax.fori_loop(0, i, body, 0)
    step(i, True)
    dv = V_HEAD_DIM
    out_t = jnp.concatenate([acc_scr[hh, 0:dv] / acc_scr[hh, dv:dv + 1] for hh in range(heads)], axis=0)
    _mix_out(x_ref[...], mp_ref[...], out_t.T, gatt_ref, wo_ref, gffn_ref, x1_ref, h2_ref)


def _flash(q, k, vt, x, mp, w, B, S, tq):
    M, D = x.shape
    nq = S // tq
    const = lambda a: pl.BlockSpec(a.shape, lambda b, i: (0, 0))
    tile = lambda n: pl.BlockSpec((tq, n), lambda b, i: (b * nq + i, 0))
    return pl.pallas_call(
        functools.partial(_flash_body, tq=tq),
        grid=(B, nq),
        in_specs=[tile(N_HEADS * LANE),
                  pl.BlockSpec((S, N_HEADS * LANE), lambda b, i: (b, 0)),
                  pl.BlockSpec((nq, N_HEADS * V_HEAD_DIM, tq), lambda b, i: (b, 0, 0)),
                  tile(D), tile(mp.shape[1]),
                  const(w['g_out_attn']), const(w['w_o']), const(w['g_ffn'])],
        out_specs=(tile(D), tile(D)),
        out_shape=(jax.ShapeDtypeStruct((M, D), f32), jax.ShapeDtypeStruct((M, D), bf16)),
        scratch_shapes=[pltpu.VMEM((N_HEADS, 1, tq), f32),
                        pltpu.VMEM((N_HEADS, V_HEAD_DIM + BF16_SUBLANES, tq), f32)],
        compiler_params=_params("parallel", "arbitrary"),
        name="flash_prompt",
    )(q, k, vt, x, mp, w['g_out_attn'], w['w_o'], w['g_ffn'])


def _mm_body(a_ref, b_ref, o_ref):
    o_ref[...] = _dot(a_ref[...].astype(bf16), b_ref[...])


def _mm(a, b, name):
    return pl.pallas_call(
        _mm_body,
        out_shape=jax.ShapeDtypeStruct((a.shape[0], b.shape[1]), f32),
        compiler_params=pltpu.CompilerParams(vmem_limit_bytes=VMEM_LIMIT),
        name=name,
    )(a, b)


def _decode_body(pt_ref, q_ref, knew_ref, qabs_ref, qpe_ref, cnew_ref, wukt_ref, cache_ckv, cache_kpet,
                 o_ref, ckv_buf, kpe_buf, sems, lhs, ct_scr, part_scr, num_scr, pes_scr, m_scr, l_scr, acc_scr,
                 *, n_pages, cp, nc):
    f = pl.program_id(0)
    total = pl.num_programs(0) - 1
    n_slots = ckv_buf.shape[0]
    ahead = n_slots - 1
    ring = f % n_slots
    nope_rows = wukt_ref.shape[0]
    page = cache_ckv.shape[1]
    tokens = cp * page

    def copies(chunk, slot_idx):
        base = (chunk // nc) * n_pages + (chunk % nc) * cp
        out = []
        for p in range(cp):
            phys = pt_ref[base + p]
            rows = pl.ds(p * page, page)
            out.append(pltpu.make_async_copy(cache_ckv.at[phys], ckv_buf.at[slot_idx, rows], sems.at[0, slot_idx]))
            out.append(pltpu.make_async_copy(cache_kpet.at[phys], kpe_buf.at[slot_idx, :, rows],
                                             sems.at[1, slot_idx]))
        return out

    @pl.when(f == 0)
    def _():
        for k in range(ahead):
            for cpy in copies(jnp.minimum(k, total - 1), k):
                cpy.start()
        ct_scr[1] = jnp.zeros(ct_scr.shape[1:], bf16)
        part_scr[1] = jnp.zeros(part_scr.shape[1:], bf16)
        num_scr[1] = jnp.zeros(num_scr.shape[1:], f32)
        pes_scr[1] = jnp.zeros(pes_scr.shape[1:], f32)
        m_scr[...] = jnp.zeros_like(m_scr)
        l_scr[...] = jnp.ones_like(l_scr)
        acc_scr[...] = jnp.zeros_like(acc_scr)
        lhs[0:nope_rows, :] = wukt_ref[...]

    @pl.when(f % nc == 0)
    def _():
        lhs[nope_rows:nope_rows + BF16_SUBLANES, :] = qabs_ref[0]

    @pl.when((f >= 1) & ((f - 1) % nc == 0))
    def _():
        s_new = jnp.sum(q_ref[0].astype(f32) * knew_ref[0].astype(f32), axis=-1, keepdims=True)
        m_scr[...] = s_new
        l_scr[...] = jnp.ones_like(l_scr)
        acc_scr[...] = jnp.broadcast_to(cnew_ref[0], acc_scr.shape)

    qpe = qpe_ref[0]
    ones = jnp.ones(qpe.shape, bf16)
    fold = (lax.broadcasted_iota(jnp.int32, (N_HEADS, 8 * N_HEADS), 1) // 8
            == lax.broadcasted_iota(jnp.int32, (N_HEADS, 8 * N_HEADS), 0)).astype(bf16)

    def main(cur):
        prev = 1 - cur
        for cpy in copies(jnp.minimum(f, total - 1), ring):
            cpy.wait()
        ssq = _dot(fold, part_scr[prev]) + pes_scr[prev]
        s = num_scr[prev] * lax.rsqrt(ssq * (1.0 / QK_HEAD_DIM) + RMS_EPS)
        m = m_scr[...]
        m_new = jnp.maximum(m, jnp.max(s, axis=-1, keepdims=True))
        alpha = jnp.exp2(m - m_new)
        p = jnp.exp2(s - m_new)
        l_scr[...] = l_scr[...] * alpha + jnp.sum(p, axis=-1, keepdims=True)
        m_scr[...] = m_new
        ct = ckv_buf[ring].astype(bf16)
        ct_scr[cur] = ct
        for cpy in copies(jnp.minimum(f + ahead, total - 1), (f + ahead) % n_slots):
            cpy.start()
        kt = lax.dot_general(lhs[...], ct, _NT, preferred_element_type=f32)
        kn = kt[0:nope_rows].reshape(N_HEADS, nope_rows // (8 * N_HEADS), 8, tokens)
        part_scr[cur] = jnp.sum(kn * kn, axis=1).reshape(8 * N_HEADS, tokens).astype(bf16)
        kp = kpe_buf[ring]
        num_scr[cur] = kt[nope_rows:nope_rows + N_HEADS] + _dot(qpe, kp.astype(bf16))
        pes_scr[cur] = _dot(ones, (kp * kp).astype(bf16))
        acc_scr[...] = acc_scr[...] * alpha + _dot(p.astype(bf16), ct_scr[prev])

    @pl.when(f % 2 == 0)
    def _():
        main(0)

    @pl.when(f % 2 == 1)
    def _():
        main(1)

    @pl.when((f >= 1) & (f % nc == 0))
    def _():
        o_ref[0] = acc_scr[...] / l_scr[...]

    @pl.when(f == total)
    def _():
        for k in range(1, ahead + 1):
            for cpy in copies(total - 1, (f + k) % n_slots):
                cpy.wait()


def _decode(page_table, q3, knew3, qabs, qpe, cnew, wukt, cache_ckv, cache_kpet, cp):
    DB, n_pages = page_table.shape
    page, kv_rank = cache_ckv.shape[1:]
    rope = cache_kpet.shape[1]
    nc = n_pages // cp
    T = cp * page
    total = DB * nc
    head_seq = lambda f, pt: jnp.minimum(f, total - 1) // nc
    tail_seq = lambda f, pt: jnp.maximum(f - 1, 0) // nc
    per_seq = lambda a, seq: pl.BlockSpec((1,) + a.shape[1:], lambda f, pt: (seq(f, pt),) + (0,) * (a.ndim - 1))
    grid_spec = pltpu.PrefetchScalarGridSpec(
        num_scalar_prefetch=1,
        grid=(total + 1,),
        in_specs=[per_seq(q3, tail_seq), per_seq(knew3, tail_seq), per_seq(qabs, head_seq), per_seq(qpe, head_seq),
                  per_seq(cnew, tail_seq), pl.BlockSpec(wukt.shape, lambda f, pt: (0, 0)),
                  pl.BlockSpec(memory_space=pl.ANY), pl.BlockSpec(memory_space=pl.ANY)],
        out_specs=pl.BlockSpec((1, N_HEADS, kv_rank), lambda f, pt: (tail_seq(f, pt), 0, 0)),
        scratch_shapes=[pltpu.VMEM((DECODE_SLOTS, T, kv_rank), f32), pltpu.VMEM((DECODE_SLOTS, rope, T), f32),
                        pltpu.SemaphoreType.DMA((2, DECODE_SLOTS)),
                        pltpu.VMEM((wukt.shape[0] + BF16_SUBLANES, kv_rank), bf16),
                        pltpu.VMEM((2, T, kv_rank), bf16), pltpu.VMEM((2, 8 * N_HEADS, T), bf16),
                        pltpu.VMEM((2, N_HEADS, T), f32), pltpu.VMEM((2, N_HEADS, T), f32),
                        pltpu.VMEM((N_HEADS, 1), f32), pltpu.VMEM((N_HEADS, 1), f32),
                        pltpu.VMEM((N_HEADS, kv_rank), f32)])
    return pl.pallas_call(
        functools.partial(_decode_body, n_pages=n_pages, cp=cp, nc=nc),
        grid_spec=grid_spec,
        out_shape=jax.ShapeDtypeStruct((DB, N_HEADS, kv_rank), f32),
        compiler_params=_params("arbitrary"),
        name="decode_attn",
    )(page_table.reshape(-1), q3, knew3, qabs, qpe, cnew, wukt, cache_ckv, cache_kpet)


def _outproj_body(x_ref, mp_ref, attn_ref, gatt_ref, wo_ref, gffn_ref, x1_ref, h2_ref):
    _mix_out(x_ref[...], mp_ref[...], attn_ref[...], gatt_ref, wo_ref, gffn_ref, x1_ref, h2_ref)


def _outproj(x, mp, attn, w, tm):
    M, D = x.shape
    const = lambda a: pl.BlockSpec(a.shape, lambda i: (0, 0))
    row = lambda n: pl.BlockSpec((tm, n), lambda i: (i, 0))
    return pl.pallas_call(
        _outproj_body,
        grid=(M // tm,),
        in_specs=[row(D), row(mp.shape[1]), row(attn.shape[1]),
                  const(w['g_out_attn']), const(w['w_o']), const(w['g_ffn'])],
        out_specs=(row(D), row(D)),
        out_shape=(jax.ShapeDtypeStruct((M, D), f32), jax.ShapeDtypeStruct((M, D), bf16)),
        compiler_params=_params("parallel"),
        name="out_proj",
    )(x, mp, attn, w['g_out_attn'], w['w_o'], w['g_ffn'])


def _ffn_finish(f, g, g1, g2, h2, x1_ref, wu_ref, wd_ref, cw_ref, cb_ref, y_ref, acc):
    gate = cb_ref[...] + g2 * cw_ref[0:1, :] + g1 * cw_ref[1:2, :] + g * cw_ref[2:3, :]
    act = (gate * jax.nn.sigmoid(gate)) * _dot(h2, wu_ref[...])
    part = _dot(act.astype(bf16), wd_ref[...])

    @pl.when(f == 0)
    def _():
        acc[...] = x1_ref[...] + part

    @pl.when(f > 0)
    def _():
        acc[...] += part

    @pl.when(f == pl.num_programs(1) - 1)
    def _():
        y_ref[...] = acc[...]


def _ffn_prompt_body(h2_ref, halo_ref, x1_ref, wg_ref, wu_ref, wd_ref, cw_ref, cb_ref,
                     y_ref, tail_ref, acc, gbuf, *, tm, tiles_per_seq):
    i = pl.program_id(0)
    f = pl.program_id(1)
    hb = BF16_SUBLANES
    h2 = h2_ref[...]
    rows = lax.broadcasted_iota(jnp.int32, (tm + hb, 1), 0)
    keep = jnp.logical_or(rows >= hb, i % tiles_per_seq != 0)
    gbuf[...] = jnp.where(keep, _dot(jnp.concatenate([halo_ref[...], h2], axis=0), wg_ref[...]), 0.0)
    g = gbuf[hb:hb + tm, :]
    tail_ref[0] = gbuf[tm:tm + hb, :]
    g1 = gbuf[hb - 1:hb - 1 + tm, :]
    g2 = gbuf[hb - 2:hb - 2 + tm, :]
    _ffn_finish(f, g, g1, g2, h2, x1_ref, wu_ref, wd_ref, cw_ref, cb_ref, y_ref, acc)


def _ffn_prompt(h2, x1, S, w, tm, tf):
    M, D = x1.shape
    F = w['w_gate'].shape[1]
    hb = BF16_SUBLANES
    nt = M // tm
    halo_blocks = tm // hb
    resident = dict(pipeline_mode=pl.Buffered(1)) if tf == F else {}
    y, tail = pl.pallas_call(
        functools.partial(_ffn_prompt_body, tm=tm, tiles_per_seq=S // tm),
        grid=(nt, F // tf),
        in_specs=[pl.BlockSpec((tm, D), lambda i, f: (i, 0)),
                  pl.BlockSpec((hb, D), lambda i, f: (jnp.maximum(i * halo_blocks - 1, 0), 0)),
                  pl.BlockSpec((tm, D), lambda i, f: (i, 0)),
                  pl.BlockSpec((D, tf), lambda i, f: (0, f), **resident),
                  pl.BlockSpec((D, tf), lambda i, f: (0, f), **resident),
                  pl.BlockSpec((tf, D), lambda i, f: (f, 0), **resident),
                  pl.BlockSpec((CONV_WIDTH, tf), lambda i, f: (0, f)),
                  pl.BlockSpec((1, tf), lambda i, f: (0, f))],
        out_specs=(pl.BlockSpec((tm, D), lambda i, f: (i, 0)),
                   pl.BlockSpec((1, hb, tf), lambda i, f: (i, 0, f))),
        out_shape=(jax.ShapeDtypeStruct((M, D), f32), jax.ShapeDtypeStruct((nt, hb, F), f32)),
        scratch_shapes=[pltpu.VMEM((tm, D), f32), pltpu.VMEM((tm + hb, tf), f32)],
        compiler_params=_params("parallel", "arbitrary"),
        name="ffn_prompt",
    )(h2, h2, x1, w['w_gate'], w['w_up'], w['w_down'], w['conv_w'], w['conv_b'])
    return y, tail


def _ffn_sample_body(h2_ref, st_ref, x1_ref, wg_ref, wu_ref, wd_ref, cw_ref, cb_ref, y_ref, g_ref, acc):
    f = pl.program_id(1)
    h2 = h2_ref[...]
    g = _dot(h2, wg_ref[...])
    g_ref[...] = g
    _ffn_finish(f, g, st_ref[1], st_ref[0], h2, x1_ref, wu_ref, wd_ref, cw_ref, cb_ref, y_ref, acc)


def _ffn_sample(h2, x1, st_t, w, tf):
    M, D = x1.shape
    F = w['w_gate'].shape[1]
    return pl.pallas_call(
        _ffn_sample_body,
        grid=(1, F // tf),
        in_specs=[pl.BlockSpec((M, D), lambda i, f: (0, 0)),
                  pl.BlockSpec((CONV_WIDTH - 1, M, tf), lambda i, f: (0, 0, f)),
                  pl.BlockSpec((M, D), lambda i, f: (0, 0)),
                  pl.BlockSpec((D, tf), lambda i, f: (0, f)),
                  pl.BlockSpec((D, tf), lambda i, f: (0, f)),
                  pl.BlockSpec((tf, D), lambda i, f: (f, 0)),
                  pl.BlockSpec((CONV_WIDTH, tf), lambda i, f: (0, f)),
                  pl.BlockSpec((1, tf), lambda i, f: (0, f))],
        out_specs=(pl.BlockSpec((M, D), lambda i, f: (0, 0)),
                   pl.BlockSpec((M, tf), lambda i, f: (0, f))),
        out_shape=(jax.ShapeDtypeStruct((M, D), f32), jax.ShapeDtypeStruct((M, F), f32)),
        scratch_shapes=[pltpu.VMEM((M, D), f32)],
        compiler_params=_params("parallel", "arbitrary"),
        name="ffn_sample",
    )(h2, st_t, x1, w['w_gate'], w['w_up'], w['w_down'], w['conv_w'], w['conv_b'])


def _qk_gain(g):
    g_pe = g[QK_NOPE_DIM:]
    return jnp.concatenate([g[:QK_NOPE_DIM], g_pe, g_pe])


def _rope_table(pos):
    half = QK_ROPE_DIM // 2
    inv = ROPE_BASE ** (-jnp.arange(0, QK_ROPE_DIM, 2, dtype=f32) / QK_ROPE_DIM)
    inv_lane = jnp.concatenate([jnp.zeros((QK_NOPE_DIM,), f32), inv, inv, jnp.zeros((LANE - QK_HEAD_DIM,), f32)])
    ang = pos.astype(f32)[:, None] * inv_lane[None, :]
    cos, sin = jnp.cos(ang), jnp.sin(ang)
    first_half = jnp.arange(LANE)[None, :] < QK_NOPE_DIM + half
    return jnp.concatenate([cos, jnp.where(first_half, -sin, 0.0), jnp.where(first_half, 0.0, sin)], axis=1)


def _prep_weights(g_mix, w_in, g_q_a, w_uq, g_qn, g_kv_a, w_uk, g_kn, w_uv, w_pool, s_pool,
                  g_out, w_o, g_ffn, w_gate, w_up, conv_w, conv_b, w_down):
    D = w_in.shape[0]
    q_rank = g_q_a.shape[0]
    kv_rank = g_kv_a.shape[0]
    pool_w = s_pool.shape[0]
    head_pad = LANE - QK_HEAD_DIM
    half = QK_ROPE_DIM // 2
    main = pool_w + q_rank + kv_rank
    w_in_p = jnp.concatenate([w_in[:, :main], jnp.zeros((D, QK_NOPE_DIM), f32), w_in[:, main:],
                              jnp.zeros((D, head_pad), f32)], axis=1).astype(bf16)
    w_uq_h = w_uq.reshape(q_rank, N_HEADS, QK_HEAD_DIM)
    w_uq_p = jnp.pad(w_uq_h, ((0, 0), (0, 0), (0, head_pad)))
    lo = w_uq_h[:, :, QK_NOPE_DIM:QK_NOPE_DIM + half]
    hi = w_uq_h[:, :, QK_NOPE_DIM + half:]
    w_uq_rot = jnp.concatenate([jnp.zeros((q_rank, N_HEADS, QK_NOPE_DIM), f32), -hi, lo,
                                jnp.zeros((q_rank, N_HEADS, head_pad), f32)], axis=2)
    w_uq_p = jnp.concatenate([w_uq_p.reshape(q_rank, N_HEADS * LANE),
                              w_uq_rot.reshape(q_rank, N_HEADS * LANE)], axis=1).astype(bf16)
    gq = _qk_gain(g_qn) * _qk_gain(g_kn) * (QK_HEAD_DIM ** -0.5 * LOG2_E)
    gq = jnp.tile(jnp.pad(gq, (0, head_pad)), N_HEADS)[None, :]
    w_uk_p = jnp.pad(w_uk, ((0, 0), (0, 0), (0, LANE - QK_NOPE_DIM))).reshape(kv_rank, N_HEADS * LANE)
    w_uk_p = w_uk_p.astype(bf16)
    w_uv_t = w_uv.reshape(kv_rank, N_HEADS * V_HEAD_DIM).T.astype(bf16)
    w_uk_t = w_uk.reshape(kv_rank, N_HEADS * QK_NOPE_DIM).T.astype(bf16)
    eye = jnp.eye(N_HEADS, dtype=f32)
    w_uk_bd = jnp.einsum('chd,hg->hdgc', jnp.pad(w_uk, ((0, 0), (0, 0), (0, LANE - QK_NOPE_DIM))), eye)
    w_uk_bd = w_uk_bd.reshape(N_HEADS * LANE, N_HEADS * kv_rank).astype(bf16)
    w_uv_bd = jnp.einsum('chv,hg->hcgv', w_uv, eye).reshape(N_HEADS * kv_rank, N_HEADS * V_HEAD_DIM).astype(bf16)
    return {
        'pool_w': pool_w,
        'g_mix': g_mix[None, :], 'w_in': w_in_p, 'g_q_a': g_q_a[None, :], 'w_uq': w_uq_p, 'g_q': gq,
        'g_kv_a': g_kv_a[None, :], 'w_uk': w_uk_p, 'w_uv_t': w_uv_t, 'w_uk_t': w_uk_t, 'w_uk_bd': w_uk_bd, 'w_uv_bd': w_uv_bd,
        'w_pool': w_pool.astype(bf16), 's_pool': s_pool[None, :],
        'g_out_pool': g_out[None, :pool_w], 'g_out_attn': g_out[None, pool_w:],
        'w_o': w_o.astype(bf16), 'g_ffn': g_ffn[None, :],
        'w_gate': w_gate.astype(bf16), 'w_up': w_up.astype(bf16), 'w_down': w_down.astype(bf16),
        'conv_w': conv_w, 'conv_b': conv_b[None, :],
    }


def _tile(n, pref):
    t = min(n, pref)
    assert n % t == 0, (n, pref)
    return t


def _ffn_cols(F):
    half = F // 2
    return half if half % LANE == 0 else F


def kernel(x_prompt, x_sample, cache_ckv, cache_kpe, state_pool, state_conv, page_table, g_mix, w_in, g_q_a,
           w_uq, g_qn, g_kv_a, w_uk, g_kn, w_uv, w_pool, s_pool, g_out, w_o, g_ffn, w_gate, w_up, conv_w,
           conv_b, w_down):
    depth = g_mix.shape[0]
    assert depth == 1, "single-layer trunk only"
    B, S, D = x_prompt.shape
    DB, DS = x_sample.shape[:2]
    assert DS == 1, "one new token per sequence"
    n_pages = page_table.shape[1]
    page = cache_ckv.shape[2]
    past_len = n_pages * page
    l = 0
    w = _prep_weights(g_mix[l], w_in[l], g_q_a[l], w_uq[l], g_qn[l], g_kv_a[l], w_uk[l], g_kn[l], w_uv[l],
                      w_pool[l], s_pool[l], g_out[l], w_o[l], g_ffn[l], w_gate[l], w_up[l], conv_w[l],
                      conv_b[l], w_down[l])
    kv_rank = g_kv_a.shape[1]
    F = w_gate.shape[2]
    tf = _ffn_cols(F)

    xp = x_prompt.reshape(B * S, D)
    tq = _tile(S, 512)
    tm = _tile(tq, 512)
    rope_p = _rope_table(jnp.arange(S, dtype=jnp.int32))
    u_p, q_p, ckv_p, kpe_p, k_p, vt_p = _proj(xp, rope_p, S // tm, w, tm, tq)
    mp_p = _pool_prompt(u_p, B, S, w, _tile(S, 512))
    x1_p, h2_p = _flash(q_p, k_p, vt_p, xp, mp_p, w, B, S, tq)
    tmf = _tile(S, 512)
    y_p, tail = _ffn_prompt(h2_p, x1_p, S, w, tmf, F)
    tiles_per_seq = S // tmf
    conv_p = tail[tiles_per_seq - 1::tiles_per_seq, BF16_SUBLANES - (CONV_WIDTH - 1):, :]

    xs = x_sample.reshape(DB, D)
    pos_s = past_len + jnp.arange(DS, dtype=jnp.int32)
    rope_s = jnp.tile(_rope_table(pos_s), (DB, 1))
    u_s, q_s, ckv_s, kpe_s, k_s, _ = _proj(xs, rope_s, 1, w, DB, DB)
    st_pool_t = jnp.transpose(state_pool[l], (1, 0, 2))
    mp_s = _pool_sample(u_s, st_pool_t, past_len, w)
    qabs = _mm(q_s, w['w_uk_bd'], "absorb_q").reshape(DB, N_HEADS, kv_rank).astype(bf16)
    qabs = jnp.pad(qabs, ((0, 0), (0, BF16_SUBLANES - N_HEADS), (0, 0)))
    q3 = q_s.reshape(DB, N_HEADS, LANE)
    qpe = q3[:, :, QK_NOPE_DIM:QK_HEAD_DIM]
    cp = _tile(n_pages, 32)
    ctx = _decode(page_table, q3, k_s.reshape(DB, N_HEADS, LANE), qabs, qpe, ckv_s.reshape(DB, 1, kv_rank),
                  w['w_uk_t'], cache_ckv[l], jnp.swapaxes(cache_kpe[l], 1, 2), cp)
    attn_s = _mm(ctx.reshape(DB, N_HEADS * kv_rank), w['w_uv_bd'], "value_up")
    x1_s, h2_s = _outproj(xs, mp_s, attn_s, w, DB)
    st_conv_t = jnp.transpose(state_conv[l], (1, 0, 2))
    y_s, g_s = _ffn_sample(h2_s, x1_s, st_conv_t, w, tf)

    kpe_sl = slice(QK_NOPE_DIM, QK_HEAD_DIM)
    P = POOL_STATE_LEN
    return (
        y_p.reshape(B, S, D),
        y_s.reshape(DB, DS, D),
        ckv_p.reshape(1, B, S, kv_rank),
        kpe_p[:, kpe_sl].reshape(1, B, S, QK_ROPE_DIM),
        u_p.reshape(B, S, -1)[None, :, S - P:, :],
        conv_p[None],
        ckv_s.reshape(1, DB, DS, kv_rank),
        kpe_s[:, kpe_sl].reshape(1, DB, DS, QK_ROPE_DIM),
        jnp.concatenate([state_pool[l], u_s[:, None, :]], axis=1)[None, :, -P:, :],
        jnp.concatenate([state_conv[l], g_s[:, None, :]], axis=1)[None, :, -(CONV_WIDTH - 1):, :],
    )
```

```python
import functools

import jax
import jax.numpy as jnp
from jax import lax
from jax.experimental import pallas as pl
from jax.experimental.pallas import tpu as pltpu

N_HEADS = 8
QK_NOPE_DIM = 64
QK_ROPE_DIM = 32
QK_HEAD_DIM = QK_NOPE_DIM + QK_ROPE_DIM
V_HEAD_DIM = 64
POOL_WINDOWS = (2, 4, 8, 16)
POOL_STATE_LEN = max(POOL_WINDOWS) - 1
CONV_WIDTH = 3
ROPE_BASE = 10000.0
RMS_EPS = 1e-6
LOG2_E = 1.4426950408889634

LANE = 128
SUBLANES = 8
MXU_DIM = 256
BF16_SUBLANES = 16
VMEM_LIMIT = 48 * 1024 * 1024
DECODE_SLOTS = 3
LOOKAHEAD = 1

_NT = (((1,), (1,)), ((), ()))

bf16 = jnp.bfloat16
f32 = jnp.float32


def _rms(x, width):
    return x * lax.rsqrt(jnp.sum(x * x, axis=-1, keepdims=True) * (1.0 / width) + RMS_EPS)


def _dot(a, b):
    return jnp.dot(a, b, preferred_element_type=f32)


def _params(*sem):
    return pltpu.CompilerParams(dimension_semantics=sem, vmem_limit_bytes=VMEM_LIMIT)


def _proj_body(x_ref, rope_ref, gmix_ref, win_ref, gqa_ref, wuq_ref, gq_ref, gkva_ref, wuk_ref, wuvt_ref,
               u_ref, q_ref, ckv_ref, kpe_ref, k_ref, vt_ref, *, pool_w, q_rank, kv_rank):
    tm = x_ref.shape[0]
    group = min(tm, MXU_DIM)
    groups = [slice(r, r + group) for r in range(0, tm, group)]
    half = QK_ROPE_DIM // 2
    hq = N_HEADS * LANE

    def in_projection(rows):
        x = x_ref[rows, :]
        h = (_rms(x, x.shape[-1]) * gmix_ref[...]).astype(bf16)
        return _dot(h, win_ref[...])

    def up_projections(rows, proj):
        u_ref[rows, :] = proj[:, :pool_w]
        o = pool_w
        cq = (_rms(proj[:, o:o + q_rank], q_rank) * gqa_ref[...]).astype(bf16)
        o += q_rank
        ckv = _rms(proj[:, o:o + kv_rank], kv_rank) * gkva_ref[...]
        o += kv_rank
        ckv_ref[rows, :] = ckv
        ckv_b = ckv.astype(bf16)
        qraw = _dot(cq, wuq_ref[...])
        kn = _dot(ckv_b, wuk_ref[...])
        vt_ref[0, :, rows] = lax.dot_general(wuvt_ref[...], ckv_b, _NT, preferred_element_type=f32).astype(bf16)
        return proj[:, o:o + LANE], qraw, kn

    def head_norms(rows, raw_kpe, qraw, kn):
        cos = rope_ref[rows, 0:LANE]
        sin_lo = rope_ref[rows, LANE:2 * LANE]
        sin_hi = rope_ref[rows, 2 * LANE:3 * LANE]
        sin_abs = sin_hi - sin_lo
        kpe = raw_kpe * cos + pltpu.roll(raw_kpe, LANE - half, 1) * sin_lo + pltpu.roll(raw_kpe, half, 1) * sin_hi
        kpe_ref[rows, :] = kpe
        for hd in range(N_HEADS):
            sl = slice(hd * LANE, (hd + 1) * LANE)
            qh = qraw[:, sl] * cos + qraw[:, hq + hd * LANE:hq + (hd + 1) * LANE] * sin_abs
            q_ref[rows, sl] = (_rms(qh, QK_HEAD_DIM) * gq_ref[:, sl]).astype(bf16)
            kh = kn[:, sl] + kpe
            k_ref[rows, sl] = _rms(kh, QK_HEAD_DIM).astype(bf16)

    projs = [in_projection(rows) for rows in groups]
    ups = [up_projections(rows, proj) for rows, proj in zip(groups, projs)]
    for rows, up in zip(groups, ups):
        head_norms(rows, *up)


def _proj(x, rope, n_rope_blocks, w, tm, slab):
    M, D = x.shape
    pool_w = w['pool_w']
    q_rank = w['g_q_a'].shape[1]
    kv_rank = w['g_kv_a'].shape[1]
    hq = N_HEADS * LANE
    hv = N_HEADS * V_HEAD_DIM
    per_slab = slab // tm
    const = lambda a: pl.BlockSpec(a.shape, lambda i: (0, 0))
    row = lambda n: pl.BlockSpec((tm, n), lambda i: (i, 0))
    out_shapes = (
        jax.ShapeDtypeStruct((M, pool_w), f32), jax.ShapeDtypeStruct((M, hq), bf16),
        jax.ShapeDtypeStruct((M, kv_rank), f32), jax.ShapeDtypeStruct((M, LANE), f32),
        jax.ShapeDtypeStruct((M, hq), bf16), jax.ShapeDtypeStruct((M // slab, hv, slab), bf16))
    return pl.pallas_call(
        functools.partial(_proj_body, pool_w=pool_w, q_rank=q_rank, kv_rank=kv_rank),
        grid=(M // tm,),
        in_specs=[row(D), pl.BlockSpec((tm, 3 * LANE), lambda i: (i % n_rope_blocks, 0)),
                  const(w['g_mix']), const(w['w_in']), const(w['g_q_a']), const(w['w_uq']),
                  const(w['g_q']), const(w['g_kv_a']), const(w['w_uk']), const(w['w_uv_t'])],
        out_specs=(row(pool_w), row(hq), row(kv_rank), row(LANE), row(hq),
                   pl.BlockSpec((1, hv, tm), lambda i: (i // per_slab, 0, i % per_slab))),
        out_shape=out_shapes,
        compiler_params=_params("parallel"),
        name="in_proj",
    )(x, rope, w['g_mix'], w['w_in'], w['g_q_a'], w['w_uq'], w['g_q'], w['g_kv_a'], w['w_uk'], w['w_uv_t'])


def _pool_finish(diffs, wpool_ref, spool_ref, gout_ref, o_ref):
    ys = []
    for g, d in enumerate(diffs):
        sl = slice(g * LANE, (g + 1) * LANE)
        ys.append(_dot(d.astype(bf16), wpool_ref[g]) * spool_ref[:, sl])
    width = LANE * len(ys)
    ssq = sum(jnp.sum(y * y, axis=-1, keepdims=True) for y in ys)
    scale = lax.rsqrt(ssq * (1.0 / width) + RMS_EPS)
    for g, y in enumerate(ys):
        sl = slice(g * LANE, (g + 1) * LANE)
        o_ref[:, sl] = (y * scale * gout_ref[:, sl]).astype(o_ref.dtype)


def _pool_prompt_body(u_ref, wpool_ref, spool_ref, gout_ref, o_ref, buf, *, ts):
    j = pl.program_id(1)
    halo = POOL_STATE_LEN + 1

    @pl.when(j == 0)
    def _():
        buf[0:halo, :] = jnp.zeros((halo, buf.shape[1]), f32)

    @pl.when(j > 0)
    def _():
        buf[0:halo, :] = buf[ts:ts + halo, :]

    buf[halo:halo + ts, :] = u_ref[...]
    pos = j * ts + lax.broadcasted_iota(jnp.int32, (ts, 1), 0)
    diffs = []
    for g, wdw in enumerate(POOL_WINDOWS):
        sl = slice(g * LANE, (g + 1) * LANE)
        x = buf[halo:halo + ts, sl]
        win = x
        for k in range(1, wdw):
            win = win + buf[halo - k:halo - k + ts, sl]
        cnt = jnp.minimum(wdw, pos + 1).astype(f32)
        diffs.append(win / cnt - x)
    _pool_finish(diffs, wpool_ref, spool_ref, gout_ref, o_ref)


def _pool_prompt(u, B, S, w, ts):
    M, W = u.shape
    ns = S // ts
    const2 = lambda a: pl.BlockSpec(a.shape, lambda b, j: (0,) * a.ndim)
    return pl.pallas_call(
        functools.partial(_pool_prompt_body, ts=ts),
        grid=(B, ns),
        in_specs=[pl.BlockSpec((ts, W), lambda b, j: (b * ns + j, 0)),
                  const2(w['w_pool']), const2(w['s_pool']), const2(w['g_out_pool'])],
        out_specs=pl.BlockSpec((ts, W), lambda b, j: (b * ns + j, 0)),
        out_shape=jax.ShapeDtypeStruct((M, W), bf16),
        scratch_shapes=[pltpu.VMEM((ts + POOL_STATE_LEN + 1, W), f32)],
        compiler_params=_params("arbitrary", "arbitrary"),
        name="pool_prompt",
    )(u, w['w_pool'], w['s_pool'], w['g_out_pool'])


def _pool_sample_body(u_ref, st_ref, wpool_ref, spool_ref, gout_ref, o_ref, *, pos):
    diffs = []
    for g, wdw in enumerate(POOL_WINDOWS):
        sl = slice(g * LANE, (g + 1) * LANE)
        x = u_ref[:, sl]
        win = x
        for k in range(1, wdw):
            win = win + st_ref[POOL_STATE_LEN - k, :, sl]
        diffs.append(win / float(min(wdw, pos + 1)) - x)
    _pool_finish(diffs, wpool_ref, spool_ref, gout_ref, o_ref)


def _pool_sample(u, st_t, pos, w):
    M, W = u.shape
    return pl.pallas_call(
        functools.partial(_pool_sample_body, pos=pos),
        out_shape=jax.ShapeDtypeStruct((M, W), bf16),
        compiler_params=pltpu.CompilerParams(vmem_limit_bytes=VMEM_LIMIT),
        name="pool_sample",
    )(u, st_t, w['w_pool'], w['s_pool'], w['g_out_pool'])


def _mix_out(x, mp, attn, gatt_ref, wo_ref, gffn_ref, x1_ref, h2_ref):
    na = (_rms(attn, attn.shape[-1]) * gatt_ref[...]).astype(bf16)
    x1 = x + _dot(jnp.concatenate([mp, na], axis=1), wo_ref[...])
    x1_ref[...] = x1
    h2_ref[...] = (_rms(x1, x1.shape[-1]) * gffn_ref[...]).astype(bf16)


def _flash_body(q_ref, k_ref, vt_ref, x_ref, mp_ref, gatt_ref, wo_ref, gffn_ref, x1_ref, h2_ref,
                m_scr, acc_scr, *, tq):
    i = pl.program_id(1)
    heads = m_scr.shape[0]
    key_idx = lax.broadcasted_iota(jnp.int32, (tq, tq), 0)
    qry_idx = lax.broadcasted_iota(jnp.int32, (tq, tq), 1)
    m_scr[...] = jnp.full(m_scr.shape, -1e30, f32)
    acc_scr[...] = jnp.zeros(acc_scr.shape, f32)
    ones_rows = jnp.ones((BF16_SUBLANES, tq), bf16)

    def step(j, masked):
        start = pl.multiple_of(j * tq, tq)

        def scores(hh):
            sl = slice(hh * LANE, (hh + 1) * LANE)
            return lax.dot_general(k_ref[pl.ds(start, tq), sl], q_ref[:, sl], _NT, preferred_element_type=f32)

        ready = [scores(hh) for hh in range(min(LOOKAHEAD, heads))]
        for hh in range(heads):
            st = ready.pop(0)
            if hh + LOOKAHEAD < heads:
                ready.append(scores(hh + LOOKAHEAD))
            if masked:
                st = jnp.where(key_idx <= qry_idx, st, -1e30)
            m = m_scr[hh]
            m_new = jnp.maximum(m, jnp.max(st, axis=0, keepdims=True))
            alpha = jnp.exp2(m - m_new)
            pt = jnp.exp2((st - m_new).astype(bf16))
            vt = jnp.concatenate([vt_ref[j, hh * V_HEAD_DIM:(hh + 1) * V_HEAD_DIM, :], ones_rows], axis=0)
            acc_scr[hh] = acc_scr[hh] * alpha + _dot(vt, pt)
            m_scr[hh] = m_new

    def body(j, carry):
        step(j, False)
        return carry

    lax.fori_loop(0, i, body, 0)
    step(i, True)
    dv = V_HEAD_DIM
    out_t = jnp.concatenate([acc_scr[hh, 0:dv] / acc_scr[hh, dv:dv + 1] for hh in range(heads)], axis=0)
    _mix_out(x_ref[...], mp_ref[...], out_t.T, gatt_ref, wo_ref, gffn_ref, x1_ref, h2_ref)


def _flash(q, k, vt, x, mp, w, B, S, tq):
    M, D = x.shape
    nq = S // tq
    const = lambda a: pl.BlockSpec(a.shape, lambda b, i: (0, 0))
    tile = lambda n: pl.BlockSpec((tq, n), lambda b, i: (b * nq + i, 0))
    return pl.pallas_call(
        functools.partial(_flash_body, tq=tq),
        grid=(B, nq),
        in_specs=[tile(N_HEADS * LANE),
                  pl.BlockSpec((S, N_HEADS * LANE), lambda b, i: (b, 0)),
                  pl.BlockSpec((nq, N_HEADS * V_HEAD_DIM, tq), lambda b, i: (b, 0, 0)),
                  tile(D), tile(mp.shape[1]),
                  const(w['g_out_attn']), const(w['w_o']), const(w['g_ffn'])],
        out_specs=(tile(D), tile(D)),
        out_shape=(jax.ShapeDtypeStruct((M, D), f32), jax.ShapeDtypeStruct((M, D), bf16)),
        scratch_shapes=[pltpu.VMEM((N_HEADS, 1, tq), f32),
                        pltpu.VMEM((N_HEADS, V_HEAD_DIM + BF16_SUBLANES, tq), f32)],
        compiler_params=_params("parallel", "arbitrary"),
        name="flash_prompt",
    )(q, k, vt, x, mp, w['g_out_attn'], w['w_o'], w['g_ffn'])


def _mm_body(a_ref, b_ref, o_ref):
    o_ref[...] = _dot(a_ref[...].astype(bf16), b_ref[...])


def _mm(a, b, name):
    return pl.pallas_call(
        _mm_body,
        out_shape=jax.ShapeDtypeStruct((a.shape[0], b.shape[1]), f32),
        compiler_params=pltpu.CompilerParams(vmem_limit_bytes=VMEM_LIMIT),
        name=name,
    )(a, b)


def _decode_body(pt_ref, q_ref, knew_ref, qabs_ref, qpe_ref, cnew_ref, wukt_ref, cache_ckv, cache_kpet,
                 o_ref, ckv_buf, kpe_buf, sems, lhs, ct_scr, part_scr, num_scr, pes_scr, m_scr, l_scr, acc_scr,
                 *, n_pages, cp, nc):
    f = pl.program_id(0)
    total = pl.num_programs(0) - 1
    n_slots = ckv_buf.shape[0]
    ahead = n_slots - 1
    ring = f % n_slots
    nope_rows = wukt_ref.shape[0]
    page = cache_ckv.shape[1]
    tokens = cp * page

    def copies(chunk, slot_idx):
        base = (chunk // nc) * n_pages + (chunk % nc) * cp
        out = []
        for p in range(cp):
            phys = pt_ref[base + p]
            rows = pl.ds(p * page, page)
            out.append(pltpu.make_async_copy(cache_ckv.at[phys], ckv_buf.at[slot_idx, rows], sems.at[0, slot_idx]))
            out.append(pltpu.make_async_copy(cache_kpet.at[phys], kpe_buf.at[slot_idx, :, rows],
                                             sems.at[1, slot_idx]))
        return out

    @pl.when(f == 0)
    def _():
        for k in range(ahead):
            for cpy in copies(jnp.minimum(k, total - 1), k):
                cpy.start()
        ct_scr[1] = jnp.zeros(ct_scr.shape[1:], bf16)
        part_scr[1] = jnp.zeros(part_scr.shape[1:], bf16)
        num_scr[1] = jnp.zeros(num_scr.shape[1:], f32)
        pes_scr[1] = jnp.zeros(pes_scr.shape[1:], f32)
        m_scr[...] = jnp.zeros_like(m_scr)
        l_scr[...] = jnp.ones_like(l_scr)
        acc_scr[...] = jnp.zeros_like(acc_scr)
        lhs[0:nope_rows, :] = wukt_ref[...]

    @pl.when(f % nc == 0)
    def _():
        lhs[nope_rows:nope_rows + BF16_SUBLANES, :] = qabs_ref[0]

    @pl.when((f >= 1) & ((f - 1) % nc == 0))
    def _():
        s_new = jnp.sum(q_ref[0].astype(f32) * knew_ref[0].astype(f32), axis=-1, keepdims=True)
        m_scr[...] = s_new
        l_scr[...] = jnp.ones_like(l_scr)
        acc_scr[...] = jnp.broadcast_to(cnew_ref[0], acc_scr.shape)

    qpe = qpe_ref[0]
    ones = jnp.ones(qpe.shape, bf16)
    fold = (lax.broadcasted_iota(jnp.int32, (N_HEADS, SUBLANES * N_HEADS), 1) // SUBLANES
            == lax.broadcasted_iota(jnp.int32, (N_HEADS, SUBLANES * N_HEADS), 0)).astype(bf16)

    def main(cur):
        prev = 1 - cur
        for cpy in copies(jnp.minimum(f, total - 1), ring):
            cpy.wait()
        ssq = _dot(fold, part_scr[prev]) + pes_scr[prev]
        s = num_scr[prev] * lax.rsqrt(ssq * (1.0 / QK_HEAD_DIM) + RMS_EPS)
        m = m_scr[...]
        m_new = jnp.maximum(m, jnp.max(s, axis=-1, keepdims=True))
        alpha = jnp.exp2(m - m_new)
        p = jnp.exp2(s - m_new)
        l_scr[...] = l_scr[...] * alpha + jnp.sum(p, axis=-1, keepdims=True)
        m_scr[...] = m_new
        ct = ckv_buf[ring].astype(bf16)
        ct_scr[cur] = ct
        for cpy in copies(jnp.minimum(f + ahead, total - 1), (f + ahead) % n_slots):
            cpy.start()
        kt = lax.dot_general(lhs[...], ct, _NT, preferred_element_type=f32)
        kn = kt[0:nope_rows].reshape(N_HEADS, nope_rows // (SUBLANES * N_HEADS), SUBLANES, tokens)
        part_scr[cur] = jnp.sum(kn * kn, axis=1).reshape(SUBLANES * N_HEADS, tokens).astype(bf16)
        kp = kpe_buf[ring]
        num_scr[cur] = kt[nope_rows:nope_rows + N_HEADS] + _dot(qpe, kp.astype(bf16))
        pes_scr[cur] = _dot(ones, (kp * kp).astype(bf16))
        acc_scr[...] = acc_scr[...] * alpha + _dot(p.astype(bf16), ct_scr[prev])

    @pl.when(f % 2 == 0)
    def _():
        main(0)

    @pl.when(f % 2 == 1)
    def _():
        main(1)

    @pl.when((f >= 1) & (f % nc == 0))
    def _():
        o_ref[0] = acc_scr[...] / l_scr[...]

    @pl.when(f == total)
    def _():
        for k in range(1, ahead + 1):
            for cpy in copies(total - 1, (f + k) % n_slots):
                cpy.wait()


def _decode(page_table, q3, knew3, qabs, qpe, cnew, wukt, cache_ckv, cache_kpet, cp):
    DB, n_pages = page_table.shape
    page, kv_rank = cache_ckv.shape[1:]
    rope = cache_kpet.shape[1]
    nc = n_pages // cp
    T = cp * page
    total = DB * nc
    head_seq = lambda f, pt: jnp.minimum(f, total - 1) // nc
    tail_seq = lambda f, pt: jnp.maximum(f - 1, 0) // nc
    per_seq = lambda a, seq: pl.BlockSpec((1,) + a.shape[1:], lambda f, pt: (seq(f, pt),) + (0,) * (a.ndim - 1))
    grid_spec = pltpu.PrefetchScalarGridSpec(
        num_scalar_prefetch=1,
        grid=(total + 1,),
        in_specs=[per_seq(q3, tail_seq), per_seq(knew3, tail_seq), per_seq(qabs, head_seq), per_seq(qpe, head_seq),
                  per_seq(cnew, tail_seq), pl.BlockSpec(wukt.shape, lambda f, pt: (0, 0)),
                  pl.BlockSpec(memory_space=pl.ANY), pl.BlockSpec(memory_space=pl.ANY)],
        out_specs=pl.BlockSpec((1, N_HEADS, kv_rank), lambda f, pt: (tail_seq(f, pt), 0, 0)),
        scratch_shapes=[pltpu.VMEM((DECODE_SLOTS, T, kv_rank), f32), pltpu.VMEM((DECODE_SLOTS, rope, T), f32),
                        pltpu.SemaphoreType.DMA((2, DECODE_SLOTS)),
                        pltpu.VMEM((wukt.shape[0] + BF16_SUBLANES, kv_rank), bf16),
                        pltpu.VMEM((2, T, kv_rank), bf16), pltpu.VMEM((2, SUBLANES * N_HEADS, T), bf16),
                        pltpu.VMEM((2, N_HEADS, T), f32), pltpu.VMEM((2, N_HEADS, T), f32),
                        pltpu.VMEM((N_HEADS, 1), f32), pltpu.VMEM((N_HEADS, 1), f32),
                        pltpu.VMEM((N_HEADS, kv_rank), f32)])
    return pl.pallas_call(
        functools.partial(_decode_body, n_pages=n_pages, cp=cp, nc=nc),
        grid_spec=grid_spec,
        out_shape=jax.ShapeDtypeStruct((DB, N_HEADS, kv_rank), f32),
        compiler_params=_params("arbitrary"),
        name="decode_attn",
    )(page_table.reshape(-1), q3, knew3, qabs, qpe, cnew, wukt, cache_ckv, cache_kpet)


def _outproj_body(x_ref, mp_ref, attn_ref, gatt_ref, wo_ref, gffn_ref, x1_ref, h2_ref):
    _mix_out(x_ref[...], mp_ref[...], attn_ref[...], gatt_ref, wo_ref, gffn_ref, x1_ref, h2_ref)


def _outproj(x, mp, attn, w, tm):
    M, D = x.shape
    const = lambda a: pl.BlockSpec(a.shape, lambda i: (0, 0))
    row = lambda n: pl.BlockSpec((tm, n), lambda i: (i, 0))
    return pl.pallas_call(
        _outproj_body,
        grid=(M // tm,),
        in_specs=[row(D), row(mp.shape[1]), row(attn.shape[1]),
                  const(w['g_out_attn']), const(w['w_o']), const(w['g_ffn'])],
        out_specs=(row(D), row(D)),
        out_shape=(jax.ShapeDtypeStruct((M, D), f32), jax.ShapeDtypeStruct((M, D), bf16)),
        compiler_params=_params("parallel"),
        name="out_proj",
    )(x, mp, attn, w['g_out_attn'], w['w_o'], w['g_ffn'])


def _ffn_finish(f, g, g1, g2, h2, x1_ref, wu_ref, wd_ref, cw_ref, cb_ref, y_ref, acc):
    gate = cb_ref[...] + g2 * cw_ref[0:1, :] + g1 * cw_ref[1:2, :] + g * cw_ref[2:3, :]
    act = (gate * jax.nn.sigmoid(gate)) * _dot(h2, wu_ref[...])
    part = _dot(act.astype(bf16), wd_ref[...])

    @pl.when(f == 0)
    def _():
        acc[...] = x1_ref[...] + part

    @pl.when(f > 0)
    def _():
        acc[...] += part

    @pl.when(f == pl.num_programs(1) - 1)
    def _():
        y_ref[...] = acc[...]


def _ffn_prompt_body(h2_ref, halo_ref, x1_ref, wg_ref, wu_ref, wd_ref, cw_ref, cb_ref,
                     y_ref, tail_ref, acc, gbuf, *, tm, tiles_per_seq):
    i = pl.program_id(0)
    f = pl.program_id(1)
    hb = BF16_SUBLANES
    h2 = h2_ref[...]
    rows = lax.broadcasted_iota(jnp.int32, (tm + hb, 1), 0)
    keep = jnp.logical_or(rows >= hb, i % tiles_per_seq != 0)
    gbuf[...] = jnp.where(keep, _dot(jnp.concatenate([halo_ref[...], h2], axis=0), wg_ref[...]), 0.0)
    g = gbuf[hb:hb + tm, :]
    tail_ref[0] = gbuf[tm:tm + hb, :]
    g1 = gbuf[hb - 1:hb - 1 + tm, :]
    g2 = gbuf[hb - 2:hb - 2 + tm, :]
    _ffn_finish(f, g, g1, g2, h2, x1_ref, wu_ref, wd_ref, cw_ref, cb_ref, y_ref, acc)


def _ffn_prompt(h2, x1, S, w, tm, tf):
    M, D = x1.shape
    F = w['w_gate'].shape[1]
    hb = BF16_SUBLANES
    nt = M // tm
    halo_blocks = tm // hb
    resident = dict(pipeline_mode=pl.Buffered(1)) if tf == F else {}
    y, tail = pl.pallas_call(
        functools.partial(_ffn_prompt_body, tm=tm, tiles_per_seq=S // tm),
        grid=(nt, F // tf),
        in_specs=[pl.BlockSpec((tm, D), lambda i, f: (i, 0)),
                  pl.BlockSpec((hb, D), lambda i, f: (jnp.maximum(i * halo_blocks - 1, 0), 0)),
                  pl.BlockSpec((tm, D), lambda i, f: (i, 0)),
                  pl.BlockSpec((D, tf), lambda i, f: (0, f), **resident),
                  pl.BlockSpec((D, tf), lambda i, f: (0, f), **resident),
                  pl.BlockSpec((tf, D), lambda i, f: (f, 0), **resident),
                  pl.BlockSpec((CONV_WIDTH, tf), lambda i, f: (0, f)),
                  pl.BlockSpec((1, tf), lambda i, f: (0, f))],
        out_specs=(pl.BlockSpec((tm, D), lambda i, f: (i, 0)),
                   pl.BlockSpec((1, hb, tf), lambda i, f: (i, 0, f))),
        out_shape=(jax.ShapeDtypeStruct((M, D), f32), jax.ShapeDtypeStruct((nt, hb, F), f32)),
        scratch_shapes=[pltpu.VMEM((tm, D), f32), pltpu.VMEM((tm + hb, tf), f32)],
        compiler_params=_params("parallel", "arbitrary"),
        name="ffn_prompt",
    )(h2, h2, x1, w['w_gate'], w['w_up'], w['w_down'], w['conv_w'], w['conv_b'])
    return y, tail


def _ffn_sample_body(h2_ref, st_ref, x1_ref, wg_ref, wu_ref, wd_ref, cw_ref, cb_ref, y_ref, g_ref, acc):
    f = pl.program_id(1)
    h2 = h2_ref[...]
    g = _dot(h2, wg_ref[...])
    g_ref[...] = g
    _ffn_finish(f, g, st_ref[1], st_ref[0], h2, x1_ref, wu_ref, wd_ref, cw_ref, cb_ref, y_ref, acc)


def _ffn_sample(h2, x1, st_t, w, tf):
    M, D = x1.shape
    F = w['w_gate'].shape[1]
    return pl.pallas_call(
        _ffn_sample_body,
        grid=(1, F // tf),
        in_specs=[pl.BlockSpec((M, D), lambda i, f: (0, 0)),
                  pl.BlockSpec((CONV_WIDTH - 1, M, tf), lambda i, f: (0, 0, f)),
                  pl.BlockSpec((M, D), lambda i, f: (0, 0)),
                  pl.BlockSpec((D, tf), lambda i, f: (0, f)),
                  pl.BlockSpec((D, tf), lambda i, f: (0, f)),
                  pl.BlockSpec((tf, D), lambda i, f: (f, 0)),
                  pl.BlockSpec((CONV_WIDTH, tf), lambda i, f: (0, f)),
                  pl.BlockSpec((1, tf), lambda i, f: (0, f))],
        out_specs=(pl.BlockSpec((M, D), lambda i, f: (0, 0)),
                   pl.BlockSpec((M, tf), lambda i, f: (0, f))),
        out_shape=(jax.ShapeDtypeStruct((M, D), f32), jax.ShapeDtypeStruct((M, F), f32)),
        scratch_shapes=[pltpu.VMEM((M, D), f32)],
        compiler_params=_params("parallel", "arbitrary"),
        name="ffn_sample",
    )(h2, st_t, x1, w['w_gate'], w['w_up'], w['w_down'], w['conv_w'], w['conv_b'])


def _qk_gain(g):
    g_pe = g[QK_NOPE_DIM:]
    return jnp.concatenate([g[:QK_NOPE_DIM], g_pe, g_pe])


def _rope_table(pos):
    half = QK_ROPE_DIM // 2
    inv = ROPE_BASE ** (-jnp.arange(0, QK_ROPE_DIM, 2, dtype=f32) / QK_ROPE_DIM)
    inv_lane = jnp.concatenate([jnp.zeros((QK_NOPE_DIM,), f32), inv, inv, jnp.zeros((LANE - QK_HEAD_DIM,), f32)])
    ang = pos.astype(f32)[:, None] * inv_lane[None, :]
    cos, sin = jnp.cos(ang), jnp.sin(ang)
    first_half = jnp.arange(LANE)[None, :] < QK_NOPE_DIM + half
    return jnp.concatenate([cos, jnp.where(first_half, -sin, 0.0), jnp.where(first_half, 0.0, sin)], axis=1)


def _prep_weights(g_mix, w_in, g_q_a, w_uq, g_qn, g_kv_a, w_uk, g_kn, w_uv, w_pool, s_pool,
                  g_out, w_o, g_ffn, w_gate, w_up, conv_w, conv_b, w_down):
    D = w_in.shape[0]
    q_rank = g_q_a.shape[0]
    kv_rank = g_kv_a.shape[0]
    pool_w = s_pool.shape[0]
    head_pad = LANE - QK_HEAD_DIM
    half = QK_ROPE_DIM // 2
    main = pool_w + q_rank + kv_rank
    w_in_p = jnp.concatenate([w_in[:, :main], jnp.zeros((D, QK_NOPE_DIM), f32), w_in[:, main:],
                              jnp.zeros((D, head_pad), f32)], axis=1).astype(bf16)
    w_uq_h = w_uq.reshape(q_rank, N_HEADS, QK_HEAD_DIM)
    w_uq_p = jnp.pad(w_uq_h, ((0, 0), (0, 0), (0, head_pad)))
    lo = w_uq_h[:, :, QK_NOPE_DIM:QK_NOPE_DIM + half]
    hi = w_uq_h[:, :, QK_NOPE_DIM + half:]
    w_uq_rot = jnp.concatenate([jnp.zeros((q_rank, N_HEADS, QK_NOPE_DIM), f32), -hi, lo,
                                jnp.zeros((q_rank, N_HEADS, head_pad), f32)], axis=2)
    w_uq_p = jnp.concatenate([w_uq_p.reshape(q_rank, N_HEADS * LANE),
                              w_uq_rot.reshape(q_rank, N_HEADS * LANE)], axis=1).astype(bf16)
    gq = _qk_gain(g_qn) * _qk_gain(g_kn) * (QK_HEAD_DIM ** -0.5 * LOG2_E)
    gq = jnp.tile(jnp.pad(gq, (0, head_pad)), N_HEADS)[None, :]
    w_uk_p = jnp.pad(w_uk, ((0, 0), (0, 0), (0, LANE - QK_NOPE_DIM))).reshape(kv_rank, N_HEADS * LANE)
    w_uk_p = w_uk_p.astype(bf16)
    w_uv_t = w_uv.reshape(kv_rank, N_HEADS * V_HEAD_DIM).T.astype(bf16)
    w_uk_t = w_uk.reshape(kv_rank, N_HEADS * QK_NOPE_DIM).T.astype(bf16)
    eye = jnp.eye(N_HEADS, dtype=f32)
    w_uk_bd = jnp.einsum('chd,hg->hdgc', jnp.pad(w_uk, ((0, 0), (0, 0), (0, LANE - QK_NOPE_DIM))), eye)
    w_uk_bd = w_uk_bd.reshape(N_HEADS * LANE, N_HEADS * kv_rank).astype(bf16)
    w_uv_bd = jnp.einsum('chv,hg->hcgv', w_uv, eye).reshape(N_HEADS * kv_rank, N_HEADS * V_HEAD_DIM).astype(bf16)
    return {
        'pool_w': pool_w,
        'g_mix': g_mix[None, :], 'w_in': w_in_p, 'g_q_a': g_q_a[None, :], 'w_uq': w_uq_p, 'g_q': gq,
        'g_kv_a': g_kv_a[None, :], 'w_uk': w_uk_p, 'w_uv_t': w_uv_t, 'w_uk_t': w_uk_t, 'w_uk_bd': w_uk_bd, 'w_uv_bd': w_uv_bd,
        'w_pool': w_pool.astype(bf16), 's_pool': s_pool[None, :],
        'g_out_pool': g_out[None, :pool_w], 'g_out_attn': g_out[None, pool_w:],
        'w_o': w_o.astype(bf16), 'g_ffn': g_ffn[None, :],
        'w_gate': w_gate.astype(bf16), 'w_up': w_up.astype(bf16), 'w_down': w_down.astype(bf16),
        'conv_w': conv_w, 'conv_b': conv_b[None, :],
    }


def _tile(n, pref):
    t = min(n, pref)
    assert n % t == 0, (n, pref)
    return t


def _ffn_cols(F):
    half = F // 2
    return half if half % LANE == 0 else F


def kernel(x_prompt, x_sample, cache_ckv, cache_kpe, state_pool, state_conv, page_table, g_mix, w_in, g_q_a,
           w_uq, g_qn, g_kv_a, w_uk, g_kn, w_uv, w_pool, s_pool, g_out, w_o, g_ffn, w_gate, w_up, conv_w,
           conv_b, w_down):
    depth = g_mix.shape[0]
    assert depth == 1, "single-layer trunk only"
    B, S, D = x_prompt.shape
    DB, DS = x_sample.shape[:2]
    assert DS == 1, "one new token per sequence"
    n_pages = page_table.shape[1]
    page = cache_ckv.shape[2]
    past_len = n_pages * page
    l = 0
    w = _prep_weights(g_mix[l], w_in[l], g_q_a[l], w_uq[l], g_qn[l], g_kv_a[l], w_uk[l], g_kn[l], w_uv[l],
                      w_pool[l], s_pool[l], g_out[l], w_o[l], g_ffn[l], w_gate[l], w_up[l], conv_w[l],
                      conv_b[l], w_down[l])
    kv_rank = g_kv_a.shape[1]
    F = w_gate.shape[2]
    tf = _ffn_cols(F)

    xp = x_prompt.reshape(B * S, D)
    tq = _tile(S, 512)
    tm = _tile(tq, 512)
    rope_p = _rope_table(jnp.arange(S, dtype=jnp.int32))
    u_p, q_p, ckv_p, kpe_p, k_p, vt_p = _proj(xp, rope_p, S // tm, w, tm, tq)
    mp_p = _pool_prompt(u_p, B, S, w, _tile(S, 512))
    x1_p, h2_p = _flash(q_p, k_p, vt_p, xp, mp_p, w, B, S, tq)
    tmf = _tile(S, 512)
    y_p, tail = _ffn_prompt(h2_p, x1_p, S, w, tmf, F)
    tiles_per_seq = S // tmf
    conv_p = tail[tiles_per_seq - 1::tiles_per_seq, BF16_SUBLANES - (CONV_WIDTH - 1):, :]

    xs = x_sample.reshape(DB, D)
    pos_s = past_len + jnp.arange(DS, dtype=jnp.int32)
    rope_s = jnp.tile(_rope_table(pos_s), (DB, 1))
    u_s, q_s, ckv_s, kpe_s, k_s, _ = _proj(xs, rope_s, 1, w, DB, DB)
    st_pool_t = jnp.transpose(state_pool[l], (1, 0, 2))
    mp_s = _pool_sample(u_s, st_pool_t, past_len, w)
    qabs = _mm(q_s, w['w_uk_bd'], "absorb_q").reshape(DB, N_HEADS, kv_rank).astype(bf16)
    qabs = jnp.pad(qabs, ((0, 0), (0, BF16_SUBLANES - N_HEADS), (0, 0)))
    q3 = q_s.reshape(DB, N_HEADS, LANE)
    qpe = q3[:, :, QK_NOPE_DIM:QK_HEAD_DIM]
    cp = _tile(n_pages, 32)
    ctx = _decode(page_table, q3, k_s.reshape(DB, N_HEADS, LANE), qabs, qpe, ckv_s.reshape(DB, 1, kv_rank),
                  w['w_uk_t'], cache_ckv[l], jnp.swapaxes(cache_kpe[l], 1, 2), cp)
    attn_s = _mm(ctx.reshape(DB, N_HEADS * kv_rank), w['w_uv_bd'], "value_up")
    x1_s, h2_s = _outproj(xs, mp_s, attn_s, w, DB)
    st_conv_t = jnp.transpose(state_conv[l], (1, 0, 2))
    y_s, g_s = _ffn_sample(h2_s, x1_s, st_conv_t, w, tf)

    kpe_sl = slice(QK_NOPE_DIM, QK_HEAD_DIM)
    P = POOL_STATE_LEN
    return (
        y_p.reshape(B, S, D),
        y_s.reshape(DB, DS, D),
        ckv_p.reshape(1, B, S, kv_rank),
        kpe_p[:, kpe_sl].reshape(1, B, S, QK_ROPE_DIM),
        u_p.reshape(B, S, -1)[None, :, S - P:, :],
        conv_p[None],
        ckv_s.reshape(1, DB, DS, kv_rank),
        kpe_s[:, kpe_sl].reshape(1, DB, DS, QK_ROPE_DIM),
        jnp.concatenate([state_pool[l], u_s[:, None, :]], axis=1)[None, :, -P:, :],
        jnp.concatenate([state_conv[l], g_s[:, None, :]], axis=1)[None, :, -(CONV_WIDTH - 1):, :],
    )
```

```python
import functools

import jax
import jax.numpy as jnp
from jax import lax
from jax.experimental import pallas as pl
from jax.experimental.pallas import tpu as pltpu

N_HEADS = 8
QK_NOPE_DIM = 64
QK_ROPE_DIM = 32
QK_HEAD_DIM = QK_NOPE_DIM + QK_ROPE_DIM
V_HEAD_DIM = 64
POOL_WINDOWS = (2, 4, 8, 16)
POOL_STATE_LEN = max(POOL_WINDOWS) - 1
CONV_WIDTH = 3
ROPE_BASE = 10000.0
RMS_EPS = 1e-6
LOG2_E = 1.4426950408889634

LANE = 128
SUBLANES = 8
MXU_DIM = 256
BF16_SUBLANES = 16
VMEM_LIMIT = 48 * 1024 * 1024
DECODE_SLOTS = 3
LOOKAHEAD = 1

_NT = (((1,), (1,)), ((), ()))

bf16 = jnp.bfloat16
f32 = jnp.float32


def _rms(x, width):
    return x * lax.rsqrt(jnp.sum(x * x, axis=-1, keepdims=True) * (1.0 / width) + RMS_EPS)


def _dot(a, b):
    return jnp.dot(a, b, preferred_element_type=f32)


def _params(*sem):
    return pltpu.CompilerParams(dimension_semantics=sem, vmem_limit_bytes=VMEM_LIMIT)


def _proj_body(x_ref, rope_ref, gmix_ref, win_ref, gqa_ref, wuq_ref, gq_ref, gkva_ref, wuk_ref, wuvt_ref,
               u_ref, q_ref, ckv_ref, kpe_ref, k_ref, vt_ref, *, pool_w, q_rank, kv_rank):
    tm = x_ref.shape[0]
    group = min(tm, MXU_DIM)
    groups = [slice(r, r + group) for r in range(0, tm, group)]
    half = QK_ROPE_DIM // 2
    hq = N_HEADS * LANE

    def in_projection(rows):
        x = x_ref[rows, :]
        h = (_rms(x, x.shape[-1]) * gmix_ref[...]).astype(bf16)
        return _dot(h, win_ref[...])

    def up_projections(rows, proj):
        u_ref[rows, :] = proj[:, :pool_w]
        o = pool_w
        cq = (_rms(proj[:, o:o + q_rank], q_rank) * gqa_ref[...]).astype(bf16)
        o += q_rank
        ckv = _rms(proj[:, o:o + kv_rank], kv_rank) * gkva_ref[...]
        o += kv_rank
        ckv_ref[rows, :] = ckv
        ckv_b = ckv.astype(bf16)
        qraw = _dot(cq, wuq_ref[...])
        kn = _dot(ckv_b, wuk_ref[...])
        vt_ref[0, :, rows] = lax.dot_general(wuvt_ref[...], ckv_b, _NT, preferred_element_type=f32).astype(bf16)
        return proj[:, o:o + LANE], qraw, kn

    def head_norms(rows, raw_kpe, qraw, kn):
        cos = rope_ref[rows, 0:LANE]
        sin_lo = rope_ref[rows, LANE:2 * LANE]
        sin_hi = rope_ref[rows, 2 * LANE:3 * LANE]
        sin_abs = sin_hi - sin_lo
        kpe = raw_kpe * cos + pltpu.roll(raw_kpe, LANE - half, 1) * sin_lo + pltpu.roll(raw_kpe, half, 1) * sin_hi
        kpe_ref[rows, :] = kpe
        for hd in range(N_HEADS):
            sl = slice(hd * LANE, (hd + 1) * LANE)
            qh = qraw[:, sl] * cos + qraw[:, hq + hd * LANE:hq + (hd + 1) * LANE] * sin_abs
            q_ref[rows, sl] = (_rms(qh, QK_HEAD_DIM) * gq_ref[:, sl]).astype(bf16)
            kh = kn[:, sl] + kpe
            k_ref[rows, sl] = _rms(kh, QK_HEAD_DIM).astype(bf16)

    projs = [in_projection(rows) for rows in groups]
    ups = [up_projections(rows, proj) for rows, proj in zip(groups, projs)]
    for rows, up in zip(groups, ups):
        head_norms(rows, *up)


def _proj(x, rope, n_rope_blocks, w, tm, slab):
    M, D = x.shape
    pool_w = w['pool_w']
    q_rank = w['g_q_a'].shape[1]
    kv_rank = w['g_kv_a'].shape[1]
    hq = N_HEADS * LANE
    hv = N_HEADS * V_HEAD_DIM
    per_slab = slab // tm
    const = lambda a: pl.BlockSpec(a.shape, lambda i: (0, 0))
    row = lambda n: pl.BlockSpec((tm, n), lambda i: (i, 0))
    out_shapes = (
        jax.ShapeDtypeStruct((M, pool_w), f32), jax.ShapeDtypeStruct((M, hq), bf16),
        jax.ShapeDtypeStruct((M, kv_rank), f32), jax.ShapeDtypeStruct((M, LANE), f32),
        jax.ShapeDtypeStruct((M, hq), bf16), jax.ShapeDtypeStruct((M // slab, hv, slab), bf16))
    return pl.pallas_call(
        functools.partial(_proj_body, pool_w=pool_w, q_rank=q_rank, kv_rank=kv_rank),
        grid=(M // tm,),
        in_specs=[row(D), pl.BlockSpec((tm, 3 * LANE), lambda i: (i % n_rope_blocks, 0)),
                  const(w['g_mix']), const(w['w_in']), const(w['g_q_a']), const(w['w_uq']),
                  const(w['g_q']), const(w['g_kv_a']), const(w['w_uk']), const(w['w_uv_t'])],
        out_specs=(row(pool_w), row(hq), row(kv_rank), row(LANE), row(hq),
                   pl.BlockSpec((1, hv, tm), lambda i: (i // per_slab, 0, i % per_slab))),
        out_shape=out_shapes,
        compiler_params=_params("parallel"),
        name="in_proj",
    )(x, rope, w['g_mix'], w['w_in'], w['g_q_a'], w['w_uq'], w['g_q'], w['g_kv_a'], w['w_uk'], w['w_uv_t'])


def _pool_finish(diffs, wpool_ref, spool_ref, gout_ref, o_ref):
    ys = []
    for g, d in enumerate(diffs):
        sl = slice(g * LANE, (g + 1) * LANE)
        ys.append(_dot(d.astype(bf16), wpool_ref[g]) * spool_ref[:, sl])
    width = LANE * len(ys)
    ssq = sum(jnp.sum(y * y, axis=-1, keepdims=True) for y in ys)
    scale = lax.rsqrt(ssq * (1.0 / width) + RMS_EPS)
    for g, y in enumerate(ys):
        sl = slice(g * LANE, (g + 1) * LANE)
        o_ref[:, sl] = (y * scale * gout_ref[:, sl]).astype(o_ref.dtype)


def _pool_prompt_body(u_ref, wpool_ref, spool_ref, gout_ref, o_ref, buf, *, ts):
    j = pl.program_id(1)
    halo = POOL_STATE_LEN + 1

    @pl.when(j == 0)
    def _():
        buf[0:halo, :] = jnp.zeros((halo, buf.shape[1]), f32)

    @pl.when(j > 0)
    def _():
        buf[0:halo, :] = buf[ts:ts + halo, :]

    buf[halo:halo + ts, :] = u_ref[...]
    pos = j * ts + lax.broadcasted_iota(jnp.int32, (ts, 1), 0)
    diffs = []
    for g, wdw in enumerate(POOL_WINDOWS):
        sl = slice(g * LANE, (g + 1) * LANE)
        x = buf[halo:halo + ts, sl]
        win = x
        for k in range(1, wdw):
            win = win + buf[halo - k:halo - k + ts, sl]
        cnt = jnp.minimum(wdw, pos + 1).astype(f32)
        diffs.append(win / cnt - x)
    _pool_finish(diffs, wpool_ref, spool_ref, gout_ref, o_ref)


def _pool_prompt(u, B, S, w, ts):
    M, W = u.shape
    ns = S // ts
    const2 = lambda a: pl.BlockSpec(a.shape, lambda b, j: (0,) * a.ndim)
    return pl.pallas_call(
        functools.partial(_pool_prompt_body, ts=ts),
        grid=(B, ns),
        in_specs=[pl.BlockSpec((ts, W), lambda b, j: (b * ns + j, 0)),
                  const2(w['w_pool']), const2(w['s_pool']), const2(w['g_out_pool'])],
        out_specs=pl.BlockSpec((ts, W), lambda b, j: (b * ns + j, 0)),
        out_shape=jax.ShapeDtypeStruct((M, W), bf16),
        scratch_shapes=[pltpu.VMEM((ts + POOL_STATE_LEN + 1, W), f32)],
        compiler_params=_params("arbitrary", "arbitrary"),
        name="pool_prompt",
    )(u, w['w_pool'], w['s_pool'], w['g_out_pool'])


def _pool_sample_body(u_ref, st_ref, wpool_ref, spool_ref, gout_ref, o_ref, *, pos):
    diffs = []
    for g, wdw in enumerate(POOL_WINDOWS):
        sl = slice(g * LANE, (g + 1) * LANE)
        x = u_ref[:, sl]
        win = x
        for k in range(1, wdw):
            win = win + st_ref[POOL_STATE_LEN - k, :, sl]
        diffs.append(win / float(min(wdw, pos + 1)) - x)
    _pool_finish(diffs, wpool_ref, spool_ref, gout_ref, o_ref)


def _pool_sample(u, st_t, pos, w):
    M, W = u.shape
    return pl.pallas_call(
        functools.partial(_pool_sample_body, pos=pos),
        out_shape=jax.ShapeDtypeStruct((M, W), bf16),
        compiler_params=pltpu.CompilerParams(vmem_limit_bytes=VMEM_LIMIT),
        name="pool_sample",
    )(u, st_t, w['w_pool'], w['s_pool'], w['g_out_pool'])


def _mix_out(x, mp, attn, gatt_ref, wo_ref, gffn_ref, x1_ref, h2_ref):
    na = (_rms(attn, attn.shape[-1]) * gatt_ref[...]).astype(bf16)
    x1 = x + _dot(jnp.concatenate([mp, na], axis=1), wo_ref[...])
    x1_ref[...] = x1
    h2_ref[...] = (_rms(x1, x1.shape[-1]) * gffn_ref[...]).astype(bf16)


def _flash_body(q_ref, k_ref, vt_ref, x_ref, mp_ref, gatt_ref, wo_ref, gffn_ref, x1_ref, h2_ref,
                m_scr, acc_scr, *, tq):
    i = pl.program_id(1)
    heads = m_scr.shape[0]
    key_idx = lax.broadcasted_iota(jnp.int32, (tq, tq), 0)
    qry_idx = lax.broadcasted_iota(jnp.int32, (tq, tq), 1)
    m_scr[...] = jnp.full(m_scr.shape, -1e30, f32)
    acc_scr[...] = jnp.zeros(acc_scr.shape, f32)
    ones_rows = jnp.ones((BF16_SUBLANES, tq), bf16)

    def step(j, masked):
        start = pl.multiple_of(j * tq, tq)

        def scores(hh):
            sl = slice(hh * LANE, (hh + 1) * LANE)
            return lax.dot_general(k_ref[pl.ds(start, tq), sl], q_ref[:, sl], _NT, preferred_element_type=f32)

        ready = [scores(hh) for hh in range(min(LOOKAHEAD, heads))]
        for hh in range(heads):
            st = ready.pop(0)
            if hh + LOOKAHEAD < heads:
                ready.append(scores(hh + LOOKAHEAD))
            if masked:
                st = jnp.where(key_idx <= qry_idx, st, -1e30)
            m = m_scr[hh]
            m_new = jnp.maximum(m, jnp.max(st, axis=0, keepdims=True))
            alpha = jnp.exp2(m - m_new)
            pt = jnp.exp2((st - m_new).astype(bf16))
            vt = jnp.concatenate([vt_ref[j, hh * V_HEAD_DIM:(hh + 1) * V_HEAD_DIM, :], ones_rows], axis=0)
            acc_scr[hh] = acc_scr[hh] * alpha + _dot(vt, pt)
            m_scr[hh] = m_new

    def body(j, carry):
        step(j, False)
        return carry

    lax.fori_loop(0, i, body, 0)
    step(i, True)
    dv = V_HEAD_DIM
    out_t = jnp.concatenate([acc_scr[hh, 0:dv] / acc_scr[hh, dv:dv + 1] for hh in range(heads)], axis=0)
    _mix_out(x_ref[...], mp_ref[...], out_t.T, gatt_ref, wo_ref, gffn_ref, x1_ref, h2_ref)


def _flash(q, k, vt, x, mp, w, B, S, tq):
    M, D = x.shape
    nq = S // tq
    const = lambda a: pl.BlockSpec(a.shape, lambda b, i: (0, 0))
    tile = lambda n: pl.BlockSpec((tq, n), lambda b, i: (b * nq + i, 0))
    return pl.pallas_call(
        functools.partial(_flash_body, tq=tq),
        grid=(B, nq),
        in_specs=[tile(N_HEADS * LANE),
                  pl.BlockSpec((S, N_HEADS * LANE), lambda b, i: (b, 0)),
                  pl.BlockSpec((nq, N_HEADS * V_HEAD_DIM, tq), lambda b, i: (b, 0, 0)),
                  tile(D), tile(mp.shape[1]),
                  const(w['g_out_attn']), const(w['w_o']), const(w['g_ffn'])],
        out_specs=(tile(D), tile(D)),
        out_shape=(jax.ShapeDtypeStruct((M, D), f32), jax.ShapeDtypeStruct((M, D), bf16)),
        scratch_shapes=[pltpu.VMEM((N_HEADS, 1, tq), f32),
                        pltpu.VMEM((N_HEADS, V_HEAD_DIM + BF16_SUBLANES, tq), f32)],
        compiler_params=_params("parallel", "arbitrary"),
        name="flash_prompt",
    )(q, k, vt, x, mp, w['g_out_attn'], w['w_o'], w['g_ffn'])


def _mm_body(a_ref, b_ref, o_ref):
    o_ref[...] = _dot(a_ref[...].astype(bf16), b_ref[...])


def _mm(a, b, name):
    return pl.pallas_call(
        _mm_body,
        out_shape=jax.ShapeDtypeStruct((a.shape[0], b.shape[1]), f32),
        compiler_params=pltpu.CompilerParams(vmem_limit_bytes=VMEM_LIMIT),
        name=name,
    )(a, b)


def _decode_body(pt_ref, q_ref, knew_ref, qabs_ref, qpe_ref, cnew_ref, wukt_ref, cache_ckv, cache_kpet,
                 o_ref, ckv_buf, kpe_buf, sems, lhs, ct_scr, part_scr, num_scr, pes_scr, p_scr, alpha_scr,
                 m_scr, l_scr, lfin_scr, acc_scr, *, n_pages, cp, nc):
    f = pl.program_id(0)
    total = pl.num_programs(0) - 2
    n_slots = ckv_buf.shape[0]
    ahead = n_slots - 1
    ring = f % n_slots
    nope_rows = wukt_ref.shape[0]
    page = cache_ckv.shape[1]
    tokens = cp * page
    ct_new, ct_old = f % 3, (f + 1) % 3
    sc_new, sc_old = f % 2, (f + 1) % 2
    pr_new, pr_old = (f + 1) % 2, f % 2

    def copies(chunk, slot_idx):
        base = (chunk // nc) * n_pages + (chunk % nc) * cp
        out = []
        for p in range(cp):
            phys = pt_ref[base + p]
            rows = pl.ds(p * page, page)
            out.append(pltpu.make_async_copy(cache_ckv.at[phys], ckv_buf.at[slot_idx, rows], sems.at[0, slot_idx]))
            out.append(pltpu.make_async_copy(cache_kpet.at[phys], kpe_buf.at[slot_idx, :, rows],
                                             sems.at[1, slot_idx]))
        return out

    @pl.when(f == 0)
    def _():
        for k in range(ahead):
            for cpy in copies(jnp.minimum(k, total - 1), k):
                cpy.start()
        ct_scr[...] = jnp.zeros(ct_scr.shape, bf16)
        part_scr[...] = jnp.zeros(part_scr.shape, bf16)
        num_scr[...] = jnp.zeros(num_scr.shape, f32)
        pes_scr[...] = jnp.zeros(pes_scr.shape, f32)
        p_scr[...] = jnp.zeros(p_scr.shape, bf16)
        alpha_scr[...] = jnp.ones(alpha_scr.shape, f32)
        m_scr[...] = jnp.zeros_like(m_scr)
        l_scr[...] = jnp.ones_like(l_scr)
        lfin_scr[...] = jnp.ones_like(lfin_scr)
        acc_scr[...] = jnp.zeros_like(acc_scr)
        lhs[0:nope_rows, :] = wukt_ref[...]

    @pl.when(f % nc == 0)
    def _():
        lhs[nope_rows:nope_rows + BF16_SUBLANES, :] = qabs_ref[0]

    @pl.when((f - 1) % nc == 0)
    def _():
        m_scr[...] = jnp.sum(q_ref[0].astype(f32) * knew_ref[0].astype(f32), axis=-1, keepdims=True)
        l_scr[...] = jnp.ones_like(l_scr)

    @pl.when((f - 2) % nc == 0)
    def _():
        acc_scr[...] = jnp.broadcast_to(cnew_ref[0], acc_scr.shape)

    qpe = qpe_ref[0]
    ones = jnp.ones(qpe.shape, bf16)
    fold = (lax.broadcasted_iota(jnp.int32, (N_HEADS, SUBLANES * N_HEADS), 1) // SUBLANES
            == lax.broadcasted_iota(jnp.int32, (N_HEADS, SUBLANES * N_HEADS), 0)).astype(bf16)

    for cpy in copies(jnp.minimum(f, total - 1), ring):
        cpy.wait()

    acc_scr[...] = acc_scr[...] * alpha_scr[pr_old] + _dot(p_scr[pr_old], ct_scr[ct_old])

    ssq = _dot(fold, part_scr[sc_old]) + pes_scr[sc_old]
    s = num_scr[sc_old] * lax.rsqrt(ssq * (1.0 / QK_HEAD_DIM) + RMS_EPS)
    m = m_scr[...]
    m_new = jnp.maximum(m, jnp.max(s, axis=-1, keepdims=True))
    alpha = jnp.exp2(m - m_new)
    p = jnp.exp2(s - m_new)
    l_new = l_scr[...] * alpha + jnp.sum(p, axis=-1, keepdims=True)
    l_scr[...] = l_new
    m_scr[...] = m_new
    alpha_scr[pr_new] = alpha
    p_scr[pr_new] = p.astype(bf16)

    ct = ckv_buf[ring].astype(bf16)
    ct_scr[ct_new] = ct
    for cpy in copies(jnp.minimum(f + ahead, total - 1), (f + ahead) % n_slots):
        cpy.start()
    kt = lax.dot_general(lhs[...], ct, _NT, preferred_element_type=f32)
    kn = kt[0:nope_rows].reshape(N_HEADS, nope_rows // (SUBLANES * N_HEADS), SUBLANES, tokens)
    part_scr[sc_new] = jnp.sum(kn * kn, axis=1).reshape(SUBLANES * N_HEADS, tokens).astype(bf16)
    kp = kpe_buf[ring]
    num_scr[sc_new] = kt[nope_rows:nope_rows + N_HEADS] + _dot(qpe, kp.astype(bf16))
    pes_scr[sc_new] = _dot(ones, (kp * kp).astype(bf16))

    @pl.when((f >= 2) & ((f - 2) % nc == nc - 1))
    def _():
        o_ref[0] = acc_scr[...] / lfin_scr[...]

    @pl.when((f - 1) % nc == nc - 1)
    def _():
        lfin_scr[...] = l_scr[...]

    @pl.when(f == total + 1)
    def _():
        for k in range(1, ahead + 1):
            for cpy in copies(total - 1, (f + k) % n_slots):
                cpy.wait()


def _decode(page_table, q3, knew3, qabs, qpe, cnew, wukt, cache_ckv, cache_kpet, cp):
    DB, n_pages = page_table.shape
    page, kv_rank = cache_ckv.shape[1:]
    rope = cache_kpet.shape[1]
    nc = n_pages // cp
    T = cp * page
    total = DB * nc
    seq_of = lambda lag: (lambda f, pt: jnp.clip(f - lag, 0, total - 1) // nc)
    per_seq = lambda a, lag: pl.BlockSpec((1,) + a.shape[1:],
                                          lambda f, pt: (seq_of(lag)(f, pt),) + (0,) * (a.ndim - 1))
    grid_spec = pltpu.PrefetchScalarGridSpec(
        num_scalar_prefetch=1,
        grid=(total + 2,),
        in_specs=[per_seq(q3, 1), per_seq(knew3, 1), per_seq(qabs, 0), per_seq(qpe, 0), per_seq(cnew, 2),
                  pl.BlockSpec(wukt.shape, lambda f, pt: (0, 0)),
                  pl.BlockSpec(memory_space=pl.ANY), pl.BlockSpec(memory_space=pl.ANY)],
        out_specs=pl.BlockSpec((1, N_HEADS, kv_rank), lambda f, pt: (seq_of(2)(f, pt), 0, 0)),
        scratch_shapes=[pltpu.VMEM((DECODE_SLOTS, T, kv_rank), f32), pltpu.VMEM((DECODE_SLOTS, rope, T), f32),
                        pltpu.SemaphoreType.DMA((2, DECODE_SLOTS)),
                        pltpu.VMEM((wukt.shape[0] + BF16_SUBLANES, kv_rank), bf16),
                        pltpu.VMEM((3, T, kv_rank), bf16), pltpu.VMEM((2, SUBLANES * N_HEADS, T), bf16),
                        pltpu.VMEM((2, N_HEADS, T), f32), pltpu.VMEM((2, N_HEADS, T), f32),
                        pltpu.VMEM((2, N_HEADS, T), bf16), pltpu.VMEM((2, N_HEADS, 1), f32),
                        pltpu.VMEM((N_HEADS, 1), f32), pltpu.VMEM((N_HEADS, 1), f32), pltpu.VMEM((N_HEADS, 1), f32),
                        pltpu.VMEM((N_HEADS, kv_rank), f32)])
    return pl.pallas_call(
        functools.partial(_decode_body, n_pages=n_pages, cp=cp, nc=nc),
        grid_spec=grid_spec,
        out_shape=jax.ShapeDtypeStruct((DB, N_HEADS, kv_rank), f32),
        compiler_params=_params("arbitrary"),
        name="decode_attn",
    )(page_table.reshape(-1), q3, knew3, qabs, qpe, cnew, wukt, cache_ckv, cache_kpet)


def _outproj_body(x_ref, mp_ref, attn_ref, gatt_ref, wo_ref, gffn_ref, x1_ref, h2_ref):
    _mix_out(x_ref[...], mp_ref[...], attn_ref[...], gatt_ref, wo_ref, gffn_ref, x1_ref, h2_ref)


def _outproj(x, mp, attn, w, tm):
    M, D = x.shape
    const = lambda a: pl.BlockSpec(a.shape, lambda i: (0, 0))
    row = lambda n: pl.BlockSpec((tm, n), lambda i: (i, 0))
    return pl.pallas_call(
        _outproj_body,
        grid=(M // tm,),
        in_specs=[row(D), row(mp.shape[1]), row(attn.shape[1]),
                  const(w['g_out_attn']), const(w['w_o']), const(w['g_ffn'])],
        out_specs=(row(D), row(D)),
        out_shape=(jax.ShapeDtypeStruct((M, D), f32), jax.ShapeDtypeStruct((M, D), bf16)),
        compiler_params=_params("parallel"),
        name="out_proj",
    )(x, mp, attn, w['g_out_attn'], w['w_o'], w['g_ffn'])


def _ffn_finish(f, g, g1, g2, h2, x1_ref, wu_ref, wd_ref, cw_ref, cb_ref, y_ref, acc):
    gate = cb_ref[...] + g2 * cw_ref[0:1, :] + g1 * cw_ref[1:2, :] + g * cw_ref[2:3, :]
    act = (gate * jax.nn.sigmoid(gate)) * _dot(h2, wu_ref[...])
    part = _dot(act.astype(bf16), wd_ref[...])

    @pl.when(f == 0)
    def _():
        acc[...] = x1_ref[...] + part

    @pl.when(f > 0)
    def _():
        acc[...] += part

    @pl.when(f == pl.num_programs(1) - 1)
    def _():
        y_ref[...] = acc[...]


def _ffn_prompt_body(h2_ref, halo_ref, x1_ref, wg_ref, wu_ref, wd_ref, cw_ref, cb_ref,
                     y_ref, tail_ref, acc, gbuf, *, tm, tiles_per_seq):
    i = pl.program_id(0)
    f = pl.program_id(1)
    hb = BF16_SUBLANES
    h2 = h2_ref[...]
    rows = lax.broadcasted_iota(jnp.int32, (tm + hb, 1), 0)
    keep = jnp.logical_or(rows >= hb, i % tiles_per_seq != 0)
    gbuf[...] = jnp.where(keep, _dot(jnp.concatenate([halo_ref[...], h2], axis=0), wg_ref[...]), 0.0)
    g = gbuf[hb:hb + tm, :]
    tail_ref[0] = gbuf[tm:tm + hb, :]
    g1 = gbuf[hb - 1:hb - 1 + tm, :]
    g2 = gbuf[hb - 2:hb - 2 + tm, :]
    _ffn_finish(f, g, g1, g2, h2, x1_ref, wu_ref, wd_ref, cw_ref, cb_ref, y_ref, acc)


def _ffn_prompt(h2, x1, S, w, tm, tf):
    M, D = x1.shape
    F = w['w_gate'].shape[1]
    hb = BF16_SUBLANES
    nt = M // tm
    halo_blocks = tm // hb
    resident = dict(pipeline_mode=pl.Buffered(1)) if tf == F else {}
    y, tail = pl.pallas_call(
        functools.partial(_ffn_prompt_body, tm=tm, tiles_per_seq=S // tm),
        grid=(nt, F // tf),
        in_specs=[pl.BlockSpec((tm, D), lambda i, f: (i, 0)),
                  pl.BlockSpec((hb, D), lambda i, f: (jnp.maximum(i * halo_blocks - 1, 0), 0)),
                  pl.BlockSpec((tm, D), lambda i, f: (i, 0)),
                  pl.BlockSpec((D, tf), lambda i, f: (0, f), **resident),
                  pl.BlockSpec((D, tf), lambda i, f: (0, f), **resident),
                  pl.BlockSpec((tf, D), lambda i, f: (f, 0), **resident),
                  pl.BlockSpec((CONV_WIDTH, tf), lambda i, f: (0, f)),
                  pl.BlockSpec((1, tf), lambda i, f: (0, f))],
        out_specs=(pl.BlockSpec((tm, D), lambda i, f: (i, 0)),
                   pl.BlockSpec((1, hb, tf), lambda i, f: (i, 0, f))),
        out_shape=(jax.ShapeDtypeStruct((M, D), f32), jax.ShapeDtypeStruct((nt, hb, F), f32)),
        scratch_shapes=[pltpu.VMEM((tm, D), f32), pltpu.VMEM((tm + hb, tf), f32)],
        compiler_params=_params("parallel", "arbitrary"),
        name="ffn_prompt",
    )(h2, h2, x1, w['w_gate'], w['w_up'], w['w_down'], w['conv_w'], w['conv_b'])
    return y, tail


def _ffn_sample_body(h2_ref, st_ref, x1_ref, wg_ref, wu_ref, wd_ref, cw_ref, cb_ref, y_ref, g_ref, acc):
    f = pl.program_id(1)
    h2 = h2_ref[...]
    g = _dot(h2, wg_ref[...])
    g_ref[...] = g
    _ffn_finish(f, g, st_ref[1], st_ref[0], h2, x1_ref, wu_ref, wd_ref, cw_ref, cb_ref, y_ref, acc)


def _ffn_sample(h2, x1, st_t, w, tf):
    M, D = x1.shape
    F = w['w_gate'].shape[1]
    return pl.pallas_call(
        _ffn_sample_body,
        grid=(1, F // tf),
        in_specs=[pl.BlockSpec((M, D), lambda i, f: (0, 0)),
                  pl.BlockSpec((CONV_WIDTH - 1, M, tf), lambda i, f: (0, 0, f)),
                  pl.BlockSpec((M, D), lambda i, f: (0, 0)),
                  pl.BlockSpec((D, tf), lambda i, f: (0, f)),
                  pl.BlockSpec((D, tf), lambda i, f: (0, f)),
                  pl.BlockSpec((tf, D), lambda i, f: (f, 0)),
                  pl.BlockSpec((CONV_WIDTH, tf), lambda i, f: (0, f)),
                  pl.BlockSpec((1, tf), lambda i, f: (0, f))],
        out_specs=(pl.BlockSpec((M, D), lambda i, f: (0, 0)),
                   pl.BlockSpec((M, tf), lambda i, f: (0, f))),
        out_shape=(jax.ShapeDtypeStruct((M, D), f32), jax.ShapeDtypeStruct((M, F), f32)),
        scratch_shapes=[pltpu.VMEM((M, D), f32)],
        compiler_params=_params("parallel", "arbitrary"),
        name="ffn_sample",
    )(h2, st_t, x1, w['w_gate'], w['w_up'], w['w_down'], w['conv_w'], w['conv_b'])


def _qk_gain(g):
    g_pe = g[QK_NOPE_DIM:]
    return jnp.concatenate([g[:QK_NOPE_DIM], g_pe, g_pe])


def _rope_table(pos):
    half = QK_ROPE_DIM // 2
    inv = ROPE_BASE ** (-jnp.arange(0, QK_ROPE_DIM, 2, dtype=f32) / QK_ROPE_DIM)
    inv_lane = jnp.concatenate([jnp.zeros((QK_NOPE_DIM,), f32), inv, inv, jnp.zeros((LANE - QK_HEAD_DIM,), f32)])
    ang = pos.astype(f32)[:, None] * inv_lane[None, :]
    cos, sin = jnp.cos(ang), jnp.sin(ang)
    first_half = jnp.arange(LANE)[None, :] < QK_NOPE_DIM + half
    return jnp.concatenate([cos, jnp.where(first_half, -sin, 0.0), jnp.where(first_half, 0.0, sin)], axis=1)


def _prep_weights(g_mix, w_in, g_q_a, w_uq, g_qn, g_kv_a, w_uk, g_kn, w_uv, w_pool, s_pool,
                  g_out, w_o, g_ffn, w_gate, w_up, conv_w, conv_b, w_down):
    D = w_in.shape[0]
    q_rank = g_q_a.shape[0]
    kv_rank = g_kv_a.shape[0]
    pool_w = s_pool.shape[0]
    head_pad = LANE - QK_HEAD_DIM
    half = QK_ROPE_DIM // 2
    main = pool_w + q_rank + kv_rank
    w_in_p = jnp.concatenate([w_in[:, :main], jnp.zeros((D, QK_NOPE_DIM), f32), w_in[:, main:],
                              jnp.zeros((D, head_pad), f32)], axis=1).astype(bf16)
    w_uq_h = w_uq.reshape(q_rank, N_HEADS, QK_HEAD_DIM)
    w_uq_p = jnp.pad(w_uq_h, ((0, 0), (0, 0), (0, head_pad)))
    lo = w_uq_h[:, :, QK_NOPE_DIM:QK_NOPE_DIM + half]
    hi = w_uq_h[:, :, QK_NOPE_DIM + half:]
    w_uq_rot = jnp.concatenate([jnp.zeros((q_rank, N_HEADS, QK_NOPE_DIM), f32), -hi, lo,
                                jnp.zeros((q_rank, N_HEADS, head_pad), f32)], axis=2)
    w_uq_p = jnp.concatenate([w_uq_p.reshape(q_rank, N_HEADS * LANE),
                              w_uq_rot.reshape(q_rank, N_HEADS * LANE)], axis=1).astype(bf16)
    gq = _qk_gain(g_qn) * _qk_gain(g_kn) * (QK_HEAD_DIM ** -0.5 * LOG2_E)
    gq = jnp.tile(jnp.pad(gq, (0, head_pad)), N_HEADS)[None, :]
    w_uk_p = jnp.pad(w_uk, ((0, 0), (0, 0), (0, LANE - QK_NOPE_DIM))).reshape(kv_rank, N_HEADS * LANE)
    w_uk_p = w_uk_p.astype(bf16)
    w_uv_t = w_uv.reshape(kv_rank, N_HEADS * V_HEAD_DIM).T.astype(bf16)
    w_uk_t = w_uk.reshape(kv_rank, N_HEADS * QK_NOPE_DIM).T.astype(bf16)
    eye = jnp.eye(N_HEADS, dtype=f32)
    w_uk_bd = jnp.einsum('chd,hg->hdgc', jnp.pad(w_uk, ((0, 0), (0, 0), (0, LANE - QK_NOPE_DIM))), eye)
    w_uk_bd = w_uk_bd.reshape(N_HEADS * LANE, N_HEADS * kv_rank).astype(bf16)
    w_uv_bd = jnp.einsum('chv,hg->hcgv', w_uv, eye).reshape(N_HEADS * kv_rank, N_HEADS * V_HEAD_DIM).astype(bf16)
    return {
        'pool_w': pool_w,
        'g_mix': g_mix[None, :], 'w_in': w_in_p, 'g_q_a': g_q_a[None, :], 'w_uq': w_uq_p, 'g_q': gq,
        'g_kv_a': g_kv_a[None, :], 'w_uk': w_uk_p, 'w_uv_t': w_uv_t, 'w_uk_t': w_uk_t, 'w_uk_bd': w_uk_bd, 'w_uv_bd': w_uv_bd,
        'w_pool': w_pool.astype(bf16), 's_pool': s_pool[None, :],
        'g_out_pool': g_out[None, :pool_w], 'g_out_attn': g_out[None, pool_w:],
        'w_o': w_o.astype(bf16), 'g_ffn': g_ffn[None, :],
        'w_gate': w_gate.astype(bf16), 'w_up': w_up.astype(bf16), 'w_down': w_down.astype(bf16),
        'conv_w': conv_w, 'conv_b': conv_b[None, :],
    }


def _tile(n, pref):
    t = min(n, pref)
    assert n % t == 0, (n, pref)
    return t


def _ffn_cols(F):
    half = F // 2
    return half if half % LANE == 0 else F


def kernel(x_prompt, x_sample, cache_ckv, cache_kpe, state_pool, state_conv, page_table, g_mix, w_in, g_q_a,
           w_uq, g_qn, g_kv_a, w_uk, g_kn, w_uv, w_pool, s_pool, g_out, w_o, g_ffn, w_gate, w_up, conv_w,
           conv_b, w_down):
    depth = g_mix.shape[0]
    assert depth == 1, "single-layer trunk only"
    B, S, D = x_prompt.shape
    DB, DS = x_sample.shape[:2]
    assert DS == 1, "one new token per sequence"
    n_pages = page_table.shape[1]
    page = cache_ckv.shape[2]
    past_len = n_pages * page
    l = 0
    w = _prep_weights(g_mix[l], w_in[l], g_q_a[l], w_uq[l], g_qn[l], g_kv_a[l], w_uk[l], g_kn[l], w_uv[l],
                      w_pool[l], s_pool[l], g_out[l], w_o[l], g_ffn[l], w_gate[l], w_up[l], conv_w[l],
                      conv_b[l], w_down[l])
    kv_rank = g_kv_a.shape[1]
    F = w_gate.shape[2]
    tf = _ffn_cols(F)

    xp = x_prompt.reshape(B * S, D)
    tq = _tile(S, 512)
    tm = _tile(tq, 512)
    rope_p = _rope_table(jnp.arange(S, dtype=jnp.int32))
    u_p, q_p, ckv_p, kpe_p, k_p, vt_p = _proj(xp, rope_p, S // tm, w, tm, tq)
    mp_p = _pool_prompt(u_p, B, S, w, _tile(S, 512))
    x1_p, h2_p = _flash(q_p, k_p, vt_p, xp, mp_p, w, B, S, tq)
    tmf = _tile(S, 512)
    y_p, tail = _ffn_prompt(h2_p, x1_p, S, w, tmf, F)
    tiles_per_seq = S // tmf
    conv_p = tail[tiles_per_seq - 1::tiles_per_seq, BF16_SUBLANES - (CONV_WIDTH - 1):, :]

    xs = x_sample.reshape(DB, D)
    pos_s = past_len + jnp.arange(DS, dtype=jnp.int32)
    rope_s = jnp.tile(_rope_table(pos_s), (DB, 1))
    u_s, q_s, ckv_s, kpe_s, k_s, _ = _proj(xs, rope_s, 1, w, DB, DB)
    st_pool_t = jnp.transpose(state_pool[l], (1, 0, 2))
    mp_s = _pool_sample(u_s, st_pool_t, past_len, w)
    qabs = _mm(q_s, w['w_uk_bd'], "absorb_q").reshape(DB, N_HEADS, kv_rank).astype(bf16)
    qabs = jnp.pad(qabs, ((0, 0), (0, BF16_SUBLANES - N_HEADS), (0, 0)))
    q3 = q_s.reshape(DB, N_HEADS, LANE)
    qpe = q3[:, :, QK_NOPE_DIM:QK_HEAD_DIM]
    cp = _tile(n_pages, 32)
    ctx = _decode(page_table, q3, k_s.reshape(DB, N_HEADS, LANE), qabs, qpe, ckv_s.reshape(DB, 1, kv_rank),
                  w['w_uk_t'], cache_ckv[l], jnp.swapaxes(cache_kpe[l], 1, 2), cp)
    attn_s = _mm(ctx.reshape(DB, N_HEADS * kv_rank), w['w_uv_bd'], "value_up")
    x1_s, h2_s = _outproj(xs, mp_s, attn_s, w, DB)
    st_conv_t = jnp.transpose(state_conv[l], (1, 0, 2))
    y_s, g_s = _ffn_sample(h2_s, x1_s, st_conv_t, w, tf)

    kpe_sl = slice(QK_NOPE_DIM, QK_HEAD_DIM)
    P = POOL_STATE_LEN
    return (
        y_p.reshape(B, S, D),
        y_s.reshape(DB, DS, D),
        ckv_p.reshape(1, B, S, kv_rank),
        kpe_p[:, kpe_sl].reshape(1, B, S, QK_ROPE_DIM),
        u_p.reshape(B, S, -1)[None, :, S - P:, :],
        conv_p[None],
        ckv_s.reshape(1, DB, DS, kv_rank),
        kpe_s[:, kpe_sl].reshape(1, DB, DS, QK_ROPE_DIM),
        jnp.concatenate([state_pool[l], u_s[:, None, :]], axis=1)[None, :, -P:, :],
        jnp.concatenate([state_conv[l], g_s[:, None, :]], axis=1)[None, :, -(CONV_WIDTH - 1):, :],
    )
```

```python
import functools

import jax
import jax.numpy as jnp
from jax import lax
from jax.experimental import pallas as pl
from jax.experimental.pallas import tpu as pltpu

N_HEADS = 8
QK_NOPE_DIM = 64
QK_ROPE_DIM = 32
QK_HEAD_DIM = QK_NOPE_DIM + QK_ROPE_DIM
V_HEAD_DIM = 64
POOL_WINDOWS = (2, 4, 8, 16)
POOL_STATE_LEN = max(POOL_WINDOWS) - 1
CONV_WIDTH = 3
ROPE_BASE = 10000.0
RMS_EPS = 1e-6
LOG2_E = 1.4426950408889634

LANE = 128
SUBLANES = 8
MXU_DIM = 256
BF16_SUBLANES = 16
VMEM_LIMIT = 48 * 1024 * 1024
DECODE_SLOTS = 3
LOOKAHEAD = 1

_NT = (((1,), (1,)), ((), ()))

bf16 = jnp.bfloat16
f32 = jnp.float32


def _rms(x, width):
    return x * lax.rsqrt(jnp.sum(x * x, axis=-1, keepdims=True) * (1.0 / width) + RMS_EPS)


def _dot(a, b):
    return jnp.dot(a, b, preferred_element_type=f32)


def _params(*sem):
    return pltpu.CompilerParams(dimension_semantics=sem, vmem_limit_bytes=VMEM_LIMIT)


def _proj_body(x_ref, rope_ref, gmix_ref, win_ref, gqa_ref, wuq_ref, gq_ref, gkva_ref, wuk_ref, wuvt_ref,
               u_ref, q_ref, ckv_ref, kpe_ref, k_ref, vt_ref, *, pool_w, q_rank, kv_rank):
    tm = x_ref.shape[0]
    group = min(tm, MXU_DIM)
    groups = [slice(r, r + group) for r in range(0, tm, group)]
    half = QK_ROPE_DIM // 2
    hq = N_HEADS * LANE

    def in_projection(rows):
        x = x_ref[rows, :]
        h = (_rms(x, x.shape[-1]) * gmix_ref[...]).astype(bf16)
        return _dot(h, win_ref[...])

    def up_projections(rows, proj):
        u_ref[rows, :] = proj[:, :pool_w]
        o = pool_w
        cq = (_rms(proj[:, o:o + q_rank], q_rank) * gqa_ref[...]).astype(bf16)
        o += q_rank
        ckv = _rms(proj[:, o:o + kv_rank], kv_rank) * gkva_ref[...]
        o += kv_rank
        ckv_ref[rows, :] = ckv
        ckv_b = ckv.astype(bf16)
        qraw = _dot(cq, wuq_ref[...])
        kn = _dot(ckv_b, wuk_ref[...])
        vt_ref[0, :, rows] = lax.dot_general(wuvt_ref[...], ckv_b, _NT, preferred_element_type=f32).astype(bf16)
        return proj[:, o:o + LANE], qraw, kn

    def head_norms(rows, raw_kpe, qraw, kn):
        cos = rope_ref[rows, 0:LANE]
        sin_lo = rope_ref[rows, LANE:2 * LANE]
        sin_hi = rope_ref[rows, 2 * LANE:3 * LANE]
        sin_abs = sin_hi - sin_lo
        kpe = raw_kpe * cos + pltpu.roll(raw_kpe, LANE - half, 1) * sin_lo + pltpu.roll(raw_kpe, half, 1) * sin_hi
        kpe_ref[rows, :] = kpe
        for hd in range(N_HEADS):
            sl = slice(hd * LANE, (hd + 1) * LANE)
            qh = qraw[:, sl] * cos + qraw[:, hq + hd * LANE:hq + (hd + 1) * LANE] * sin_abs
            q_ref[rows, sl] = (_rms(qh, QK_HEAD_DIM) * gq_ref[:, sl]).astype(bf16)
            kh = kn[:, sl] + kpe
            k_ref[rows, sl] = _rms(kh, QK_HEAD_DIM).astype(bf16)

    projs = [in_projection(rows) for rows in groups]
    ups = [up_projections(rows, proj) for rows, proj in zip(groups, projs)]
    for rows, up in zip(groups, ups):
        head_norms(rows, *up)


def _proj(x, rope, n_rope_blocks, w, tm, slab):
    M, D = x.shape
    pool_w = w['pool_w']
    q_rank = w['g_q_a'].shape[1]
    kv_rank = w['g_kv_a'].shape[1]
    hq = N_HEADS * LANE
    hv = N_HEADS * V_HEAD_DIM
    per_slab = slab // tm
    const = lambda a: pl.BlockSpec(a.shape, lambda i: (0, 0))
    row = lambda n: pl.BlockSpec((tm, n), lambda i: (i, 0))
    out_shapes = (
        jax.ShapeDtypeStruct((M, pool_w), f32), jax.ShapeDtypeStruct((M, hq), bf16),
        jax.ShapeDtypeStruct((M, kv_rank), f32), jax.ShapeDtypeStruct((M, LANE), f32),
        jax.ShapeDtypeStruct((M, hq), bf16), jax.ShapeDtypeStruct((M // slab, hv, slab), bf16))
    return pl.pallas_call(
        functools.partial(_proj_body, pool_w=pool_w, q_rank=q_rank, kv_rank=kv_rank),
        grid=(M // tm,),
        in_specs=[row(D), pl.BlockSpec((tm, 3 * LANE), lambda i: (i % n_rope_blocks, 0)),
                  const(w['g_mix']), const(w['w_in']), const(w['g_q_a']), const(w['w_uq']),
                  const(w['g_q']), const(w['g_kv_a']), const(w['w_uk']), const(w['w_uv_t'])],
        out_specs=(row(pool_w), row(hq), row(kv_rank), row(LANE), row(hq),
                   pl.BlockSpec((1, hv, tm), lambda i: (i // per_slab, 0, i % per_slab))),
        out_shape=out_shapes,
        compiler_params=_params("parallel"),
        name="in_proj",
    )(x, rope, w['g_mix'], w['w_in'], w['g_q_a'], w['w_uq'], w['g_q'], w['g_kv_a'], w['w_uk'], w['w_uv_t'])


def _pool_finish(diffs, wpool_ref, spool_ref, gout_ref, o_ref):
    ys = []
    for g, d in enumerate(diffs):
        sl = slice(g * LANE, (g + 1) * LANE)
        ys.append(_dot(d.astype(bf16), wpool_ref[g]) * spool_ref[:, sl])
    width = LANE * len(ys)
    ssq = sum(jnp.sum(y * y, axis=-1, keepdims=True) for y in ys)
    scale = lax.rsqrt(ssq * (1.0 / width) + RMS_EPS)
    for g, y in enumerate(ys):
        sl = slice(g * LANE, (g + 1) * LANE)
        o_ref[:, sl] = (y * scale * gout_ref[:, sl]).astype(o_ref.dtype)


def _pool_prompt_body(u_ref, wpool_ref, spool_ref, gout_ref, o_ref, buf, *, ts):
    j = pl.program_id(1)
    halo = POOL_STATE_LEN + 1

    @pl.when(j == 0)
    def _():
        buf[0:halo, :] = jnp.zeros((halo, buf.shape[1]), f32)

    @pl.when(j > 0)
    def _():
        buf[0:halo, :] = buf[ts:ts + halo, :]

    buf[halo:halo + ts, :] = u_ref[...]
    pos = j * ts + lax.broadcasted_iota(jnp.int32, (ts, 1), 0)
    diffs = []
    for g, wdw in enumerate(POOL_WINDOWS):
        sl = slice(g * LANE, (g + 1) * LANE)
        x = buf[halo:halo + ts, sl]
        win = x
        for k in range(1, wdw):
            win = win + buf[halo - k:halo - k + ts, sl]
        cnt = jnp.minimum(wdw, pos + 1).astype(f32)
        diffs.append(win / cnt - x)
    _pool_finish(diffs, wpool_ref, spool_ref, gout_ref, o_ref)


def _pool_prompt(u, B, S, w, ts):
    M, W = u.shape
    ns = S // ts
    const2 = lambda a: pl.BlockSpec(a.shape, lambda b, j: (0,) * a.ndim)
    return pl.pallas_call(
        functools.partial(_pool_prompt_body, ts=ts),
        grid=(B, ns),
        in_specs=[pl.BlockSpec((ts, W), lambda b, j: (b * ns + j, 0)),
                  const2(w['w_pool']), const2(w['s_pool']), const2(w['g_out_pool'])],
        out_specs=pl.BlockSpec((ts, W), lambda b, j: (b * ns + j, 0)),
        out_shape=jax.ShapeDtypeStruct((M, W), bf16),
        scratch_shapes=[pltpu.VMEM((ts + POOL_STATE_LEN + 1, W), f32)],
        compiler_params=_params("arbitrary", "arbitrary"),
        name="pool_prompt",
    )(u, w['w_pool'], w['s_pool'], w['g_out_pool'])


def _pool_sample_body(u_ref, st_ref, wpool_ref, spool_ref, gout_ref, o_ref, *, pos):
    diffs = []
    for g, wdw in enumerate(POOL_WINDOWS):
        sl = slice(g * LANE, (g + 1) * LANE)
        x = u_ref[:, sl]
        win = x
        for k in range(1, wdw):
            win = win + st_ref[POOL_STATE_LEN - k, :, sl]
        diffs.append(win / float(min(wdw, pos + 1)) - x)
    _pool_finish(diffs, wpool_ref, spool_ref, gout_ref, o_ref)


def _pool_sample(u, st_t, pos, w):
    M, W = u.shape
    return pl.pallas_call(
        functools.partial(_pool_sample_body, pos=pos),
        out_shape=jax.ShapeDtypeStruct((M, W), bf16),
        compiler_params=pltpu.CompilerParams(vmem_limit_bytes=VMEM_LIMIT),
        name="pool_sample",
    )(u, st_t, w['w_pool'], w['s_pool'], w['g_out_pool'])


def _mix_out(x, mp, attn, gatt_ref, wo_ref, gffn_ref, x1_ref, h2_ref):
    na = (_rms(attn, attn.shape[-1]) * gatt_ref[...]).astype(bf16)
    x1 = x + _dot(jnp.concatenate([mp, na], axis=1), wo_ref[...])
    x1_ref[...] = x1
    h2_ref[...] = (_rms(x1, x1.shape[-1]) * gffn_ref[...]).astype(bf16)


def _flash_body(q_ref, k_ref, vt_ref, x_ref, mp_ref, gatt_ref, wo_ref, gffn_ref, x1_ref, h2_ref,
                m_scr, acc_scr, *, tq):
    i = pl.program_id(1)
    heads = m_scr.shape[0]
    key_idx = lax.broadcasted_iota(jnp.int32, (tq, tq), 0)
    qry_idx = lax.broadcasted_iota(jnp.int32, (tq, tq), 1)
    m_scr[...] = jnp.full(m_scr.shape, -1e30, f32)
    acc_scr[...] = jnp.zeros(acc_scr.shape, f32)
    ones_rows = jnp.ones((BF16_SUBLANES, tq), bf16)

    def step(j, masked):
        start = pl.multiple_of(j * tq, tq)

        def scores(hh):
            sl = slice(hh * LANE, (hh + 1) * LANE)
            return lax.dot_general(k_ref[pl.ds(start, tq), sl], q_ref[:, sl], _NT, preferred_element_type=f32)

        ready = [scores(hh) for hh in range(min(LOOKAHEAD, heads))]
        for hh in range(heads):
            st = ready.pop(0)
            if hh + LOOKAHEAD < heads:
                ready.append(scores(hh + LOOKAHEAD))
            if masked:
                st = jnp.where(key_idx <= qry_idx, st, -1e30)
            m = m_scr[hh]
            m_new = jnp.maximum(m, jnp.max(st, axis=0, keepdims=True))
            alpha = jnp.exp2(m - m_new)
            pt = jnp.exp2(st - m_new).astype(bf16)
            vt = jnp.concatenate([vt_ref[j, hh * V_HEAD_DIM:(hh + 1) * V_HEAD_DIM, :], ones_rows], axis=0)
            acc_scr[hh] = acc_scr[hh] * alpha + _dot(vt, pt)
            m_scr[hh] = m_new

    def body(j, carry):
        step(j, False)
        return carry

    lax.fori_loop(0, i, body, 0)
    step(i, True)
    dv = V_HEAD_DIM
    out_t = jnp.concatenate([acc_scr[hh, 0:dv] / acc_scr[hh, dv:dv + 1] for hh in range(heads)], axis=0)
    _mix_out(x_ref[...], mp_ref[...], out_t.T, gatt_ref, wo_ref, gffn_ref, x1_ref, h2_ref)


def _flash(q, k, vt, x, mp, w, B, S, tq):
    M, D = x.shape
    nq = S // tq
    const = lambda a: pl.BlockSpec(a.shape, lambda b, i: (0, 0))
    tile = lambda n: pl.BlockSpec((tq, n), lambda b, i: (b * nq + i, 0))
    return pl.pallas_call(
        functools.partial(_flash_body, tq=tq),
        grid=(B, nq),
        in_specs=[tile(N_HEADS * LANE),
                  pl.BlockSpec((S, N_HEADS * LANE), lambda b, i: (b, 0)),
                  pl.BlockSpec((nq, N_HEADS * V_HEAD_DIM, tq), lambda b, i: (b, 0, 0)),
                  tile(D), tile(mp.shape[1]),
                  const(w['g_out_attn']), const(w['w_o']), const(w['g_ffn'])],
        out_specs=(tile(D), tile(D)),
        out_shape=(jax.ShapeDtypeStruct((M, D), f32), jax.ShapeDtypeStruct((M, D), bf16)),
        scratch_shapes=[pltpu.VMEM((N_HEADS, 1, tq), f32),
                        pltpu.VMEM((N_HEADS, V_HEAD_DIM + BF16_SUBLANES, tq), f32)],
        compiler_params=_params("parallel", "arbitrary"),
        name="flash_prompt",
    )(q, k, vt, x, mp, w['g_out_attn'], w['w_o'], w['g_ffn'])


def _mm_body(a_ref, b_ref, o_ref):
    o_ref[...] = _dot(a_ref[...].astype(bf16), b_ref[...])


def _mm(a, b, name):
    return pl.pallas_call(
        _mm_body,
        out_shape=jax.ShapeDtypeStruct((a.shape[0], b.shape[1]), f32),
        compiler_params=pltpu.CompilerParams(vmem_limit_bytes=VMEM_LIMIT),
        name=name,
    )(a, b)


def _decode_body(pt_ref, q_ref, knew_ref, qabs_ref, qpe_ref, cnew_ref, wukt_ref, cache_ckv, cache_kpet,
                 o_ref, ckv_buf, kpe_buf, sems, lhs, ct_scr, part_scr, num_scr, pes_scr, m_scr, l_scr, acc_scr,
                 *, n_pages, cp, nc):
    f = pl.program_id(0)
    total = pl.num_programs(0) - 1
    n_slots = ckv_buf.shape[0]
    ahead = n_slots - 1
    ring = f % n_slots
    nope_rows = wukt_ref.shape[0]
    page = cache_ckv.shape[1]
    tokens = cp * page

    def copies(chunk, slot_idx):
        base = (chunk // nc) * n_pages + (chunk % nc) * cp
        out = []
        for p in range(cp):
            phys = pt_ref[base + p]
            rows = pl.ds(p * page, page)
            out.append(pltpu.make_async_copy(cache_ckv.at[phys], ckv_buf.at[slot_idx, rows], sems.at[0, slot_idx]))
            out.append(pltpu.make_async_copy(cache_kpet.at[phys], kpe_buf.at[slot_idx, :, rows],
                                             sems.at[1, slot_idx]))
        return out

    @pl.when(f == 0)
    def _():
        for k in range(ahead):
            for cpy in copies(jnp.minimum(k, total - 1), k):
                cpy.start()
        ct_scr[1] = jnp.zeros(ct_scr.shape[1:], bf16)
        part_scr[1] = jnp.zeros(part_scr.shape[1:], bf16)
        num_scr[1] = jnp.zeros(num_scr.shape[1:], f32)
        pes_scr[1] = jnp.zeros(pes_scr.shape[1:], f32)
        m_scr[...] = jnp.zeros_like(m_scr)
        l_scr[...] = jnp.ones_like(l_scr)
        acc_scr[...] = jnp.zeros_like(acc_scr)
        lhs[0:nope_rows, :] = wukt_ref[...]

    @pl.when(f % nc == 0)
    def _():
        lhs[nope_rows:nope_rows + BF16_SUBLANES, :] = qabs_ref[0]

    @pl.when((f >= 1) & ((f - 1) % nc == 0))
    def _():
        s_new = jnp.sum(q_ref[0].astype(f32) * knew_ref[0].astype(f32), axis=-1, keepdims=True)
        m_scr[...] = s_new
        l_scr[...] = jnp.ones_like(l_scr)
        acc_scr[...] = jnp.broadcast_to(cnew_ref[0], acc_scr.shape)

    qpe = qpe_ref[0]
    ones = jnp.ones(qpe.shape, bf16)
    fold = (lax.broadcasted_iota(jnp.int32, (N_HEADS, SUBLANES * N_HEADS), 1) // SUBLANES
            == lax.broadcasted_iota(jnp.int32, (N_HEADS, SUBLANES * N_HEADS), 0)).astype(bf16)

    def main(cur):
        prev = 1 - cur
        for cpy in copies(jnp.minimum(f, total - 1), ring):
            cpy.wait()
        ssq = _dot(fold, part_scr[prev]) + pes_scr[prev]
        s = num_scr[prev] * lax.rsqrt(ssq * (1.0 / QK_HEAD_DIM) + RMS_EPS)
        m = m_scr[...]
        m_new = jnp.maximum(m, jnp.max(s, axis=-1, keepdims=True))
        alpha = jnp.exp2(m - m_new)
        p = jnp.exp2(s - m_new)
        l_scr[...] = l_scr[...] * alpha + jnp.sum(p, axis=-1, keepdims=True)
        m_scr[...] = m_new
        ct = ckv_buf[ring].astype(bf16)
        ct_scr[cur] = ct
        for cpy in copies(jnp.minimum(f + ahead, total - 1), (f + ahead) % n_slots):
            cpy.start()
        kt = lax.dot_general(lhs[...], ct, _NT, preferred_element_type=f32)
        kn = kt[0:nope_rows].reshape(N_HEADS, nope_rows // (SUBLANES * N_HEADS), SUBLANES, tokens)
        part_scr[cur] = jnp.sum(kn * kn, axis=1).reshape(SUBLANES * N_HEADS, tokens).astype(bf16)
        kp = kpe_buf[ring]
        num_scr[cur] = kt[nope_rows:nope_rows + N_HEADS] + _dot(qpe, kp.astype(bf16))
        pes_scr[cur] = _dot(ones, (kp * kp).astype(bf16))
        acc_scr[...] = acc_scr[...] * alpha + _dot(p.astype(bf16), ct_scr[prev])

    @pl.when(f % 2 == 0)
    def _():
        main(0)

    @pl.when(f % 2 == 1)
    def _():
        main(1)

    @pl.when((f >= 1) & (f % nc == 0))
    def _():
        o_ref[0] = acc_scr[...] / l_scr[...]

    @pl.when(f == total)
    def _():
        for k in range(1, ahead + 1):
            for cpy in copies(total - 1, (f + k) % n_slots):
                cpy.wait()


def _decode(page_table, q3, knew3, qabs, qpe, cnew, wukt, cache_ckv, cache_kpet, cp):
    DB, n_pages = page_table.shape
    page, kv_rank = cache_ckv.shape[1:]
    rope = cache_kpet.shape[1]
    nc = n_pages // cp
    T = cp * page
    total = DB * nc
    head_seq = lambda f, pt: jnp.minimum(f, total - 1) // nc
    tail_seq = lambda f, pt: jnp.maximum(f - 1, 0) // nc
    per_seq = lambda a, seq: pl.BlockSpec((1,) + a.shape[1:], lambda f, pt: (seq(f, pt),) + (0,) * (a.ndim - 1))
    grid_spec = pltpu.PrefetchScalarGridSpec(
        num_scalar_prefetch=1,
        grid=(total + 1,),
        in_specs=[per_seq(q3, tail_seq), per_seq(knew3, tail_seq), per_seq(qabs, head_seq), per_seq(qpe, head_seq),
                  per_seq(cnew, tail_seq), pl.BlockSpec(wukt.shape, lambda f, pt: (0, 0)),
                  pl.BlockSpec(memory_space=pl.ANY), pl.BlockSpec(memory_space=pl.ANY)],
        out_specs=pl.BlockSpec((1, N_HEADS, kv_rank), lambda f, pt: (tail_seq(f, pt), 0, 0)),
        scratch_shapes=[pltpu.VMEM((DECODE_SLOTS, T, kv_rank), f32), pltpu.VMEM((DECODE_SLOTS, rope, T), f32),
                        pltpu.SemaphoreType.DMA((2, DECODE_SLOTS)),
                        pltpu.VMEM((wukt.shape[0] + BF16_SUBLANES, kv_rank), bf16),
                        pltpu.VMEM((2, T, kv_rank), bf16), pltpu.VMEM((2, SUBLANES * N_HEADS, T), bf16),
                        pltpu.VMEM((2, N_HEADS, T), f32), pltpu.VMEM((2, N_HEADS, T), f32),
                        pltpu.VMEM((N_HEADS, 1), f32), pltpu.VMEM((N_HEADS, 1), f32),
                        pltpu.VMEM((N_HEADS, kv_rank), f32)])
    return pl.pallas_call(
        functools.partial(_decode_body, n_pages=n_pages, cp=cp, nc=nc),
        grid_spec=grid_spec,
        out_shape=jax.ShapeDtypeStruct((DB, N_HEADS, kv_rank), f32),
        compiler_params=_params("arbitrary"),
        name="decode_attn",
    )(page_table.reshape(-1), q3, knew3, qabs, qpe, cnew, wukt, cache_ckv, cache_kpet)


def _outproj_body(x_ref, mp_ref, attn_ref, gatt_ref, wo_ref, gffn_ref, x1_ref, h2_ref):
    _mix_out(x_ref[...], mp_ref[...], attn_ref[...], gatt_ref, wo_ref, gffn_ref, x1_ref, h2_ref)


def _outproj(x, mp, attn, w, tm):
    M, D = x.shape
    const = lambda a: pl.BlockSpec(a.shape, lambda i: (0, 0))
    row = lambda n: pl.BlockSpec((tm, n), lambda i: (i, 0))
    return pl.pallas_call(
        _outproj_body,
        grid=(M // tm,),
        in_specs=[row(D), row(mp.shape[1]), row(attn.shape[1]),
                  const(w['g_out_attn']), const(w['w_o']), const(w['g_ffn'])],
        out_specs=(row(D), row(D)),
        out_shape=(jax.ShapeDtypeStruct((M, D), f32), jax.ShapeDtypeStruct((M, D), bf16)),
        compiler_params=_params("parallel"),
        name="out_proj",
    )(x, mp, attn, w['g_out_attn'], w['w_o'], w['g_ffn'])


def _ffn_finish(f, g, g1, g2, h2, x1_ref, wu_ref, wd_ref, cw_ref, cb_ref, y_ref, acc):
    gate = cb_ref[...] + g2 * cw_ref[0:1, :] + g1 * cw_ref[1:2, :] + g * cw_ref[2:3, :]
    act = (gate * jax.nn.sigmoid(gate)) * _dot(h2, wu_ref[...])
    part = _dot(act.astype(bf16), wd_ref[...])

    @pl.when(f == 0)
    def _():
        acc[...] = x1_ref[...] + part

    @pl.when(f > 0)
    def _():
        acc[...] += part

    @pl.when(f == pl.num_programs(1) - 1)
    def _():
        y_ref[...] = acc[...]


def _ffn_prompt_body(h2_ref, halo_ref, x1_ref, wg_ref, wu_ref, wd_ref, cw_ref, cb_ref,
                     y_ref, tail_ref, acc, gbuf, *, tm, tiles_per_seq):
    i = pl.program_id(0)
    f = pl.program_id(1)
    hb = BF16_SUBLANES
    h2 = h2_ref[...]
    rows = lax.broadcasted_iota(jnp.int32, (tm + hb, 1), 0)
    keep = jnp.logical_or(rows >= hb, i % tiles_per_seq != 0)
    gbuf[...] = jnp.where(keep, _dot(jnp.concatenate([halo_ref[...], h2], axis=0), wg_ref[...]), 0.0)
    g = gbuf[hb:hb + tm, :]
    tail_ref[0] = gbuf[tm:tm + hb, :]
    g1 = gbuf[hb - 1:hb - 1 + tm, :]
    g2 = gbuf[hb - 2:hb - 2 + tm, :]
    _ffn_finish(f, g, g1, g2, h2, x1_ref, wu_ref, wd_ref, cw_ref, cb_ref, y_ref, acc)


def _ffn_prompt(h2, x1, S, w, tm, tf):
    M, D = x1.shape
    F = w['w_gate'].shape[1]
    hb = BF16_SUBLANES
    nt = M // tm
    halo_blocks = tm // hb
    resident = dict(pipeline_mode=pl.Buffered(1)) if tf == F else {}
    y, tail = pl.pallas_call(
        functools.partial(_ffn_prompt_body, tm=tm, tiles_per_seq=S // tm),
        grid=(nt, F // tf),
        in_specs=[pl.BlockSpec((tm, D), lambda i, f: (i, 0)),
                  pl.BlockSpec((hb, D), lambda i, f: (jnp.maximum(i * halo_blocks - 1, 0), 0)),
                  pl.BlockSpec((tm, D), lambda i, f: (i, 0)),
                  pl.BlockSpec((D, tf), lambda i, f: (0, f), **resident),
                  pl.BlockSpec((D, tf), lambda i, f: (0, f), **resident),
                  pl.BlockSpec((tf, D), lambda i, f: (f, 0), **resident),
                  pl.BlockSpec((CONV_WIDTH, tf), lambda i, f: (0, f)),
                  pl.BlockSpec((1, tf), lambda i, f: (0, f))],
        out_specs=(pl.BlockSpec((tm, D), lambda i, f: (i, 0)),
                   pl.BlockSpec((1, hb, tf), lambda i, f: (i, 0, f))),
        out_shape=(jax.ShapeDtypeStruct((M, D), f32), jax.ShapeDtypeStruct((nt, hb, F), f32)),
        scratch_shapes=[pltpu.VMEM((tm, D), f32), pltpu.VMEM((tm + hb, tf), f32)],
        compiler_params=_params("parallel", "arbitrary"),
        name="ffn_prompt",
    )(h2, h2, x1, w['w_gate'], w['w_up'], w['w_down'], w['conv_w'], w['conv_b'])
    return y, tail


def _ffn_sample_body(h2_ref, st_ref, x1_ref, wg_ref, wu_ref, wd_ref, cw_ref, cb_ref, y_ref, g_ref, acc):
    f = pl.program_id(1)
    h2 = h2_ref[...]
    g = _dot(h2, wg_ref[...])
    g_ref[...] = g
    _ffn_finish(f, g, st_ref[1], st_ref[0], h2, x1_ref, wu_ref, wd_ref, cw_ref, cb_ref, y_ref, acc)


def _ffn_sample(h2, x1, st_t, w, tf):
    M, D = x1.shape
    F = w['w_gate'].shape[1]
    return pl.pallas_call(
        _ffn_sample_body,
        grid=(1, F // tf),
        in_specs=[pl.BlockSpec((M, D), lambda i, f: (0, 0)),
                  pl.BlockSpec((CONV_WIDTH - 1, M, tf), lambda i, f: (0, 0, f)),
                  pl.BlockSpec((M, D), lambda i, f: (0, 0)),
                  pl.BlockSpec((D, tf), lambda i, f: (0, f)),
                  pl.BlockSpec((D, tf), lambda i, f: (0, f)),
                  pl.BlockSpec((tf, D), lambda i, f: (f, 0)),
                  pl.BlockSpec((CONV_WIDTH, tf), lambda i, f: (0, f)),
                  pl.BlockSpec((1, tf), lambda i, f: (0, f))],
        out_specs=(pl.BlockSpec((M, D), lambda i, f: (0, 0)),
                   pl.BlockSpec((M, tf), lambda i, f: (0, f))),
        out_shape=(jax.ShapeDtypeStruct((M, D), f32), jax.ShapeDtypeStruct((M, F), f32)),
        scratch_shapes=[pltpu.VMEM((M, D), f32)],
        compiler_params=_params("parallel", "arbitrary"),
        name="ffn_sample",
    )(h2, st_t, x1, w['w_gate'], w['w_up'], w['w_down'], w['conv_w'], w['conv_b'])


def _qk_gain(g):
    g_pe = g[QK_NOPE_DIM:]
    return jnp.concatenate([g[:QK_NOPE_DIM], g_pe, g_pe])


def _rope_table(pos):
    half = QK_ROPE_DIM // 2
    inv = ROPE_BASE ** (-jnp.arange(0, QK_ROPE_DIM, 2, dtype=f32) / QK_ROPE_DIM)
    inv_lane = jnp.concatenate([jnp.zeros((QK_NOPE_DIM,), f32), inv, inv, jnp.zeros((LANE - QK_HEAD_DIM,), f32)])
    ang = pos.astype(f32)[:, None] * inv_lane[None, :]
    cos, sin = jnp.cos(ang), jnp.sin(ang)
    first_half = jnp.arange(LANE)[None, :] < QK_NOPE_DIM + half
    return jnp.concatenate([cos, jnp.where(first_half, -sin, 0.0), jnp.where(first_half, 0.0, sin)], axis=1)


def _prep_weights(g_mix, w_in, g_q_a, w_uq, g_qn, g_kv_a, w_uk, g_kn, w_uv, w_pool, s_pool,
                  g_out, w_o, g_ffn, w_gate, w_up, conv_w, conv_b, w_down):
    D = w_in.shape[0]
    q_rank = g_q_a.shape[0]
    kv_rank = g_kv_a.shape[0]
    pool_w = s_pool.shape[0]
    head_pad = LANE - QK_HEAD_DIM
    half = QK_ROPE_DIM // 2
    main = pool_w + q_rank + kv_rank
    w_in_p = jnp.concatenate([w_in[:, :main], jnp.zeros((D, QK_NOPE_DIM), f32), w_in[:, main:],
                              jnp.zeros((D, head_pad), f32)], axis=1).astype(bf16)
    w_uq_h = w_uq.reshape(q_rank, N_HEADS, QK_HEAD_DIM)
    w_uq_p = jnp.pad(w_uq_h, ((0, 0), (0, 0), (0, head_pad)))
    lo = w_uq_h[:, :, QK_NOPE_DIM:QK_NOPE_DIM + half]
    hi = w_uq_h[:, :, QK_NOPE_DIM + half:]
    w_uq_rot = jnp.concatenate([jnp.zeros((q_rank, N_HEADS, QK_NOPE_DIM), f32), -hi, lo,
                                jnp.zeros((q_rank, N_HEADS, head_pad), f32)], axis=2)
    w_uq_p = jnp.concatenate([w_uq_p.reshape(q_rank, N_HEADS * LANE),
                              w_uq_rot.reshape(q_rank, N_HEADS * LANE)], axis=1).astype(bf16)
    gq = _qk_gain(g_qn) * _qk_gain(g_kn) * (QK_HEAD_DIM ** -0.5 * LOG2_E)
    gq = jnp.tile(jnp.pad(gq, (0, head_pad)), N_HEADS)[None, :]
    w_uk_p = jnp.pad(w_uk, ((0, 0), (0, 0), (0, LANE - QK_NOPE_DIM))).reshape(kv_rank, N_HEADS * LANE)
    w_uk_p = w_uk_p.astype(bf16)
    w_uv_t = w_uv.reshape(kv_rank, N_HEADS * V_HEAD_DIM).T.astype(bf16)
    w_uk_t = w_uk.reshape(kv_rank, N_HEADS * QK_NOPE_DIM).T.astype(bf16)
    eye = jnp.eye(N_HEADS, dtype=f32)
    w_uk_bd = jnp.einsum('chd,hg->hdgc', jnp.pad(w_uk, ((0, 0), (0, 0), (0, LANE - QK_NOPE_DIM))), eye)
    w_uk_bd = w_uk_bd.reshape(N_HEADS * LANE, N_HEADS * kv_rank).astype(bf16)
    w_uv_bd = jnp.einsum('chv,hg->hcgv', w_uv, eye).reshape(N_HEADS * kv_rank, N_HEADS * V_HEAD_DIM).astype(bf16)
    return {
        'pool_w': pool_w,
        'g_mix': g_mix[None, :], 'w_in': w_in_p, 'g_q_a': g_q_a[None, :], 'w_uq': w_uq_p, 'g_q': gq,
        'g_kv_a': g_kv_a[None, :], 'w_uk': w_uk_p, 'w_uv_t': w_uv_t, 'w_uk_t': w_uk_t, 'w_uk_bd': w_uk_bd, 'w_uv_bd': w_uv_bd,
        'w_pool': w_pool.astype(bf16), 's_pool': s_pool[None, :],
        'g_out_pool': g_out[None, :pool_w], 'g_out_attn': g_out[None, pool_w:],
        'w_o': w_o.astype(bf16), 'g_ffn': g_ffn[None, :],
        'w_gate': w_gate.astype(bf16), 'w_up': w_up.astype(bf16), 'w_down': w_down.astype(bf16),
        'conv_w': conv_w, 'conv_b': conv_b[None, :],
    }


def _tile(n, pref):
    t = min(n, pref)
    assert n % t == 0, (n, pref)
    return t


def _ffn_cols(F):
    half = F // 2
    return half if half % LANE == 0 else F


def kernel(x_prompt, x_sample, cache_ckv, cache_kpe, state_pool, state_conv, page_table, g_mix, w_in, g_q_a,
           w_uq, g_qn, g_kv_a, w_uk, g_kn, w_uv, w_pool, s_pool, g_out, w_o, g_ffn, w_gate, w_up, conv_w,
           conv_b, w_down):
    depth = g_mix.shape[0]
    assert depth == 1, "single-layer trunk only"
    B, S, D = x_prompt.shape
    DB, DS = x_sample.shape[:2]
    assert DS == 1, "one new token per sequence"
    n_pages = page_table.shape[1]
    page = cache_ckv.shape[2]
    past_len = n_pages * page
    l = 0
    w = _prep_weights(g_mix[l], w_in[l], g_q_a[l], w_uq[l], g_qn[l], g_kv_a[l], w_uk[l], g_kn[l], w_uv[l],
                      w_pool[l], s_pool[l], g_out[l], w_o[l], g_ffn[l], w_gate[l], w_up[l], conv_w[l],
                      conv_b[l], w_down[l])
    kv_rank = g_kv_a.shape[1]
    F = w_gate.shape[2]
    tf = _ffn_cols(F)

    xp = x_prompt.reshape(B * S, D)
    tq = _tile(S, 512)
    tm = _tile(tq, 512)
    rope_p = _rope_table(jnp.arange(S, dtype=jnp.int32))
    u_p, q_p, ckv_p, kpe_p, k_p, vt_p = _proj(xp, rope_p, S // tm, w, tm, tq)
    mp_p = _pool_prompt(u_p, B, S, w, _tile(S, 512))
    x1_p, h2_p = _flash(q_p, k_p, vt_p, xp, mp_p, w, B, S, tq)
    tmf = _tile(S, 512)
    y_p, tail = _ffn_prompt(h2_p, x1_p, S, w, tmf, F)
    tiles_per_seq = S // tmf
    conv_p = tail[tiles_per_seq - 1::tiles_per_seq, BF16_SUBLANES - (CONV_WIDTH - 1):, :]

    xs = x_sample.reshape(DB, D)
    pos_s = past_len + jnp.arange(DS, dtype=jnp.int32)
    rope_s = jnp.tile(_rope_table(pos_s), (DB, 1))
    u_s, q_s, ckv_s, kpe_s, k_s, _ = _proj(xs, rope_s, 1, w, DB, DB)
    st_pool_t = jnp.transpose(state_pool[l], (1, 0, 2))
    mp_s = _pool_sample(u_s, st_pool_t, past_len, w)
    qabs = _mm(q_s, w['w_uk_bd'], "absorb_q").reshape(DB, N_HEADS, kv_rank).astype(bf16)
    qabs = jnp.pad(qabs, ((0, 0), (0, BF16_SUBLANES - N_HEADS), (0, 0)))
    q3 = q_s.reshape(DB, N_HEADS, LANE)
    qpe = q3[:, :, QK_NOPE_DIM:QK_HEAD_DIM]
    cp = _tile(n_pages, 32)
    ctx = _decode(page_table, q3, k_s.reshape(DB, N_HEADS, LANE), qabs, qpe, ckv_s.reshape(DB, 1, kv_rank),
                  w['w_uk_t'], cache_ckv[l], jnp.swapaxes(cache_kpe[l], 1, 2), cp)
    attn_s = _mm(ctx.reshape(DB, N_HEADS * kv_rank), w['w_uv_bd'], "value_up")
    x1_s, h2_s = _outproj(xs, mp_s, attn_s, w, DB)
    st_conv_t = jnp.transpose(state_conv[l], (1, 0, 2))
    y_s, g_s = _ffn_sample(h2_s, x1_s, st_conv_t, w, tf)

    kpe_sl = slice(QK_NOPE_DIM, QK_HEAD_DIM)
    P = POOL_STATE_LEN
    return (
        y_p.reshape(B, S, D),
        y_s.reshape(DB, DS, D),
        ckv_p.reshape(1, B, S, kv_rank),
        kpe_p[:, kpe_sl].reshape(1, B, S, QK_ROPE_DIM),
        u_p.reshape(B, S, -1)[None, :, S - P:, :],
        conv_p[None],
        ckv_s.reshape(1, DB, DS, kv_rank),
        kpe_s[:, kpe_sl].reshape(1, DB, DS, QK_ROPE_DIM),
        jnp.concatenate([state_pool[l], u_s[:, None, :]], axis=1)[None, :, -P:, :],
        jnp.concatenate([state_conv[l], g_s[:, None, :]], axis=1)[None, :, -(CONV_WIDTH - 1):, :],
    )
```

```python
import functools

import jax
import jax.numpy as jnp
from jax import lax
from jax.experimental import pallas as pl
from jax.experimental.pallas import tpu as pltpu

N_HEADS = 8
QK_NOPE_DIM = 64
QK_ROPE_DIM = 32
QK_HEAD_DIM = QK_NOPE_DIM + QK_ROPE_DIM
V_HEAD_DIM = 64
POOL_WINDOWS = (2, 4, 8, 16)
POOL_STATE_LEN = max(POOL_WINDOWS) - 1
CONV_WIDTH = 3
ROPE_BASE = 10000.0
RMS_EPS = 1e-6
LOG2_E = 1.4426950408889634

LANE = 128
SUBLANES = 8
MXU_DIM = 256
BF16_SUBLANES = 16
VMEM_LIMIT = 48 * 1024 * 1024
DECODE_SLOTS = 3
LOOKAHEAD = 2

_NT = (((1,), (1,)), ((), ()))

bf16 = jnp.bfloat16
f32 = jnp.float32


def _rms(x, width):
    return x * lax.rsqrt(jnp.sum(x * x, axis=-1, keepdims=True) * (1.0 / width) + RMS_EPS)


def _dot(a, b):
    return jnp.dot(a, b, preferred_element_type=f32)


def _params(*sem):
    return pltpu.CompilerParams(dimension_semantics=sem, vmem_limit_bytes=VMEM_LIMIT)


def _proj_body(x_ref, rope_ref, gmix_ref, win_ref, gqa_ref, wuq_ref, gq_ref, gkva_ref, wuk_ref, wuvt_ref,
               u_ref, q_ref, ckv_ref, kpe_ref, k_ref, vt_ref, *, pool_w, q_rank, kv_rank):
    tm = x_ref.shape[0]
    group = min(tm, MXU_DIM)
    groups = [slice(r, r + group) for r in range(0, tm, group)]
    half = QK_ROPE_DIM // 2
    hq = N_HEADS * LANE

    def in_projection(rows):
        x = x_ref[rows, :]
        h = (_rms(x, x.shape[-1]) * gmix_ref[...]).astype(bf16)
        return _dot(h, win_ref[...])

    def up_projections(rows, proj):
        u_ref[rows, :] = proj[:, :pool_w]
        o = pool_w
        cq = (_rms(proj[:, o:o + q_rank], q_rank) * gqa_ref[...]).astype(bf16)
        o += q_rank
        ckv = _rms(proj[:, o:o + kv_rank], kv_rank) * gkva_ref[...]
        o += kv_rank
        ckv_ref[rows, :] = ckv
        ckv_b = ckv.astype(bf16)
        qraw = _dot(cq, wuq_ref[...])
        kn = _dot(ckv_b, wuk_ref[...])
        vt_ref[0, :, rows] = lax.dot_general(wuvt_ref[...], ckv_b, _NT, preferred_element_type=f32).astype(bf16)
        return proj[:, o:o + LANE], qraw, kn

    def head_norms(rows, raw_kpe, qraw, kn):
        cos = rope_ref[rows, 0:LANE]
        sin_lo = rope_ref[rows, LANE:2 * LANE]
        sin_hi = rope_ref[rows, 2 * LANE:3 * LANE]
        sin_abs = sin_hi - sin_lo
        kpe = raw_kpe * cos + pltpu.roll(raw_kpe, LANE - half, 1) * sin_lo + pltpu.roll(raw_kpe, half, 1) * sin_hi
        kpe_ref[rows, :] = kpe
        for hd in range(N_HEADS):
            sl = slice(hd * LANE, (hd + 1) * LANE)
            qh = qraw[:, sl] * cos + qraw[:, hq + hd * LANE:hq + (hd + 1) * LANE] * sin_abs
            q_ref[rows, sl] = (_rms(qh, QK_HEAD_DIM) * gq_ref[:, sl]).astype(bf16)
            kh = kn[:, sl] + kpe
            k_ref[rows, sl] = _rms(kh, QK_HEAD_DIM).astype(bf16)

    projs = [in_projection(rows) for rows in groups]
    ups = [up_projections(rows, proj) for rows, proj in zip(groups, projs)]
    for rows, up in zip(groups, ups):
        head_norms(rows, *up)


def _proj(x, rope, n_rope_blocks, w, tm, slab):
    M, D = x.shape
    pool_w = w['pool_w']
    q_rank = w['g_q_a'].shape[1]
    kv_rank = w['g_kv_a'].shape[1]
    hq = N_HEADS * LANE
    hv = N_HEADS * V_HEAD_DIM
    per_slab = slab // tm
    const = lambda a: pl.BlockSpec(a.shape, lambda i: (0, 0))
    row = lambda n: pl.BlockSpec((tm, n), lambda i: (i, 0))
    out_shapes = (
        jax.ShapeDtypeStruct((M, pool_w), f32), jax.ShapeDtypeStruct((M, hq), bf16),
        jax.ShapeDtypeStruct((M, kv_rank), f32), jax.ShapeDtypeStruct((M, LANE), f32),
        jax.ShapeDtypeStruct((M, hq), bf16), jax.ShapeDtypeStruct((M // slab, hv, slab), bf16))
    return pl.pallas_call(
        functools.partial(_proj_body, pool_w=pool_w, q_rank=q_rank, kv_rank=kv_rank),
        grid=(M // tm,),
        in_specs=[row(D), pl.BlockSpec((tm, 3 * LANE), lambda i: (i % n_rope_blocks, 0)),
                  const(w['g_mix']), const(w['w_in']), const(w['g_q_a']), const(w['w_uq']),
                  const(w['g_q']), const(w['g_kv_a']), const(w['w_uk']), const(w['w_uv_t'])],
        out_specs=(row(pool_w), row(hq), row(kv_rank), row(LANE), row(hq),
                   pl.BlockSpec((1, hv, tm), lambda i: (i // per_slab, 0, i % per_slab))),
        out_shape=out_shapes,
        compiler_params=_params("parallel"),
        name="in_proj",
    )(x, rope, w['g_mix'], w['w_in'], w['g_q_a'], w['w_uq'], w['g_q'], w['g_kv_a'], w['w_uk'], w['w_uv_t'])


def _pool_finish(diffs, wpool_ref, spool_ref, gout_ref, o_ref):
    ys = []
    for g, d in enumerate(diffs):
        sl = slice(g * LANE, (g + 1) * LANE)
        ys.append(_dot(d.astype(bf16), wpool_ref[g]) * spool_ref[:, sl])
    width = LANE * len(ys)
    ssq = sum(jnp.sum(y * y, axis=-1, keepdims=True) for y in ys)
    scale = lax.rsqrt(ssq * (1.0 / width) + RMS_EPS)
    for g, y in enumerate(ys):
        sl = slice(g * LANE, (g + 1) * LANE)
        o_ref[:, sl] = (y * scale * gout_ref[:, sl]).astype(o_ref.dtype)


def _pool_prompt_body(u_ref, wpool_ref, spool_ref, gout_ref, o_ref, buf, *, ts):
    j = pl.program_id(1)
    halo = POOL_STATE_LEN + 1

    @pl.when(j == 0)
    def _():
        buf[0:halo, :] = jnp.zeros((halo, buf.shape[1]), f32)

    @pl.when(j > 0)
    def _():
        buf[0:halo, :] = buf[ts:ts + halo, :]

    buf[halo:halo + ts, :] = u_ref[...]
    pos = j * ts + lax.broadcasted_iota(jnp.int32, (ts, 1), 0)
    diffs = []
    for g, wdw in enumerate(POOL_WINDOWS):
        sl = slice(g * LANE, (g + 1) * LANE)
        x = buf[halo:halo + ts, sl]
        win = x
        for k in range(1, wdw):
            win = win + buf[halo - k:halo - k + ts, sl]
        cnt = jnp.minimum(wdw, pos + 1).astype(f32)
        diffs.append(win / cnt - x)
    _pool_finish(diffs, wpool_ref, spool_ref, gout_ref, o_ref)


def _pool_prompt(u, B, S, w, ts):
    M, W = u.shape
    ns = S // ts
    const2 = lambda a: pl.BlockSpec(a.shape, lambda b, j: (0,) * a.ndim)
    return pl.pallas_call(
        functools.partial(_pool_prompt_body, ts=ts),
        grid=(B, ns),
        in_specs=[pl.BlockSpec((ts, W), lambda b, j: (b * ns + j, 0)),
                  const2(w['w_pool']), const2(w['s_pool']), const2(w['g_out_pool'])],
        out_specs=pl.BlockSpec((ts, W), lambda b, j: (b * ns + j, 0)),
        out_shape=jax.ShapeDtypeStruct((M, W), bf16),
        scratch_shapes=[pltpu.VMEM((ts + POOL_STATE_LEN + 1, W), f32)],
        compiler_params=_params("arbitrary", "arbitrary"),
        name="pool_prompt",
    )(u, w['w_pool'], w['s_pool'], w['g_out_pool'])


def _pool_sample_body(u_ref, st_ref, wpool_ref, spool_ref, gout_ref, o_ref, *, pos):
    diffs = []
    for g, wdw in enumerate(POOL_WINDOWS):
        sl = slice(g * LANE, (g + 1) * LANE)
        x = u_ref[:, sl]
        win = x
        for k in range(1, wdw):
            win = win + st_ref[POOL_STATE_LEN - k, :, sl]
        diffs.append(win / float(min(wdw, pos + 1)) - x)
    _pool_finish(diffs, wpool_ref, spool_ref, gout_ref, o_ref)


def _pool_sample(u, st_t, pos, w):
    M, W = u.shape
    return pl.pallas_call(
        functools.partial(_pool_sample_body, pos=pos),
        out_shape=jax.ShapeDtypeStruct((M, W), bf16),
        compiler_params=pltpu.CompilerParams(vmem_limit_bytes=VMEM_LIMIT),
        name="pool_sample",
    )(u, st_t, w['w_pool'], w['s_pool'], w['g_out_pool'])


def _mix_out(x, mp, attn, gatt_ref, wo_ref, gffn_ref, x1_ref, h2_ref):
    na = (_rms(attn, attn.shape[-1]) * gatt_ref[...]).astype(bf16)
    x1 = x + _dot(jnp.concatenate([mp, na], axis=1), wo_ref[...])
    x1_ref[...] = x1
    h2_ref[...] = (_rms(x1, x1.shape[-1]) * gffn_ref[...]).astype(bf16)


def _flash_body(q_ref, k_ref, vt_ref, x_ref, mp_ref, gatt_ref, wo_ref, gffn_ref, x1_ref, h2_ref,
                m_scr, acc_scr, *, tq):
    i = pl.program_id(1)
    heads = m_scr.shape[0]
    key_idx = lax.broadcasted_iota(jnp.int32, (tq, tq), 0)
    qry_idx = lax.broadcasted_iota(jnp.int32, (tq, tq), 1)
    m_scr[...] = jnp.full(m_scr.shape, -1e30, f32)
    acc_scr[...] = jnp.zeros(acc_scr.shape, f32)
    ones_rows = jnp.ones((BF16_SUBLANES, tq), bf16)

    def step(j, masked):
        start = pl.multiple_of(j * tq, tq)

        def scores(hh):
            sl = slice(hh * LANE, (hh + 1) * LANE)
            return lax.dot_general(k_ref[pl.ds(start, tq), sl], q_ref[:, sl], _NT, preferred_element_type=f32)

        ready = [scores(hh) for hh in range(min(LOOKAHEAD, heads))]
        for hh in range(heads):
            st = ready.pop(0)
            if hh + LOOKAHEAD < heads:
                ready.append(scores(hh + LOOKAHEAD))
            if masked:
                st = jnp.where(key_idx <= qry_idx, st, -1e30)
            m = m_scr[hh]
            m_new = jnp.maximum(m, jnp.max(st, axis=0, keepdims=True))
            alpha = jnp.exp2(m - m_new)
            pt = jnp.exp2(st - m_new).astype(bf16)
            vt = jnp.concatenate([vt_ref[j, hh * V_HEAD_DIM:(hh + 1) * V_HEAD_DIM, :], ones_rows], axis=0)
            acc_scr[hh] = acc_scr[hh] * alpha + _dot(vt, pt)
            m_scr[hh] = m_new

    def body(j, carry):
        step(j, False)
        return carry

    lax.fori_loop(0, i, body, 0)
    step(i, True)
    dv = V_HEAD_DIM
    out_t = jnp.concatenate([acc_scr[hh, 0:dv] / acc_scr[hh, dv:dv + 1] for hh in range(heads)], axis=0)
    _mix_out(x_ref[...], mp_ref[...], out_t.T, gatt_ref, wo_ref, gffn_ref, x1_ref, h2_ref)


def _flash(q, k, vt, x, mp, w, B, S, tq):
    M, D = x.shape
    nq = S // tq
    const = lambda a: pl.BlockSpec(a.shape, lambda b, i: (0, 0))
    tile = lambda n: pl.BlockSpec((tq, n), lambda b, i: (b * nq + i, 0))
    return pl.pallas_call(
        functools.partial(_flash_body, tq=tq),
        grid=(B, nq),
        in_specs=[tile(N_HEADS * LANE),
                  pl.BlockSpec((S, N_HEADS * LANE), lambda b, i: (b, 0)),
                  pl.BlockSpec((nq, N_HEADS * V_HEAD_DIM, tq), lambda b, i: (b, 0, 0)),
                  tile(D), tile(mp.shape[1]),
                  const(w['g_out_attn']), const(w['w_o']), const(w['g_ffn'])],
        out_specs=(tile(D), tile(D)),
        out_shape=(jax.ShapeDtypeStruct((M, D), f32), jax.ShapeDtypeStruct((M, D), bf16)),
        scratch_shapes=[pltpu.VMEM((N_HEADS, 1, tq), f32),
                        pltpu.VMEM((N_HEADS, V_HEAD_DIM + BF16_SUBLANES, tq), f32)],
        compiler_params=_params("parallel", "arbitrary"),
        name="flash_prompt",
    )(q, k, vt, x, mp, w['g_out_attn'], w['w_o'], w['g_ffn'])


def _mm_body(a_ref, b_ref, o_ref):
    o_ref[...] = _dot(a_ref[...].astype(bf16), b_ref[...])


def _mm(a, b, name):
    return pl.pallas_call(
        _mm_body,
        out_shape=jax.ShapeDtypeStruct((a.shape[0], b.shape[1]), f32),
        compiler_params=pltpu.CompilerParams(vmem_limit_bytes=VMEM_LIMIT),
        name=name,
    )(a, b)


def _decode_body(pt_ref, q_ref, knew_ref, qabs_ref, qpe_ref, cnew_ref, wukt_ref, cache_ckv, cache_kpet,
                 o_ref, ckv_buf, kpe_buf, sems, lhs, ct_scr, part_scr, num_scr, pes_scr, m_scr, l_scr, acc_scr,
                 *, n_pages, cp, nc):
    f = pl.program_id(0)
    total = pl.num_programs(0) - 1
    n_slots = ckv_buf.shape[0]
    ahead = n_slots - 1
    ring = f % n_slots
    nope_rows = wukt_ref.shape[0]
    page = cache_ckv.shape[1]
    tokens = cp * page

    def copies(chunk, slot_idx):
        base = (chunk // nc) * n_pages + (chunk % nc) * cp
        out = []
        for p in range(cp):
            phys = pt_ref[base + p]
            rows = pl.ds(p * page, page)
            out.append(pltpu.make_async_copy(cache_ckv.at[phys], ckv_buf.at[slot_idx, rows], sems.at[0, slot_idx]))
            out.append(pltpu.make_async_copy(cache_kpet.at[phys], kpe_buf.at[slot_idx, :, rows],
                                             sems.at[1, slot_idx]))
        return out

    @pl.when(f == 0)
    def _():
        for k in range(ahead):
            for cpy in copies(jnp.minimum(k, total - 1), k):
                cpy.start()
        ct_scr[1] = jnp.zeros(ct_scr.shape[1:], bf16)
        part_scr[1] = jnp.zeros(part_scr.shape[1:], bf16)
        num_scr[1] = jnp.zeros(num_scr.shape[1:], f32)
        pes_scr[1] = jnp.zeros(pes_scr.shape[1:], f32)
        m_scr[...] = jnp.zeros_like(m_scr)
        l_scr[...] = jnp.ones_like(l_scr)
        acc_scr[...] = jnp.zeros_like(acc_scr)
        lhs[0:nope_rows, :] = wukt_ref[...]

    @pl.when(f % nc == 0)
    def _():
        lhs[nope_rows:nope_rows + BF16_SUBLANES, :] = qabs_ref[0]

    @pl.when((f >= 1) & ((f - 1) % nc == 0))
    def _():
        s_new = jnp.sum(q_ref[0].astype(f32) * knew_ref[0].astype(f32), axis=-1, keepdims=True)
        m_scr[...] = s_new
        l_scr[...] = jnp.ones_like(l_scr)
        acc_scr[...] = jnp.broadcast_to(cnew_ref[0], acc_scr.shape)

    qpe = qpe_ref[0]
    ones = jnp.ones(qpe.shape, bf16)
    fold = (lax.broadcasted_iota(jnp.int32, (N_HEADS, SUBLANES * N_HEADS), 1) // SUBLANES
            == lax.broadcasted_iota(jnp.int32, (N_HEADS, SUBLANES * N_HEADS), 0)).astype(bf16)

    def main(cur):
        prev = 1 - cur
        for cpy in copies(jnp.minimum(f, total - 1), ring):
            cpy.wait()
        ssq = _dot(fold, part_scr[prev]) + pes_scr[prev]
        s = num_scr[prev] * lax.rsqrt(ssq * (1.0 / QK_HEAD_DIM) + RMS_EPS)
        m = m_scr[...]
        m_new = jnp.maximum(m, jnp.max(s, axis=-1, keepdims=True))
        alpha = jnp.exp2(m - m_new)
        p = jnp.exp2(s - m_new)
        l_scr[...] = l_scr[...] * alpha + jnp.sum(p, axis=-1, keepdims=True)
        m_scr[...] = m_new
        ct = ckv_buf[ring].astype(bf16)
        ct_scr[cur] = ct
        for cpy in copies(jnp.minimum(f + ahead, total - 1), (f + ahead) % n_slots):
            cpy.start()
        kt = lax.dot_general(lhs[...], ct, _NT, preferred_element_type=f32)
        kn = kt[0:nope_rows].reshape(N_HEADS, nope_rows // (SUBLANES * N_HEADS), SUBLANES, tokens)
        part_scr[cur] = jnp.sum(kn * kn, axis=1).reshape(SUBLANES * N_HEADS, tokens).astype(bf16)
        kp = kpe_buf[ring]
        num_scr[cur] = kt[nope_rows:nope_rows + N_HEADS] + _dot(qpe, kp.astype(bf16))
        pes_scr[cur] = _dot(ones, (kp * kp).astype(bf16))
        acc_scr[...] = acc_scr[...] * alpha + _dot(p.astype(bf16), ct_scr[prev])

    @pl.when(f % 2 == 0)
    def _():
        main(0)

    @pl.when(f % 2 == 1)
    def _():
        main(1)

    @pl.when((f >= 1) & (f % nc == 0))
    def _():
        o_ref[0] = acc_scr[...] / l_scr[...]

    @pl.when(f == total)
    def _():
        for k in range(1, ahead + 1):
            for cpy in copies(total - 1, (f + k) % n_slots):
                cpy.wait()


def _decode(page_table, q3, knew3, qabs, qpe, cnew, wukt, cache_ckv, cache_kpet, cp):
    DB, n_pages = page_table.shape
    page, kv_rank = cache_ckv.shape[1:]
    rope = cache_kpet.shape[1]
    nc = n_pages // cp
    T = cp * page
    total = DB * nc
    head_seq = lambda f, pt: jnp.minimum(f, total - 1) // nc
    tail_seq = lambda f, pt: jnp.maximum(f - 1, 0) // nc
    per_seq = lambda a, seq: pl.BlockSpec((1,) + a.shape[1:], lambda f, pt: (seq(f, pt),) + (0,) * (a.ndim - 1))
    grid_spec = pltpu.PrefetchScalarGridSpec(
        num_scalar_prefetch=1,
        grid=(total + 1,),
        in_specs=[per_seq(q3, tail_seq), per_seq(knew3, tail_seq), per_seq(qabs, head_seq), per_seq(qpe, head_seq),
                  per_seq(cnew, tail_seq), pl.BlockSpec(wukt.shape, lambda f, pt: (0, 0)),
                  pl.BlockSpec(memory_space=pl.ANY), pl.BlockSpec(memory_space=pl.ANY)],
        out_specs=pl.BlockSpec((1, N_HEADS, kv_rank), lambda f, pt: (tail_seq(f, pt), 0, 0)),
        scratch_shapes=[pltpu.VMEM((DECODE_SLOTS, T, kv_rank), f32), pltpu.VMEM((DECODE_SLOTS, rope, T), f32),
                        pltpu.SemaphoreType.DMA((2, DECODE_SLOTS)),
                        pltpu.VMEM((wukt.shape[0] + BF16_SUBLANES, kv_rank), bf16),
                        pltpu.VMEM((2, T, kv_rank), bf16), pltpu.VMEM((2, SUBLANES * N_HEADS, T), bf16),
                        pltpu.VMEM((2, N_HEADS, T), f32), pltpu.VMEM((2, N_HEADS, T), f32),
                        pltpu.VMEM((N_HEADS, 1), f32), pltpu.VMEM((N_HEADS, 1), f32),
                        pltpu.VMEM((N_HEADS, kv_rank), f32)])
    return pl.pallas_call(
        functools.partial(_decode_body, n_pages=n_pages, cp=cp, nc=nc),
        grid_spec=grid_spec,
        out_shape=jax.ShapeDtypeStruct((DB, N_HEADS, kv_rank), f32),
        compiler_params=_params("arbitrary"),
        name="decode_attn",
    )(page_table.reshape(-1), q3, knew3, qabs, qpe, cnew, wukt, cache_ckv, cache_kpet)


def _outproj_body(x_ref, mp_ref, attn_ref, gatt_ref, wo_ref, gffn_ref, x1_ref, h2_ref):
    _mix_out(x_ref[...], mp_ref[...], attn_ref[...], gatt_ref, wo_ref, gffn_ref, x1_ref, h2_ref)


def _outproj(x, mp, attn, w, tm):
    M, D = x.shape
    const = lambda a: pl.BlockSpec(a.shape, lambda i: (0, 0))
    row = lambda n: pl.BlockSpec((tm, n), lambda i: (i, 0))
    return pl.pallas_call(
        _outproj_body,
        grid=(M // tm,),
        in_specs=[row(D), row(mp.shape[1]), row(attn.shape[1]),
                  const(w['g_out_attn']), const(w['w_o']), const(w['g_ffn'])],
        out_specs=(row(D), row(D)),
        out_shape=(jax.ShapeDtypeStruct((M, D), f32), jax.ShapeDtypeStruct((M, D), bf16)),
        compiler_params=_params("parallel"),
        name="out_proj",
    )(x, mp, attn, w['g_out_attn'], w['w_o'], w['g_ffn'])


def _ffn_finish(f, g, g1, g2, h2, x1_ref, wu_ref, wd_ref, cw_ref, cb_ref, y_ref, acc):
    gate = cb_ref[...] + g2 * cw_ref[0:1, :] + g1 * cw_ref[1:2, :] + g * cw_ref[2:3, :]
    act = (gate * jax.nn.sigmoid(gate)) * _dot(h2, wu_ref[...])
    part = _dot(act.astype(bf16), wd_ref[...])

    @pl.when(f == 0)
    def _():
        acc[...] = x1_ref[...] + part

    @pl.when(f > 0)
    def _():
        acc[...] += part

    @pl.when(f == pl.num_programs(1) - 1)
    def _():
        y_ref[...] = acc[...]


def _ffn_prompt_body(h2_ref, halo_ref, x1_ref, wg_ref, wu_ref, wd_ref, cw_ref, cb_ref,
                     y_ref, tail_ref, acc, gbuf, *, tm, tiles_per_seq):
    i = pl.program_id(0)
    f = pl.program_id(1)
    hb = BF16_SUBLANES
    h2 = h2_ref[...]
    rows = lax.broadcasted_iota(jnp.int32, (tm + hb, 1), 0)
    keep = jnp.logical_or(rows >= hb, i % tiles_per_seq != 0)
    gbuf[...] = jnp.where(keep, _dot(jnp.concatenate([halo_ref[...], h2], axis=0), wg_ref[...]), 0.0)
    g = gbuf[hb:hb + tm, :]
    tail_ref[0] = gbuf[tm:tm + hb, :]
    g1 = gbuf[hb - 1:hb - 1 + tm, :]
    g2 = gbuf[hb - 2:hb - 2 + tm, :]
    _ffn_finish(f, g, g1, g2, h2, x1_ref, wu_ref, wd_ref, cw_ref, cb_ref, y_ref, acc)


def _ffn_prompt(h2, x1, S, w, tm, tf):
    M, D = x1.shape
    F = w['w_gate'].shape[1]
    hb = BF16_SUBLANES
    nt = M // tm
    halo_blocks = tm // hb
    resident = dict(pipeline_mode=pl.Buffered(1)) if tf == F else {}
    y, tail = pl.pallas_call(
        functools.partial(_ffn_prompt_body, tm=tm, tiles_per_seq=S // tm),
        grid=(nt, F // tf),
        in_specs=[pl.BlockSpec((tm, D), lambda i, f: (i, 0)),
                  pl.BlockSpec((hb, D), lambda i, f: (jnp.maximum(i * halo_blocks - 1, 0), 0)),
                  pl.BlockSpec((tm, D), lambda i, f: (i, 0)),
                  pl.BlockSpec((D, tf), lambda i, f: (0, f), **resident),
                  pl.BlockSpec((D, tf), lambda i, f: (0, f), **resident),
                  pl.BlockSpec((tf, D), lambda i, f: (f, 0), **resident),
                  pl.BlockSpec((CONV_WIDTH, tf), lambda i, f: (0, f)),
                  pl.BlockSpec((1, tf), lambda i, f: (0, f))],
        out_specs=(pl.BlockSpec((tm, D), lambda i, f: (i, 0)),
                   pl.BlockSpec((1, hb, tf), lambda i, f: (i, 0, f))),
        out_shape=(jax.ShapeDtypeStruct((M, D), f32), jax.ShapeDtypeStruct((nt, hb, F), f32)),
        scratch_shapes=[pltpu.VMEM((tm, D), f32), pltpu.VMEM((tm + hb, tf), f32)],
        compiler_params=_params("parallel", "arbitrary"),
        name="ffn_prompt",
    )(h2, h2, x1, w['w_gate'], w['w_up'], w['w_down'], w['conv_w'], w['conv_b'])
    return y, tail


def _ffn_sample_body(h2_ref, st_ref, x1_ref, wg_ref, wu_ref, wd_ref, cw_ref, cb_ref, y_ref, g_ref, acc):
    f = pl.program_id(1)
    h2 = h2_ref[...]
    g = _dot(h2, wg_ref[...])
    g_ref[...] = g
    _ffn_finish(f, g, st_ref[1], st_ref[0], h2, x1_ref, wu_ref, wd_ref, cw_ref, cb_ref, y_ref, acc)


def _ffn_sample(h2, x1, st_t, w, tf):
    M, D = x1.shape
    F = w['w_gate'].shape[1]
    return pl.pallas_call(
        _ffn_sample_body,
        grid=(1, F // tf),
        in_specs=[pl.BlockSpec((M, D), lambda i, f: (0, 0)),
                  pl.BlockSpec((CONV_WIDTH - 1, M, tf), lambda i, f: (0, 0, f)),
                  pl.BlockSpec((M, D), lambda i, f: (0, 0)),
                  pl.BlockSpec((D, tf), lambda i, f: (0, f)),
                  pl.BlockSpec((D, tf), lambda i, f: (0, f)),
                  pl.BlockSpec((tf, D), lambda i, f: (f, 0)),
                  pl.BlockSpec((CONV_WIDTH, tf), lambda i, f: (0, f)),
                  pl.BlockSpec((1, tf), lambda i, f: (0, f))],
        out_specs=(pl.BlockSpec((M, D), lambda i, f: (0, 0)),
                   pl.BlockSpec((M, tf), lambda i, f: (0, f))),
        out_shape=(jax.ShapeDtypeStruct((M, D), f32), jax.ShapeDtypeStruct((M, F), f32)),
        scratch_shapes=[pltpu.VMEM((M, D), f32)],
        compiler_params=_params("parallel", "arbitrary"),
        name="ffn_sample",
    )(h2, st_t, x1, w['w_gate'], w['w_up'], w['w_down'], w['conv_w'], w['conv_b'])


def _qk_gain(g):
    g_pe = g[QK_NOPE_DIM:]
    return jnp.concatenate([g[:QK_NOPE_DIM], g_pe, g_pe])


def _rope_table(pos):
    half = QK_ROPE_DIM // 2
    inv = ROPE_BASE ** (-jnp.arange(0, QK_ROPE_DIM, 2, dtype=f32) / QK_ROPE_DIM)
    inv_lane = jnp.concatenate([jnp.zeros((QK_NOPE_DIM,), f32), inv, inv, jnp.zeros((LANE - QK_HEAD_DIM,), f32)])
    ang = pos.astype(f32)[:, None] * inv_lane[None, :]
    cos, sin = jnp.cos(ang), jnp.sin(ang)
    first_half = jnp.arange(LANE)[None, :] < QK_NOPE_DIM + half
    return jnp.concatenate([cos, jnp.where(first_half, -sin, 0.0), jnp.where(first_half, 0.0, sin)], axis=1)


def _prep_weights(g_mix, w_in, g_q_a, w_uq, g_qn, g_kv_a, w_uk, g_kn, w_uv, w_pool, s_pool,
                  g_out, w_o, g_ffn, w_gate, w_up, conv_w, conv_b, w_down):
    D = w_in.shape[0]
    q_rank = g_q_a.shape[0]
    kv_rank = g_kv_a.shape[0]
    pool_w = s_pool.shape[0]
    head_pad = LANE - QK_HEAD_DIM
    half = QK_ROPE_DIM // 2
    main = pool_w + q_rank + kv_rank
    w_in_p = jnp.concatenate([w_in[:, :main], jnp.zeros((D, QK_NOPE_DIM), f32), w_in[:, main:],
                              jnp.zeros((D, head_pad), f32)], axis=1).astype(bf16)
    w_uq_h = w_uq.reshape(q_rank, N_HEADS, QK_HEAD_DIM)
    w_uq_p = jnp.pad(w_uq_h, ((0, 0), (0, 0), (0, head_pad)))
    lo = w_uq_h[:, :, QK_NOPE_DIM:QK_NOPE_DIM + half]
    hi = w_uq_h[:, :, QK_NOPE_DIM + half:]
    w_uq_rot = jnp.concatenate([jnp.zeros((q_rank, N_HEADS, QK_NOPE_DIM), f32), -hi, lo,
                                jnp.zeros((q_rank, N_HEADS, head_pad), f32)], axis=2)
    w_uq_p = jnp.concatenate([w_uq_p.reshape(q_rank, N_HEADS * LANE),
                              w_uq_rot.reshape(q_rank, N_HEADS * LANE)], axis=1).astype(bf16)
    gq = _qk_gain(g_qn) * _qk_gain(g_kn) * (QK_HEAD_DIM ** -0.5 * LOG2_E)
    gq = jnp.tile(jnp.pad(gq, (0, head_pad)), N_HEADS)[None, :]
    w_uk_p = jnp.pad(w_uk, ((0, 0), (0, 0), (0, LANE - QK_NOPE_DIM))).reshape(kv_rank, N_HEADS * LANE)
    w_uk_p = w_uk_p.astype(bf16)
    w_uv_t = w_uv.reshape(kv_rank, N_HEADS * V_HEAD_DIM).T.astype(bf16)
    w_uk_t = w_uk.reshape(kv_rank, N_HEADS * QK_NOPE_DIM).T.astype(bf16)
    eye = jnp.eye(N_HEADS, dtype=f32)
    w_uk_bd = jnp.einsum('chd,hg->hdgc', jnp.pad(w_uk, ((0, 0), (0, 0), (0, LANE - QK_NOPE_DIM))), eye)
    w_uk_bd = w_uk_bd.reshape(N_HEADS * LANE, N_HEADS * kv_rank).astype(bf16)
    w_uv_bd = jnp.einsum('chv,hg->hcgv', w_uv, eye).reshape(N_HEADS * kv_rank, N_HEADS * V_HEAD_DIM).astype(bf16)
    return {
        'pool_w': pool_w,
        'g_mix': g_mix[None, :], 'w_in': w_in_p, 'g_q_a': g_q_a[None, :], 'w_uq': w_uq_p, 'g_q': gq,
        'g_kv_a': g_kv_a[None, :], 'w_uk': w_uk_p, 'w_uv_t': w_uv_t, 'w_uk_t': w_uk_t, 'w_uk_bd': w_uk_bd, 'w_uv_bd': w_uv_bd,
        'w_pool': w_pool.astype(bf16), 's_pool': s_pool[None, :],
        'g_out_pool': g_out[None, :pool_w], 'g_out_attn': g_out[None, pool_w:],
        'w_o': w_o.astype(bf16), 'g_ffn': g_ffn[None, :],
        'w_gate': w_gate.astype(bf16), 'w_up': w_up.astype(bf16), 'w_down': w_down.astype(bf16),
        'conv_w': conv_w, 'conv_b': conv_b[None, :],
    }


def _tile(n, pref):
    t = min(n, pref)
    assert n % t == 0, (n, pref)
    return t


def _ffn_cols(F):
    half = F // 2
    return half if half % LANE == 0 else F


def kernel(x_prompt, x_sample, cache_ckv, cache_kpe, state_pool, state_conv, page_table, g_mix, w_in, g_q_a,
           w_uq, g_qn, g_kv_a, w_uk, g_kn, w_uv, w_pool, s_pool, g_out, w_o, g_ffn, w_gate, w_up, conv_w,
           conv_b, w_down):
    depth = g_mix.shape[0]
    assert depth == 1, "single-layer trunk only"
    B, S, D = x_prompt.shape
    DB, DS = x_sample.shape[:2]
    assert DS == 1, "one new token per sequence"
    n_pages = page_table.shape[1]
    page = cache_ckv.shape[2]
    past_len = n_pages * page
    l = 0
    w = _prep_weights(g_mix[l], w_in[l], g_q_a[l], w_uq[l], g_qn[l], g_kv_a[l], w_uk[l], g_kn[l], w_uv[l],
                      w_pool[l], s_pool[l], g_out[l], w_o[l], g_ffn[l], w_gate[l], w_up[l], conv_w[l],
                      conv_b[l], w_down[l])
    kv_rank = g_kv_a.shape[1]
    F = w_gate.shape[2]
    tf = _ffn_cols(F)

    xp = x_prompt.reshape(B * S, D)
    tq = _tile(S, 512)
    tm = _tile(tq, 512)
    rope_p = _rope_table(jnp.arange(S, dtype=jnp.int32))
    u_p, q_p, ckv_p, kpe_p, k_p, vt_p = _proj(xp, rope_p, S // tm, w, tm, tq)
    mp_p = _pool_prompt(u_p, B, S, w, _tile(S, 512))
    x1_p, h2_p = _flash(q_p, k_p, vt_p, xp, mp_p, w, B, S, tq)
    tmf = _tile(S, 512)
    y_p, tail = _ffn_prompt(h2_p, x1_p, S, w, tmf, F)
    tiles_per_seq = S // tmf
    conv_p = tail[tiles_per_seq - 1::tiles_per_seq, BF16_SUBLANES - (CONV_WIDTH - 1):, :]

    xs = x_sample.reshape(DB, D)
    pos_s = past_len + jnp.arange(DS, dtype=jnp.int32)
    rope_s = jnp.tile(_rope_table(pos_s), (DB, 1))
    u_s, q_s, ckv_s, kpe_s, k_s, _ = _proj(xs, rope_s, 1, w, DB, DB)
    st_pool_t = jnp.transpose(state_pool[l], (1, 0, 2))
    mp_s = _pool_sample(u_s, st_pool_t, past_len, w)
    qabs = _mm(q_s, w['w_uk_bd'], "absorb_q").reshape(DB, N_HEADS, kv_rank).astype(bf16)
    qabs = jnp.pad(qabs, ((0, 0), (0, BF16_SUBLANES - N_HEADS), (0, 0)))
    q3 = q_s.reshape(DB, N_HEADS, LANE)
    qpe = q3[:, :, QK_NOPE_DIM:QK_HEAD_DIM]
    cp = _tile(n_pages, 32)
    ctx = _decode(page_table, q3, k_s.reshape(DB, N_HEADS, LANE), qabs, qpe, ckv_s.reshape(DB, 1, kv_rank),
                  w['w_uk_t'], cache_ckv[l], jnp.swapaxes(cache_kpe[l], 1, 2), cp)
    attn_s = _mm(ctx.reshape(DB, N_HEADS * kv_rank), w['w_uv_bd'], "value_up")
    x1_s, h2_s = _outproj(xs, mp_s, attn_s, w, DB)
    st_conv_t = jnp.transpose(state_conv[l], (1, 0, 2))
    y_s, g_s = _ffn_sample(h2_s, x1_s, st_conv_t, w, tf)

    kpe_sl = slice(QK_NOPE_DIM, QK_HEAD_DIM)
    P = POOL_STATE_LEN
    return (
        y_p.reshape(B, S, D),
        y_s.reshape(DB, DS, D),
        ckv_p.reshape(1, B, S, kv_rank),
        kpe_p[:, kpe_sl].reshape(1, B, S, QK_ROPE_DIM),
        u_p.reshape(B, S, -1)[None, :, S - P:, :],
        conv_p[None],
        ckv_s.reshape(1, DB, DS, kv_rank),
        kpe_s[:, kpe_sl].reshape(1, DB, DS, QK_ROPE_DIM),
        jnp.concatenate([state_pool[l], u_s[:, None, :]], axis=1)[None, :, -P:, :],
        jnp.concatenate([state_conv[l], g_s[:, None, :]], axis=1)[None, :, -(CONV_WIDTH - 1):, :],
    )
```

```python
import functools

import jax
import jax.numpy as jnp
from jax import lax
from jax.experimental import pallas as pl
from jax.experimental.pallas import tpu as pltpu

N_HEADS = 8
QK_NOPE_DIM = 64
QK_ROPE_DIM = 32
QK_HEAD_DIM = QK_NOPE_DIM + QK_ROPE_DIM
V_HEAD_DIM = 64
POOL_WINDOWS = (2, 4, 8, 16)
POOL_STATE_LEN = max(POOL_WINDOWS) - 1
CONV_WIDTH = 3
ROPE_BASE = 10000.0
RMS_EPS = 1e-6
LOG2_E = 1.4426950408889634

LANE = 128
SUBLANES = 8
MXU_DIM = 256
BF16_SUBLANES = 16
VMEM_LIMIT = 48 * 1024 * 1024
DECODE_SLOTS = 3
LOOKAHEAD = 2

_NT = (((1,), (1,)), ((), ()))

bf16 = jnp.bfloat16
f32 = jnp.float32


def _rms(x, width):
    return x * lax.rsqrt(jnp.sum(x * x, axis=-1, keepdims=True) * (1.0 / width) + RMS_EPS)


def _dot(a, b):
    return jnp.dot(a, b, preferred_element_type=f32)


def _params(*sem):
    return pltpu.CompilerParams(dimension_semantics=sem, vmem_limit_bytes=VMEM_LIMIT)


def _proj_body(x_ref, rope_ref, gmix_ref, win_ref, gqa_ref, wuq_ref, gq_ref, gkva_ref, wuk_ref, wuvt_ref,
               u_ref, q_ref, ckv_ref, kpe_ref, k_ref, vt_ref, *, pool_w, q_rank, kv_rank):
    tm = x_ref.shape[0]
    group = min(tm, MXU_DIM)
    groups = [slice(r, r + group) for r in range(0, tm, group)]
    half = QK_ROPE_DIM // 2
    hq = N_HEADS * LANE

    def in_projection(rows):
        x = x_ref[rows, :]
        h = (_rms(x, x.shape[-1]) * gmix_ref[...]).astype(bf16)
        return _dot(h, win_ref[...])

    def up_projections(rows, proj):
        u_ref[rows, :] = proj[:, :pool_w]
        o = pool_w
        cq = (_rms(proj[:, o:o + q_rank], q_rank) * gqa_ref[...]).astype(bf16)
        o += q_rank
        ckv = _rms(proj[:, o:o + kv_rank], kv_rank) * gkva_ref[...]
        o += kv_rank
        ckv_ref[rows, :] = ckv
        ckv_b = ckv.astype(bf16)
        qraw = _dot(cq, wuq_ref[...])
        kn = _dot(ckv_b, wuk_ref[...])
        vt_ref[0, :, rows] = lax.dot_general(wuvt_ref[...], ckv_b, _NT, preferred_element_type=f32).astype(bf16)
        return proj[:, o:o + LANE], qraw, kn

    def head_norms(rows, raw_kpe, qraw, kn):
        cos = rope_ref[rows, 0:LANE]
        sin_lo = rope_ref[rows, LANE:2 * LANE]
        sin_hi = rope_ref[rows, 2 * LANE:3 * LANE]
        sin_abs = sin_hi - sin_lo
        kpe = raw_kpe * cos + pltpu.roll(raw_kpe, LANE - half, 1) * sin_lo + pltpu.roll(raw_kpe, half, 1) * sin_hi
        kpe_ref[rows, :] = kpe
        for hd in range(N_HEADS):
            sl = slice(hd * LANE, (hd + 1) * LANE)
            qh = qraw[:, sl] * cos + qraw[:, hq + hd * LANE:hq + (hd + 1) * LANE] * sin_abs
            q_ref[rows, sl] = (_rms(qh, QK_HEAD_DIM) * gq_ref[:, sl]).astype(bf16)
            kh = kn[:, sl] + kpe
            k_ref[rows, sl] = _rms(kh, QK_HEAD_DIM).astype(bf16)

    projs = [in_projection(rows) for rows in groups]
    ups = [up_projections(rows, proj) for rows, proj in zip(groups, projs)]
    for rows, up in zip(groups, ups):
        head_norms(rows, *up)


def _proj(x, rope, n_rope_blocks, w, tm, slab):
    M, D = x.shape
    pool_w = w['pool_w']
    q_rank = w['g_q_a'].shape[1]
    kv_rank = w['g_kv_a'].shape[1]
    hq = N_HEADS * LANE
    hv = N_HEADS * V_HEAD_DIM
    per_slab = slab // tm
    const = lambda a: pl.BlockSpec(a.shape, lambda i: (0, 0))
    row = lambda n: pl.BlockSpec((tm, n), lambda i: (i, 0))
    out_shapes = (
        jax.ShapeDtypeStruct((M, pool_w), f32), jax.ShapeDtypeStruct((M, hq), bf16),
        jax.ShapeDtypeStruct((M, kv_rank), f32), jax.ShapeDtypeStruct((M, LANE), f32),
        jax.ShapeDtypeStruct((M, hq), bf16), jax.ShapeDtypeStruct((M // slab, hv, slab), bf16))
    return pl.pallas_call(
        functools.partial(_proj_body, pool_w=pool_w, q_rank=q_rank, kv_rank=kv_rank),
        grid=(M // tm,),
        in_specs=[row(D), pl.BlockSpec((tm, 3 * LANE), lambda i: (i % n_rope_blocks, 0)),
                  const(w['g_mix']), const(w['w_in']), const(w['g_q_a']), const(w['w_uq']),
                  const(w['g_q']), const(w['g_kv_a']), const(w['w_uk']), const(w['w_uv_t'])],
        out_specs=(row(pool_w), row(hq), row(kv_rank), row(LANE), row(hq),
                   pl.BlockSpec((1, hv, tm), lambda i: (i // per_slab, 0, i % per_slab))),
        out_shape=out_shapes,
        compiler_params=_params("parallel"),
        name="in_proj",
    )(x, rope, w['g_mix'], w['w_in'], w['g_q_a'], w['w_uq'], w['g_q'], w['g_kv_a'], w['w_uk'], w['w_uv_t'])


def _pool_finish(diffs, wpool_ref, spool_ref, gout_ref, o_ref):
    ys = []
    for g, d in enumerate(diffs):
        sl = slice(g * LANE, (g + 1) * LANE)
        ys.append(_dot(d.astype(bf16), wpool_ref[g]) * spool_ref[:, sl])
    width = LANE * len(ys)
    ssq = sum(jnp.sum(y * y, axis=-1, keepdims=True) for y in ys)
    scale = lax.rsqrt(ssq * (1.0 / width) + RMS_EPS)
    for g, y in enumerate(ys):
        sl = slice(g * LANE, (g + 1) * LANE)
        o_ref[:, sl] = (y * scale * gout_ref[:, sl]).astype(o_ref.dtype)


def _pool_prompt_body(u_ref, wpool_ref, spool_ref, gout_ref, o_ref, buf, lvl, *, ts):
    j = pl.program_id(1)
    halo = POOL_STATE_LEN + 1

    @pl.when(j == 0)
    def _():
        buf[0:halo, :] = jnp.zeros((halo, buf.shape[1]), f32)

    @pl.when(j > 0)
    def _():
        buf[0:halo, :] = buf[ts:ts + halo, :]

    buf[halo:halo + ts, :] = u_ref[...]
    pos = j * ts + lax.broadcasted_iota(jnp.int32, (ts, 1), 0)
    diffs = []
    end = halo + ts
    for g, wdw in enumerate(POOL_WINDOWS):
        sl = slice(g * LANE, (g + 1) * LANE)
        x = buf[halo:end, sl]
        src, width = buf, 1
        while width < wdw:
            first = 2 * width - 1
            lvl[first:end, sl] = src[first:end, sl] + src[first - width:end - width, sl]
            src, width = lvl, 2 * width
        win = src[halo:end, sl]
        cnt = jnp.minimum(wdw, pos + 1).astype(f32)
        diffs.append(win / cnt - x)
    _pool_finish(diffs, wpool_ref, spool_ref, gout_ref, o_ref)


def _pool_prompt(u, B, S, w, ts):
    M, W = u.shape
    ns = S // ts
    const2 = lambda a: pl.BlockSpec(a.shape, lambda b, j: (0,) * a.ndim)
    return pl.pallas_call(
        functools.partial(_pool_prompt_body, ts=ts),
        grid=(B, ns),
        in_specs=[pl.BlockSpec((ts, W), lambda b, j: (b * ns + j, 0)),
                  const2(w['w_pool']), const2(w['s_pool']), const2(w['g_out_pool'])],
        out_specs=pl.BlockSpec((ts, W), lambda b, j: (b * ns + j, 0)),
        out_shape=jax.ShapeDtypeStruct((M, W), bf16),
        scratch_shapes=[pltpu.VMEM((ts + POOL_STATE_LEN + 1, W), f32)] * 2,
        compiler_params=_params("arbitrary", "arbitrary"),
        name="pool_prompt",
    )(u, w['w_pool'], w['s_pool'], w['g_out_pool'])


def _pool_sample_body(u_ref, st_ref, wpool_ref, spool_ref, gout_ref, o_ref, *, pos):
    diffs = []
    for g, wdw in enumerate(POOL_WINDOWS):
        sl = slice(g * LANE, (g + 1) * LANE)
        x = u_ref[:, sl]
        win = x
        for k in range(1, wdw):
            win = win + st_ref[POOL_STATE_LEN - k, :, sl]
        diffs.append(win / float(min(wdw, pos + 1)) - x)
    _pool_finish(diffs, wpool_ref, spool_ref, gout_ref, o_ref)


def _pool_sample(u, st_t, pos, w):
    M, W = u.shape
    return pl.pallas_call(
        functools.partial(_pool_sample_body, pos=pos),
        out_shape=jax.ShapeDtypeStruct((M, W), bf16),
        compiler_params=pltpu.CompilerParams(vmem_limit_bytes=VMEM_LIMIT),
        name="pool_sample",
    )(u, st_t, w['w_pool'], w['s_pool'], w['g_out_pool'])


def _mix_out(x, mp, attn, gatt_ref, wo_ref, gffn_ref, x1_ref, h2_ref):
    na = (_rms(attn, attn.shape[-1]) * gatt_ref[...]).astype(bf16)
    x1 = x + _dot(jnp.concatenate([mp, na], axis=1), wo_ref[...])
    x1_ref[...] = x1
    h2_ref[...] = (_rms(x1, x1.shape[-1]) * gffn_ref[...]).astype(bf16)


def _flash_body(q_ref, k_ref, vt_ref, x_ref, mp_ref, gatt_ref, wo_ref, gffn_ref, x1_ref, h2_ref,
                m_scr, acc_scr, *, tq):
    i = pl.program_id(1)
    heads = m_scr.shape[0]
    key_idx = lax.broadcasted_iota(jnp.int32, (tq, tq), 0)
    qry_idx = lax.broadcasted_iota(jnp.int32, (tq, tq), 1)
    m_scr[...] = jnp.full(m_scr.shape, -1e30, f32)
    acc_scr[...] = jnp.zeros(acc_scr.shape, f32)
    ones_rows = jnp.ones((BF16_SUBLANES, tq), bf16)

    def step(j, masked):
        start = pl.multiple_of(j * tq, tq)

        def scores(hh):
            sl = slice(hh * LANE, (hh + 1) * LANE)
            return lax.dot_general(k_ref[pl.ds(start, tq), sl], q_ref[:, sl], _NT, preferred_element_type=f32)

        ready = [scores(hh) for hh in range(min(LOOKAHEAD, heads))]
        for hh in range(heads):
            st = ready.pop(0)
            if hh + LOOKAHEAD < heads:
                ready.append(scores(hh + LOOKAHEAD))
            if masked:
                st = jnp.where(key_idx <= qry_idx, st, -1e30)
            m = m_scr[hh]
            m_new = jnp.maximum(m, jnp.max(st, axis=0, keepdims=True))
            alpha = jnp.exp2(m - m_new)
            pt = jnp.exp2(st - m_new).astype(bf16)
            vt = jnp.concatenate([vt_ref[j, hh * V_HEAD_DIM:(hh + 1) * V_HEAD_DIM, :], ones_rows], axis=0)
            acc_scr[hh] = acc_scr[hh] * alpha + _dot(vt, pt)
            m_scr[hh] = m_new

    def body(j, carry):
        step(j, False)
        return carry

    lax.fori_loop(0, i, body, 0)
    step(i, True)
    dv = V_HEAD_DIM
    out_t = jnp.concatenate([acc_scr[hh, 0:dv] / acc_scr[hh, dv:dv + 1] for hh in range(heads)], axis=0)
    _mix_out(x_ref[...], mp_ref[...], out_t.T, gatt_ref, wo_ref, gffn_ref, x1_ref, h2_ref)


def _flash(q, k, vt, x, mp, w, B, S, tq):
    M, D = x.shape
    nq = S // tq
    const = lambda a: pl.BlockSpec(a.shape, lambda b, i: (0, 0))
    tile = lambda n: pl.BlockSpec((tq, n), lambda b, i: (b * nq + i, 0))
    return pl.pallas_call(
        functools.partial(_flash_body, tq=tq),
        grid=(B, nq),
        in_specs=[tile(N_HEADS * LANE),
                  pl.BlockSpec((S, N_HEADS * LANE), lambda b, i: (b, 0)),
                  pl.BlockSpec((nq, N_HEADS * V_HEAD_DIM, tq), lambda b, i: (b, 0, 0)),
                  tile(D), tile(mp.shape[1]),
                  const(w['g_out_attn']), const(w['w_o']), const(w['g_ffn'])],
        out_specs=(tile(D), tile(D)),
        out_shape=(jax.ShapeDtypeStruct((M, D), f32), jax.ShapeDtypeStruct((M, D), bf16)),
        scratch_shapes=[pltpu.VMEM((N_HEADS, 1, tq), f32),
                        pltpu.VMEM((N_HEADS, V_HEAD_DIM + BF16_SUBLANES, tq), f32)],
        compiler_params=_params("parallel", "arbitrary"),
        name="flash_prompt",
    )(q, k, vt, x, mp, w['g_out_attn'], w['w_o'], w['g_ffn'])


def _mm_body(a_ref, b_ref, o_ref):
    o_ref[...] = _dot(a_ref[...].astype(bf16), b_ref[...])


def _mm(a, b, name):
    return pl.pallas_call(
        _mm_body,
        out_shape=jax.ShapeDtypeStruct((a.shape[0], b.shape[1]), f32),
        compiler_params=pltpu.CompilerParams(vmem_limit_bytes=VMEM_LIMIT),
        name=name,
    )(a, b)


def _decode_body(pt_ref, q_ref, knew_ref, qabs_ref, qpe_ref, cnew_ref, wukt_ref, cache_ckv, cache_kpet,
                 o_ref, ckv_buf, kpe_buf, sems, lhs, ct_scr, part_scr, num_scr, pes_scr, m_scr, l_scr, acc_scr,
                 *, n_pages, cp, nc):
    f = pl.program_id(0)
    total = pl.num_programs(0) - 1
    n_slots = ckv_buf.shape[0]
    ahead = n_slots - 1
    ring = f % n_slots
    nope_rows = wukt_ref.shape[0]
    page = cache_ckv.shape[1]
    tokens = cp * page

    def copies(chunk, slot_idx):
        base = (chunk // nc) * n_pages + (chunk % nc) * cp
        out = []
        for p in range(cp):
            phys = pt_ref[base + p]
            rows = pl.ds(p * page, page)
            out.append(pltpu.make_async_copy(cache_ckv.at[phys], ckv_buf.at[slot_idx, rows], sems.at[0, slot_idx]))
            out.append(pltpu.make_async_copy(cache_kpet.at[phys], kpe_buf.at[slot_idx, :, rows],
                                             sems.at[1, slot_idx]))
        return out

    @pl.when(f == 0)
    def _():
        for k in range(ahead):
            for cpy in copies(jnp.minimum(k, total - 1), k):
                cpy.start()
        ct_scr[1] = jnp.zeros(ct_scr.shape[1:], bf16)
        part_scr[1] = jnp.zeros(part_scr.shape[1:], bf16)
        num_scr[1] = jnp.zeros(num_scr.shape[1:], f32)
        pes_scr[1] = jnp.zeros(pes_scr.shape[1:], f32)
        m_scr[...] = jnp.zeros_like(m_scr)
        l_scr[...] = jnp.ones_like(l_scr)
        acc_scr[...] = jnp.zeros_like(acc_scr)
        lhs[0:nope_rows, :] = wukt_ref[...]

    @pl.when(f % nc == 0)
    def _():
        lhs[nope_rows:nope_rows + BF16_SUBLANES, :] = qabs_ref[0]

    @pl.when((f >= 1) & ((f - 1) % nc == 0))
    def _():
        s_new = jnp.sum(q_ref[0].astype(f32) * knew_ref[0].astype(f32), axis=-1, keepdims=True)
        m_scr[...] = s_new
        l_scr[...] = jnp.ones_like(l_scr)
        acc_scr[...] = jnp.broadcast_to(cnew_ref[0], acc_scr.shape)

    qpe = qpe_ref[0]
    ones = jnp.ones(qpe.shape, bf16)
    fold = (lax.broadcasted_iota(jnp.int32, (N_HEADS, SUBLANES * N_HEADS), 1) // SUBLANES
            == lax.broadcasted_iota(jnp.int32, (N_HEADS, SUBLANES * N_HEADS), 0)).astype(bf16)

    def main(cur):
        prev = 1 - cur
        for cpy in copies(jnp.minimum(f, total - 1), ring):
            cpy.wait()
        ssq = _dot(fold, part_scr[prev]) + pes_scr[prev]
        s = num_scr[prev] * lax.rsqrt(ssq * (1.0 / QK_HEAD_DIM) + RMS_EPS)
        m = m_scr[...]
        m_new = jnp.maximum(m, jnp.max(s, axis=-1, keepdims=True))
        alpha = jnp.exp2(m - m_new)
        p = jnp.exp2(s - m_new)
        l_scr[...] = l_scr[...] * alpha + jnp.sum(p, axis=-1, keepdims=True)
        m_scr[...] = m_new
        ct = ckv_buf[ring].astype(bf16)
        ct_scr[cur] = ct
        for cpy in copies(jnp.minimum(f + ahead, total - 1), (f + ahead) % n_slots):
            cpy.start()
        kt = lax.dot_general(lhs[...], ct, _NT, preferred_element_type=f32)
        kn = kt[0:nope_rows].reshape(N_HEADS, nope_rows // (SUBLANES * N_HEADS), SUBLANES, tokens)
        part_scr[cur] = jnp.sum(kn * kn, axis=1).reshape(SUBLANES * N_HEADS, tokens).astype(bf16)
        kp = kpe_buf[ring]
        num_scr[cur] = kt[nope_rows:nope_rows + N_HEADS] + _dot(qpe, kp.astype(bf16))
        pes_scr[cur] = _dot(ones, (kp * kp).astype(bf16))
        acc_scr[...] = acc_scr[...] * alpha + _dot(p.astype(bf16), ct_scr[prev])

    @pl.when(f % 2 == 0)
    def _():
        main(0)

    @pl.when(f % 2 == 1)
    def _():
        main(1)

    @pl.when((f >= 1) & (f % nc == 0))
    def _():
        o_ref[0] = acc_scr[...] / l_scr[...]

    @pl.when(f == total)
    def _():
        for k in range(1, ahead + 1):
            for cpy in copies(total - 1, (f + k) % n_slots):
                cpy.wait()


def _decode(page_table, q3, knew3, qabs, qpe, cnew, wukt, cache_ckv, cache_kpet, cp):
    DB, n_pages = page_table.shape
    page, kv_rank = cache_ckv.shape[1:]
    rope = cache_kpet.shape[1]
    nc = n_pages // cp
    T = cp * page
    total = DB * nc
    head_seq = lambda f, pt: jnp.minimum(f, total - 1) // nc
    tail_seq = lambda f, pt: jnp.maximum(f - 1, 0) // nc
    per_seq = lambda a, seq: pl.BlockSpec((1,) + a.shape[1:], lambda f, pt: (seq(f, pt),) + (0,) * (a.ndim - 1))
    grid_spec = pltpu.PrefetchScalarGridSpec(
        num_scalar_prefetch=1,
        grid=(total + 1,),
        in_specs=[per_seq(q3, tail_seq), per_seq(knew3, tail_seq), per_seq(qabs, head_seq), per_seq(qpe, head_seq),
                  per_seq(cnew, tail_seq), pl.BlockSpec(wukt.shape, lambda f, pt: (0, 0)),
                  pl.BlockSpec(memory_space=pl.ANY), pl.BlockSpec(memory_space=pl.ANY)],
        out_specs=pl.BlockSpec((1, N_HEADS, kv_rank), lambda f, pt: (tail_seq(f, pt), 0, 0)),
        scratch_shapes=[pltpu.VMEM((DECODE_SLOTS, T, kv_rank), f32), pltpu.VMEM((DECODE_SLOTS, rope, T), f32),
                        pltpu.SemaphoreType.DMA((2, DECODE_SLOTS)),
                        pltpu.VMEM((wukt.shape[0] + BF16_SUBLANES, kv_rank), bf16),
                        pltpu.VMEM((2, T, kv_rank), bf16), pltpu.VMEM((2, SUBLANES * N_HEADS, T), bf16),
                        pltpu.VMEM((2, N_HEADS, T), f32), pltpu.VMEM((2, N_HEADS, T), f32),
                        pltpu.VMEM((N_HEADS, 1), f32), pltpu.VMEM((N_HEADS, 1), f32),
                        pltpu.VMEM((N_HEADS, kv_rank), f32)])
    return pl.pallas_call(
        functools.partial(_decode_body, n_pages=n_pages, cp=cp, nc=nc),
        grid_spec=grid_spec,
        out_shape=jax.ShapeDtypeStruct((DB, N_HEADS, kv_rank), f32),
        compiler_params=_params("arbitrary"),
        name="decode_attn",
    )(page_table.reshape(-1), q3, knew3, qabs, qpe, cnew, wukt, cache_ckv, cache_kpet)


def _outproj_body(x_ref, mp_ref, attn_ref, gatt_ref, wo_ref, gffn_ref, x1_ref, h2_ref):
    _mix_out(x_ref[...], mp_ref[...], attn_ref[...], gatt_ref, wo_ref, gffn_ref, x1_ref, h2_ref)


def _outproj(x, mp, attn, w, tm):
    M, D = x.shape
    const = lambda a: pl.BlockSpec(a.shape, lambda i: (0, 0))
    row = lambda n: pl.BlockSpec((tm, n), lambda i: (i, 0))
    return pl.pallas_call(
        _outproj_body,
        grid=(M // tm,),
        in_specs=[row(D), row(mp.shape[1]), row(attn.shape[1]),
                  const(w['g_out_attn']), const(w['w_o']), const(w['g_ffn'])],
        out_specs=(row(D), row(D)),
        out_shape=(jax.ShapeDtypeStruct((M, D), f32), jax.ShapeDtypeStruct((M, D), bf16)),
        compiler_params=_params("parallel"),
        name="out_proj",
    )(x, mp, attn, w['g_out_attn'], w['w_o'], w['g_ffn'])


def _ffn_finish(f, g, g1, g2, h2, x1_ref, wu_ref, wd_ref, cw_ref, cb_ref, y_ref, acc):
    gate = cb_ref[...] + g2 * cw_ref[0:1, :] + g1 * cw_ref[1:2, :] + g * cw_ref[2:3, :]
    act = (gate * jax.nn.sigmoid(gate)) * _dot(h2, wu_ref[...])
    part = _dot(act.astype(bf16), wd_ref[...])

    @pl.when(f == 0)
    def _():
        acc[...] = x1_ref[...] + part

    @pl.when(f > 0)
    def _():
        acc[...] += part

    @pl.when(f == pl.num_programs(1) - 1)
    def _():
        y_ref[...] = acc[...]


def _ffn_prompt_body(h2_ref, halo_ref, x1_ref, wg_ref, wu_ref, wd_ref, cw_ref, cb_ref,
                     y_ref, tail_ref, acc, gbuf, *, tm, tiles_per_seq):
    i = pl.program_id(0)
    f = pl.program_id(1)
    hb = BF16_SUBLANES
    h2 = h2_ref[...]
    rows = lax.broadcasted_iota(jnp.int32, (tm + hb, 1), 0)
    keep = jnp.logical_or(rows >= hb, i % tiles_per_seq != 0)
    gbuf[...] = jnp.where(keep, _dot(jnp.concatenate([halo_ref[...], h2], axis=0), wg_ref[...]), 0.0)
    g = gbuf[hb:hb + tm, :]
    tail_ref[0] = gbuf[tm:tm + hb, :]
    g1 = gbuf[hb - 1:hb - 1 + tm, :]
    g2 = gbuf[hb - 2:hb - 2 + tm, :]
    _ffn_finish(f, g, g1, g2, h2, x1_ref, wu_ref, wd_ref, cw_ref, cb_ref, y_ref, acc)


def _ffn_prompt(h2, x1, S, w, tm, tf):
    M, D = x1.shape
    F = w['w_gate'].shape[1]
    hb = BF16_SUBLANES
    nt = M // tm
    halo_blocks = tm // hb
    resident = dict(pipeline_mode=pl.Buffered(1)) if tf == F else {}
    y, tail = pl.pallas_call(
        functools.partial(_ffn_prompt_body, tm=tm, tiles_per_seq=S // tm),
        grid=(nt, F // tf),
        in_specs=[pl.BlockSpec((tm, D), lambda i, f: (i, 0)),
                  pl.BlockSpec((hb, D), lambda i, f: (jnp.maximum(i * halo_blocks - 1, 0), 0)),
                  pl.BlockSpec((tm, D), lambda i, f: (i, 0)),
                  pl.BlockSpec((D, tf), lambda i, f: (0, f), **resident),
                  pl.BlockSpec((D, tf), lambda i, f: (0, f), **resident),
                  pl.BlockSpec((tf, D), lambda i, f: (f, 0), **resident),
                  pl.BlockSpec((CONV_WIDTH, tf), lambda i, f: (0, f)),
                  pl.BlockSpec((1, tf), lambda i, f: (0, f))],
        out_specs=(pl.BlockSpec((tm, D), lambda i, f: (i, 0)),
                   pl.BlockSpec((1, hb, tf), lambda i, f: (i, 0, f))),
        out_shape=(jax.ShapeDtypeStruct((M, D), f32), jax.ShapeDtypeStruct((nt, hb, F), f32)),
        scratch_shapes=[pltpu.VMEM((tm, D), f32), pltpu.VMEM((tm + hb, tf), f32)],
        compiler_params=_params("parallel", "arbitrary"),
        name="ffn_prompt",
    )(h2, h2, x1, w['w_gate'], w['w_up'], w['w_down'], w['conv_w'], w['conv_b'])
    return y, tail


def _ffn_sample_body(h2_ref, st_ref, x1_ref, wg_ref, wu_ref, wd_ref, cw_ref, cb_ref, y_ref, g_ref, acc):
    f = pl.program_id(1)
    h2 = h2_ref[...]
    g = _dot(h2, wg_ref[...])
    g_ref[...] = g
    _ffn_finish(f, g, st_ref[1], st_ref[0], h2, x1_ref, wu_ref, wd_ref, cw_ref, cb_ref, y_ref, acc)


def _ffn_sample(h2, x1, st_t, w, tf):
    M, D = x1.shape
    F = w['w_gate'].shape[1]
    return pl.pallas_call(
        _ffn_sample_body,
        grid=(1, F // tf),
        in_specs=[pl.BlockSpec((M, D), lambda i, f: (0, 0)),
                  pl.BlockSpec((CONV_WIDTH - 1, M, tf), lambda i, f: (0, 0, f)),
                  pl.BlockSpec((M, D), lambda i, f: (0, 0)),
                  pl.BlockSpec((D, tf), lambda i, f: (0, f)),
                  pl.BlockSpec((D, tf), lambda i, f: (0, f)),
                  pl.BlockSpec((tf, D), lambda i, f: (f, 0)),
                  pl.BlockSpec((CONV_WIDTH, tf), lambda i, f: (0, f)),
                  pl.BlockSpec((1, tf), lambda i, f: (0, f))],
        out_specs=(pl.BlockSpec((M, D), lambda i, f: (0, 0)),
                   pl.BlockSpec((M, tf), lambda i, f: (0, f))),
        out_shape=(jax.ShapeDtypeStruct((M, D), f32), jax.ShapeDtypeStruct((M, F), f32)),
        scratch_shapes=[pltpu.VMEM((M, D), f32)],
        compiler_params=_params("parallel", "arbitrary"),
        name="ffn_sample",
    )(h2, st_t, x1, w['w_gate'], w['w_up'], w['w_down'], w['conv_w'], w['conv_b'])


def _qk_gain(g):
    g_pe = g[QK_NOPE_DIM:]
    return jnp.concatenate([g[:QK_NOPE_DIM], g_pe, g_pe])


def _rope_table(pos):
    half = QK_ROPE_DIM // 2
    inv = ROPE_BASE ** (-jnp.arange(0, QK_ROPE_DIM, 2, dtype=f32) / QK_ROPE_DIM)
    inv_lane = jnp.concatenate([jnp.zeros((QK_NOPE_DIM,), f32), inv, inv, jnp.zeros((LANE - QK_HEAD_DIM,), f32)])
    ang = pos.astype(f32)[:, None] * inv_lane[None, :]
    cos, sin = jnp.cos(ang), jnp.sin(ang)
    first_half = jnp.arange(LANE)[None, :] < QK_NOPE_DIM + half
    return jnp.concatenate([cos, jnp.where(first_half, -sin, 0.0), jnp.where(first_half, 0.0, sin)], axis=1)


def _prep_weights(g_mix, w_in, g_q_a, w_uq, g_qn, g_kv_a, w_uk, g_kn, w_uv, w_pool, s_pool,
                  g_out, w_o, g_ffn, w_gate, w_up, conv_w, conv_b, w_down):
    D = w_in.shape[0]
    q_rank = g_q_a.shape[0]
    kv_rank = g_kv_a.shape[0]
    pool_w = s_pool.shape[0]
    head_pad = LANE - QK_HEAD_DIM
    half = QK_ROPE_DIM // 2
    main = pool_w + q_rank + kv_rank
    w_in_p = jnp.concatenate([w_in[:, :main], jnp.zeros((D, QK_NOPE_DIM), f32), w_in[:, main:],
                              jnp.zeros((D, head_pad), f32)], axis=1).astype(bf16)
    w_uq_h = w_uq.reshape(q_rank, N_HEADS, QK_HEAD_DIM)
    w_uq_p = jnp.pad(w_uq_h, ((0, 0), (0, 0), (0, head_pad)))
    lo = w_uq_h[:, :, QK_NOPE_DIM:QK_NOPE_DIM + half]
    hi = w_uq_h[:, :, QK_NOPE_DIM + half:]
    w_uq_rot = jnp.concatenate([jnp.zeros((q_rank, N_HEADS, QK_NOPE_DIM), f32), -hi, lo,
                                jnp.zeros((q_rank, N_HEADS, head_pad), f32)], axis=2)
    w_uq_p = jnp.concatenate([w_uq_p.reshape(q_rank, N_HEADS * LANE),
                              w_uq_rot.reshape(q_rank, N_HEADS * LANE)], axis=1).astype(bf16)
    gq = _qk_gain(g_qn) * _qk_gain(g_kn) * (QK_HEAD_DIM ** -0.5 * LOG2_E)
    gq = jnp.tile(jnp.pad(gq, (0, head_pad)), N_HEADS)[None, :]
    w_uk_p = jnp.pad(w_uk, ((0, 0), (0, 0), (0, LANE - QK_NOPE_DIM))).reshape(kv_rank, N_HEADS * LANE)
    w_uk_p = w_uk_p.astype(bf16)
    w_uv_t = w_uv.reshape(kv_rank, N_HEADS * V_HEAD_DIM).T.astype(bf16)
    w_uk_t = w_uk.reshape(kv_rank, N_HEADS * QK_NOPE_DIM).T.astype(bf16)
    eye = jnp.eye(N_HEADS, dtype=f32)
    w_uk_bd = jnp.einsum('chd,hg->hdgc', jnp.pad(w_uk, ((0, 0), (0, 0), (0, LANE - QK_NOPE_DIM))), eye)
    w_uk_bd = w_uk_bd.reshape(N_HEADS * LANE, N_HEADS * kv_rank).astype(bf16)
    w_uv_bd = jnp.einsum('chv,hg->hcgv', w_uv, eye).reshape(N_HEADS * kv_rank, N_HEADS * V_HEAD_DIM).astype(bf16)
    return {
        'pool_w': pool_w,
        'g_mix': g_mix[None, :], 'w_in': w_in_p, 'g_q_a': g_q_a[None, :], 'w_uq': w_uq_p, 'g_q': gq,
        'g_kv_a': g_kv_a[None, :], 'w_uk': w_uk_p, 'w_uv_t': w_uv_t, 'w_uk_t': w_uk_t, 'w_uk_bd': w_uk_bd, 'w_uv_bd': w_uv_bd,
        'w_pool': w_pool.astype(bf16), 's_pool': s_pool[None, :],
        'g_out_pool': g_out[None, :pool_w], 'g_out_attn': g_out[None, pool_w:],
        'w_o': w_o.astype(bf16), 'g_ffn': g_ffn[None, :],
        'w_gate': w_gate.astype(bf16), 'w_up': w_up.astype(bf16), 'w_down': w_down.astype(bf16),
        'conv_w': conv_w, 'conv_b': conv_b[None, :],
    }


def _tile(n, pref):
    t = min(n, pref)
    assert n % t == 0, (n, pref)
    return t


def _ffn_cols(F):
    half = F // 2
    return half if half % LANE == 0 else F


def kernel(x_prompt, x_sample, cache_ckv, cache_kpe, state_pool, state_conv, page_table, g_mix, w_in, g_q_a,
           w_uq, g_qn, g_kv_a, w_uk, g_kn, w_uv, w_pool, s_pool, g_out, w_o, g_ffn, w_gate, w_up, conv_w,
           conv_b, w_down):
    depth = g_mix.shape[0]
    assert depth == 1, "single-layer trunk only"
    B, S, D = x_prompt.shape
    DB, DS = x_sample.shape[:2]
    assert DS == 1, "one new token per sequence"
    n_pages = page_table.shape[1]
    page = cache_ckv.shape[2]
    past_len = n_pages * page
    l = 0
    w = _prep_weights(g_mix[l], w_in[l], g_q_a[l], w_uq[l], g_qn[l], g_kv_a[l], w_uk[l], g_kn[l], w_uv[l],
                      w_pool[l], s_pool[l], g_out[l], w_o[l], g_ffn[l], w_gate[l], w_up[l], conv_w[l],
                      conv_b[l], w_down[l])
    kv_rank = g_kv_a.shape[1]
    F = w_gate.shape[2]
    tf = _ffn_cols(F)

    xp = x_prompt.reshape(B * S, D)
    tq = _tile(S, 512)
    tm = _tile(tq, 512)
    rope_p = _rope_table(jnp.arange(S, dtype=jnp.int32))
    u_p, q_p, ckv_p, kpe_p, k_p, vt_p = _proj(xp, rope_p, S // tm, w, tm, tq)
    mp_p = _pool_prompt(u_p, B, S, w, _tile(S, 512))
    x1_p, h2_p = _flash(q_p, k_p, vt_p, xp, mp_p, w, B, S, tq)
    tmf = _tile(S, 512)
    y_p, tail = _ffn_prompt(h2_p, x1_p, S, w, tmf, F)
    tiles_per_seq = S // tmf
    conv_p = tail[tiles_per_seq - 1::tiles_per_seq, BF16_SUBLANES - (CONV_WIDTH - 1):, :]

    xs = x_sample.reshape(DB, D)
    pos_s = past_len + jnp.arange(DS, dtype=jnp.int32)
    rope_s = jnp.tile(_rope_table(pos_s), (DB, 1))
    u_s, q_s, ckv_s, kpe_s, k_s, _ = _proj(xs, rope_s, 1, w, DB, DB)
    st_pool_t = jnp.transpose(state_pool[l], (1, 0, 2))
    mp_s = _pool_sample(u_s, st_pool_t, past_len, w)
    qabs = _mm(q_s, w['w_uk_bd'], "absorb_q").reshape(DB, N_HEADS, kv_rank).astype(bf16)
    qabs = jnp.pad(qabs, ((0, 0), (0, BF16_SUBLANES - N_HEADS), (0, 0)))
    q3 = q_s.reshape(DB, N_HEADS, LANE)
    qpe = q3[:, :, QK_NOPE_DIM:QK_HEAD_DIM]
    cp = _tile(n_pages, 32)
    ctx = _decode(page_table, q3, k_s.reshape(DB, N_HEADS, LANE), qabs, qpe, ckv_s.reshape(DB, 1, kv_rank),
                  w['w_uk_t'], cache_ckv[l], jnp.swapaxes(cache_kpe[l], 1, 2), cp)
    attn_s = _mm(ctx.reshape(DB, N_HEADS * kv_rank), w['w_uv_bd'], "value_up")
    x1_s, h2_s = _outproj(xs, mp_s, attn_s, w, DB)
    st_conv_t = jnp.transpose(state_conv[l], (1, 0, 2))
    y_s, g_s = _ffn_sample(h2_s, x1_s, st_conv_t, w, tf)

    kpe_sl = slice(QK_NOPE_DIM, QK_HEAD_DIM)
    P = POOL_STATE_LEN
    return (
        y_p.reshape(B, S, D),
        y_s.reshape(DB, DS, D),
        ckv_p.reshape(1, B, S, kv_rank),
        kpe_p[:, kpe_sl].reshape(1, B, S, QK_ROPE_DIM),
        u_p.reshape(B, S, -1)[None, :, S - P:, :],
        conv_p[None],
        ckv_s.reshape(1, DB, DS, kv_rank),
        kpe_s[:, kpe_sl].reshape(1, DB, DS, QK_ROPE_DIM),
        jnp.concatenate([state_pool[l], u_s[:, None, :]], axis=1)[None, :, -P:, :],
        jnp.concatenate([state_conv[l], g_s[:, None, :]], axis=1)[None, :, -(CONV_WIDTH - 1):, :],
    )
```

```python
import functools

import jax
import jax.numpy as jnp
from jax import lax
from jax.experimental import pallas as pl
from jax.experimental.pallas import tpu as pltpu

N_HEADS = 8
QK_NOPE_DIM = 64
QK_ROPE_DIM = 32
QK_HEAD_DIM = QK_NOPE_DIM + QK_ROPE_DIM
V_HEAD_DIM = 64
POOL_WINDOWS = (2, 4, 8, 16)
POOL_STATE_LEN = max(POOL_WINDOWS) - 1
CONV_WIDTH = 3
ROPE_BASE = 10000.0
RMS_EPS = 1e-6
LOG2_E = 1.4426950408889634

LANE = 128
SUBLANES = 8
MXU_DIM = 256
BF16_SUBLANES = 16
VMEM_LIMIT = 48 * 1024 * 1024
DECODE_SLOTS = 3
LOOKAHEAD = 2

_NT = (((1,), (1,)), ((), ()))

bf16 = jnp.bfloat16
f32 = jnp.float32


def _rms(x, width):
    return x * lax.rsqrt(jnp.sum(x * x, axis=-1, keepdims=True) * (1.0 / width) + RMS_EPS)


def _dot(a, b):
    return jnp.dot(a, b, preferred_element_type=f32)


def _params(*sem):
    return pltpu.CompilerParams(dimension_semantics=sem, vmem_limit_bytes=VMEM_LIMIT)


def _proj_body(x_ref, rope_ref, gmix_ref, win_ref, gqa_ref, wuq_ref, gq_ref, gkva_ref, wuk_ref, wuvt_ref,
               u_ref, q_ref, ckv_ref, kpe_ref, k_ref, vt_ref, *, pool_w, q_rank, kv_rank):
    tm = x_ref.shape[0]
    group = min(tm, MXU_DIM)
    groups = [slice(r, r + group) for r in range(0, tm, group)]
    half = QK_ROPE_DIM // 2
    hq = N_HEADS * LANE

    def in_projection(rows):
        x = x_ref[rows, :]
        h = (_rms(x, x.shape[-1]) * gmix_ref[...]).astype(bf16)
        return _dot(h, win_ref[...])

    def up_projections(rows, proj):
        u_ref[rows, :] = proj[:, :pool_w]
        o = pool_w
        cq = (_rms(proj[:, o:o + q_rank], q_rank) * gqa_ref[...]).astype(bf16)
        o += q_rank
        ckv = _rms(proj[:, o:o + kv_rank], kv_rank) * gkva_ref[...]
        o += kv_rank
        ckv_ref[rows, :] = ckv
        ckv_b = ckv.astype(bf16)
        qraw = _dot(cq, wuq_ref[...])
        kn = _dot(ckv_b, wuk_ref[...])
        vt_ref[0, :, rows] = lax.dot_general(wuvt_ref[...], ckv_b, _NT, preferred_element_type=f32).astype(bf16)
        return proj[:, o:o + LANE], qraw, kn

    def head_norms(rows, raw_kpe, qraw, kn):
        cos = rope_ref[rows, 0:LANE]
        sin_lo = rope_ref[rows, LANE:2 * LANE]
        sin_hi = rope_ref[rows, 2 * LANE:3 * LANE]
        sin_abs = sin_hi - sin_lo
        kpe = raw_kpe * cos + pltpu.roll(raw_kpe, LANE - half, 1) * sin_lo + pltpu.roll(raw_kpe, half, 1) * sin_hi
        kpe_ref[rows, :] = kpe
        for hd in range(N_HEADS):
            sl = slice(hd * LANE, (hd + 1) * LANE)
            qh = qraw[:, sl] * cos + qraw[:, hq + hd * LANE:hq + (hd + 1) * LANE] * sin_abs
            q_ref[rows, sl] = (_rms(qh, QK_HEAD_DIM) * gq_ref[:, sl]).astype(bf16)
            kh = kn[:, sl] + kpe
            k_ref[rows, sl] = _rms(kh, QK_HEAD_DIM).astype(bf16)

    projs = [in_projection(rows) for rows in groups]
    ups = [up_projections(rows, proj) for rows, proj in zip(groups, projs)]
    for rows, up in zip(groups, ups):
        head_norms(rows, *up)


def _proj(x, rope, n_rope_blocks, w, tm, slab):
    M, D = x.shape
    pool_w = w['pool_w']
    q_rank = w['g_q_a'].shape[1]
    kv_rank = w['g_kv_a'].shape[1]
    hq = N_HEADS * LANE
    hv = N_HEADS * V_HEAD_DIM
    per_slab = slab // tm
    const = lambda a: pl.BlockSpec(a.shape, lambda i: (0, 0))
    row = lambda n: pl.BlockSpec((tm, n), lambda i: (i, 0))
    out_shapes = (
        jax.ShapeDtypeStruct((M, pool_w), f32), jax.ShapeDtypeStruct((M, hq), bf16),
        jax.ShapeDtypeStruct((M, kv_rank), f32), jax.ShapeDtypeStruct((M, LANE), f32),
        jax.ShapeDtypeStruct((M, hq), bf16), jax.ShapeDtypeStruct((M // slab, hv, slab), bf16))
    return pl.pallas_call(
        functools.partial(_proj_body, pool_w=pool_w, q_rank=q_rank, kv_rank=kv_rank),
        grid=(M // tm,),
        in_specs=[row(D), pl.BlockSpec((tm, 3 * LANE), lambda i: (i % n_rope_blocks, 0)),
                  const(w['g_mix']), const(w['w_in']), const(w['g_q_a']), const(w['w_uq']),
                  const(w['g_q']), const(w['g_kv_a']), const(w['w_uk']), const(w['w_uv_t'])],
        out_specs=(row(pool_w), row(hq), row(kv_rank), row(LANE), row(hq),
                   pl.BlockSpec((1, hv, tm), lambda i: (i // per_slab, 0, i % per_slab))),
        out_shape=out_shapes,
        compiler_params=_params("parallel"),
        name="in_proj",
    )(x, rope, w['g_mix'], w['w_in'], w['g_q_a'], w['w_uq'], w['g_q'], w['g_kv_a'], w['w_uk'], w['w_uv_t'])


def _pool_finish(diffs, wpool_ref, spool_ref, gout_ref, o_ref):
    ys = []
    for g, d in enumerate(diffs):
        sl = slice(g * LANE, (g + 1) * LANE)
        ys.append(_dot(d.astype(bf16), wpool_ref[g]) * spool_ref[:, sl])
    width = LANE * len(ys)
    ssq = sum(jnp.sum(y * y, axis=-1, keepdims=True) for y in ys)
    scale = lax.rsqrt(ssq * (1.0 / width) + RMS_EPS)
    for g, y in enumerate(ys):
        sl = slice(g * LANE, (g + 1) * LANE)
        o_ref[:, sl] = (y * scale * gout_ref[:, sl]).astype(o_ref.dtype)


def _pool_prompt_body(u_ref, wpool_ref, spool_ref, gout_ref, o_ref, buf, lvl, *, ts):
    j = pl.program_id(1)
    halo = POOL_STATE_LEN + 1

    @pl.when(j == 0)
    def _():
        buf[0:halo, :] = jnp.zeros((halo, buf.shape[1]), f32)

    @pl.when(j > 0)
    def _():
        buf[0:halo, :] = buf[ts:ts + halo, :]

    buf[halo:halo + ts, :] = u_ref[...]
    pos = j * ts + lax.broadcasted_iota(jnp.int32, (ts, 1), 0)
    diffs = []
    end = halo + ts
    for g, wdw in enumerate(POOL_WINDOWS):
        sl = slice(g * LANE, (g + 1) * LANE)
        x = buf[halo:end, sl]
        src, width = buf, 1
        while width < wdw:
            first = 2 * width - 1
            lvl[first:end, sl] = src[first:end, sl] + src[first - width:end - width, sl]
            src, width = lvl, 2 * width
        win = src[halo:end, sl]
        cnt = jnp.minimum(wdw, pos + 1).astype(f32)
        diffs.append(win / cnt - x)
    _pool_finish(diffs, wpool_ref, spool_ref, gout_ref, o_ref)


def _pool_prompt(u, B, S, w, ts):
    M, W = u.shape
    ns = S // ts
    const2 = lambda a: pl.BlockSpec(a.shape, lambda b, j: (0,) * a.ndim)
    return pl.pallas_call(
        functools.partial(_pool_prompt_body, ts=ts),
        grid=(B, ns),
        in_specs=[pl.BlockSpec((ts, W), lambda b, j: (b * ns + j, 0)),
                  const2(w['w_pool']), const2(w['s_pool']), const2(w['g_out_pool'])],
        out_specs=pl.BlockSpec((ts, W), lambda b, j: (b * ns + j, 0)),
        out_shape=jax.ShapeDtypeStruct((M, W), bf16),
        scratch_shapes=[pltpu.VMEM((ts + POOL_STATE_LEN + 1, W), f32)] * 2,
        compiler_params=_params("arbitrary", "arbitrary"),
        name="pool_prompt",
    )(u, w['w_pool'], w['s_pool'], w['g_out_pool'])


def _pool_sample_body(u_ref, st_ref, wpool_ref, spool_ref, gout_ref, o_ref, *, pos):
    diffs = []
    for g, wdw in enumerate(POOL_WINDOWS):
        sl = slice(g * LANE, (g + 1) * LANE)
        x = u_ref[:, sl]
        win = x
        for k in range(1, wdw):
            win = win + st_ref[POOL_STATE_LEN - k, :, sl]
        diffs.append(win / float(min(wdw, pos + 1)) - x)
    _pool_finish(diffs, wpool_ref, spool_ref, gout_ref, o_ref)


def _pool_sample(u, st_t, pos, w):
    M, W = u.shape
    return pl.pallas_call(
        functools.partial(_pool_sample_body, pos=pos),
        out_shape=jax.ShapeDtypeStruct((M, W), bf16),
        compiler_params=pltpu.CompilerParams(vmem_limit_bytes=VMEM_LIMIT),
        name="pool_sample",
    )(u, st_t, w['w_pool'], w['s_pool'], w['g_out_pool'])


def _mix_out(x, mp, attn, gatt_ref, wo_ref, gffn_ref, x1_ref, h2_ref):
    na = (_rms(attn, attn.shape[-1]) * gatt_ref[...]).astype(bf16)
    x1 = x + _dot(jnp.concatenate([mp, na], axis=1), wo_ref[...])
    x1_ref[...] = x1
    h2_ref[...] = (_rms(x1, x1.shape[-1]) * gffn_ref[...]).astype(bf16)


def _flash_body(q_ref, k_ref, vt_ref, x_ref, mp_ref, gatt_ref, wo_ref, gffn_ref, x1_ref, h2_ref,
                m_scr, acc_scr, *, tq):
    i = pl.program_id(1)
    heads = m_scr.shape[0]
    key_idx = lax.broadcasted_iota(jnp.int32, (tq, tq), 0)
    qry_idx = lax.broadcasted_iota(jnp.int32, (tq, tq), 1)
    m_scr[...] = jnp.full(m_scr.shape, -1e30, f32)
    acc_scr[...] = jnp.zeros(acc_scr.shape, f32)
    ones_rows = jnp.ones((BF16_SUBLANES, tq), bf16)

    def step(j, masked):
        start = pl.multiple_of(j * tq, tq)

        def scores(hh):
            sl = slice(hh * LANE, (hh + 1) * LANE)
            return lax.dot_general(k_ref[pl.ds(start, tq), sl], q_ref[:, sl], _NT, preferred_element_type=f32)

        ready = [scores(hh) for hh in range(min(LOOKAHEAD, heads))]
        for hh in range(heads):
            st = ready.pop(0)
            if hh + LOOKAHEAD < heads:
                ready.append(scores(hh + LOOKAHEAD))
            if masked:
                st = jnp.where(key_idx <= qry_idx, st, -1e30)
            m = m_scr[hh]
            m_new = jnp.maximum(m, jnp.max(st, axis=0, keepdims=True))
            alpha = jnp.exp2(m - m_new)
            pt = jnp.exp2(st - m_new).astype(bf16)
            vt = jnp.concatenate([vt_ref[j, hh * V_HEAD_DIM:(hh + 1) * V_HEAD_DIM, :], ones_rows], axis=0)
            acc_scr[hh] = acc_scr[hh] * alpha + _dot(vt, pt)
            m_scr[hh] = m_new

    def body(j, carry):
        step(j, False)
        return carry

    lax.fori_loop(0, i, body, 0)
    step(i, True)
    dv = V_HEAD_DIM
    out_t = jnp.concatenate([acc_scr[hh, 0:dv] / acc_scr[hh, dv:dv + 1] for hh in range(heads)], axis=0)
    _mix_out(x_ref[...], mp_ref[...], out_t.T, gatt_ref, wo_ref, gffn_ref, x1_ref, h2_ref)


def _flash(q, k, vt, x, mp, w, B, S, tq):
    M, D = x.shape
    nq = S // tq
    const = lambda a: pl.BlockSpec(a.shape, lambda b, i: (0, 0))
    tile = lambda n: pl.BlockSpec((tq, n), lambda b, i: (b * nq + i, 0))
    return pl.pallas_call(
        functools.partial(_flash_body, tq=tq),
        grid=(B, nq),
        in_specs=[tile(N_HEADS * LANE),
                  pl.BlockSpec((S, N_HEADS * LANE), lambda b, i: (b, 0)),
                  pl.BlockSpec((nq, N_HEADS * V_HEAD_DIM, tq), lambda b, i: (b, 0, 0)),
                  tile(D), tile(mp.shape[1]),
                  const(w['g_out_attn']), const(w['w_o']), const(w['g_ffn'])],
        out_specs=(tile(D), tile(D)),
        out_shape=(jax.ShapeDtypeStruct((M, D), f32), jax.ShapeDtypeStruct((M, D), bf16)),
        scratch_shapes=[pltpu.VMEM((N_HEADS, 1, tq), f32),
                        pltpu.VMEM((N_HEADS, V_HEAD_DIM + BF16_SUBLANES, tq), f32)],
        compiler_params=_params("parallel", "arbitrary"),
        name="flash_prompt",
    )(q, k, vt, x, mp, w['g_out_attn'], w['w_o'], w['g_ffn'])


def _mm_body(a_ref, b_ref, o_ref):
    o_ref[...] = _dot(a_ref[...].astype(bf16), b_ref[...])


def _mm(a, b, name):
    return pl.pallas_call(
        _mm_body,
        out_shape=jax.ShapeDtypeStruct((a.shape[0], b.shape[1]), f32),
        compiler_params=pltpu.CompilerParams(vmem_limit_bytes=VMEM_LIMIT),
        name=name,
    )(a, b)


def _decode_body(pt_ref, q_ref, knew_ref, qabs_ref, qpe_ref, cnew_ref, wukt_ref, cache_ckv, cache_kpet,
                 o_ref, ckv_buf, kpe_buf, sems, lhs, ct_scr, part_scr, num_scr, pes_scr, m_scr, l_scr, acc_scr,
                 *, n_pages, cp, nc):
    f = pl.program_id(0)
    total = pl.num_programs(0) - 1
    n_slots = ckv_buf.shape[0]
    ahead = n_slots - 1
    ring = f % n_slots
    nope_rows = wukt_ref.shape[0]
    page = cache_ckv.shape[1]
    tokens = cp * page

    def copies(chunk, slot_idx):
        base = (chunk // nc) * n_pages + (chunk % nc) * cp
        out = []
        for p in range(cp):
            phys = pt_ref[base + p]
            rows = pl.ds(p * page, page)
            out.append(pltpu.make_async_copy(cache_ckv.at[phys], ckv_buf.at[slot_idx, rows], sems.at[0, slot_idx]))
            out.append(pltpu.make_async_copy(cache_kpet.at[phys], kpe_buf.at[slot_idx, :, rows],
                                             sems.at[1, slot_idx]))
        return out

    @pl.when(f == 0)
    def _():
        for k in range(ahead):
            for cpy in copies(jnp.minimum(k, total - 1), k):
                cpy.start()
        ct_scr[1] = jnp.zeros(ct_scr.shape[1:], bf16)
        part_scr[1] = jnp.zeros(part_scr.shape[1:], bf16)
        num_scr[1] = jnp.zeros(num_scr.shape[1:], f32)
        pes_scr[1] = jnp.zeros(pes_scr.shape[1:], f32)
        m_scr[...] = jnp.zeros_like(m_scr)
        l_scr[...] = jnp.ones_like(l_scr)
        acc_scr[...] = jnp.zeros_like(acc_scr)
        lhs[0:nope_rows, :] = wukt_ref[...]

    @pl.when(f % nc == 0)
    def _():
        lhs[nope_rows:nope_rows + BF16_SUBLANES, :] = qabs_ref[0]

    @pl.when((f >= 1) & ((f - 1) % nc == 0))
    def _():
        s_new = jnp.sum(q_ref[0].astype(f32) * knew_ref[0].astype(f32), axis=-1, keepdims=True)
        m_scr[...] = s_new
        l_scr[...] = jnp.ones_like(l_scr)
        acc_scr[...] = jnp.broadcast_to(cnew_ref[0], acc_scr.shape)

    qpe = qpe_ref[0]
    ones = jnp.ones(qpe.shape, bf16)
    fold = (lax.broadcasted_iota(jnp.int32, (N_HEADS, SUBLANES * N_HEADS), 1) // SUBLANES
            == lax.broadcasted_iota(jnp.int32, (N_HEADS, SUBLANES * N_HEADS), 0)).astype(bf16)

    def main(cur):
        prev = 1 - cur
        for cpy in copies(jnp.minimum(f, total - 1), ring):
            cpy.wait()
        ssq = _dot(fold, part_scr[prev]) + pes_scr[prev]
        s = num_scr[prev] * lax.rsqrt(ssq * (1.0 / QK_HEAD_DIM) + RMS_EPS)
        m = m_scr[...]
        m_new = jnp.maximum(m, jnp.max(s, axis=-1, keepdims=True))
        alpha = jnp.exp2(m - m_new)
        p = jnp.exp2(s - m_new)
        l_scr[...] = l_scr[...] * alpha + jnp.sum(p, axis=-1, keepdims=True)
        m_scr[...] = m_new
        ct = ckv_buf[ring].astype(bf16)
        ct_scr[cur] = ct
        for cpy in copies(jnp.minimum(f + ahead, total - 1), (f + ahead) % n_slots):
            cpy.start()
        kt = lax.dot_general(lhs[...], ct, _NT, preferred_element_type=f32)
        kn = kt[0:nope_rows].reshape(N_HEADS, nope_rows // (SUBLANES * N_HEADS), SUBLANES, tokens)
        part_scr[cur] = jnp.sum(kn * kn, axis=1).reshape(SUBLANES * N_HEADS, tokens).astype(bf16)
        kp = kpe_buf[ring]
        num_scr[cur] = kt[nope_rows:nope_rows + N_HEADS] + _dot(qpe, kp.astype(bf16))
        pes_scr[cur] = _dot(ones, (kp * kp).astype(bf16))
        acc_scr[...] = acc_scr[...] * alpha + _dot(p.astype(bf16), ct_scr[prev])

    @pl.when(f % 2 == 0)
    def _():
        main(0)

    @pl.when(f % 2 == 1)
    def _():
        main(1)

    @pl.when((f >= 1) & (f % nc == 0))
    def _():
        o_ref[0] = acc_scr[...] / l_scr[...]

    @pl.when(f == total)
    def _():
        for k in range(1, ahead + 1):
            for cpy in copies(total - 1, (f + k) % n_slots):
                cpy.wait()


def _decode(page_table, q3, knew3, qabs, qpe, cnew, wukt, cache_ckv, cache_kpet, cp):
    DB, n_pages = page_table.shape
    page, kv_rank = cache_ckv.shape[1:]
    rope = cache_kpet.shape[1]
    nc = n_pages // cp
    T = cp * page
    total = DB * nc
    head_seq = lambda f, pt: jnp.minimum(f, total - 1) // nc
    tail_seq = lambda f, pt: jnp.maximum(f - 1, 0) // nc
    per_seq = lambda a, seq: pl.BlockSpec((1,) + a.shape[1:], lambda f, pt: (seq(f, pt),) + (0,) * (a.ndim - 1))
    grid_spec = pltpu.PrefetchScalarGridSpec(
        num_scalar_prefetch=1,
        grid=(total + 1,),
        in_specs=[per_seq(q3, tail_seq), per_seq(knew3, tail_seq), per_seq(qabs, head_seq), per_seq(qpe, head_seq),
                  per_seq(cnew, tail_seq), pl.BlockSpec(wukt.shape, lambda f, pt: (0, 0)),
                  pl.BlockSpec(memory_space=pl.ANY), pl.BlockSpec(memory_space=pl.ANY)],
        out_specs=pl.BlockSpec((1, N_HEADS, kv_rank), lambda f, pt: (tail_seq(f, pt), 0, 0)),
        scratch_shapes=[pltpu.VMEM((DECODE_SLOTS, T, kv_rank), f32), pltpu.VMEM((DECODE_SLOTS, rope, T), f32),
                        pltpu.SemaphoreType.DMA((2, DECODE_SLOTS)),
                        pltpu.VMEM((wukt.shape[0] + BF16_SUBLANES, kv_rank), bf16),
                        pltpu.VMEM((2, T, kv_rank), bf16), pltpu.VMEM((2, SUBLANES * N_HEADS, T), bf16),
                        pltpu.VMEM((2, N_HEADS, T), f32), pltpu.VMEM((2, N_HEADS, T), f32),
                        pltpu.VMEM((N_HEADS, 1), f32), pltpu.VMEM((N_HEADS, 1), f32),
                        pltpu.VMEM((N_HEADS, kv_rank), f32)])
    return pl.pallas_call(
        functools.partial(_decode_body, n_pages=n_pages, cp=cp, nc=nc),
        grid_spec=grid_spec,
        out_shape=jax.ShapeDtypeStruct((DB, N_HEADS, kv_rank), f32),
        compiler_params=_params("arbitrary"),
        name="decode_attn",
    )(page_table.reshape(-1), q3, knew3, qabs, qpe, cnew, wukt, cache_ckv, cache_kpet)


def _outproj_body(x_ref, mp_ref, attn_ref, gatt_ref, wo_ref, gffn_ref, x1_ref, h2_ref):
    _mix_out(x_ref[...], mp_ref[...], attn_ref[...], gatt_ref, wo_ref, gffn_ref, x1_ref, h2_ref)


def _outproj(x, mp, attn, w, tm):
    M, D = x.shape
    const = lambda a: pl.BlockSpec(a.shape, lambda i: (0, 0))
    row = lambda n: pl.BlockSpec((tm, n), lambda i: (i, 0))
    return pl.pallas_call(
        _outproj_body,
        grid=(M // tm,),
        in_specs=[row(D), row(mp.shape[1]), row(attn.shape[1]),
                  const(w['g_out_attn']), const(w['w_o']), const(w['g_ffn'])],
        out_specs=(row(D), row(D)),
        out_shape=(jax.ShapeDtypeStruct((M, D), f32), jax.ShapeDtypeStruct((M, D), bf16)),
        compiler_params=_params("parallel"),
        name="out_proj",
    )(x, mp, attn, w['g_out_attn'], w['w_o'], w['g_ffn'])


def _ffn_finish(f, g, g1, g2, h2, x1_ref, wu_ref, wd_ref, cw_ref, cb_ref, y_ref, acc):
    gate = cb_ref[...] + g2 * cw_ref[0:1, :] + g1 * cw_ref[1:2, :] + g * cw_ref[2:3, :]
    act = (gate * jax.nn.sigmoid(gate)) * _dot(h2, wu_ref[...])
    part = _dot(act.astype(bf16), wd_ref[...])
    if acc is None:
        y_ref[...] = x1_ref[...] + part
        return

    @pl.when(f == 0)
    def _():
        acc[...] = x1_ref[...] + part

    @pl.when(f > 0)
    def _():
        acc[...] += part

    @pl.when(f == pl.num_programs(1) - 1)
    def _():
        y_ref[...] = acc[...]


def _ffn_prompt_body(h2_ref, halo_ref, x1_ref, wg_ref, wu_ref, wd_ref, cw_ref, cb_ref,
                     y_ref, tail_ref, gbuf, *, tm, tiles_per_seq):
    i = pl.program_id(0)
    hb = BF16_SUBLANES
    h2 = h2_ref[...]
    rows = lax.broadcasted_iota(jnp.int32, (tm + hb, 1), 0)
    keep = jnp.logical_or(rows >= hb, i % tiles_per_seq != 0)
    gbuf[...] = jnp.where(keep, _dot(jnp.concatenate([halo_ref[...], h2], axis=0), wg_ref[...]), 0.0)
    g = gbuf[hb:hb + tm, :]
    tail_ref[0] = gbuf[tm:tm + hb, :]
    g1 = gbuf[hb - 1:hb - 1 + tm, :]
    g2 = gbuf[hb - 2:hb - 2 + tm, :]
    _ffn_finish(None, g, g1, g2, h2, x1_ref, wu_ref, wd_ref, cw_ref, cb_ref, y_ref, None)


def _ffn_prompt(h2, x1, S, w, tm):
    M, D = x1.shape
    F = w['w_gate'].shape[1]
    hb = BF16_SUBLANES
    nt = M // tm
    halo_blocks = tm // hb
    resident = lambda shape: pl.BlockSpec(shape, lambda i: (0, 0), pipeline_mode=pl.Buffered(1))
    y, tail = pl.pallas_call(
        functools.partial(_ffn_prompt_body, tm=tm, tiles_per_seq=S // tm),
        grid=(nt,),
        in_specs=[pl.BlockSpec((tm, D), lambda i: (i, 0)),
                  pl.BlockSpec((hb, D), lambda i: (jnp.maximum(i * halo_blocks - 1, 0), 0)),
                  pl.BlockSpec((tm, D), lambda i: (i, 0)),
                  resident((D, F)), resident((D, F)), resident((F, D)),
                  pl.BlockSpec((CONV_WIDTH, F), lambda i: (0, 0)),
                  pl.BlockSpec((1, F), lambda i: (0, 0))],
        out_specs=(pl.BlockSpec((tm, D), lambda i: (i, 0)),
                   pl.BlockSpec((1, hb, F), lambda i: (i, 0, 0))),
        out_shape=(jax.ShapeDtypeStruct((M, D), f32), jax.ShapeDtypeStruct((nt, hb, F), f32)),
        scratch_shapes=[pltpu.VMEM((tm + hb, F), f32)],
        compiler_params=_params("parallel"),
        name="ffn_prompt",
    )(h2, h2, x1, w['w_gate'], w['w_up'], w['w_down'], w['conv_w'], w['conv_b'])
    return y, tail


def _ffn_sample_body(h2_ref, st_ref, x1_ref, wg_ref, wu_ref, wd_ref, cw_ref, cb_ref, y_ref, g_ref, acc):
    f = pl.program_id(1)
    h2 = h2_ref[...]
    g = _dot(h2, wg_ref[...])
    g_ref[...] = g
    _ffn_finish(f, g, st_ref[1], st_ref[0], h2, x1_ref, wu_ref, wd_ref, cw_ref, cb_ref, y_ref, acc)


def _ffn_sample(h2, x1, st_t, w, tf):
    M, D = x1.shape
    F = w['w_gate'].shape[1]
    return pl.pallas_call(
        _ffn_sample_body,
        grid=(1, F // tf),
        in_specs=[pl.BlockSpec((M, D), lambda i, f: (0, 0)),
                  pl.BlockSpec((CONV_WIDTH - 1, M, tf), lambda i, f: (0, 0, f)),
                  pl.BlockSpec((M, D), lambda i, f: (0, 0)),
                  pl.BlockSpec((D, tf), lambda i, f: (0, f)),
                  pl.BlockSpec((D, tf), lambda i, f: (0, f)),
                  pl.BlockSpec((tf, D), lambda i, f: (f, 0)),
                  pl.BlockSpec((CONV_WIDTH, tf), lambda i, f: (0, f)),
                  pl.BlockSpec((1, tf), lambda i, f: (0, f))],
        out_specs=(pl.BlockSpec((M, D), lambda i, f: (0, 0)),
                   pl.BlockSpec((M, tf), lambda i, f: (0, f))),
        out_shape=(jax.ShapeDtypeStruct((M, D), f32), jax.ShapeDtypeStruct((M, F), f32)),
        scratch_shapes=[pltpu.VMEM((M, D), f32)],
        compiler_params=_params("parallel", "arbitrary"),
        name="ffn_sample",
    )(h2, st_t, x1, w['w_gate'], w['w_up'], w['w_down'], w['conv_w'], w['conv_b'])


def _qk_gain(g):
    g_pe = g[QK_NOPE_DIM:]
    return jnp.concatenate([g[:QK_NOPE_DIM], g_pe, g_pe])


def _rope_table(pos):
    half = QK_ROPE_DIM // 2
    inv = ROPE_BASE ** (-jnp.arange(0, QK_ROPE_DIM, 2, dtype=f32) / QK_ROPE_DIM)
    inv_lane = jnp.concatenate([jnp.zeros((QK_NOPE_DIM,), f32), inv, inv, jnp.zeros((LANE - QK_HEAD_DIM,), f32)])
    ang = pos.astype(f32)[:, None] * inv_lane[None, :]
    cos, sin = jnp.cos(ang), jnp.sin(ang)
    first_half = jnp.arange(LANE)[None, :] < QK_NOPE_DIM + half
    return jnp.concatenate([cos, jnp.where(first_half, -sin, 0.0), jnp.where(first_half, 0.0, sin)], axis=1)


def _prep_weights(g_mix, w_in, g_q_a, w_uq, g_qn, g_kv_a, w_uk, g_kn, w_uv, w_pool, s_pool,
                  g_out, w_o, g_ffn, w_gate, w_up, conv_w, conv_b, w_down):
    D = w_in.shape[0]
    q_rank = g_q_a.shape[0]
    kv_rank = g_kv_a.shape[0]
    pool_w = s_pool.shape[0]
    head_pad = LANE - QK_HEAD_DIM
    half = QK_ROPE_DIM // 2
    main = pool_w + q_rank + kv_rank
    w_in_p = jnp.concatenate([w_in[:, :main], jnp.zeros((D, QK_NOPE_DIM), f32), w_in[:, main:],
                              jnp.zeros((D, head_pad), f32)], axis=1).astype(bf16)
    w_uq_h = w_uq.reshape(q_rank, N_HEADS, QK_HEAD_DIM)
    w_uq_p = jnp.pad(w_uq_h, ((0, 0), (0, 0), (0, head_pad)))
    lo = w_uq_h[:, :, QK_NOPE_DIM:QK_NOPE_DIM + half]
    hi = w_uq_h[:, :, QK_NOPE_DIM + half:]
    w_uq_rot = jnp.concatenate([jnp.zeros((q_rank, N_HEADS, QK_NOPE_DIM), f32), -hi, lo,
                                jnp.zeros((q_rank, N_HEADS, head_pad), f32)], axis=2)
    w_uq_p = jnp.concatenate([w_uq_p.reshape(q_rank, N_HEADS * LANE),
                              w_uq_rot.reshape(q_rank, N_HEADS * LANE)], axis=1).astype(bf16)
    gq = _qk_gain(g_qn) * _qk_gain(g_kn) * (QK_HEAD_DIM ** -0.5 * LOG2_E)
    gq = jnp.tile(jnp.pad(gq, (0, head_pad)), N_HEADS)[None, :]
    w_uk_p = jnp.pad(w_uk, ((0, 0), (0, 0), (0, LANE - QK_NOPE_DIM))).reshape(kv_rank, N_HEADS * LANE)
    w_uk_p = w_uk_p.astype(bf16)
    w_uv_t = w_uv.reshape(kv_rank, N_HEADS * V_HEAD_DIM).T.astype(bf16)
    w_uk_t = w_uk.reshape(kv_rank, N_HEADS * QK_NOPE_DIM).T.astype(bf16)
    eye = jnp.eye(N_HEADS, dtype=f32)
    w_uk_bd = jnp.einsum('chd,hg->hdgc', jnp.pad(w_uk, ((0, 0), (0, 0), (0, LANE - QK_NOPE_DIM))), eye)
    w_uk_bd = w_uk_bd.reshape(N_HEADS * LANE, N_HEADS * kv_rank).astype(bf16)
    w_uv_bd = jnp.einsum('chv,hg->hcgv', w_uv, eye).reshape(N_HEADS * kv_rank, N_HEADS * V_HEAD_DIM).astype(bf16)
    return {
        'pool_w': pool_w,
        'g_mix': g_mix[None, :], 'w_in': w_in_p, 'g_q_a': g_q_a[None, :], 'w_uq': w_uq_p, 'g_q': gq,
        'g_kv_a': g_kv_a[None, :], 'w_uk': w_uk_p, 'w_uv_t': w_uv_t, 'w_uk_t': w_uk_t, 'w_uk_bd': w_uk_bd, 'w_uv_bd': w_uv_bd,
        'w_pool': w_pool.astype(bf16), 's_pool': s_pool[None, :],
        'g_out_pool': g_out[None, :pool_w], 'g_out_attn': g_out[None, pool_w:],
        'w_o': w_o.astype(bf16), 'g_ffn': g_ffn[None, :],
        'w_gate': w_gate.astype(bf16), 'w_up': w_up.astype(bf16), 'w_down': w_down.astype(bf16),
        'conv_w': conv_w, 'conv_b': conv_b[None, :],
    }


def _tile(n, pref):
    t = min(n, pref)
    assert n % t == 0, (n, pref)
    return t


def _ffn_cols(F):
    half = F // 2
    return half if half % LANE == 0 else F


def kernel(x_prompt, x_sample, cache_ckv, cache_kpe, state_pool, state_conv, page_table, g_mix, w_in, g_q_a,
           w_uq, g_qn, g_kv_a, w_uk, g_kn, w_uv, w_pool, s_pool, g_out, w_o, g_ffn, w_gate, w_up, conv_w,
           conv_b, w_down):
    depth = g_mix.shape[0]
    assert depth == 1, "single-layer trunk only"
    B, S, D = x_prompt.shape
    DB, DS = x_sample.shape[:2]
    assert DS == 1, "one new token per sequence"
    n_pages = page_table.shape[1]
    page = cache_ckv.shape[2]
    past_len = n_pages * page
    l = 0
    w = _prep_weights(g_mix[l], w_in[l], g_q_a[l], w_uq[l], g_qn[l], g_kv_a[l], w_uk[l], g_kn[l], w_uv[l],
                      w_pool[l], s_pool[l], g_out[l], w_o[l], g_ffn[l], w_gate[l], w_up[l], conv_w[l],
                      conv_b[l], w_down[l])
    kv_rank = g_kv_a.shape[1]
    F = w_gate.shape[2]
    tf = _ffn_cols(F)

    xp = x_prompt.reshape(B * S, D)
    tq = _tile(S, 512)
    tm = _tile(tq, 512)
    rope_p = _rope_table(jnp.arange(S, dtype=jnp.int32))
    u_p, q_p, ckv_p, kpe_p, k_p, vt_p = _proj(xp, rope_p, S // tm, w, tm, tq)
    mp_p = _pool_prompt(u_p, B, S, w, _tile(S, 512))
    x1_p, h2_p = _flash(q_p, k_p, vt_p, xp, mp_p, w, B, S, tq)
    tmf = _tile(S, 512)
    y_p, tail = _ffn_prompt(h2_p, x1_p, S, w, tmf)
    tiles_per_seq = S // tmf
    conv_p = tail[tiles_per_seq - 1::tiles_per_seq, BF16_SUBLANES - (CONV_WIDTH - 1):, :]

    xs = x_sample.reshape(DB, D)
    pos_s = past_len + jnp.arange(DS, dtype=jnp.int32)
    rope_s = jnp.tile(_rope_table(pos_s), (DB, 1))
    u_s, q_s, ckv_s, kpe_s, k_s, _ = _proj(xs, rope_s, 1, w, DB, DB)
    st_pool_t = jnp.transpose(state_pool[l], (1, 0, 2))
    mp_s = _pool_sample(u_s, st_pool_t, past_len, w)
    qabs = _mm(q_s, w['w_uk_bd'], "absorb_q").reshape(DB, N_HEADS, kv_rank).astype(bf16)
    qabs = jnp.pad(qabs, ((0, 0), (0, BF16_SUBLANES - N_HEADS), (0, 0)))
    q3 = q_s.reshape(DB, N_HEADS, LANE)
    qpe = q3[:, :, QK_NOPE_DIM:QK_HEAD_DIM]
    cp = _tile(n_pages, 32)
    ctx = _decode(page_table, q3, k_s.reshape(DB, N_HEADS, LANE), qabs, qpe, ckv_s.reshape(DB, 1, kv_rank),
                  w['w_uk_t'], cache_ckv[l], jnp.swapaxes(cache_kpe[l], 1, 2), cp)
    attn_s = _mm(ctx.reshape(DB, N_HEADS * kv_rank), w['w_uv_bd'], "value_up")
    x1_s, h2_s = _outproj(xs, mp_s, attn_s, w, DB)
    st_conv_t = jnp.transpose(state_conv[l], (1, 0, 2))
    y_s, g_s = _ffn_sample(h2_s, x1_s, st_conv_t, w, tf)

    kpe_sl = slice(QK_NOPE_DIM, QK_HEAD_DIM)
    P = POOL_STATE_LEN
    return (
        y_p.reshape(B, S, D),
        y_s.reshape(DB, DS, D),
        ckv_p.reshape(1, B, S, kv_rank),
        kpe_p[:, kpe_sl].reshape(1, B, S, QK_ROPE_DIM),
        u_p.reshape(B, S, -1)[None, :, S - P:, :],
        conv_p[None],
        ckv_s.reshape(1, DB, DS, kv_rank),
        kpe_s[:, kpe_sl].reshape(1, DB, DS, QK_ROPE_DIM),
        jnp.concatenate([state_pool[l], u_s[:, None, :]], axis=1)[None, :, -P:, :],
        jnp.concatenate([state_conv[l], g_s[:, None, :]], axis=1)[None, :, -(CONV_WIDTH - 1):, :],
    )
```
